```python
import jax, jax.numpy as jnp
from jax import lax
import numpy as np

D_MODEL = 1024
BATCH = 32
SEQ = 256
DEPTH = 2
DEC_BATCH = 8
DEC_SEQ = 1024
PAST_LEN = 512

GRID_W = 64
N_HEADS = 4
QK_NOPE = 128
QK_ROPE = 64
V_DIM = 128
Q_LORA = 384
KV_LORA = 256
POOL_W = 512
POOL_GROUPS = 4
POOL_WINDOWS = (2, 4, 8, 16)
POOL_CH = POOL_W // POOL_GROUPS
D_MIX = N_HEADS * V_DIM + POOL_W
D_IN = Q_LORA + KV_LORA + QK_ROPE + POOL_W
D_FF = 2816
N_EXPERTS = 8
TOP_K = 2
D_FF_EXPERT = 3584
N_DENSE = (DEPTH + 1) // 2
N_MOE = DEPTH // 2
ROPE_BASE = 10000.0
EPS = 1e-6
Q_BLOCK = 128

kernel_name = "hymba_mla_pool_prefix_dit_step"


def rmsnorm(x, g):
    xf = x.astype(jnp.float32)
    y = xf * lax.rsqrt(jnp.mean(xf * xf, axis=-1, keepdims=True) + EPS)
    return (y * g.astype(jnp.float32)).astype(x.dtype)


def ada_params(cond, w_ada, b_ada):
    m = jax.nn.silu(cond) @ w_ada + b_ada
    return jnp.split(m[..., None, :], 6, axis=-1)


def modulate(h, shift, scale):
    return h * (1.0 + scale) + shift


def rope_2d_tables(n_tok):
    t = jnp.arange(n_tok)
    rows = (t // GRID_W).astype(jnp.float32)
    cols = (t % GRID_W).astype(jnp.float32)
    half = QK_ROPE // 2
    freqs = ROPE_BASE ** (-jnp.arange(0, half, 2, dtype=jnp.float32) / half)
    ang_r = rows[:, None] * freqs
    ang_c = cols[:, None] * freqs
    ang = jnp.concatenate([ang_r, ang_r, ang_c, ang_c], axis=-1)
    return jnp.cos(ang), jnp.sin(ang)


def rotate_half_axial(x):
    xr = x.reshape(x.shape[:-1] + (2, 2, QK_ROPE // 4))
    return jnp.concatenate([-xr[..., 1:, :], xr[..., :1, :]], axis=-2).reshape(x.shape)


def apply_rope(x, cos, sin):
    xf = x.astype(jnp.float32)
    return (xf * cos + rotate_half_axial(xf) * sin).astype(x.dtype)


def mixer_inputs(h, w_in, q_norm_g, kv_norm_g, w_uq):
    b, s, _ = h.shape
    proj = h @ w_in
    cq, ckv_raw, kpe, u = jnp.split(proj, [Q_LORA, Q_LORA + KV_LORA, Q_LORA + KV_LORA + QK_ROPE], axis=-1)
    q = (rmsnorm(cq, q_norm_g) @ w_uq).reshape(b, s, N_HEADS, QK_NOPE + QK_ROPE)
    ckv = rmsnorm(ckv_raw, kv_norm_g)
    return q[..., :QK_NOPE], q[..., QK_NOPE:], ckv, kpe, u


def expand_kv(ckv, w_ukv):
    b, s, _ = ckv.shape
    kv = (ckv @ w_ukv).reshape(b, s, N_HEADS, QK_NOPE + V_DIM)
    return kv[..., :QK_NOPE], kv[..., QK_NOPE:]


def mla_attend(q_nope, q_pe, k_nope, k_pe, v):
    b, sq = q_nope.shape[:2]
    nb = sq // Q_BLOCK
    scale = (QK_NOPE + QK_ROPE) ** -0.5

    def block(qs):
        qn, qp = qs
        s = (jnp.einsum('bqhd,bkhd->bhqk', qn, k_nope, preferred_element_type=jnp.float32)
             + jnp.einsum('bqhr,bkr->bhqk', qp, k_pe, preferred_element_type=jnp.float32))
        p = jax.nn.softmax(s * scale, axis=-1).astype(v.dtype)
        return jnp.einsum('bhqk,bkhd->bqhd', p, v)

    def to_blocks(t):
        return t.reshape((b, nb, Q_BLOCK) + t.shape[2:]).swapaxes(0, 1)

    o = lax.map(block, (to_blocks(q_nope), to_blocks(q_pe)))
    return o.swapaxes(0, 1).reshape(b, sq, N_HEADS * V_DIM)


def multiscale_pool(u):
    b, s, _ = u.shape
    ug = u.reshape(b, s, POOL_GROUPS, POOL_CH).astype(jnp.float32)
    cs = jnp.concatenate([jnp.zeros((b, 1, POOL_GROUPS, POOL_CH), jnp.float32), jnp.cumsum(ug, axis=1)], axis=1)
    t = jnp.arange(s)
    outs = []
    for g, w in enumerate(POOL_WINDOWS):
        lo = jnp.maximum(t - w // 2, 0)
        hi = jnp.minimum(t + w // 2, s)
        cnt = (hi - lo).astype(jnp.float32)[None, :, None]
        outs.append((cs[:, hi, g] - cs[:, lo, g]) / cnt - ug[:, :, g])
    return jnp.stack(outs, axis=2).astype(u.dtype)


def mixer_output(attn, u, w_pool, pool_scale, w_out):
    b, s = attn.shape[:2]
    pooled = jnp.einsum('bsgc,gcd->bsgd', multiscale_pool(u), w_pool).reshape(b, s, POOL_W) * pool_scale
    return jnp.concatenate([attn, pooled], axis=-1) @ w_out


def swiglu(h, w_gate, w_up, w_down):
    return (jax.nn.silu(h @ w_gate) * (h @ w_up)) @ w_down


def moe_swiglu(h, w_router, we_gate, we_up, we_down):
    b, s, d = h.shape
    x = h.reshape(b * s, d)
    logits = (x @ w_router).astype(jnp.float32)
    top_v, top_i = lax.top_k(logits, TOP_K)
    top_w = jax.nn.softmax(top_v, axis=-1)
    gates = jnp.sum(jax.nn.one_hot(top_i, N_EXPERTS, dtype=jnp.float32) * top_w[..., None], axis=1).astype(h.dtype)
    y = jnp.zeros_like(x)
    for e in range(N_EXPERTS):
        y = y + gates[:, e:e + 1] * swiglu(x, we_gate[e], we_up[e], we_down[e])
    return y.reshape(b, s, d)


def channel_mixer(l, h, ffn_w_gate, ffn_w_up, ffn_w_down, moe_w_router, moe_w_gate, moe_w_up, moe_w_down):
    j = l // 2
    if l % 2 == 0:
        return swiglu(h, ffn_w_gate[j], ffn_w_up[j], ffn_w_down[j])
    return moe_swiglu(h, moe_w_router[j], moe_w_gate[j], moe_w_up[j], moe_w_down[j])


def setup_inputs(seed: int = 0) -> dict:
    key = jax.random.key(seed)
    ks = iter(jax.random.split(key, 32))
    f32 = jnp.float32

    def nrm(shape, fan_in, mult=1.0):
        return jax.random.normal(next(ks), shape, f32) * (mult * fan_in ** -0.5)

    def gain(shape):
        return 1.0 + 0.02 * jax.random.normal(next(ks), shape, f32)

    return {
        "x_prompt": jax.random.normal(next(ks), (BATCH, SEQ, D_MODEL), f32),
        "x_sample": jax.random.normal(next(ks), (DEC_BATCH, DEC_SEQ, D_MODEL), f32),
        "c": jax.random.normal(next(ks), (DEC_BATCH, D_MODEL), f32),
        "cache_ckv": jax.random.normal(next(ks), (DEC_BATCH, DEPTH, PAST_LEN, KV_LORA), f32),
        "cache_kpe": jax.random.normal(next(ks), (DEC_BATCH, DEPTH, PAST_LEN, QK_ROPE), f32),
        "c_ctx": jax.random.normal(next(ks), (D_MODEL,), f32),
        "norm1_g": gain((DEPTH, D_MODEL)),
        "norm2_g": gain((DEPTH, D_MODEL)),
        "w_ada": nrm((DEPTH, D_MODEL, 6 * D_MODEL), D_MODEL, 0.5),
        "b_ada": 0.02 * jax.random.normal(next(ks), (DEPTH, 6 * D_MODEL), f32),
        "w_in": nrm((DEPTH, D_MODEL, D_IN), D_MODEL),
        "q_norm_g": gain((DEPTH, Q_LORA)),
        "kv_norm_g": gain((DEPTH, KV_LORA)),
        "w_uq": nrm((DEPTH, Q_LORA, N_HEADS * (QK_NOPE + QK_ROPE)), Q_LORA),
        "w_ukv": nrm((DEPTH, KV_LORA, N_HEADS * (QK_NOPE + V_DIM)), KV_LORA),
        "w_pool": nrm((DEPTH, POOL_GROUPS, POOL_CH, POOL_CH), POOL_CH),
        "pool_scale": gain((DEPTH, POOL_W)),
        "w_out": nrm((DEPTH, D_MIX, D_MODEL), D_MIX),
        "ffn_w_gate": nrm((N_DENSE, D_MODEL, D_FF), D_MODEL),
        "ffn_w_up": nrm((N_DENSE, D_MODEL, D_FF), D_MODEL),
        "ffn_w_down": nrm((N_DENSE, D_FF, D_MODEL), D_FF),
        "moe_w_router": nrm((N_MOE, D_MODEL, N_EXPERTS), D_MODEL),
        "moe_w_gate": nrm((N_MOE, N_EXPERTS, D_MODEL, D_FF_EXPERT), D_MODEL),
        "moe_w_up": nrm((N_MOE, N_EXPERTS, D_MODEL, D_FF_EXPERT), D_MODEL),
        "moe_w_down": nrm((N_MOE, N_EXPERTS, D_FF_EXPERT, D_MODEL), D_FF_EXPERT),
        "final_norm_g": gain((D_MODEL,)),
    }


def reference(x_prompt, x_sample, c, cache_ckv, cache_kpe, c_ctx,
              norm1_g, norm2_g, w_ada, b_ada, w_in, q_norm_g, kv_norm_g, w_uq, w_ukv,
              w_pool, pool_scale, w_out, ffn_w_gate, ffn_w_up, ffn_w_down,
              moe_w_router, moe_w_gate, moe_w_up, moe_w_down, final_norm_g):
    n_lat = x_sample.shape[1]
    cos, sin = rope_2d_tables(n_lat)
    xc, xl = x_prompt, x_sample
    new_ckv, new_kpe = [], []
    for l in range(DEPTH):
        sh1, sc1, g1, sh2, sc2, g2 = ada_params(c_ctx, w_ada[l], b_ada[l])
        h = modulate(rmsnorm(xc, norm1_g[l]), sh1, sc1)
        q_nope, q_pe, ckv, kpe, u = mixer_inputs(h, w_in[l], q_norm_g[l], kv_norm_g[l], w_uq[l])
        k_nope, v = expand_kv(ckv, w_ukv[l])
        attn = mla_attend(q_nope, q_pe, k_nope, kpe, v)
        xc = xc + g1 * mixer_output(attn, u, w_pool[l], pool_scale[l], w_out[l])
        h = modulate(rmsnorm(xc, norm2_g[l]), sh2, sc2)
        xc = xc + g2 * channel_mixer(l, h, ffn_w_gate, ffn_w_up, ffn_w_down,
                                     moe_w_router, moe_w_gate, moe_w_up, moe_w_down)
        new_ckv.append(ckv)
        new_kpe.append(kpe)

        sh1, sc1, g1, sh2, sc2, g2 = ada_params(c, w_ada[l], b_ada[l])
        h = modulate(rmsnorm(xl, norm1_g[l]), sh1, sc1)
        q_nope, q_pe, ckv_lat, kpe_lat, u = mixer_inputs(h, w_in[l], q_norm_g[l], kv_norm_g[l], w_uq[l])
        q_pe = apply_rope(q_pe, cos[:, None, :], sin[:, None, :])
        kpe_lat = apply_rope(kpe_lat, cos, sin)
        ckv_all = jnp.concatenate([ckv_lat, cache_ckv[:, l].astype(ckv_lat.dtype)], axis=1)
        kpe_all = jnp.concatenate([kpe_lat, cache_kpe[:, l].astype(kpe_lat.dtype)], axis=1)
        k_nope, v = expand_kv(ckv_all, w_ukv[l])
        attn = mla_attend(q_nope, q_pe, k_nope, kpe_all, v)
        xl = xl + g1 * mixer_output(attn, u, w_pool[l], pool_scale[l], w_out[l])
        h = modulate(rmsnorm(xl, norm2_g[l]), sh2, sc2)
        xl = xl + g2 * channel_mixer(l, h, ffn_w_gate, ffn_w_up, ffn_w_down,
                                     moe_w_router, moe_w_gate, moe_w_up, moe_w_down)

    y_prompt = rmsnorm(xc, final_norm_g)
    y_sample = rmsnorm(xl, final_norm_g)
    new_ckv_arr = jnp.stack(new_ckv, axis=1)
    new_kpe_arr = jnp.stack(new_kpe, axis=1)
    return (y_prompt, y_sample, new_ckv_arr, new_kpe_arr)
```

```python
import functools

import jax
import jax.numpy as jnp
from jax import lax
from jax.experimental import pallas as pl
from jax.experimental.pallas import tpu as pltpu

D_MODEL = 1024
BATCH = 32
SEQ = 256
DEPTH = 2
DEC_BATCH = 8
DEC_SEQ = 1024
PAST_LEN = 512
GRID_W = 64
N_HEADS = 4
QK_NOPE = 128
QK_ROPE = 64
V_DIM = 128
Q_LORA = 384
KV_LORA = 256
POOL_W = 512
POOL_GROUPS = 4
POOL_WINDOWS = (2, 4, 8, 16)
POOL_CH = POOL_W // POOL_GROUPS
D_FF = 2816
N_EXPERTS = 8
D_FF_EXPERT = 3584
ROPE_BASE = 10000.0
EPS = 1e-6

T_CTX = BATCH * SEQ
T_LAT = DEC_BATCH * DEC_SEQ
T_ALL = T_CTX + T_LAT

LANES = 128
HEAD_PAD = 256
QK_W = N_HEADS * HEAD_PAD
V_W = N_HEADS * V_DIM
OFF_CKV = Q_LORA
OFF_U = Q_LORA + KV_LORA
OFF_KA = OFF_U + POOL_W
OFF_KB = OFF_KA + LANES
N_PROJ = OFF_KB + LANES

TILE_IN = 1024
CHUNK = 256
HALO = 8
REGION = CHUNK + 2 * HALO
TILE_MIX = 512
TILE_FF = 512
COND_ROWS = 16
VMEM_LIMIT = 56 * 1024 * 1024

F32 = jnp.float32
BF16 = jnp.bfloat16


def _rms(x):
    return x * lax.rsqrt(jnp.mean(x * x, axis=-1, keepdims=True) + EPS)


def _dot(a, b):
    return jnp.dot(a, b, preferred_element_type=F32)


def _dot_nt(a, b):
    return lax.dot_general(a, b, (((1,), (1,)), ((), ())), preferred_element_type=F32)


def _silu(x):
    return x * (1.0 / (1.0 + jnp.exp(-x)))


def _cparams(*sem):
    return pltpu.CompilerParams(dimension_semantics=sem, vmem_limit_bytes=VMEM_LIMIT)


def _ada_kernel(cond_ref, w_ref, b_ref, o_ref):
    c = cond_ref[...]
    o_ref[...] = jnp.dot(_silu(c), w_ref[...], preferred_element_type=F32,
                         precision=lax.Precision.HIGHEST) + b_ref[...]


def _ada_call(cond, w_ada, b_ada):
    n_blk = 4
    bw = 6 * D_MODEL // n_blk
    return pl.pallas_call(
        _ada_kernel,
        out_shape=jax.ShapeDtypeStruct((DEPTH, COND_ROWS, 6 * D_MODEL), F32),
        grid=(DEPTH, n_blk),
        in_specs=[
            pl.BlockSpec((COND_ROWS, D_MODEL), lambda l, j: (0, 0)),
            pl.BlockSpec((None, D_MODEL, bw), lambda l, j: (l, 0, j)),
            pl.BlockSpec((None, 1, bw), lambda l, j: (l, 0, j)),
        ],
        out_specs=pl.BlockSpec((None, COND_ROWS, bw), lambda l, j: (l, 0, j)),
        compiler_params=_cparams("arbitrary", "arbitrary"),
        name="ada_params",
    )(cond, w_ada, b_ada.reshape(DEPTH, 1, 6 * D_MODEL))


def _cache_kernel(ckv_ref, kpe_ref, wuk_ref, wuv_ref, dup_ref, k_ref, v_ref):
    ckv = ckv_ref[...].astype(BF16)
    knope = _dot(ckv, wuk_ref[...])
    v_ref[...] = _dot(ckv, wuv_ref[...]).astype(BF16)
    kdup = _dot(kpe_ref[...].astype(BF16), dup_ref[...]).astype(BF16)
    for h in range(N_HEADS):
        k_ref[:, h * HEAD_PAD:h * HEAD_PAD + QK_NOPE] = knope[:, h * QK_NOPE:(h + 1) * QK_NOPE].astype(BF16)
        k_ref[:, h * HEAD_PAD + QK_NOPE:(h + 1) * HEAD_PAD] = kdup


def _cache_call(cache_ckv, cache_kpe, w_uk, w_uv, dup):
    n_tok = DEC_BATCH * PAST_LEN
    return pl.pallas_call(
        _cache_kernel,
        out_shape=(jax.ShapeDtypeStruct((DEPTH, n_tok, QK_W), BF16),
                   jax.ShapeDtypeStruct((DEPTH, n_tok, V_W), BF16)),
        grid=(DEPTH, DEC_BATCH),
        in_specs=[
            pl.BlockSpec((None, None, PAST_LEN, KV_LORA), lambda l, b: (b, l, 0, 0)),
            pl.BlockSpec((None, None, PAST_LEN, QK_ROPE), lambda l, b: (b, l, 0, 0)),
            pl.BlockSpec((None, KV_LORA, N_HEADS * QK_NOPE), lambda l, b: (l, 0, 0)),
            pl.BlockSpec((None, KV_LORA, V_W), lambda l, b: (l, 0, 0)),
            pl.BlockSpec((QK_ROPE, LANES), lambda l, b: (0, 0)),
        ],
        out_specs=(pl.BlockSpec((None, PAST_LEN, QK_W), lambda l, b: (l, b, 0)),
                   pl.BlockSpec((None, PAST_LEN, V_W), lambda l, b: (l, b, 0))),
        compiler_params=_cparams("arbitrary", "arbitrary"),
        name="cache_kv",
    )(cache_ckv, cache_kpe, w_uk, w_uv, dup)


def _in_kernel(x_ref, sh_ref, sc_ref, g_ref, win_ref, qg_ref, kvg_ref, wuq_ref, wuk_ref, wuv_ref,
               tq_ref, tc_ref, ts_ref, wpool_ref, ps_ref,
               q_out, k_out, v_out, ckv_out, kpe_out, pool_out, u_scr):
    is_ctx = pl.program_id(0) < T_CTX // TILE_IN
    shift = sh_ref[...]
    scale1 = 1.0 + sc_ref[...]
    qk_scale = (QK_NOPE + QK_ROPE) ** -0.5
    n_chunks = TILE_IN // CHUNK

    for c in range(n_chunks):
        r0 = c * CHUNK
        rows = slice(r0, r0 + CHUNK)
        h = (_rms(x_ref[rows, :]) * g_ref[...] * scale1 + shift).astype(BF16)
        proj = _dot(h, win_ref[...])

        qn = (_rms(proj[:, :Q_LORA]) * qg_ref[...]).astype(BF16)
        q = _dot(qn, wuq_ref[...]) * qk_scale
        tq = tq_ref[rows, :]
        for hd in range(N_HEADS):
            lo = hd * HEAD_PAD
            q_out[rows, lo:lo + QK_NOPE] = q[:, lo:lo + QK_NOPE].astype(BF16)
            q_out[rows, lo + QK_NOPE:lo + HEAD_PAD] = (q[:, lo + QK_NOPE:lo + HEAD_PAD] * tq).astype(BF16)

        ckv = _rms(proj[:, OFF_CKV:OFF_U]) * kvg_ref[...]
        ckv_out[rows, :] = ckv
        ckv_b = ckv.astype(BF16)
        knope = _dot(ckv_b, wuk_ref[...])
        v_out[rows, :] = _dot(ckv_b, wuv_ref[...]).astype(BF16)

        k_a = proj[:, OFF_KA:OFF_KB]
        k_b = proj[:, OFF_KB:N_PROJ]
        kpe_out[rows, :] = k_a[:, :QK_ROPE]
        kr = (k_a * tc_ref[rows, :] + k_b * ts_ref[rows, :]).astype(BF16)
        for hd in range(N_HEADS):
            lo = hd * HEAD_PAD
            k_out[rows, lo:lo + QK_NOPE] = knope[:, hd * QK_NOPE:(hd + 1) * QK_NOPE].astype(BF16)
            k_out[rows, lo + QK_NOPE:lo + HEAD_PAD] = kr

        base = c * REGION + HALO
        u_scr[base:base + CHUNK, :] = proj[:, OFF_U:OFF_KA]

    zeros = jnp.zeros((HALO, POOL_W), F32)

    @pl.when(is_ctx)
    def _():
        for c in range(n_chunks):
            u_scr[c * REGION:c * REGION + HALO, :] = zeros
            u_scr[c * REGION + HALO + CHUNK:(c + 1) * REGION, :] = zeros

    @pl.when(jnp.logical_not(is_ctx))
    def _():
        for c in range(n_chunks):
            lo_dst = slice(c * REGION, c * REGION + HALO)
            hi_dst = slice(c * REGION + HALO + CHUNK, (c + 1) * REGION)
            if c == 0:
                u_scr[lo_dst, :] = zeros
            else:
                prev_end = (c - 1) * REGION + HALO + CHUNK
                u_scr[lo_dst, :] = u_scr[prev_end - HALO:prev_end, :]
            if c == n_chunks - 1:
                u_scr[hi_dst, :] = zeros
            else:
                nxt = (c + 1) * REGION + HALO
                u_scr[hi_dst, :] = u_scr[nxt:nxt + HALO, :]

    seq_len = jnp.where(is_ctx, SEQ, DEC_SEQ)
    row = lax.broadcasted_iota(jnp.int32, (CHUNK, POOL_CH), 0)
    for c in range(n_chunks):
        base = c * REGION + HALO
        rows = slice(c * CHUNK, (c + 1) * CHUNK)
        t = row + jnp.where(is_ctx, 0, c * CHUNK)
        for g, w in enumerate(POOL_WINDOWS):
            cols = slice(g * POOL_CH, (g + 1) * POOL_CH)
            acc = u_scr[base - w // 2:base - w // 2 + CHUNK, cols]
            for j in range(-w // 2 + 1, w // 2):
                acc = acc + u_scr[base + j:base + j + CHUNK, cols]
            cnt = jnp.minimum(t + w // 2, seq_len) - jnp.maximum(t - w // 2, 0)
            pooled = acc / cnt.astype(F32) - u_scr[base:base + CHUNK, cols]
            lin = _dot(pooled.astype(BF16), wpool_ref[g]) * ps_ref[:, cols]
            pool_out[rows, cols] = lin.astype(BF16)


def _in_call(x, ada, g1, w_in, qg, kvg, w_uq, w_uk, w_uv, tq, tc, ts, w_pool, pscale):
    n_tiles = T_ALL // TILE_IN
    n_ctx = T_CTX // TILE_IN

    def cond_row(i):
        return jnp.maximum(i - n_ctx + 1, 0)

    def tab(i):
        return (jnp.minimum(jnp.maximum(i - n_ctx + 1, 0), 1), 0, 0)

    const2 = lambda i: (0, 0)
    return pl.pallas_call(
        _in_kernel,
        out_shape=(jax.ShapeDtypeStruct((T_ALL, QK_W), BF16),
                   jax.ShapeDtypeStruct((T_ALL, QK_W), BF16),
                   jax.ShapeDtypeStruct((T_ALL, V_W), BF16),
                   jax.ShapeDtypeStruct((T_ALL, KV_LORA), F32),
                   jax.ShapeDtypeStruct((T_ALL, QK_ROPE), F32),
                   jax.ShapeDtypeStruct((T_ALL, POOL_W), BF16)),
        grid=(n_tiles,),
        in_specs=[
            pl.BlockSpec((TILE_IN, D_MODEL), lambda i: (i, 0)),
            pl.BlockSpec((None, 1, D_MODEL), lambda i: (cond_row(i), 0, 0)),
            pl.BlockSpec((None, 1, D_MODEL), lambda i: (cond_row(i), 0, 1)),
            pl.BlockSpec((1, D_MODEL), const2),
            pl.BlockSpec((D_MODEL, N_PROJ), const2),
            pl.BlockSpec((1, Q_LORA), const2),
            pl.BlockSpec((1, KV_LORA), const2),
            pl.BlockSpec((Q_LORA, QK_W), const2),
            pl.BlockSpec((KV_LORA, N_HEADS * QK_NOPE), const2),
            pl.BlockSpec((KV_LORA, V_W), const2),
            pl.BlockSpec((None, TILE_IN, LANES), tab),
            pl.BlockSpec((None, TILE_IN, LANES), tab),
            pl.BlockSpec((None, TILE_IN, LANES), tab),
            pl.BlockSpec((POOL_GROUPS, POOL_CH, POOL_CH), lambda i: (0, 0, 0)),
            pl.BlockSpec((1, POOL_W), const2),
        ],
        out_specs=(pl.BlockSpec((TILE_IN, QK_W), lambda i: (i, 0)),
                   pl.BlockSpec((TILE_IN, QK_W), lambda i: (i, 0)),
                   pl.BlockSpec((TILE_IN, V_W), lambda i: (i, 0)),
                   pl.BlockSpec((TILE_IN, KV_LORA), lambda i: (i, 0)),
                   pl.BlockSpec((TILE_IN, QK_ROPE), lambda i: (i, 0)),
                   pl.BlockSpec((TILE_IN, POOL_W), lambda i: (i, 0))),
        scratch_shapes=[pltpu.VMEM((TILE_IN // CHUNK * REGION, POOL_W), F32)],
        compiler_params=_cparams("arbitrary"),
        name="in_proj",
    )(x, ada, ada, g1, w_in, qg, kvg, w_uq, w_uk, w_uv, tq, tc, ts, w_pool, pscale)


def _softmax_pv(scores, values):
    m = scores[0].max(axis=-1, keepdims=True)
    for s in scores[1:]:
        m = jnp.maximum(m, s.max(axis=-1, keepdims=True))
    den = None
    out = None
    for s, v in zip(scores, values):
        p = jnp.exp(s - m)
        d = p.sum(axis=-1, keepdims=True)
        o = _dot(p.astype(BF16), v)
        den = d if den is None else den + d
        out = o if out is None else out + o
    return out / den


def _attn_ctx_kernel(q_ref, k_ref, v_ref, o_ref):
    for hd in range(N_HEADS):
        qk = slice(hd * HEAD_PAD, (hd + 1) * HEAD_PAD)
        vv = slice(hd * V_DIM, (hd + 1) * V_DIM)
        s = _dot_nt(q_ref[:, qk], k_ref[:, qk])
        o_ref[:, vv] = _softmax_pv([s], [v_ref[:, vv]]).astype(BF16)


def _attn_lat_kernel(q_ref, k_ref, v_ref, kc_ref, vc_ref, prev_ref, o_ref):
    del prev_ref
    for hd in range(N_HEADS):
        qk = slice(hd * HEAD_PAD, (hd + 1) * HEAD_PAD)
        vv = slice(hd * V_DIM, (hd + 1) * V_DIM)
        q = q_ref[:, qk]
        s1 = _dot_nt(q, k_ref[:, qk])
        s2 = _dot_nt(q, kc_ref[:, qk])
        o_ref[:, vv] = _softmax_pv([s1, s2], [v_ref[:, vv], vc_ref[:, vv]]).astype(BF16)


def _attn_call(q, k, v, kc, vc):
    attn = pl.pallas_call(
        _attn_ctx_kernel,
        out_shape=jax.ShapeDtypeStruct((T_ALL, V_W), BF16),
        grid=(BATCH,),
        in_specs=[pl.BlockSpec((SEQ, QK_W), lambda b: (b, 0)),
                  pl.BlockSpec((SEQ, QK_W), lambda b: (b, 0)),
                  pl.BlockSpec((SEQ, V_W), lambda b: (b, 0))],
        out_specs=pl.BlockSpec((SEQ, V_W), lambda b: (b, 0)),
        compiler_params=_cparams("arbitrary"),
        name="attn_ctx",
    )(q, k, v)

    tq = 256
    n_q = DEC_SEQ // tq
    lat0 = T_CTX // DEC_SEQ
    return pl.pallas_call(
        _attn_lat_kernel,
        out_shape=jax.ShapeDtypeStruct((T_ALL, V_W), BF16),
        grid=(DEC_BATCH, n_q),
        in_specs=[pl.BlockSpec((tq, QK_W), lambda b, i: (T_CTX // tq + b * n_q + i, 0)),
                  pl.BlockSpec((DEC_SEQ, QK_W), lambda b, i: (lat0 + b, 0)),
                  pl.BlockSpec((DEC_SEQ, V_W), lambda b, i: (lat0 + b, 0)),
                  pl.BlockSpec((PAST_LEN, QK_W), lambda b, i: (b, 0)),
                  pl.BlockSpec((PAST_LEN, V_W), lambda b, i: (b, 0)),
                  pl.BlockSpec(memory_space=pl.ANY)],
        out_specs=pl.BlockSpec((tq, V_W), lambda b, i: (T_CTX // tq + b * n_q + i, 0)),
        input_output_aliases={5: 0},
        compiler_params=_cparams("arbitrary", "arbitrary"),
        name="attn_lat",
    )(q, k, v, kc, vc, attn)


def _mix_kernel(with_router, attn_ref, pool_ref, x_ref, g1_ref, sh2_ref, sc2_ref, n2_ref, wo_ref, *rest):
    if with_router:
        wr_ref, x_out, h_out, gate_out = rest
    else:
        x_out, h_out = rest
    y = _dot(attn_ref[...], wo_ref[:V_W, :]) + _dot(pool_ref[...], wo_ref[V_W:, :])
    x_new = x_ref[...] + g1_ref[...] * y
    x_out[...] = x_new
    h = (_rms(x_new) * n2_ref[...] * (1.0 + sc2_ref[...]) + sh2_ref[...]).astype(BF16)
    h_out[...] = h
    if with_router:
        logits = _dot(h, wr_ref[...])
        lane = lax.broadcasted_iota(jnp.int32, logits.shape, 1)
        neg = float(jnp.finfo(F32).min)
        lg = jnp.where(lane < N_EXPERTS, logits, neg)
        m1 = lg.max(axis=-1, keepdims=True)
        i1 = jnp.where(lg == m1, lane, LANES).min(axis=-1, keepdims=True)
        lg2 = jnp.where(lane == i1, neg, lg)
        m2 = lg2.max(axis=-1, keepdims=True)
        i2 = jnp.where(lg2 == m2, lane, LANES).min(axis=-1, keepdims=True)
        e = jnp.exp(m2 - m1)
        w1 = 1.0 / (1.0 + e)
        w2 = e / (1.0 + e)
        gate_out[...] = jnp.where(lane == i1, w1, 0.0) + jnp.where(lane == i2, w2, 0.0)


def _mod_row(i, tile):
    n_ctx = T_CTX // tile
    per_seq = DEC_SEQ // tile
    return jnp.where(i < n_ctx, 0, 1 + (i - n_ctx) // per_seq)


def _mix_call(attn, pooled, x, ada, n2, w_out, w_router):
    with_router = w_router is not None
    tm = TILE_MIX
    row = lambda i: (i, 0)
    const2 = lambda i: (0, 0)
    mod = lambda k: pl.BlockSpec((None, 1, D_MODEL), lambda i: (_mod_row(i, tm), 0, k))
    in_specs = [pl.BlockSpec((tm, V_W), row), pl.BlockSpec((tm, POOL_W), row), pl.BlockSpec((tm, D_MODEL), row),
                mod(2), mod(3), mod(4),
                pl.BlockSpec((1, D_MODEL), const2), pl.BlockSpec((D_MODEL, D_MODEL), const2)]
    out_shape = [jax.ShapeDtypeStruct((T_ALL, D_MODEL), F32), jax.ShapeDtypeStruct((T_ALL, D_MODEL), BF16)]
    out_specs = [pl.BlockSpec((tm, D_MODEL), row), pl.BlockSpec((tm, D_MODEL), row)]
    args = [attn, pooled, x, ada, ada, ada, n2, w_out]
    if with_router:
        in_specs.append(pl.BlockSpec((D_MODEL, LANES), const2))
        out_shape.append(jax.ShapeDtypeStruct((T_ALL, LANES), F32))
        out_specs.append(pl.BlockSpec((tm, LANES), row))
        args.append(w_router)
    return pl.pallas_call(
        functools.partial(_mix_kernel, with_router),
        out_shape=tuple(out_shape),
        grid=(T_ALL // tm,),
        in_specs=in_specs,
        out_specs=tuple(out_specs),
        compiler_params=_cparams("arbitrary"),
        name="mix_router" if with_router else "mix",
    )(*args)


def _ffn_kernel(h_ref, x_ref, g2_ref, wg_ref, wu_ref, wd_ref, o_ref, acc_ref):
    j = pl.program_id(1)
    h = h_ref[...]
    a = _silu(_dot(h, wg_ref[...])) * _dot(h, wu_ref[...])
    part = _dot(a.astype(BF16), wd_ref[...])

    @pl.when(j == 0)
    def _():
        acc_ref[...] = part

    @pl.when(j > 0)
    def _():
        acc_ref[...] += part

    @pl.when(j == pl.num_programs(1) - 1)
    def _():
        o_ref[...] = x_ref[...] + g2_ref[...] * acc_ref[...]


def _ffn_call(h, x, ada, wg, wu, wd):
    tm = TILE_FF
    tf = D_FF // 2
    return pl.pallas_call(
        _ffn_kernel,
        out_shape=jax.ShapeDtypeStruct((T_ALL, D_MODEL), F32),
        grid=(T_ALL // tm, D_FF // tf),
        in_specs=[pl.BlockSpec((tm, D_MODEL), lambda i, j: (i, 0)),
                  pl.BlockSpec((tm, D_MODEL), lambda i, j: (i, 0)),
                  pl.BlockSpec((None, 1, D_MODEL), lambda i, j: (_mod_row(i, tm), 0, 5)),
                  pl.BlockSpec((D_MODEL, tf), lambda i, j: (0, j)),
                  pl.BlockSpec((D_MODEL, tf), lambda i, j: (0, j)),
                  pl.BlockSpec((tf, D_MODEL), lambda i, j: (j, 0))],
        out_specs=pl.BlockSpec((tm, D_MODEL), lambda i, j: (i, 0)),
        scratch_shapes=[pltpu.VMEM((tm, D_MODEL), F32)],
        compiler_params=_cparams("arbitrary", "arbitrary"),
        name="ffn_dense",
    )(h, x, ada, wg, wu, wd)


def _moe_kernel(h_ref, x_ref, g2_ref, gate_ref, wg_ref, wu_ref, wd_ref, fg_ref, o_ref, acc_ref):
    e = pl.program_id(1)
    j = pl.program_id(2)
    h = h_ref[...]
    a = _silu(_dot(h, wg_ref[...])) * _dot(h, wu_ref[...])
    lane = lax.broadcasted_iota(jnp.int32, gate_ref.shape, 1)
    gate = jnp.where(lane == e, gate_ref[...], 0.0).sum(axis=-1, keepdims=True)
    part = gate * _dot(a.astype(BF16), wd_ref[...])
    first = jnp.logical_and(e == 0, j == 0)

    @pl.when(first)
    def _():
        acc_ref[...] = part

    @pl.when(jnp.logical_not(first))
    def _():
        acc_ref[...] += part

    @pl.when(jnp.logical_and(e == pl.num_programs(1) - 1, j == pl.num_programs(2) - 1))
    def _():
        o_ref[...] = _rms(x_ref[...] + g2_ref[...] * acc_ref[...]) * fg_ref[...]


def _moe_call(h, x, ada, gates, wg, wu, wd, final_g):
    tm = 1024
    tf = 512
    return pl.pallas_call(
        _moe_kernel,
        out_shape=jax.ShapeDtypeStruct((T_ALL, D_MODEL), F32),
        grid=(T_ALL // tm, N_EXPERTS, D_FF_EXPERT // tf),
        in_specs=[pl.BlockSpec((tm, D_MODEL), lambda i, e, j: (i, 0)),
                  pl.BlockSpec((tm, D_MODEL), lambda i, e, j: (i, 0)),
                  pl.BlockSpec((None, 1, D_MODEL), lambda i, e, j: (_mod_row(i, tm), 0, 5)),
                  pl.BlockSpec((tm, LANES), lambda i, e, j: (i, 0)),
                  pl.BlockSpec((None, D_MODEL, tf), lambda i, e, j: (e, 0, j)),
                  pl.BlockSpec((None, D_MODEL, tf), lambda i, e, j: (e, 0, j)),
                  pl.BlockSpec((None, tf, D_MODEL), lambda i, e, j: (e, j, 0)),
                  pl.BlockSpec((1, D_MODEL), lambda i, e, j: (0, 0))],
        out_specs=pl.BlockSpec((tm, D_MODEL), lambda i, e, j: (i, 0)),
        scratch_shapes=[pltpu.VMEM((tm, D_MODEL), F32)],
        compiler_params=_cparams("arbitrary", "arbitrary", "arbitrary"),
        name="moe_experts",
    )(h, x, ada, gates, wg, wu, wd, final_g)


def _rot_cols(w):
    q = QK_ROPE // 4
    return jnp.concatenate([-w[:, q:2 * q], w[:, :q], -w[:, 3 * q:], w[:, 2 * q:3 * q]], axis=1)


def _rope_tables():
    t = jnp.arange(DEC_SEQ)
    rows = (t // GRID_W).astype(F32)
    cols = (t % GRID_W).astype(F32)
    half = QK_ROPE // 2
    freqs = ROPE_BASE ** (-jnp.arange(0, half, 2, dtype=F32) / half)
    ang_r = rows[:, None] * freqs
    ang_c = cols[:, None] * freqs
    ang = jnp.concatenate([ang_r, ang_r, ang_c, ang_c], axis=-1)
    cos, sin = jnp.cos(ang), jnp.sin(ang)
    one, zero = jnp.ones_like(cos), jnp.zeros_like(cos)
    tq = jnp.stack([jnp.concatenate([one, zero], -1), jnp.concatenate([cos, sin], -1)])
    tc = jnp.stack([jnp.concatenate([one, one], -1), jnp.concatenate([cos, cos], -1)])
    ts = jnp.stack([jnp.concatenate([zero, zero], -1), jnp.concatenate([sin, sin], -1)])
    return tq, tc, ts


def _layer_weights(l, w_in, w_uq, w_ukv):
    wi = w_in[l]
    w_kpe = wi[:, OFF_U:OFF_U + QK_ROPE]
    w_rot = _rot_cols(w_kpe)
    w_in_r = jnp.concatenate([wi[:, :OFF_U], wi[:, OFF_U + QK_ROPE:], w_kpe, w_kpe, w_rot, w_rot], axis=1)
    heads = []
    for h in range(N_HEADS):
        blk = w_uq[l][:, h * (QK_NOPE + QK_ROPE):(h + 1) * (QK_NOPE + QK_ROPE)]
        heads += [blk, _rot_cols(blk[:, QK_NOPE:])]
    w_uq_r = jnp.concatenate(heads, axis=1)
    kv = w_ukv[l].reshape(KV_LORA, N_HEADS, QK_NOPE + V_DIM)
    w_uk = kv[:, :, :QK_NOPE].reshape(KV_LORA, N_HEADS * QK_NOPE)
    w_uv = kv[:, :, QK_NOPE:].reshape(KV_LORA, V_W)
    return w_in_r.astype(BF16), w_uq_r.astype(BF16), w_uk.astype(BF16), w_uv.astype(BF16)


def kernel(x_prompt, x_sample, c, cache_ckv, cache_kpe, c_ctx, norm1_g, norm2_g, w_ada, b_ada, w_in, q_norm_g,
           kv_norm_g, w_uq, w_ukv, w_pool, pool_scale, w_out, ffn_w_gate, ffn_w_up, ffn_w_down, moe_w_router,
           moe_w_gate, moe_w_up, moe_w_down, final_norm_g):
    x = jnp.concatenate([x_prompt.reshape(T_CTX, D_MODEL), x_sample.reshape(T_LAT, D_MODEL)], axis=0)
    cond = jnp.concatenate([c_ctx[None, :], c, jnp.zeros((COND_ROWS - 1 - DEC_BATCH, D_MODEL), F32)], axis=0)
    ada_all = _ada_call(cond, w_ada, b_ada)
    tq, tc, ts = _rope_tables()

    lw = [_layer_weights(l, w_in, w_uq, w_ukv) for l in range(DEPTH)]
    dup = jnp.concatenate([jnp.eye(QK_ROPE, dtype=BF16)] * 2, axis=1)
    kc_all, vc_all = _cache_call(cache_ckv, cache_kpe, jnp.stack([w[2] for w in lw]),
                                 jnp.stack([w[3] for w in lw]), dup)

    new_ckv, new_kpe = [], []
    for l in range(DEPTH):
        w_in_r, w_uq_r, w_uk, w_uv = lw[l]
        ada = ada_all[l].reshape(COND_ROWS, 1, 6 * D_MODEL)
        q, k, v, ckv, kpe, pooled = _in_call(
            x, ada, norm1_g[l][None, :], w_in_r, q_norm_g[l][None, :], kv_norm_g[l][None, :], w_uq_r, w_uk, w_uv,
            tq, tc, ts, w_pool[l].astype(BF16), pool_scale[l][None, :])
        new_ckv.append(ckv[:T_CTX].reshape(BATCH, SEQ, KV_LORA))
        new_kpe.append(kpe[:T_CTX].reshape(BATCH, SEQ, QK_ROPE))
        attn = _attn_call(q, k, v, kc_all[l], vc_all[l])
        j = l // 2
        if l % 2 == 0:
            x, h2 = _mix_call(attn, pooled, x, ada, norm2_g[l][None, :], w_out[l].astype(BF16), None)
            x = _ffn_call(h2, x, ada, ffn_w_gate[j].astype(BF16), ffn_w_up[j].astype(BF16),
                          ffn_w_down[j].astype(BF16))
        else:
            w_r = jnp.pad(moe_w_router[j], ((0, 0), (0, LANES - N_EXPERTS))).astype(BF16)
            x, h2, gates = _mix_call(attn, pooled, x, ada, norm2_g[l][None, :], w_out[l].astype(BF16), w_r)
            x = _moe_call(h2, x, ada, gates, moe_w_gate[j].astype(BF16), moe_w_up[j].astype(BF16),
                          moe_w_down[j].astype(BF16), final_norm_g[None, :])

    y_prompt = x[:T_CTX].reshape(BATCH, SEQ, D_MODEL)
    y_sample = x[T_CTX:].reshape(DEC_BATCH, DEC_SEQ, D_MODEL)
    return y_prompt, y_sample, jnp.stack(new_ckv, axis=1), jnp.stack(new_kpe, axis=1)
```

```python
import functools

import jax
import jax.numpy as jnp
from jax import lax
from jax.experimental import pallas as pl
from jax.experimental.pallas import tpu as pltpu

D_MODEL = 1024
BATCH = 32
SEQ = 256
DEPTH = 2
DEC_BATCH = 8
DEC_SEQ = 1024
PAST_LEN = 512
GRID_W = 64
N_HEADS = 4
QK_NOPE = 128
QK_ROPE = 64
V_DIM = 128
Q_LORA = 384
KV_LORA = 256
POOL_W = 512
POOL_GROUPS = 4
POOL_WINDOWS = (2, 4, 8, 16)
POOL_CH = POOL_W // POOL_GROUPS
D_FF = 2816
N_EXPERTS = 8
D_FF_EXPERT = 3584
ROPE_BASE = 10000.0
EPS = 1e-6

T_CTX = BATCH * SEQ
T_LAT = DEC_BATCH * DEC_SEQ
T_ALL = T_CTX + T_LAT

LANES = 128
HEAD_PAD = 256
QK_W = N_HEADS * HEAD_PAD
V_W = N_HEADS * V_DIM
OFF_CKV = Q_LORA
OFF_U = Q_LORA + KV_LORA
OFF_KA = OFF_U + POOL_W
OFF_KB = OFF_KA + LANES
N_PROJ = OFF_KB + LANES

TILE_IN = 1024
CHUNK = 256
HALO = 8
REGION = CHUNK + 2 * HALO
TILE_MIX = 512
TILE_FF = 512
TOP_K = 2
SLOT_UNIT = 16
UNITS_PER_TILE = TILE_MIX * TOP_K // SLOT_UNIT + N_EXPERTS
SLOTS_PER_TILE = UNITS_PER_TILE * SLOT_UNIT
N_ROUTE_TILES = T_ALL // TILE_MIX
N_UNITS = N_ROUTE_TILES * UNITS_PER_TILE
GEMM_UNITS = 64
GEMM_ROWS = GEMM_UNITS * SLOT_UNIT
N_GEMM_TILES = -(-N_UNITS // GEMM_UNITS) + N_EXPERTS
TILE_FE = 512
COND_ROWS = 16
VMEM_LIMIT = 56 * 1024 * 1024

F32 = jnp.float32
BF16 = jnp.bfloat16


def _rms(x):
    return x * lax.rsqrt(jnp.mean(x * x, axis=-1, keepdims=True) + EPS)


def _dot(a, b):
    return jnp.dot(a, b, preferred_element_type=F32)


def _dot_nt(a, b):
    return lax.dot_general(a, b, (((1,), (1,)), ((), ())), preferred_element_type=F32)


def _silu(x):
    return x * (1.0 / (1.0 + jnp.exp(-x)))


def _cparams(*sem):
    return pltpu.CompilerParams(dimension_semantics=sem, vmem_limit_bytes=VMEM_LIMIT)


def _ada_kernel(cond_ref, w_ref, b_ref, o_ref):
    c = cond_ref[...]
    o_ref[...] = jnp.dot(_silu(c), w_ref[...], preferred_element_type=F32,
                         precision=lax.Precision.HIGHEST) + b_ref[...]


def _ada_call(cond, w_ada, b_ada):
    n_blk = 4
    bw = 6 * D_MODEL // n_blk
    return pl.pallas_call(
        _ada_kernel,
        out_shape=jax.ShapeDtypeStruct((DEPTH, COND_ROWS, 6 * D_MODEL), F32),
        grid=(DEPTH, n_blk),
        in_specs=[
            pl.BlockSpec((COND_ROWS, D_MODEL), lambda l, j: (0, 0)),
            pl.BlockSpec((None, D_MODEL, bw), lambda l, j: (l, 0, j)),
            pl.BlockSpec((None, 1, bw), lambda l, j: (l, 0, j)),
        ],
        out_specs=pl.BlockSpec((None, COND_ROWS, bw), lambda l, j: (l, 0, j)),
        compiler_params=_cparams("arbitrary", "arbitrary"),
        name="ada_params",
    )(cond, w_ada, b_ada.reshape(DEPTH, 1, 6 * D_MODEL))


def _cache_kernel(ckv_ref, kpe_ref, wuk_ref, wuv_ref, dup_ref, k_ref, v_ref):
    ckv = ckv_ref[...].astype(BF16)
    knope = _dot(ckv, wuk_ref[...])
    v_ref[...] = _dot(ckv, wuv_ref[...]).astype(BF16)
    kdup = _dot(kpe_ref[...].astype(BF16), dup_ref[...]).astype(BF16)
    for h in range(N_HEADS):
        k_ref[:, h * HEAD_PAD:h * HEAD_PAD + QK_NOPE] = knope[:, h * QK_NOPE:(h + 1) * QK_NOPE].astype(BF16)
        k_ref[:, h * HEAD_PAD + QK_NOPE:(h + 1) * HEAD_PAD] = kdup


def _cache_call(cache_ckv, cache_kpe, w_uk, w_uv, dup):
    n_tok = DEC_BATCH * PAST_LEN
    return pl.pallas_call(
        _cache_kernel,
        out_shape=(jax.ShapeDtypeStruct((DEPTH, n_tok, QK_W), BF16),
                   jax.ShapeDtypeStruct((DEPTH, n_tok, V_W), BF16)),
        grid=(DEPTH, DEC_BATCH),
        in_specs=[
            pl.BlockSpec((None, None, PAST_LEN, KV_LORA), lambda l, b: (b, l, 0, 0)),
            pl.BlockSpec((None, None, PAST_LEN, QK_ROPE), lambda l, b: (b, l, 0, 0)),
            pl.BlockSpec((None, KV_LORA, N_HEADS * QK_NOPE), lambda l, b: (l, 0, 0)),
            pl.BlockSpec((None, KV_LORA, V_W), lambda l, b: (l, 0, 0)),
            pl.BlockSpec((QK_ROPE, LANES), lambda l, b: (0, 0)),
        ],
        out_specs=(pl.BlockSpec((None, PAST_LEN, QK_W), lambda l, b: (l, b, 0)),
                   pl.BlockSpec((None, PAST_LEN, V_W), lambda l, b: (l, b, 0))),
        compiler_params=_cparams("arbitrary", "arbitrary"),
        name="cache_kv",
    )(cache_ckv, cache_kpe, w_uk, w_uv, dup)


def _in_kernel(x_ref, sh_ref, sc_ref, g_ref, win_ref, qg_ref, kvg_ref, wuq_ref, wuk_ref, wuv_ref,
               tq_ref, tc_ref, ts_ref, wpool_ref, ps_ref,
               q_out, k_out, v_out, ckv_out, kpe_out, pool_out, u_scr):
    is_ctx = pl.program_id(0) < T_CTX // TILE_IN
    shift = sh_ref[...]
    scale1 = 1.0 + sc_ref[...]
    qk_scale = (QK_NOPE + QK_ROPE) ** -0.5
    n_chunks = TILE_IN // CHUNK

    for c in range(n_chunks):
        r0 = c * CHUNK
        rows = slice(r0, r0 + CHUNK)
        h = (_rms(x_ref[rows, :]) * g_ref[...] * scale1 + shift).astype(BF16)
        proj = _dot(h, win_ref[...])

        qn = (_rms(proj[:, :Q_LORA]) * qg_ref[...]).astype(BF16)
        q = _dot(qn, wuq_ref[...]) * qk_scale
        tq = tq_ref[rows, :]
        for hd in range(N_HEADS):
            lo = hd * HEAD_PAD
            q_out[rows, lo:lo + QK_NOPE] = q[:, lo:lo + QK_NOPE].astype(BF16)
            q_out[rows, lo + QK_NOPE:lo + HEAD_PAD] = (q[:, lo + QK_NOPE:lo + HEAD_PAD] * tq).astype(BF16)

        ckv = _rms(proj[:, OFF_CKV:OFF_U]) * kvg_ref[...]
        ckv_out[rows, :] = ckv
        ckv_b = ckv.astype(BF16)
        knope = _dot(ckv_b, wuk_ref[...])
        v_out[rows, :] = _dot(ckv_b, wuv_ref[...]).astype(BF16)

        k_a = proj[:, OFF_KA:OFF_KB]
        k_b = proj[:, OFF_KB:N_PROJ]
        kpe_out[rows, :] = k_a[:, :QK_ROPE]
        kr = (k_a * tc_ref[rows, :] + k_b * ts_ref[rows, :]).astype(BF16)
        for hd in range(N_HEADS):
            lo = hd * HEAD_PAD
            k_out[rows, lo:lo + QK_NOPE] = knope[:, hd * QK_NOPE:(hd + 1) * QK_NOPE].astype(BF16)
            k_out[rows, lo + QK_NOPE:lo + HEAD_PAD] = kr

        base = c * REGION + HALO
        u_scr[base:base + CHUNK, :] = proj[:, OFF_U:OFF_KA]

    zeros = jnp.zeros((HALO, POOL_W), F32)

    @pl.when(is_ctx)
    def _():
        for c in range(n_chunks):
            u_scr[c * REGION:c * REGION + HALO, :] = zeros
            u_scr[c * REGION + HALO + CHUNK:(c + 1) * REGION, :] = zeros

    @pl.when(jnp.logical_not(is_ctx))
    def _():
        for c in range(n_chunks):
            lo_dst = slice(c * REGION, c * REGION + HALO)
            hi_dst = slice(c * REGION + HALO + CHUNK, (c + 1) * REGION)
            if c == 0:
                u_scr[lo_dst, :] = zeros
            else:
                prev_end = (c - 1) * REGION + HALO + CHUNK
                u_scr[lo_dst, :] = u_scr[prev_end - HALO:prev_end, :]
            if c == n_chunks - 1:
                u_scr[hi_dst, :] = zeros
            else:
                nxt = (c + 1) * REGION + HALO
                u_scr[hi_dst, :] = u_scr[nxt:nxt + HALO, :]

    seq_len = jnp.where(is_ctx, SEQ, DEC_SEQ)
    row = lax.broadcasted_iota(jnp.int32, (CHUNK, POOL_CH), 0)
    for c in range(n_chunks):
        base = c * REGION + HALO
        rows = slice(c * CHUNK, (c + 1) * CHUNK)
        t = row + jnp.where(is_ctx, 0, c * CHUNK)
        for g, w in enumerate(POOL_WINDOWS):
            cols = slice(g * POOL_CH, (g + 1) * POOL_CH)
            acc = u_scr[base - w // 2:base - w // 2 + CHUNK, cols]
            for j in range(-w // 2 + 1, w // 2):
                acc = acc + u_scr[base + j:base + j + CHUNK, cols]
            cnt = jnp.minimum(t + w // 2, seq_len) - jnp.maximum(t - w // 2, 0)
            pooled = acc / cnt.astype(F32) - u_scr[base:base + CHUNK, cols]
            lin = _dot(pooled.astype(BF16), wpool_ref[g]) * ps_ref[:, cols]
            pool_out[rows, cols] = lin.astype(BF16)


def _in_call(x, ada, g1, w_in, qg, kvg, w_uq, w_uk, w_uv, tq, tc, ts, w_pool, pscale):
    n_tiles = T_ALL // TILE_IN
    n_ctx = T_CTX // TILE_IN

    def cond_row(i):
        return jnp.maximum(i - n_ctx + 1, 0)

    def tab(i):
        return (jnp.minimum(jnp.maximum(i - n_ctx + 1, 0), 1), 0, 0)

    const2 = lambda i: (0, 0)
    return pl.pallas_call(
        _in_kernel,
        out_shape=(jax.ShapeDtypeStruct((T_ALL, QK_W), BF16),
                   jax.ShapeDtypeStruct((T_ALL, QK_W), BF16),
                   jax.ShapeDtypeStruct((T_ALL, V_W), BF16),
                   jax.ShapeDtypeStruct((T_ALL, KV_LORA), F32),
                   jax.ShapeDtypeStruct((T_ALL, QK_ROPE), F32),
                   jax.ShapeDtypeStruct((T_ALL, POOL_W), BF16)),
        grid=(n_tiles,),
        in_specs=[
            pl.BlockSpec((TILE_IN, D_MODEL), lambda i: (i, 0)),
            pl.BlockSpec((None, 1, D_MODEL), lambda i: (cond_row(i), 0, 0)),
            pl.BlockSpec((None, 1, D_MODEL), lambda i: (cond_row(i), 0, 1)),
            pl.BlockSpec((1, D_MODEL), const2),
            pl.BlockSpec((D_MODEL, N_PROJ), const2),
            pl.BlockSpec((1, Q_LORA), const2),
            pl.BlockSpec((1, KV_LORA), const2),
            pl.BlockSpec((Q_LORA, QK_W), const2),
            pl.BlockSpec((KV_LORA, N_HEADS * QK_NOPE), const2),
            pl.BlockSpec((KV_LORA, V_W), const2),
            pl.BlockSpec((None, TILE_IN, LANES), tab),
            pl.BlockSpec((None, TILE_IN, LANES), tab),
            pl.BlockSpec((None, TILE_IN, LANES), tab),
            pl.BlockSpec((POOL_GROUPS, POOL_CH, POOL_CH), lambda i: (0, 0, 0)),
            pl.BlockSpec((1, POOL_W), const2),
        ],
        out_specs=(pl.BlockSpec((TILE_IN, QK_W), lambda i: (i, 0)),
                   pl.BlockSpec((TILE_IN, QK_W), lambda i: (i, 0)),
                   pl.BlockSpec((TILE_IN, V_W), lambda i: (i, 0)),
                   pl.BlockSpec((TILE_IN, KV_LORA), lambda i: (i, 0)),
                   pl.BlockSpec((TILE_IN, QK_ROPE), lambda i: (i, 0)),
                   pl.BlockSpec((TILE_IN, POOL_W), lambda i: (i, 0))),
        scratch_shapes=[pltpu.VMEM((TILE_IN // CHUNK * REGION, POOL_W), F32)],
        compiler_params=_cparams("arbitrary"),
        name="in_proj",
    )(x, ada, ada, g1, w_in, qg, kvg, w_uq, w_uk, w_uv, tq, tc, ts, w_pool, pscale)


def _softmax_pv(scores, values):
    m = scores[0].max(axis=-1, keepdims=True)
    for s in scores[1:]:
        m = jnp.maximum(m, s.max(axis=-1, keepdims=True))
    den = None
    out = None
    for s, v in zip(scores, values):
        p = jnp.exp(s - m)
        d = p.sum(axis=-1, keepdims=True)
        o = _dot(p.astype(BF16), v)
        den = d if den is None else den + d
        out = o if out is None else out + o
    return out / den


def _attn_ctx_kernel(q_ref, k_ref, v_ref, o_ref):
    for hd in range(N_HEADS):
        qk = slice(hd * HEAD_PAD, (hd + 1) * HEAD_PAD)
        vv = slice(hd * V_DIM, (hd + 1) * V_DIM)
        s = _dot_nt(q_ref[:, qk], k_ref[:, qk])
        o_ref[:, vv] = _softmax_pv([s], [v_ref[:, vv]]).astype(BF16)


def _attn_lat_kernel(q_ref, k_ref, v_ref, kc_ref, vc_ref, o_ref):
    for hd in range(N_HEADS):
        qk = slice(hd * HEAD_PAD, (hd + 1) * HEAD_PAD)
        vv = slice(hd * V_DIM, (hd + 1) * V_DIM)
        q = q_ref[:, qk]
        s1 = _dot_nt(q, k_ref[:, qk])
        s2 = _dot_nt(q, kc_ref[:, qk])
        o_ref[:, vv] = _softmax_pv([s1, s2], [v_ref[:, vv], vc_ref[:, vv]]).astype(BF16)


def _attn_call(q, k, v, kc, vc):
    attn_ctx = pl.pallas_call(
        _attn_ctx_kernel,
        out_shape=jax.ShapeDtypeStruct((T_CTX, V_W), BF16),
        grid=(BATCH,),
        in_specs=[pl.BlockSpec((SEQ, QK_W), lambda b: (b, 0)),
                  pl.BlockSpec((SEQ, QK_W), lambda b: (b, 0)),
                  pl.BlockSpec((SEQ, V_W), lambda b: (b, 0))],
        out_specs=pl.BlockSpec((SEQ, V_W), lambda b: (b, 0)),
        compiler_params=_cparams("arbitrary"),
        name="attn_ctx",
    )(q, k, v)

    tq = 256
    n_q = DEC_SEQ // tq
    lat0 = T_CTX // DEC_SEQ
    attn_lat = pl.pallas_call(
        _attn_lat_kernel,
        out_shape=jax.ShapeDtypeStruct((T_LAT, V_W), BF16),
        grid=(DEC_BATCH, n_q),
        in_specs=[pl.BlockSpec((tq, QK_W), lambda b, i: (T_CTX // tq + b * n_q + i, 0)),
                  pl.BlockSpec((DEC_SEQ, QK_W), lambda b, i: (lat0 + b, 0)),
                  pl.BlockSpec((DEC_SEQ, V_W), lambda b, i: (lat0 + b, 0)),
                  pl.BlockSpec((PAST_LEN, QK_W), lambda b, i: (b, 0)),
                  pl.BlockSpec((PAST_LEN, V_W), lambda b, i: (b, 0))],
        out_specs=pl.BlockSpec((tq, V_W), lambda b, i: (b * n_q + i, 0)),
        compiler_params=_cparams("arbitrary", "arbitrary"),
        name="attn_lat",
    )(q, k, v, kc, vc)
    return attn_ctx, attn_lat


def _mix_kernel(with_router, attn_c_ref, attn_l_ref, pool_ref, x_ref, g1_ref, sh2_ref, sc2_ref, n2_ref, wo_ref,
                *rest):
    if with_router:
        wr_ref, ltri_ref, utri_ref, x_out, xs_out, gs_out, route_out, meta_out = rest
    else:
        x_out, h_out = rest
    is_ctx = pl.program_id(0) < T_CTX // TILE_MIX
    attn = jnp.where(is_ctx, attn_c_ref[...], attn_l_ref[...])
    y = _dot(attn, wo_ref[:V_W, :]) + _dot(pool_ref[...], wo_ref[V_W:, :])
    x_new = x_ref[...] + g1_ref[...] * y
    x_out[...] = x_new
    h = (_rms(x_new) * n2_ref[...] * (1.0 + sc2_ref[...]) + sh2_ref[...]).astype(BF16)
    if not with_router:
        h_out[...] = h
        return

    logits = _dot(h, wr_ref[...])
    lane = lax.broadcasted_iota(jnp.int32, logits.shape, 1)
    neg = float(jnp.finfo(F32).min)
    lg = jnp.where(lane < N_EXPERTS, logits, neg)
    m1 = lg.max(axis=-1, keepdims=True)
    i1 = jnp.where(lg == m1, lane, LANES).min(axis=-1, keepdims=True)
    lg2 = jnp.where(lane == i1, neg, lg)
    m2 = lg2.max(axis=-1, keepdims=True)
    i2 = jnp.where(lg2 == m2, lane, LANES).min(axis=-1, keepdims=True)
    e = jnp.exp(m2 - m1)
    w1 = 1.0 / (1.0 + e)
    w2 = e / (1.0 + e)

    sel1 = lane == i1
    sel2 = lane == i2
    member = jnp.where(jnp.logical_or(sel1, sel2), 1.0, 0.0)
    rank = _dot(ltri_ref[...], member.astype(BF16))
    n_tok = member.sum(axis=0, keepdims=True)
    units = jnp.floor((n_tok + (SLOT_UNIT - 1)) * (1.0 / SLOT_UNIT))
    unit_off = _dot(jnp.broadcast_to(units, (8, LANES)).astype(BF16), utri_ref[...])[0:1, :]
    slot_of = SLOT_UNIT * unit_off + rank
    slot1 = jnp.where(sel1, slot_of, 0.0).sum(axis=-1, keepdims=True)
    slot2 = jnp.where(sel2, slot_of, 0.0).sum(axis=-1, keepdims=True)
    route = jnp.where(lane == 0, slot1, jnp.where(lane == 1, slot2,
                      jnp.where(lane == 2, w1, jnp.where(lane == 3, w2, 0.0))))
    route_out[...] = route
    sub = lax.broadcasted_iota(jnp.int32, (8, LANES), 0)
    meta_out[...] = jnp.where(sub == 0, units, jnp.where(sub == 1, unit_off, 0.0)).astype(jnp.int32)

    rt = route.T
    s1 = rt[0:1, :].astype(jnp.int32)
    s2 = rt[1:2, :].astype(jnp.int32)
    srow = lax.broadcasted_iota(jnp.int32, (SLOTS_PER_TILE, TILE_MIX), 0)
    hit1 = srow == s1
    hit2 = srow == s2
    perm = jnp.where(jnp.logical_or(hit1, hit2), 1.0, 0.0).astype(BF16)
    xs_out[...] = _dot(perm, h).astype(BF16)
    gate = (jnp.where(hit1, rt[2:3, :], 0.0) + jnp.where(hit2, rt[3:4, :], 0.0)).sum(axis=-1, keepdims=True)
    gs_out[...] = jnp.broadcast_to(gate, (SLOTS_PER_TILE, LANES))


def _mod_row(i, tile):
    n_ctx = T_CTX // tile
    per_seq = DEC_SEQ // tile
    return jnp.where(i < n_ctx, 0, 1 + (i - n_ctx) // per_seq)


def _mix_call(attn_ctx, attn_lat, pooled, x, ada, n2, w_out, w_router):
    with_router = w_router is not None
    tm = TILE_MIX
    n_ctx = T_CTX // tm
    row = lambda i: (i, 0)
    const2 = lambda i: (0, 0)
    mod = lambda k: pl.BlockSpec((None, 1, D_MODEL), lambda i: (_mod_row(i, tm), 0, k))
    in_specs = [pl.BlockSpec((tm, V_W), lambda i: (jnp.minimum(i, n_ctx - 1), 0)),
                pl.BlockSpec((tm, V_W), lambda i: (jnp.maximum(i - n_ctx, 0), 0)),
                pl.BlockSpec((tm, POOL_W), row), pl.BlockSpec((tm, D_MODEL), row),
                mod(2), mod(3), mod(4),
                pl.BlockSpec((1, D_MODEL), const2), pl.BlockSpec((D_MODEL, D_MODEL), const2)]
    args = [attn_ctx, attn_lat, pooled, x, ada, ada, ada, n2, w_out]
    if with_router:
        t_i = jnp.arange(tm)
        ltri = (t_i[None, :] < t_i[:, None]).astype(BF16)
        l_i = jnp.arange(LANES)
        utri = (l_i[:, None] < l_i[None, :]).astype(BF16)
        in_specs += [pl.BlockSpec((D_MODEL, LANES), const2), pl.BlockSpec((tm, tm), const2),
                     pl.BlockSpec((LANES, LANES), const2)]
        args += [w_router, ltri, utri]
        out_shape = [jax.ShapeDtypeStruct((T_ALL, D_MODEL), F32),
                     jax.ShapeDtypeStruct((N_ROUTE_TILES * SLOTS_PER_TILE, D_MODEL), BF16),
                     jax.ShapeDtypeStruct((N_ROUTE_TILES * SLOTS_PER_TILE, LANES), F32),
                     jax.ShapeDtypeStruct((T_ALL, LANES), F32),
                     jax.ShapeDtypeStruct((N_ROUTE_TILES, 8, LANES), jnp.int32)]
        out_specs = [pl.BlockSpec((tm, D_MODEL), row), pl.BlockSpec((SLOTS_PER_TILE, D_MODEL), row),
                     pl.BlockSpec((SLOTS_PER_TILE, LANES), row), pl.BlockSpec((tm, LANES), row),
                     pl.BlockSpec((None, 8, LANES), lambda i: (i, 0, 0))]
    else:
        out_shape = [jax.ShapeDtypeStruct((T_ALL, D_MODEL), F32), jax.ShapeDtypeStruct((T_ALL, D_MODEL), BF16)]
        out_specs = [pl.BlockSpec((tm, D_MODEL), row), pl.BlockSpec((tm, D_MODEL), row)]
    return pl.pallas_call(
        functools.partial(_mix_kernel, with_router),
        out_shape=tuple(out_shape),
        grid=(T_ALL // tm,),
        in_specs=in_specs,
        out_specs=tuple(out_specs),
        compiler_params=_cparams("arbitrary"),
        name="mix_router" if with_router else "mix",
    )(*args)


def _ffn_kernel(h_ref, x_ref, g2_ref, wg_ref, wu_ref, wd_ref, o_ref, acc_ref):
    j = pl.program_id(1)
    h = h_ref[...]
    a = _silu(_dot(h, wg_ref[...])) * _dot(h, wu_ref[...])
    part = _dot(a.astype(BF16), wd_ref[...])

    @pl.when(j == 0)
    def _():
        acc_ref[...] = part

    @pl.when(j > 0)
    def _():
        acc_ref[...] += part

    @pl.when(j == pl.num_programs(1) - 1)
    def _():
        o_ref[...] = x_ref[...] + g2_ref[...] * acc_ref[...]


def _ffn_call(h, x, ada, wg, wu, wd):
    tm = TILE_FF
    tf = D_FF // 2
    return pl.pallas_call(
        _ffn_kernel,
        out_shape=jax.ShapeDtypeStruct((T_ALL, D_MODEL), F32),
        grid=(T_ALL // tm, D_FF // tf),
        in_specs=[pl.BlockSpec((tm, D_MODEL), lambda i, j: (i, 0)),
                  pl.BlockSpec((tm, D_MODEL), lambda i, j: (i, 0)),
                  pl.BlockSpec((None, 1, D_MODEL), lambda i, j: (_mod_row(i, tm), 0, 5)),
                  pl.BlockSpec((D_MODEL, tf), lambda i, j: (0, j)),
                  pl.BlockSpec((D_MODEL, tf), lambda i, j: (0, j)),
                  pl.BlockSpec((tf, D_MODEL), lambda i, j: (j, 0))],
        out_specs=pl.BlockSpec((tm, D_MODEL), lambda i, j: (i, 0)),
        scratch_shapes=[pltpu.VMEM((tm, D_MODEL), F32)],
        compiler_params=_cparams("arbitrary", "arbitrary"),
        name="ffn_dense",
    )(h, x, ada, wg, wu, wd)


def _route_tables(meta):
    units = meta[:, 0, :N_EXPERTS]
    offs = meta[:, 1, :N_EXPERTS]
    cum = jnp.cumsum(units, axis=0)
    total = cum[-1]
    tiles_e = (total + GEMM_UNITS - 1) // GEMM_UNITS
    tile_end = jnp.cumsum(tiles_e)
    n_act = tile_end[-1]
    m = jnp.arange(N_GEMM_TILES)
    m_eff = jnp.minimum(m, jnp.maximum(n_act - 1, 0))
    expert = jnp.minimum(jnp.sum(tile_end[None, :] <= m_eff[:, None], axis=1), N_EXPERTS - 1)
    first_q = (m_eff - (tile_end - tiles_e)[expert]) * GEMM_UNITS
    count = jnp.where(m < n_act, jnp.clip(total[expert] - first_q, 0, GEMM_UNITS), 0)
    q = first_q[:, None] + jnp.arange(GEMM_UNITS)[None, :]
    cum_e = cum.T[expert]
    src_tile = jnp.minimum(jnp.sum(cum_e[:, None, :] <= q[:, :, None], axis=2), N_ROUTE_TILES - 1)
    before = jnp.take_along_axis(cum_e, src_tile, axis=1) - units.T[expert[:, None], src_tile]
    uid = src_tile * UNITS_PER_TILE + offs.T[expert[:, None], src_tile] + (q - before)
    uid = jnp.clip(uid, 0, N_UNITS - 1)
    i32 = jnp.int32
    return expert.astype(i32), count.astype(i32), uid.reshape(-1).astype(i32), n_act.reshape(1).astype(i32)


def _gemm_kernel(em_ref, cnt_ref, ul_ref, nact_ref, xs_hbm, gs_hbm, wg_ref, wu_ref, wd_ref, ys_in, ys_hbm,
                 xbuf, gbuf, obuf, acc_ref, sem_in, sem_out):
    del em_ref, ys_in
    m = pl.program_id(0)
    j = pl.program_id(1)
    last_j = pl.num_programs(1) - 1
    n_act = nact_ref[0]
    active = m < n_act
    slot = m % 2

    def rows(r):
        return pl.ds(pl.multiple_of(r * SLOT_UNIT, SLOT_UNIT), SLOT_UNIT)

    def in_copies(mm, sl, r):
        uid = ul_ref[mm * GEMM_UNITS + r]
        return (pltpu.make_async_copy(xs_hbm.at[uid], xbuf.at[sl, rows(r)], sem_in.at[sl]),
                pltpu.make_async_copy(gs_hbm.at[uid], gbuf.at[sl, rows(r)], sem_in.at[sl]))

    def out_copy(mm, r):
        uid = ul_ref[mm * GEMM_UNITS + r]
        return pltpu.make_async_copy(obuf.at[rows(r)], ys_hbm.at[uid], sem_out.at[0])

    def start_in(mm, sl):
        n = cnt_ref[mm]

        def issue(r, carry):
            for cp in in_copies(mm, sl, r):
                cp.start()
            return carry

        def clear(r, carry):
            xbuf[sl, rows(r), :] = jnp.zeros((SLOT_UNIT, D_MODEL), BF16)
            gbuf[sl, rows(r), :] = jnp.zeros((SLOT_UNIT, LANES), F32)
            return carry

        lax.fori_loop(0, n, issue, 0)
        lax.fori_loop(n, GEMM_UNITS, clear, 0)

    def wait_in(mm, sl):
        def body(r, carry):
            for cp in in_copies(mm, sl, r):
                cp.wait()
            return carry

        lax.fori_loop(0, cnt_ref[mm], body, 0)

    def start_out(mm):
        def body(r, carry):
            out_copy(mm, r).start()
            return carry

        lax.fori_loop(0, cnt_ref[mm], body, 0)

    def wait_out(mm):
        def body(r, carry):
            out_copy(mm, r).wait()
            return carry

        lax.fori_loop(0, cnt_ref[mm], body, 0)

    @pl.when(jnp.logical_and(active, j == 0))
    def _():
        @pl.when(m == 0)
        def _():
            start_in(0, 0)

        wait_in(m, slot)

        @pl.when(m + 1 < n_act)
        def _():
            start_in(m + 1, 1 - slot)

    @pl.when(active)
    def _():
        h = xbuf[slot]
        a = _silu(_dot(h, wg_ref[...])) * _dot(h, wu_ref[...])
        part = _dot(a.astype(BF16), wd_ref[...])

        @pl.when(j == 0)
        def _():
            acc_ref[...] = part

        @pl.when(j > 0)
        def _():
            acc_ref[...] += part

    @pl.when(jnp.logical_and(active, j == last_j))
    def _():
        @pl.when(m > 0)
        def _():
            wait_out(m - 1)

        gate = gbuf[slot]
        for cb in range(D_MODEL // LANES):
            cols = slice(cb * LANES, (cb + 1) * LANES)
            obuf[:, cols] = (acc_ref[:, cols] * gate).astype(BF16)
        start_out(m)

        @pl.when(m == n_act - 1)
        def _():
            wait_out(m)


def _gemm_call(expert, count, uids, n_act, xs, gs, wg, wu, wd):
    tf = TILE_FE
    n_j = D_FF_EXPERT // tf

    def w_col(m, j, em, cnt, ul, nact):
        return (em[m], 0, jnp.where(m < nact[0], j, n_j - 1))

    def w_row(m, j, em, cnt, ul, nact):
        return (em[m], jnp.where(m < nact[0], j, n_j - 1), 0)

    ys0 = jnp.zeros((N_UNITS, SLOT_UNIT, D_MODEL), BF16)
    grid_spec = pltpu.PrefetchScalarGridSpec(
        num_scalar_prefetch=4,
        grid=(N_GEMM_TILES, n_j),
        in_specs=[pl.BlockSpec(memory_space=pl.ANY),
                  pl.BlockSpec(memory_space=pl.ANY),
                  pl.BlockSpec((None, D_MODEL, tf), w_col),
                  pl.BlockSpec((None, D_MODEL, tf), w_col),
                  pl.BlockSpec((None, tf, D_MODEL), w_row),
                  pl.BlockSpec(memory_space=pl.ANY)],
        out_specs=pl.BlockSpec(memory_space=pl.ANY),
        scratch_shapes=[pltpu.VMEM((2, GEMM_ROWS, D_MODEL), BF16),
                        pltpu.VMEM((2, GEMM_ROWS, LANES), F32),
                        pltpu.VMEM((GEMM_ROWS, D_MODEL), BF16),
                        pltpu.VMEM((GEMM_ROWS, D_MODEL), F32),
                        pltpu.SemaphoreType.DMA((2,)),
                        pltpu.SemaphoreType.DMA((1,))])
    return pl.pallas_call(
        _gemm_kernel,
        out_shape=jax.ShapeDtypeStruct((N_UNITS, SLOT_UNIT, D_MODEL), BF16),
        grid_spec=grid_spec,
        input_output_aliases={9: 0},
        compiler_params=_cparams("arbitrary", "arbitrary"),
        name="moe_experts",
    )(expert, count, uids, n_act,
      xs.reshape(N_UNITS, SLOT_UNIT, D_MODEL), gs.reshape(N_UNITS, SLOT_UNIT, LANES), wg, wu, wd, ys0)


def _combine_kernel(ys_ref, route_ref, x_ref, g2_ref, fg_ref, o_ref):
    route = route_ref[...]
    s1 = route[:, 0:1].astype(jnp.int32)
    s2 = route[:, 1:2].astype(jnp.int32)
    scol = lax.broadcasted_iota(jnp.int32, (TILE_MIX, SLOTS_PER_TILE), 1)
    unperm = jnp.where(jnp.logical_or(scol == s1, scol == s2), 1.0, 0.0).astype(BF16)
    y = _dot(unperm, ys_ref[...])
    o_ref[...] = _rms(x_ref[...] + g2_ref[...] * y) * fg_ref[...]


def _combine_call(ys, route, x, ada, final_g):
    tm = TILE_MIX
    return pl.pallas_call(
        _combine_kernel,
        out_shape=jax.ShapeDtypeStruct((T_ALL, D_MODEL), F32),
        grid=(T_ALL // tm,),
        in_specs=[pl.BlockSpec((SLOTS_PER_TILE, D_MODEL), lambda i: (i, 0)),
                  pl.BlockSpec((tm, LANES), lambda i: (i, 0)),
                  pl.BlockSpec((tm, D_MODEL), lambda i: (i, 0)),
                  pl.BlockSpec((None, 1, D_MODEL), lambda i: (_mod_row(i, tm), 0, 5)),
                  pl.BlockSpec((1, D_MODEL), lambda i: (0, 0))],
        out_specs=pl.BlockSpec((tm, D_MODEL), lambda i: (i, 0)),
        compiler_params=_cparams("arbitrary"),
        name="moe_combine",
    )(ys.reshape(N_UNITS * SLOT_UNIT, D_MODEL), route, x, ada, final_g)


def _rot_cols(w):
    q = QK_ROPE // 4
    return jnp.concatenate([-w[:, q:2 * q], w[:, :q], -w[:, 3 * q:], w[:, 2 * q:3 * q]], axis=1)


def _rope_tables():
    t = jnp.arange(DEC_SEQ)
    rows = (t // GRID_W).astype(F32)
    cols = (t % GRID_W).astype(F32)
    half = QK_ROPE // 2
    freqs = ROPE_BASE ** (-jnp.arange(0, half, 2, dtype=F32) / half)
    ang_r = rows[:, None] * freqs
    ang_c = cols[:, None] * freqs
    ang = jnp.concatenate([ang_r, ang_r, ang_c, ang_c], axis=-1)
    cos, sin = jnp.cos(ang), jnp.sin(ang)
    one, zero = jnp.ones_like(cos), jnp.zeros_like(cos)
    tq = jnp.stack([jnp.concatenate([one, zero], -1), jnp.concatenate([cos, sin], -1)])
    tc = jnp.stack([jnp.concatenate([one, one], -1), jnp.concatenate([cos, cos], -1)])
    ts = jnp.stack([jnp.concatenate([zero, zero], -1), jnp.concatenate([sin, sin], -1)])
    return tq, tc, ts


def _layer_weights(l, w_in, w_uq, w_ukv):
    wi = w_in[l]
    w_kpe = wi[:, OFF_U:OFF_U + QK_ROPE]
    w_rot = _rot_cols(w_kpe)
    w_in_r = jnp.concatenate([wi[:, :OFF_U], wi[:, OFF_U + QK_ROPE:], w_kpe, w_kpe, w_rot, w_rot], axis=1)
    heads = []
    for h in range(N_HEADS):
        blk = w_uq[l][:, h * (QK_NOPE + QK_ROPE):(h + 1) * (QK_NOPE + QK_ROPE)]
        heads += [blk, _rot_cols(blk[:, QK_NOPE:])]
    w_uq_r = jnp.concatenate(heads, axis=1)
    kv = w_ukv[l].reshape(KV_LORA, N_HEADS, QK_NOPE + V_DIM)
    w_uk = kv[:, :, :QK_NOPE].reshape(KV_LORA, N_HEADS * QK_NOPE)
    w_uv = kv[:, :, QK_NOPE:].reshape(KV_LORA, V_W)
    return w_in_r.astype(BF16), w_uq_r.astype(BF16), w_uk.astype(BF16), w_uv.astype(BF16)


def kernel(x_prompt, x_sample, c, cache_ckv, cache_kpe, c_ctx, norm1_g, norm2_g, w_ada, b_ada, w_in, q_norm_g,
           kv_norm_g, w_uq, w_ukv, w_pool, pool_scale, w_out, ffn_w_gate, ffn_w_up, ffn_w_down, moe_w_router,
           moe_w_gate, moe_w_up, moe_w_down, final_norm_g):
    x = jnp.concatenate([x_prompt.reshape(T_CTX, D_MODEL), x_sample.reshape(T_LAT, D_MODEL)], axis=0)
    cond = jnp.concatenate([c_ctx[None, :], c, jnp.zeros((COND_ROWS - 1 - DEC_BATCH, D_MODEL), F32)], axis=0)
    ada_all = _ada_call(cond, w_ada, b_ada)
    tq, tc, ts = _rope_tables()

    lw = [_layer_weights(l, w_in, w_uq, w_ukv) for l in range(DEPTH)]
    dup = jnp.concatenate([jnp.eye(QK_ROPE, dtype=BF16)] * 2, axis=1)
    kc_all, vc_all = _cache_call(cache_ckv, cache_kpe, jnp.stack([w[2] for w in lw]),
                                 jnp.stack([w[3] for w in lw]), dup)

    new_ckv, new_kpe = [], []
    for l in range(DEPTH):
        w_in_r, w_uq_r, w_uk, w_uv = lw[l]
        ada = ada_all[l].reshape(COND_ROWS, 1, 6 * D_MODEL)
        q, k, v, ckv, kpe, pooled = _in_call(
            x, ada, norm1_g[l][None, :], w_in_r, q_norm_g[l][None, :], kv_norm_g[l][None, :], w_uq_r, w_uk, w_uv,
            tq, tc, ts, w_pool[l].astype(BF16), pool_scale[l][None, :])
        new_ckv.append(ckv[:T_CTX].reshape(BATCH, SEQ, KV_LORA))
        new_kpe.append(kpe[:T_CTX].reshape(BATCH, SEQ, QK_ROPE))
        attn_c, attn_l = _attn_call(q, k, v, kc_all[l], vc_all[l])
        j = l // 2
        if l % 2 == 0:
            x, h2 = _mix_call(attn_c, attn_l, pooled, x, ada, norm2_g[l][None, :], w_out[l].astype(BF16), None)
            x = _ffn_call(h2, x, ada, ffn_w_gate[j].astype(BF16), ffn_w_up[j].astype(BF16),
                          ffn_w_down[j].astype(BF16))
        else:
            w_r = jnp.pad(moe_w_router[j], ((0, 0), (0, LANES - N_EXPERTS))).astype(BF16)
            x, xs, gs, route, meta = _mix_call(attn_c, attn_l, pooled, x, ada, norm2_g[l][None, :],
                                               w_out[l].astype(BF16), w_r)
            expert, count, uids, n_act = _route_tables(meta)
            ys = _gemm_call(expert, count, uids, n_act, xs, gs, moe_w_gate[j].astype(BF16),
                            moe_w_up[j].astype(BF16), moe_w_down[j].astype(BF16))
            x = _combine_call(ys, route, x, ada, final_norm_g[None, :])

    y_prompt = x[:T_CTX].reshape(BATCH, SEQ, D_MODEL)
    y_sample = x[T_CTX:].reshape(DEC_BATCH, DEC_SEQ, D_MODEL)
    return y_prompt, y_sample, jnp.stack(new_ckv, axis=1), jnp.stack(new_kpe, axis=1)
```

```python
import functools

import jax
import jax.numpy as jnp
from jax import lax
from jax.experimental import pallas as pl
from jax.experimental.pallas import tpu as pltpu

D_MODEL = 1024
BATCH = 32
SEQ = 256
DEPTH = 2
DEC_BATCH = 8
DEC_SEQ = 1024
PAST_LEN = 512
GRID_W = 64
N_HEADS = 4
QK_NOPE = 128
QK_ROPE = 64
V_DIM = 128
Q_LORA = 384
KV_LORA = 256
POOL_W = 512
POOL_GROUPS = 4
POOL_WINDOWS = (2, 4, 8, 16)
POOL_CH = POOL_W // POOL_GROUPS
D_FF = 2816
N_EXPERTS = 8
D_FF_EXPERT = 3584
ROPE_BASE = 10000.0
EPS = 1e-6

T_CTX = BATCH * SEQ
T_LAT = DEC_BATCH * DEC_SEQ
T_ALL = T_CTX + T_LAT

LANES = 128
HEAD_PAD = 256
QK_W = N_HEADS * HEAD_PAD
V_W = N_HEADS * V_DIM
OFF_CKV = Q_LORA
OFF_U = Q_LORA + KV_LORA
OFF_KA = OFF_U + POOL_W
OFF_KB = OFF_KA + LANES
N_PROJ = OFF_KB + LANES

TILE_IN = 1024
CHUNK = 256
HALO = 8
REGION = CHUNK + 2 * HALO
TILE_MIX = 512
TILE_FF = 512
TOP_K = 2
SLOT_UNIT = 16
UNITS_PER_TILE = TILE_MIX * TOP_K // SLOT_UNIT + N_EXPERTS
SLOTS_PER_TILE = UNITS_PER_TILE * SLOT_UNIT
N_ROUTE_TILES = T_ALL // TILE_MIX
N_UNITS = N_ROUTE_TILES * UNITS_PER_TILE
GEMM_UNITS = 64
GEMM_ROWS = GEMM_UNITS * SLOT_UNIT
N_GEMM_TILES = -(-N_UNITS // GEMM_UNITS) + N_EXPERTS
TILE_FE = D_FF_EXPERT // 2
FF_CHUNK = 256
COND_ROWS = 16
VMEM_LIMIT = 56 * 1024 * 1024

F32 = jnp.float32
BF16 = jnp.bfloat16


def _rms(x):
    return x * lax.rsqrt(jnp.mean(x * x, axis=-1, keepdims=True) + EPS)


def _dot(a, b):
    return jnp.dot(a, b, preferred_element_type=F32)


def _dot_nt(a, b):
    return lax.dot_general(a, b, (((1,), (1,)), ((), ())), preferred_element_type=F32)


def _silu(x):
    return x * (1.0 / (1.0 + jnp.exp(-x)))


def _cparams(*sem):
    return pltpu.CompilerParams(dimension_semantics=sem, vmem_limit_bytes=VMEM_LIMIT)


def _ada_kernel(cond_ref, w_ref, b_ref, o_ref):
    c = cond_ref[...]
    o_ref[...] = jnp.dot(_silu(c), w_ref[...], preferred_element_type=F32,
                         precision=lax.Precision.HIGHEST) + b_ref[...]


def _ada_call(cond, w_ada, b_ada):
    n_blk = 4
    bw = 6 * D_MODEL // n_blk
    return pl.pallas_call(
        _ada_kernel,
        out_shape=jax.ShapeDtypeStruct((DEPTH, COND_ROWS, 6 * D_MODEL), F32),
        grid=(DEPTH, n_blk),
        in_specs=[
            pl.BlockSpec((COND_ROWS, D_MODEL), lambda l, j: (0, 0)),
            pl.BlockSpec((None, D_MODEL, bw), lambda l, j: (l, 0, j)),
            pl.BlockSpec((None, 1, bw), lambda l, j: (l, 0, j)),
        ],
        out_specs=pl.BlockSpec((None, COND_ROWS, bw), lambda l, j: (l, 0, j)),
        compiler_params=_cparams("arbitrary", "arbitrary"),
        name="ada_params",
    )(cond, w_ada, b_ada.reshape(DEPTH, 1, 6 * D_MODEL))


def _cache_kernel(ckv_ref, kpe_ref, wuk_ref, wuv_ref, dup_ref, k_ref, v_ref):
    ckv = ckv_ref[...].astype(BF16)
    knope = _dot(ckv, wuk_ref[...])
    v_ref[...] = _dot(ckv, wuv_ref[...]).astype(BF16)
    kdup = _dot(kpe_ref[...].astype(BF16), dup_ref[...]).astype(BF16)
    for h in range(N_HEADS):
        k_ref[:, h * HEAD_PAD:h * HEAD_PAD + QK_NOPE] = knope[:, h * QK_NOPE:(h + 1) * QK_NOPE].astype(BF16)
        k_ref[:, h * HEAD_PAD + QK_NOPE:(h + 1) * HEAD_PAD] = kdup


def _cache_call(cache_ckv, cache_kpe, w_uk, w_uv, dup):
    n_tok = DEC_BATCH * PAST_LEN
    return pl.pallas_call(
        _cache_kernel,
        out_shape=(jax.ShapeDtypeStruct((DEPTH, n_tok, QK_W), BF16),
                   jax.ShapeDtypeStruct((DEPTH, n_tok, V_W), BF16)),
        grid=(DEPTH, DEC_BATCH),
        in_specs=[
            pl.BlockSpec((None, None, PAST_LEN, KV_LORA), lambda l, b: (b, l, 0, 0)),
            pl.BlockSpec((None, None, PAST_LEN, QK_ROPE), lambda l, b: (b, l, 0, 0)),
            pl.BlockSpec((None, KV_LORA, N_HEADS * QK_NOPE), lambda l, b: (l, 0, 0)),
            pl.BlockSpec((None, KV_LORA, V_W), lambda l, b: (l, 0, 0)),
            pl.BlockSpec((QK_ROPE, LANES), lambda l, b: (0, 0)),
        ],
        out_specs=(pl.BlockSpec((None, PAST_LEN, QK_W), lambda l, b: (l, b, 0)),
                   pl.BlockSpec((None, PAST_LEN, V_W), lambda l, b: (l, b, 0))),
        compiler_params=_cparams("arbitrary", "arbitrary"),
        name="cache_kv",
    )(cache_ckv, cache_kpe, w_uk, w_uv, dup)


def _in_kernel(x_ref, sh_ref, sc_ref, g_ref, win_ref, qg_ref, kvg_ref, wuq_ref, wuk_ref, wuv_ref,
               tq_ref, tc_ref, ts_ref, wpool_ref, ps_ref,
               q_out, k_out, v_out, ckv_out, kpe_out, pool_out, u_scr):
    is_ctx = pl.program_id(0) < T_CTX // TILE_IN
    shift = sh_ref[...]
    scale1 = 1.0 + sc_ref[...]
    qk_scale = (QK_NOPE + QK_ROPE) ** -0.5
    n_chunks = TILE_IN // CHUNK

    for c in range(n_chunks):
        r0 = c * CHUNK
        rows = slice(r0, r0 + CHUNK)
        h = (_rms(x_ref[rows, :]) * g_ref[...] * scale1 + shift).astype(BF16)
        proj = _dot(h, win_ref[...])

        qn = (_rms(proj[:, :Q_LORA]) * qg_ref[...]).astype(BF16)
        q = _dot(qn, wuq_ref[...]) * qk_scale
        tq = tq_ref[rows, :]
        for hd in range(N_HEADS):
            lo = hd * HEAD_PAD
            q_out[rows, lo:lo + QK_NOPE] = q[:, lo:lo + QK_NOPE].astype(BF16)
            q_out[rows, lo + QK_NOPE:lo + HEAD_PAD] = (q[:, lo + QK_NOPE:lo + HEAD_PAD] * tq).astype(BF16)

        ckv = _rms(proj[:, OFF_CKV:OFF_U]) * kvg_ref[...]
        ckv_out[rows, :] = ckv
        ckv_b = ckv.astype(BF16)
        knope = _dot(ckv_b, wuk_ref[...])
        v_out[rows, :] = _dot(ckv_b, wuv_ref[...]).astype(BF16)

        k_a = proj[:, OFF_KA:OFF_KB]
        k_b = proj[:, OFF_KB:N_PROJ]
        kpe_out[rows, :] = k_a[:, :QK_ROPE]
        kr = (k_a * tc_ref[rows, :] + k_b * ts_ref[rows, :]).astype(BF16)
        for hd in range(N_HEADS):
            lo = hd * HEAD_PAD
            k_out[rows, lo:lo + QK_NOPE] = knope[:, hd * QK_NOPE:(hd + 1) * QK_NOPE].astype(BF16)
            k_out[rows, lo + QK_NOPE:lo + HEAD_PAD] = kr

        base = c * REGION + HALO
        u_scr[base:base + CHUNK, :] = proj[:, OFF_U:OFF_KA]

    zeros = jnp.zeros((HALO, POOL_W), F32)

    @pl.when(is_ctx)
    def _():
        for c in range(n_chunks):
            u_scr[c * REGION:c * REGION + HALO, :] = zeros
            u_scr[c * REGION + HALO + CHUNK:(c + 1) * REGION, :] = zeros

    @pl.when(jnp.logical_not(is_ctx))
    def _():
        for c in range(n_chunks):
            lo_dst = slice(c * REGION, c * REGION + HALO)
            hi_dst = slice(c * REGION + HALO + CHUNK, (c + 1) * REGION)
            if c == 0:
                u_scr[lo_dst, :] = zeros
            else:
                prev_end = (c - 1) * REGION + HALO + CHUNK
                u_scr[lo_dst, :] = u_scr[prev_end - HALO:prev_end, :]
            if c == n_chunks - 1:
                u_scr[hi_dst, :] = zeros
            else:
                nxt = (c + 1) * REGION + HALO
                u_scr[hi_dst, :] = u_scr[nxt:nxt + HALO, :]

    seq_len = jnp.where(is_ctx, SEQ, DEC_SEQ)
    row = lax.broadcasted_iota(jnp.int32, (CHUNK, POOL_CH), 0)
    for c in range(n_chunks):
        base = c * REGION + HALO
        rows = slice(c * CHUNK, (c + 1) * CHUNK)
        t = row + jnp.where(is_ctx, 0, c * CHUNK)
        for g, w in enumerate(POOL_WINDOWS):
            cols = slice(g * POOL_CH, (g + 1) * POOL_CH)
            acc = u_scr[base - w // 2:base - w // 2 + CHUNK, cols]
            for j in range(-w // 2 + 1, w // 2):
                acc = acc + u_scr[base + j:base + j + CHUNK, cols]
            cnt = jnp.minimum(t + w // 2, seq_len) - jnp.maximum(t - w // 2, 0)
            pooled = acc / cnt.astype(F32) - u_scr[base:base + CHUNK, cols]
            lin = _dot(pooled.astype(BF16), wpool_ref[g]) * ps_ref[:, cols]
            pool_out[rows, cols] = lin.astype(BF16)


def _in_call(x, ada, g1, w_in, qg, kvg, w_uq, w_uk, w_uv, tq, tc, ts, w_pool, pscale):
    n_tiles = T_ALL // TILE_IN
    n_ctx = T_CTX // TILE_IN

    def cond_row(i):
        return jnp.maximum(i - n_ctx + 1, 0)

    def tab(i):
        return (jnp.minimum(jnp.maximum(i - n_ctx + 1, 0), 1), 0, 0)

    const2 = lambda i: (0, 0)
    return pl.pallas_call(
        _in_kernel,
        out_shape=(jax.ShapeDtypeStruct((T_ALL, QK_W), BF16),
                   jax.ShapeDtypeStruct((T_ALL, QK_W), BF16),
                   jax.ShapeDtypeStruct((T_ALL, V_W), BF16),
                   jax.ShapeDtypeStruct((T_ALL, KV_LORA), F32),
                   jax.ShapeDtypeStruct((T_ALL, QK_ROPE), F32),
                   jax.ShapeDtypeStruct((T_ALL, POOL_W), BF16)),
        grid=(n_tiles,),
        in_specs=[
            pl.BlockSpec((TILE_IN, D_MODEL), lambda i: (i, 0)),
            pl.BlockSpec((None, 1, D_MODEL), lambda i: (cond_row(i), 0, 0)),
            pl.BlockSpec((None, 1, D_MODEL), lambda i: (cond_row(i), 0, 1)),
            pl.BlockSpec((1, D_MODEL), const2),
            pl.BlockSpec((D_MODEL, N_PROJ), const2),
            pl.BlockSpec((1, Q_LORA), const2),
            pl.BlockSpec((1, KV_LORA), const2),
            pl.BlockSpec((Q_LORA, QK_W), const2),
            pl.BlockSpec((KV_LORA, N_HEADS * QK_NOPE), const2),
            pl.BlockSpec((KV_LORA, V_W), const2),
            pl.BlockSpec((None, TILE_IN, LANES), tab),
            pl.BlockSpec((None, TILE_IN, LANES), tab),
            pl.BlockSpec((None, TILE_IN, LANES), tab),
            pl.BlockSpec((POOL_GROUPS, POOL_CH, POOL_CH), lambda i: (0, 0, 0)),
            pl.BlockSpec((1, POOL_W), const2),
        ],
        out_specs=(pl.BlockSpec((TILE_IN, QK_W), lambda i: (i, 0)),
                   pl.BlockSpec((TILE_IN, QK_W), lambda i: (i, 0)),
                   pl.BlockSpec((TILE_IN, V_W), lambda i: (i, 0)),
                   pl.BlockSpec((TILE_IN, KV_LORA), lambda i: (i, 0)),
                   pl.BlockSpec((TILE_IN, QK_ROPE), lambda i: (i, 0)),
                   pl.BlockSpec((TILE_IN, POOL_W), lambda i: (i, 0))),
        scratch_shapes=[pltpu.VMEM((TILE_IN // CHUNK * REGION, POOL_W), F32)],
        compiler_params=_cparams("arbitrary"),
        name="in_proj",
    )(x, ada, ada, g1, w_in, qg, kvg, w_uq, w_uk, w_uv, tq, tc, ts, w_pool, pscale)


def _softmax_pv(scores, values):
    m = scores[0].max(axis=-1, keepdims=True)
    for s in scores[1:]:
        m = jnp.maximum(m, s.max(axis=-1, keepdims=True))
    den = None
    out = None
    for s, v in zip(scores, values):
        p = jnp.exp(s - m)
        d = p.sum(axis=-1, keepdims=True)
        o = _dot(p.astype(BF16), v)
        den = d if den is None else den + d
        out = o if out is None else out + o
    return out / den


def _attn_ctx_kernel(q_ref, k_ref, v_ref, o_ref):
    for hd in range(N_HEADS):
        qk = slice(hd * HEAD_PAD, (hd + 1) * HEAD_PAD)
        vv = slice(hd * V_DIM, (hd + 1) * V_DIM)
        s = _dot_nt(q_ref[:, qk], k_ref[:, qk])
        o_ref[:, vv] = _softmax_pv([s], [v_ref[:, vv]]).astype(BF16)


def _attn_lat_kernel(q_ref, k_ref, v_ref, kc_ref, vc_ref, o_ref):
    for hd in range(N_HEADS):
        qk = slice(hd * HEAD_PAD, (hd + 1) * HEAD_PAD)
        vv = slice(hd * V_DIM, (hd + 1) * V_DIM)
        q = q_ref[:, qk]
        s1 = _dot_nt(q, k_ref[:, qk])
        s2 = _dot_nt(q, kc_ref[:, qk])
        o_ref[:, vv] = _softmax_pv([s1, s2], [v_ref[:, vv], vc_ref[:, vv]]).astype(BF16)


def _attn_call(q, k, v, kc, vc):
    attn_ctx = pl.pallas_call(
        _attn_ctx_kernel,
        out_shape=jax.ShapeDtypeStruct((T_CTX, V_W), BF16),
        grid=(BATCH,),
        in_specs=[pl.BlockSpec((SEQ, QK_W), lambda b: (b, 0)),
                  pl.BlockSpec((SEQ, QK_W), lambda b: (b, 0)),
                  pl.BlockSpec((SEQ, V_W), lambda b: (b, 0))],
        out_specs=pl.BlockSpec((SEQ, V_W), lambda b: (b, 0)),
        compiler_params=_cparams("arbitrary"),
        name="attn_ctx",
    )(q, k, v)

    tq = 256
    n_q = DEC_SEQ // tq
    lat0 = T_CTX // DEC_SEQ
    attn_lat = pl.pallas_call(
        _attn_lat_kernel,
        out_shape=jax.ShapeDtypeStruct((T_LAT, V_W), BF16),
        grid=(DEC_BATCH, n_q),
        in_specs=[pl.BlockSpec((tq, QK_W), lambda b, i: (T_CTX // tq + b * n_q + i, 0)),
                  pl.BlockSpec((DEC_SEQ, QK_W), lambda b, i: (lat0 + b, 0)),
                  pl.BlockSpec((DEC_SEQ, V_W), lambda b, i: (lat0 + b, 0)),
                  pl.BlockSpec((PAST_LEN, QK_W), lambda b, i: (b, 0)),
                  pl.BlockSpec((PAST_LEN, V_W), lambda b, i: (b, 0))],
        out_specs=pl.BlockSpec((tq, V_W), lambda b, i: (b * n_q + i, 0)),
        compiler_params=_cparams("arbitrary", "arbitrary"),
        name="attn_lat",
    )(q, k, v, kc, vc)
    return attn_ctx, attn_lat


def _mix_kernel(with_router, attn_c_ref, attn_l_ref, pool_ref, x_ref, g1_ref, sh2_ref, sc2_ref, n2_ref, wo_ref,
                *rest):
    if with_router:
        wr_ref, ltri_ref, utri_ref, x_out, xs_out, gs_out, route_out, meta_out = rest
    else:
        x_out, h_out = rest
    is_ctx = pl.program_id(0) < T_CTX // TILE_MIX
    attn = jnp.where(is_ctx, attn_c_ref[...], attn_l_ref[...])
    y = _dot(attn, wo_ref[:V_W, :]) + _dot(pool_ref[...], wo_ref[V_W:, :])
    x_new = x_ref[...] + g1_ref[...] * y
    x_out[...] = x_new
    h = (_rms(x_new) * n2_ref[...] * (1.0 + sc2_ref[...]) + sh2_ref[...]).astype(BF16)
    if not with_router:
        h_out[...] = h
        return

    logits = _dot(h, wr_ref[...])
    lane = lax.broadcasted_iota(jnp.int32, logits.shape, 1)
    neg = float(jnp.finfo(F32).min)
    lg = jnp.where(lane < N_EXPERTS, logits, neg)
    m1 = lg.max(axis=-1, keepdims=True)
    i1 = jnp.where(lg == m1, lane, LANES).min(axis=-1, keepdims=True)
    lg2 = jnp.where(lane == i1, neg, lg)
    m2 = lg2.max(axis=-1, keepdims=True)
    i2 = jnp.where(lg2 == m2, lane, LANES).min(axis=-1, keepdims=True)
    e = jnp.exp(m2 - m1)
    w1 = 1.0 / (1.0 + e)
    w2 = e / (1.0 + e)

    sel1 = lane == i1
    sel2 = lane == i2
    member = jnp.where(jnp.logical_or(sel1, sel2), 1.0, 0.0)
    rank = _dot(ltri_ref[...], member.astype(BF16))
    n_tok = member.sum(axis=0, keepdims=True)
    units = jnp.floor((n_tok + (SLOT_UNIT - 1)) * (1.0 / SLOT_UNIT))
    unit_off = _dot(jnp.broadcast_to(units, (8, LANES)).astype(BF16), utri_ref[...])[0:1, :]
    slot_of = SLOT_UNIT * unit_off + rank
    slot1 = jnp.where(sel1, slot_of, 0.0).sum(axis=-1, keepdims=True)
    slot2 = jnp.where(sel2, slot_of, 0.0).sum(axis=-1, keepdims=True)
    route = jnp.where(lane == 0, slot1, jnp.where(lane == 1, slot2,
                      jnp.where(lane == 2, w1, jnp.where(lane == 3, w2, 0.0))))
    route_out[...] = route
    sub = lax.broadcasted_iota(jnp.int32, (8, LANES), 0)
    meta_out[...] = jnp.where(sub == 0, units, jnp.where(sub == 1, unit_off, 0.0)).astype(jnp.int32)

    rt = route.T
    s1 = rt[0:1, :].astype(jnp.int32)
    s2 = rt[1:2, :].astype(jnp.int32)
    srow = lax.broadcasted_iota(jnp.int32, (SLOTS_PER_TILE, TILE_MIX), 0)
    hit1 = srow == s1
    hit2 = srow == s2
    perm = jnp.where(jnp.logical_or(hit1, hit2), 1.0, 0.0).astype(BF16)
    xs_out[...] = _dot(perm, h).astype(BF16)
    gate = (jnp.where(hit1, rt[2:3, :], 0.0) + jnp.where(hit2, rt[3:4, :], 0.0)).sum(axis=-1, keepdims=True)
    gs_out[...] = jnp.broadcast_to(gate, (SLOTS_PER_TILE, LANES))


def _mod_row(i, tile):
    n_ctx = T_CTX // tile
    per_seq = DEC_SEQ // tile
    return jnp.where(i < n_ctx, 0, 1 + (i - n_ctx) // per_seq)


def _mix_call(attn_ctx, attn_lat, pooled, x, ada, n2, w_out, w_router):
    with_router = w_router is not None
    tm = TILE_MIX
    n_ctx = T_CTX // tm
    row = lambda i: (i, 0)
    const2 = lambda i: (0, 0)
    mod = lambda k: pl.BlockSpec((None, 1, D_MODEL), lambda i: (_mod_row(i, tm), 0, k))
    in_specs = [pl.BlockSpec((tm, V_W), lambda i: (jnp.minimum(i, n_ctx - 1), 0)),
                pl.BlockSpec((tm, V_W), lambda i: (jnp.maximum(i - n_ctx, 0), 0)),
                pl.BlockSpec((tm, POOL_W), row), pl.BlockSpec((tm, D_MODEL), row),
                mod(2), mod(3), mod(4),
                pl.BlockSpec((1, D_MODEL), const2), pl.BlockSpec((D_MODEL, D_MODEL), const2)]
    args = [attn_ctx, attn_lat, pooled, x, ada, ada, ada, n2, w_out]
    if with_router:
        t_i = jnp.arange(tm)
        ltri = (t_i[None, :] < t_i[:, None]).astype(BF16)
        l_i = jnp.arange(LANES)
        utri = (l_i[:, None] < l_i[None, :]).astype(BF16)
        in_specs += [pl.BlockSpec((D_MODEL, LANES), const2), pl.BlockSpec((tm, tm), const2),
                     pl.BlockSpec((LANES, LANES), const2)]
        args += [w_router, ltri, utri]
        out_shape = [jax.ShapeDtypeStruct((T_ALL, D_MODEL), F32),
                     jax.ShapeDtypeStruct((N_ROUTE_TILES * SLOTS_PER_TILE, D_MODEL), BF16),
                     jax.ShapeDtypeStruct((N_ROUTE_TILES * SLOTS_PER_TILE, LANES), F32),
                     jax.ShapeDtypeStruct((T_ALL, LANES), F32),
                     jax.ShapeDtypeStruct((N_ROUTE_TILES, 8, LANES), jnp.int32)]
        out_specs = [pl.BlockSpec((tm, D_MODEL), row), pl.BlockSpec((SLOTS_PER_TILE, D_MODEL), row),
                     pl.BlockSpec((SLOTS_PER_TILE, LANES), row), pl.BlockSpec((tm, LANES), row),
                     pl.BlockSpec((None, 8, LANES), lambda i: (i, 0, 0))]
    else:
        out_shape = [jax.ShapeDtypeStruct((T_ALL, D_MODEL), F32), jax.ShapeDtypeStruct((T_ALL, D_MODEL), BF16)]
        out_specs = [pl.BlockSpec((tm, D_MODEL), row), pl.BlockSpec((tm, D_MODEL), row)]
    return pl.pallas_call(
        functools.partial(_mix_kernel, with_router),
        out_shape=tuple(out_shape),
        grid=(T_ALL // tm,),
        in_specs=in_specs,
        out_specs=tuple(out_specs),
        compiler_params=_cparams("arbitrary"),
        name="mix_router" if with_router else "mix",
    )(*args)


def _swiglu_hidden(h, wg_ref, wu_ref, a_buf):
    width = a_buf.shape[1]
    for c0 in range(0, width, FF_CHUNK):
        cols = slice(c0, min(c0 + FF_CHUNK, width))
        a_buf[:, cols] = (_silu(_dot(h, wg_ref[:, cols])) * _dot(h, wu_ref[:, cols])).astype(BF16)


def _ffn_kernel(h_ref, x_ref, g2_ref, wg_ref, wu_ref, wd_ref, o_ref, a_buf):
    _swiglu_hidden(h_ref[...], wg_ref, wu_ref, a_buf)
    o_ref[...] = x_ref[...] + g2_ref[...] * _dot(a_buf[...], wd_ref[...])


def _ffn_call(h, x, ada, wg, wu, wd):
    tm = TILE_FF
    resident = lambda shape: pl.BlockSpec(shape, lambda i: (0, 0), pipeline_mode=pl.Buffered(1))
    return pl.pallas_call(
        _ffn_kernel,
        out_shape=jax.ShapeDtypeStruct((T_ALL, D_MODEL), F32),
        grid=(T_ALL // tm,),
        in_specs=[pl.BlockSpec((tm, D_MODEL), lambda i: (i, 0)),
                  pl.BlockSpec((tm, D_MODEL), lambda i: (i, 0)),
                  pl.BlockSpec((None, 1, D_MODEL), lambda i: (_mod_row(i, tm), 0, 5)),
                  resident((D_MODEL, D_FF)), resident((D_MODEL, D_FF)), resident((D_FF, D_MODEL))],
        out_specs=pl.BlockSpec((tm, D_MODEL), lambda i: (i, 0)),
        scratch_shapes=[pltpu.VMEM((tm, D_FF), BF16)],
        compiler_params=_cparams("arbitrary"),
        name="ffn_dense",
    )(h, x, ada, wg, wu, wd)


def _route_tables(meta):
    units = meta[:, 0, :N_EXPERTS]
    offs = meta[:, 1, :N_EXPERTS]
    cum = jnp.cumsum(units, axis=0)
    total = cum[-1]
    tiles_e = (total + GEMM_UNITS - 1) // GEMM_UNITS
    tile_end = jnp.cumsum(tiles_e)
    n_act = tile_end[-1]
    m = jnp.arange(N_GEMM_TILES)
    m_eff = jnp.minimum(m, jnp.maximum(n_act - 1, 0))
    expert = jnp.minimum(jnp.sum(tile_end[None, :] <= m_eff[:, None], axis=1), N_EXPERTS - 1)
    first_q = (m_eff - (tile_end - tiles_e)[expert]) * GEMM_UNITS
    count = jnp.where(m < n_act, jnp.clip(total[expert] - first_q, 0, GEMM_UNITS), 0)
    q = first_q[:, None] + jnp.arange(GEMM_UNITS)[None, :]
    cum_e = cum.T[expert]
    src_tile = jnp.minimum(jnp.sum(cum_e[:, None, :] <= q[:, :, None], axis=2), N_ROUTE_TILES - 1)
    before = jnp.take_along_axis(cum_e, src_tile, axis=1) - units.T[expert[:, None], src_tile]
    uid = src_tile * UNITS_PER_TILE + offs.T[expert[:, None], src_tile] + (q - before)
    uid = jnp.clip(uid, 0, N_UNITS - 1)
    i32 = jnp.int32
    return expert.astype(i32), count.astype(i32), uid.reshape(-1).astype(i32), n_act.reshape(1).astype(i32)


def _gemm_kernel(em_ref, cnt_ref, ul_ref, nact_ref, xs_hbm, gs_hbm, wg_ref, wu_ref, wd_ref, ys_in, ys_hbm,
                 xbuf, gbuf, obuf, acc_ref, a_buf, sem_in, sem_out):
    del em_ref, ys_in
    m = pl.program_id(0)
    j = pl.program_id(1)
    last_j = pl.num_programs(1) - 1
    n_act = nact_ref[0]
    active = m < n_act
    slot = m % 2

    def rows(r):
        return pl.ds(pl.multiple_of(r * SLOT_UNIT, SLOT_UNIT), SLOT_UNIT)

    def in_copies(mm, sl, r):
        uid = ul_ref[mm * GEMM_UNITS + r]
        return (pltpu.make_async_copy(xs_hbm.at[uid], xbuf.at[sl, rows(r)], sem_in.at[sl]),
                pltpu.make_async_copy(gs_hbm.at[uid], gbuf.at[sl, rows(r)], sem_in.at[sl]))

    def out_copy(mm, r):
        uid = ul_ref[mm * GEMM_UNITS + r]
        return pltpu.make_async_copy(obuf.at[rows(r)], ys_hbm.at[uid], sem_out.at[0])

    def start_in(mm, sl):
        n = cnt_ref[mm]

        def issue(r, carry):
            for cp in in_copies(mm, sl, r):
                cp.start()
            return carry

        def clear(r, carry):
            xbuf[sl, rows(r), :] = jnp.zeros((SLOT_UNIT, D_MODEL), BF16)
            gbuf[sl, rows(r), :] = jnp.zeros((SLOT_UNIT, LANES), F32)
            return carry

        lax.fori_loop(0, n, issue, 0)
        lax.fori_loop(n, GEMM_UNITS, clear, 0)

    def wait_in(mm, sl):
        def body(r, carry):
            for cp in in_copies(mm, sl, r):
                cp.wait()
            return carry

        lax.fori_loop(0, cnt_ref[mm], body, 0)

    def start_out(mm):
        def body(r, carry):
            out_copy(mm, r).start()
            return carry

        lax.fori_loop(0, cnt_ref[mm], body, 0)

    def wait_out(mm):
        def body(r, carry):
            out_copy(mm, r).wait()
            return carry

        lax.fori_loop(0, cnt_ref[mm], body, 0)

    @pl.when(jnp.logical_and(active, j == 0))
    def _():
        @pl.when(m == 0)
        def _():
            start_in(0, 0)

        wait_in(m, slot)

        @pl.when(m + 1 < n_act)
        def _():
            start_in(m + 1, 1 - slot)

    @pl.when(active)
    def _():
        _swiglu_hidden(xbuf[slot], wg_ref, wu_ref, a_buf)
        part = _dot(a_buf[...], wd_ref[...])

        @pl.when(j == 0)
        def _():
            acc_ref[...] = part

        @pl.when(j > 0)
        def _():
            acc_ref[...] += part

    @pl.when(jnp.logical_and(active, j == last_j))
    def _():
        @pl.when(m > 0)
        def _():
            wait_out(m - 1)

        gate = gbuf[slot]
        for cb in range(D_MODEL // LANES):
            cols = slice(cb * LANES, (cb + 1) * LANES)
            obuf[:, cols] = (acc_ref[:, cols] * gate).astype(BF16)
        start_out(m)

        @pl.when(m == n_act - 1)
        def _():
            wait_out(m)


def _gemm_call(expert, count, uids, n_act, xs, gs, wg, wu, wd):
    tf = TILE_FE
    n_j = D_FF_EXPERT // tf

    def w_col(m, j, em, cnt, ul, nact):
        return (em[m], 0, jnp.where(m < nact[0], j, n_j - 1))

    def w_row(m, j, em, cnt, ul, nact):
        return (em[m], jnp.where(m < nact[0], j, n_j - 1), 0)

    ys0 = jnp.zeros((N_UNITS, SLOT_UNIT, D_MODEL), BF16)
    grid_spec = pltpu.PrefetchScalarGridSpec(
        num_scalar_prefetch=4,
        grid=(N_GEMM_TILES, n_j),
        in_specs=[pl.BlockSpec(memory_space=pl.ANY),
                  pl.BlockSpec(memory_space=pl.ANY),
                  pl.BlockSpec((None, D_MODEL, tf), w_col),
                  pl.BlockSpec((None, D_MODEL, tf), w_col),
                  pl.BlockSpec((None, tf, D_MODEL), w_row),
                  pl.BlockSpec(memory_space=pl.ANY)],
        out_specs=pl.BlockSpec(memory_space=pl.ANY),
        scratch_shapes=[pltpu.VMEM((2, GEMM_ROWS, D_MODEL), BF16),
                        pltpu.VMEM((2, GEMM_ROWS, LANES), F32),
                        pltpu.VMEM((GEMM_ROWS, D_MODEL), BF16),
                        pltpu.VMEM((GEMM_ROWS, D_MODEL), F32),
                        pltpu.VMEM((GEMM_ROWS, tf), BF16),
                        pltpu.SemaphoreType.DMA((2,)),
                        pltpu.SemaphoreType.DMA((1,))])
    return pl.pallas_call(
        _gemm_kernel,
        out_shape=jax.ShapeDtypeStruct((N_UNITS, SLOT_UNIT, D_MODEL), BF16),
        grid_spec=grid_spec,
        input_output_aliases={9: 0},
        compiler_params=_cparams("arbitrary", "arbitrary"),
        name="moe_experts",
    )(expert, count, uids, n_act,
      xs.reshape(N_UNITS, SLOT_UNIT, D_MODEL), gs.reshape(N_UNITS, SLOT_UNIT, LANES), wg, wu, wd, ys0)


def _combine_kernel(ys_ref, route_ref, x_ref, g2_ref, fg_ref, o_ref):
    route = route_ref[...]
    s1 = route[:, 0:1].astype(jnp.int32)
    s2 = route[:, 1:2].astype(jnp.int32)
    scol = lax.broadcasted_iota(jnp.int32, (TILE_MIX, SLOTS_PER_TILE), 1)
    unperm = jnp.where(jnp.logical_or(scol == s1, scol == s2), 1.0, 0.0).astype(BF16)
    y = _dot(unperm, ys_ref[...])
    o_ref[...] = _rms(x_ref[...] + g2_ref[...] * y) * fg_ref[...]


def _combine_call(ys, route, x, ada, final_g):
    tm = TILE_MIX
    return pl.pallas_call(
        _combine_kernel,
        out_shape=jax.ShapeDtypeStruct((T_ALL, D_MODEL), F32),
        grid=(T_ALL // tm,),
        in_specs=[pl.BlockSpec((SLOTS_PER_TILE, D_MODEL), lambda i: (i, 0)),
                  pl.BlockSpec((tm, LANES), lambda i: (i, 0)),
                  pl.BlockSpec((tm, D_MODEL), lambda i: (i, 0)),
                  pl.BlockSpec((None, 1, D_MODEL), lambda i: (_mod_row(i, tm), 0, 5)),
                  pl.BlockSpec((1, D_MODEL), lambda i: (0, 0))],
        out_specs=pl.BlockSpec((tm, D_MODEL), lambda i: (i, 0)),
        compiler_params=_cparams("arbitrary"),
        name="moe_combine",
    )(ys.reshape(N_UNITS * SLOT_UNIT, D_MODEL), route, x, ada, final_g)


def _rot_cols(w):
    q = QK_ROPE // 4
    return jnp.concatenate([-w[:, q:2 * q], w[:, :q], -w[:, 3 * q:], w[:, 2 * q:3 * q]], axis=1)


def _rope_tables():
    t = jnp.arange(DEC_SEQ)
    rows = (t // GRID_W).astype(F32)
    cols = (t % GRID_W).astype(F32)
    half = QK_ROPE // 2
    freqs = ROPE_BASE ** (-jnp.arange(0, half, 2, dtype=F32) / half)
    ang_r = rows[:, None] * freqs
    ang_c = cols[:, None] * freqs
    ang = jnp.concatenate([ang_r, ang_r, ang_c, ang_c], axis=-1)
    cos, sin = jnp.cos(ang), jnp.sin(ang)
    one, zero = jnp.ones_like(cos), jnp.zeros_like(cos)
    tq = jnp.stack([jnp.concatenate([one, zero], -1), jnp.concatenate([cos, sin], -1)])
    tc = jnp.stack([jnp.concatenate([one, one], -1), jnp.concatenate([cos, cos], -1)])
    ts = jnp.stack([jnp.concatenate([zero, zero], -1), jnp.concatenate([sin, sin], -1)])
    return tq, tc, ts


def _layer_weights(l, w_in, w_uq, w_ukv):
    wi = w_in[l]
    w_kpe = wi[:, OFF_U:OFF_U + QK_ROPE]
    w_rot = _rot_cols(w_kpe)
    w_in_r = jnp.concatenate([wi[:, :OFF_U], wi[:, OFF_U + QK_ROPE:], w_kpe, w_kpe, w_rot, w_rot], axis=1)
    heads = []
    for h in range(N_HEADS):
        blk = w_uq[l][:, h * (QK_NOPE + QK_ROPE):(h + 1) * (QK_NOPE + QK_ROPE)]
        heads += [blk, _rot_cols(blk[:, QK_NOPE:])]
    w_uq_r = jnp.concatenate(heads, axis=1)
    kv = w_ukv[l].reshape(KV_LORA, N_HEADS, QK_NOPE + V_DIM)
    w_uk = kv[:, :, :QK_NOPE].reshape(KV_LORA, N_HEADS * QK_NOPE)
    w_uv = kv[:, :, QK_NOPE:].reshape(KV_LORA, V_W)
    return w_in_r.astype(BF16), w_uq_r.astype(BF16), w_uk.astype(BF16), w_uv.astype(BF16)


def kernel(x_prompt, x_sample, c, cache_ckv, cache_kpe, c_ctx, norm1_g, norm2_g, w_ada, b_ada, w_in, q_norm_g,
           kv_norm_g, w_uq, w_ukv, w_pool, pool_scale, w_out, ffn_w_gate, ffn_w_up, ffn_w_down, moe_w_router,
           moe_w_gate, moe_w_up, moe_w_down, final_norm_g):
    x = jnp.concatenate([x_prompt.reshape(T_CTX, D_MODEL), x_sample.reshape(T_LAT, D_MODEL)], axis=0)
    cond = jnp.concatenate([c_ctx[None, :], c, jnp.zeros((COND_ROWS - 1 - DEC_BATCH, D_MODEL), F32)], axis=0)
    ada_all = _ada_call(cond, w_ada, b_ada)
    tq, tc, ts = _rope_tables()

    lw = [_layer_weights(l, w_in, w_uq, w_ukv) for l in range(DEPTH)]
    dup = jnp.concatenate([jnp.eye(QK_ROPE, dtype=BF16)] * 2, axis=1)
    kc_all, vc_all = _cache_call(cache_ckv, cache_kpe, jnp.stack([w[2] for w in lw]),
                                 jnp.stack([w[3] for w in lw]), dup)

    new_ckv, new_kpe = [], []
    for l in range(DEPTH):
        w_in_r, w_uq_r, w_uk, w_uv = lw[l]
        ada = ada_all[l].reshape(COND_ROWS, 1, 6 * D_MODEL)
        q, k, v, ckv, kpe, pooled = _in_call(
            x, ada, norm1_g[l][None, :], w_in_r, q_norm_g[l][None, :], kv_norm_g[l][None, :], w_uq_r, w_uk, w_uv,
            tq, tc, ts, w_pool[l].astype(BF16), pool_scale[l][None, :])
        new_ckv.append(ckv[:T_CTX].reshape(BATCH, SEQ, KV_LORA))
        new_kpe.append(kpe[:T_CTX].reshape(BATCH, SEQ, QK_ROPE))
        attn_c, attn_l = _attn_call(q, k, v, kc_all[l], vc_all[l])
        j = l // 2
        if l % 2 == 0:
            x, h2 = _mix_call(attn_c, attn_l, pooled, x, ada, norm2_g[l][None, :], w_out[l].astype(BF16), None)
            x = _ffn_call(h2, x, ada, ffn_w_gate[j].astype(BF16), ffn_w_up[j].astype(BF16),
                          ffn_w_down[j].astype(BF16))
        else:
            w_r = jnp.pad(moe_w_router[j], ((0, 0), (0, LANES - N_EXPERTS))).astype(BF16)
            x, xs, gs, route, meta = _mix_call(attn_c, attn_l, pooled, x, ada, norm2_g[l][None, :],
                                               w_out[l].astype(BF16), w_r)
            expert, count, uids, n_act = _route_tables(meta)
            ys = _gemm_call(expert, count, uids, n_act, xs, gs, moe_w_gate[j].astype(BF16),
                            moe_w_up[j].astype(BF16), moe_w_down[j].astype(BF16))
            x = _combine_call(ys, route, x, ada, final_norm_g[None, :])

    y_prompt = x[:T_CTX].reshape(BATCH, SEQ, D_MODEL)
    y_sample = x[T_CTX:].reshape(DEC_BATCH, DEC_SEQ, D_MODEL)
    return y_prompt, y_sample, jnp.stack(new_ckv, axis=1), jnp.stack(new_kpe, axis=1)
```

```python
import functools

import jax
import jax.numpy as jnp
from jax import lax
from jax.experimental import pallas as pl
from jax.experimental.pallas import tpu as pltpu

D_MODEL = 1024
BATCH = 32
SEQ = 256
DEPTH = 2
DEC_BATCH = 8
DEC_SEQ = 1024
PAST_LEN = 512
GRID_W = 64
N_HEADS = 4
QK_NOPE = 128
QK_ROPE = 64
V_DIM = 128
Q_LORA = 384
KV_LORA = 256
POOL_W = 512
POOL_GROUPS = 4
POOL_WINDOWS = (2, 4, 8, 16)
POOL_CH = POOL_W // POOL_GROUPS
D_FF = 2816
N_EXPERTS = 8
D_FF_EXPERT = 3584
ROPE_BASE = 10000.0
EPS = 1e-6

T_CTX = BATCH * SEQ
T_LAT = DEC_BATCH * DEC_SEQ
T_ALL = T_CTX + T_LAT

LANES = 128
HEAD_PAD = 256
QK_W = N_HEADS * HEAD_PAD
V_W = N_HEADS * V_DIM
OFF_CKV = Q_LORA
OFF_U = Q_LORA + KV_LORA
OFF_KA = OFF_U + POOL_W
OFF_KB = OFF_KA + LANES
N_PROJ = OFF_KB + LANES

TILE_IN = 1024
CHUNK = 256
HALO = 8
REGION = CHUNK + 2 * HALO
TILE_MIX = 512
TILE_FF = 512
TOP_K = 2
SLOT_UNIT = 16
UNITS_PER_TILE = TILE_MIX * TOP_K // SLOT_UNIT + N_EXPERTS
SLOTS_PER_TILE = UNITS_PER_TILE * SLOT_UNIT
N_ROUTE_TILES = T_ALL // TILE_MIX
N_UNITS = N_ROUTE_TILES * UNITS_PER_TILE
GEMM_UNITS = 64
GEMM_ROWS = GEMM_UNITS * SLOT_UNIT
GEMM_QUARTERS = 4
N_GEMM_TILES = -(-N_UNITS // GEMM_UNITS) + N_EXPERTS
TILE_FE = D_FF_EXPERT // 2
FF_CHUNK = 256
COND_ROWS = 16
VMEM_LIMIT = 56 * 1024 * 1024

F32 = jnp.float32
BF16 = jnp.bfloat16


def _rms(x):
    return x * lax.rsqrt(jnp.mean(x * x, axis=-1, keepdims=True) + EPS)


def _dot(a, b):
    return jnp.dot(a, b, preferred_element_type=F32)


def _dot_nt(a, b):
    return lax.dot_general(a, b, (((1,), (1,)), ((), ())), preferred_element_type=F32)


def _silu(x):
    return x * (1.0 / (1.0 + jnp.exp(-x)))


def _cparams(*sem):
    return pltpu.CompilerParams(dimension_semantics=sem, vmem_limit_bytes=VMEM_LIMIT)


def _ada_kernel(cond_ref, w_ref, b_ref, o_ref):
    c = cond_ref[...]
    o_ref[...] = jnp.dot(_silu(c), w_ref[...], preferred_element_type=F32,
                         precision=lax.Precision.HIGHEST) + b_ref[...]


def _ada_call(cond, w_ada, b_ada):
    n_blk = 4
    bw = 6 * D_MODEL // n_blk
    return pl.pallas_call(
        _ada_kernel,
        out_shape=jax.ShapeDtypeStruct((DEPTH, COND_ROWS, 6 * D_MODEL), F32),
        grid=(DEPTH, n_blk),
        in_specs=[
            pl.BlockSpec((COND_ROWS, D_MODEL), lambda l, j: (0, 0)),
            pl.BlockSpec((None, D_MODEL, bw), lambda l, j: (l, 0, j)),
            pl.BlockSpec((None, 1, bw), lambda l, j: (l, 0, j)),
        ],
        out_specs=pl.BlockSpec((None, COND_ROWS, bw), lambda l, j: (l, 0, j)),
        compiler_params=_cparams("arbitrary", "arbitrary"),
        name="ada_params",
    )(cond, w_ada, b_ada.reshape(DEPTH, 1, 6 * D_MODEL))


def _cache_kernel(ckv_ref, kpe_ref, wuk_ref, wuv_ref, dup_ref, k_ref, v_ref):
    ckv = ckv_ref[...].astype(BF16)
    knope = _dot(ckv, wuk_ref[...])
    v_ref[...] = _dot(ckv, wuv_ref[...]).astype(BF16)
    kdup = _dot(kpe_ref[...].astype(BF16), dup_ref[...]).astype(BF16)
    for h in range(N_HEADS):
        k_ref[:, h * HEAD_PAD:h * HEAD_PAD + QK_NOPE] = knope[:, h * QK_NOPE:(h + 1) * QK_NOPE].astype(BF16)
        k_ref[:, h * HEAD_PAD + QK_NOPE:(h + 1) * HEAD_PAD] = kdup


def _cache_call(cache_ckv, cache_kpe, w_uk, w_uv, dup):
    n_tok = DEC_BATCH * PAST_LEN
    return pl.pallas_call(
        _cache_kernel,
        out_shape=(jax.ShapeDtypeStruct((DEPTH, n_tok, QK_W), BF16),
                   jax.ShapeDtypeStruct((DEPTH, n_tok, V_W), BF16)),
        grid=(DEPTH, DEC_BATCH),
        in_specs=[
            pl.BlockSpec((None, None, PAST_LEN, KV_LORA), lambda l, b: (b, l, 0, 0)),
            pl.BlockSpec((None, None, PAST_LEN, QK_ROPE), lambda l, b: (b, l, 0, 0)),
            pl.BlockSpec((None, KV_LORA, N_HEADS * QK_NOPE), lambda l, b: (l, 0, 0)),
            pl.BlockSpec((None, KV_LORA, V_W), lambda l, b: (l, 0, 0)),
            pl.BlockSpec((QK_ROPE, LANES), lambda l, b: (0, 0)),
        ],
        out_specs=(pl.BlockSpec((None, PAST_LEN, QK_W), lambda l, b: (l, b, 0)),
                   pl.BlockSpec((None, PAST_LEN, V_W), lambda l, b: (l, b, 0))),
        compiler_params=_cparams("arbitrary", "arbitrary"),
        name="cache_kv",
    )(cache_ckv, cache_kpe, w_uk, w_uv, dup)


def _in_kernel(xc_ref, xl_ref, sh_ref, sc_ref, g_ref, win_ref, qg_ref, kvg_ref, wuq_ref, wuk_ref, wuv_ref,
               tq_ref, tc_ref, ts_ref, wpool_ref, ps_ref,
               q_out, k_out, v_out, ckv_out, kpe_out, pool_out, u_scr):
    is_ctx = pl.program_id(0) < T_CTX // TILE_IN
    shift = sh_ref[...]
    scale1 = 1.0 + sc_ref[...]
    qk_scale = (QK_NOPE + QK_ROPE) ** -0.5
    n_chunks = TILE_IN // CHUNK

    for c in range(n_chunks):
        r0 = c * CHUNK
        rows = slice(r0, r0 + CHUNK)
        x = jnp.where(is_ctx, xc_ref[rows, :], xl_ref[rows, :])
        h = (_rms(x) * g_ref[...] * scale1 + shift).astype(BF16)
        proj = _dot(h, win_ref[...])

        qn = (_rms(proj[:, :Q_LORA]) * qg_ref[...]).astype(BF16)
        q = _dot(qn, wuq_ref[...]) * qk_scale
        tq = tq_ref[rows, :]
        for hd in range(N_HEADS):
            lo = hd * HEAD_PAD
            q_out[rows, lo:lo + QK_NOPE] = q[:, lo:lo + QK_NOPE].astype(BF16)
            q_out[rows, lo + QK_NOPE:lo + HEAD_PAD] = (q[:, lo + QK_NOPE:lo + HEAD_PAD] * tq).astype(BF16)

        ckv = _rms(proj[:, OFF_CKV:OFF_U]) * kvg_ref[...]
        ckv_out[rows, :] = ckv
        ckv_b = ckv.astype(BF16)
        knope = _dot(ckv_b, wuk_ref[...])
        v_out[rows, :] = _dot(ckv_b, wuv_ref[...]).astype(BF16)

        k_a = proj[:, OFF_KA:OFF_KB]
        k_b = proj[:, OFF_KB:N_PROJ]
        kpe_out[rows, :] = k_a[:, :QK_ROPE]
        kr = (k_a * tc_ref[rows, :] + k_b * ts_ref[rows, :]).astype(BF16)
        for hd in range(N_HEADS):
            lo = hd * HEAD_PAD
            k_out[rows, lo:lo + QK_NOPE] = knope[:, hd * QK_NOPE:(hd + 1) * QK_NOPE].astype(BF16)
            k_out[rows, lo + QK_NOPE:lo + HEAD_PAD] = kr

        base = c * REGION + HALO
        u_scr[base:base + CHUNK, :] = proj[:, OFF_U:OFF_KA]

    zeros = jnp.zeros((HALO, POOL_W), F32)

    @pl.when(is_ctx)
    def _():
        for c in range(n_chunks):
            u_scr[c * REGION:c * REGION + HALO, :] = zeros
            u_scr[c * REGION + HALO + CHUNK:(c + 1) * REGION, :] = zeros

    @pl.when(jnp.logical_not(is_ctx))
    def _():
        for c in range(n_chunks):
            lo_dst = slice(c * REGION, c * REGION + HALO)
            hi_dst = slice(c * REGION + HALO + CHUNK, (c + 1) * REGION)
            if c == 0:
                u_scr[lo_dst, :] = zeros
            else:
                prev_end = (c - 1) * REGION + HALO + CHUNK
                u_scr[lo_dst, :] = u_scr[prev_end - HALO:prev_end, :]
            if c == n_chunks - 1:
                u_scr[hi_dst, :] = zeros
            else:
                nxt = (c + 1) * REGION + HALO
                u_scr[hi_dst, :] = u_scr[nxt:nxt + HALO, :]

    seq_len = jnp.where(is_ctx, SEQ, DEC_SEQ)
    row = lax.broadcasted_iota(jnp.int32, (CHUNK, POOL_CH), 0)
    for c in range(n_chunks):
        base = c * REGION + HALO
        rows = slice(c * CHUNK, (c + 1) * CHUNK)
        t = row + jnp.where(is_ctx, 0, c * CHUNK)
        for g, w in enumerate(POOL_WINDOWS):
            cols = slice(g * POOL_CH, (g + 1) * POOL_CH)
            acc = u_scr[base - w // 2:base - w // 2 + CHUNK, cols]
            for j in range(-w // 2 + 1, w // 2):
                acc = acc + u_scr[base + j:base + j + CHUNK, cols]
            cnt = jnp.minimum(t + w // 2, seq_len) - jnp.maximum(t - w // 2, 0)
            pooled = acc / cnt.astype(F32) - u_scr[base:base + CHUNK, cols]
            lin = _dot(pooled.astype(BF16), wpool_ref[g]) * ps_ref[:, cols]
            pool_out[rows, cols] = lin.astype(BF16)


def _in_call(x_ctx, x_lat, ada, g1, w_in, qg, kvg, w_uq, w_uk, w_uv, tq, tc, ts, w_pool, pscale):
    n_tiles = T_ALL // TILE_IN
    n_ctx = T_CTX // TILE_IN
    lat_tile0 = n_ctx if x_lat.shape[0] == T_ALL else 0

    def cond_row(i):
        return jnp.maximum(i - n_ctx + 1, 0)

    def tab(i):
        return (jnp.minimum(jnp.maximum(i - n_ctx + 1, 0), 1), 0, 0)

    ctx_or_spare = lambda i: (jnp.minimum(i, n_ctx), 0)
    const2 = lambda i: (0, 0)
    return pl.pallas_call(
        _in_kernel,
        out_shape=(jax.ShapeDtypeStruct((T_ALL, QK_W), BF16),
                   jax.ShapeDtypeStruct((T_ALL, QK_W), BF16),
                   jax.ShapeDtypeStruct((T_ALL, V_W), BF16),
                   jax.ShapeDtypeStruct((T_CTX + TILE_IN, KV_LORA), F32),
                   jax.ShapeDtypeStruct((T_CTX + TILE_IN, QK_ROPE), F32),
                   jax.ShapeDtypeStruct((T_ALL, POOL_W), BF16)),
        grid=(n_tiles,),
        in_specs=[
            pl.BlockSpec((TILE_IN, D_MODEL), lambda i: (jnp.minimum(i, n_ctx - 1), 0)),
            pl.BlockSpec((TILE_IN, D_MODEL), lambda i: (jnp.maximum(i - n_ctx, 0) + lat_tile0, 0)),
            pl.BlockSpec((None, 1, D_MODEL), lambda i: (cond_row(i), 0, 0)),
            pl.BlockSpec((None, 1, D_MODEL), lambda i: (cond_row(i), 0, 1)),
            pl.BlockSpec((1, D_MODEL), const2),
            pl.BlockSpec((D_MODEL, N_PROJ), const2),
            pl.BlockSpec((1, Q_LORA), const2),
            pl.BlockSpec((1, KV_LORA), const2),
            pl.BlockSpec((Q_LORA, QK_W), const2),
            pl.BlockSpec((KV_LORA, N_HEADS * QK_NOPE), const2),
            pl.BlockSpec((KV_LORA, V_W), const2),
            pl.BlockSpec((None, TILE_IN, LANES), tab),
            pl.BlockSpec((None, TILE_IN, LANES), tab),
            pl.BlockSpec((None, TILE_IN, LANES), tab),
            pl.BlockSpec((POOL_GROUPS, POOL_CH, POOL_CH), lambda i: (0, 0, 0)),
            pl.BlockSpec((1, POOL_W), const2),
        ],
        out_specs=(pl.BlockSpec((TILE_IN, QK_W), lambda i: (i, 0)),
                   pl.BlockSpec((TILE_IN, QK_W), lambda i: (i, 0)),
                   pl.BlockSpec((TILE_IN, V_W), lambda i: (i, 0)),
                   pl.BlockSpec((TILE_IN, KV_LORA), ctx_or_spare),
                   pl.BlockSpec((TILE_IN, QK_ROPE), ctx_or_spare),
                   pl.BlockSpec((TILE_IN, POOL_W), lambda i: (i, 0))),
        scratch_shapes=[pltpu.VMEM((TILE_IN // CHUNK * REGION, POOL_W), F32)],
        compiler_params=_cparams("arbitrary"),
        name="in_proj",
    )(x_ctx, x_lat, ada, ada, g1, w_in, qg, kvg, w_uq, w_uk, w_uv, tq, tc, ts, w_pool, pscale)


def _softmax_pv(scores, values):
    m = scores[0].max(axis=-1, keepdims=True)
    for s in scores[1:]:
        m = jnp.maximum(m, s.max(axis=-1, keepdims=True))
    den = None
    out = None
    for s, v in zip(scores, values):
        p = jnp.exp(s - m)
        d = p.sum(axis=-1, keepdims=True)
        o = _dot(p.astype(BF16), v)
        den = d if den is None else den + d
        out = o if out is None else out + o
    return out / den


def _attn_ctx_kernel(q_ref, k_ref, v_ref, o_ref):
    for hd in range(N_HEADS):
        qk = slice(hd * HEAD_PAD, (hd + 1) * HEAD_PAD)
        vv = slice(hd * V_DIM, (hd + 1) * V_DIM)
        s = _dot_nt(q_ref[:, qk], k_ref[:, qk])
        o_ref[:, vv] = _softmax_pv([s], [v_ref[:, vv]]).astype(BF16)


def _attn_lat_kernel(q_ref, k_ref, v_ref, kc_ref, vc_ref, o_ref):
    for hd in range(N_HEADS):
        qk = slice(hd * HEAD_PAD, (hd + 1) * HEAD_PAD)
        vv = slice(hd * V_DIM, (hd + 1) * V_DIM)
        q = q_ref[:, qk]
        s1 = _dot_nt(q, k_ref[:, qk])
        s2 = _dot_nt(q, kc_ref[:, qk])
        o_ref[:, vv] = _softmax_pv([s1, s2], [v_ref[:, vv], vc_ref[:, vv]]).astype(BF16)


def _attn_call(q, k, v, kc, vc):
    attn_ctx = pl.pallas_call(
        _attn_ctx_kernel,
        out_shape=jax.ShapeDtypeStruct((T_CTX, V_W), BF16),
        grid=(BATCH,),
        in_specs=[pl.BlockSpec((SEQ, QK_W), lambda b: (b, 0)),
                  pl.BlockSpec((SEQ, QK_W), lambda b: (b, 0)),
                  pl.BlockSpec((SEQ, V_W), lambda b: (b, 0))],
        out_specs=pl.BlockSpec((SEQ, V_W), lambda b: (b, 0)),
        compiler_params=_cparams("arbitrary"),
        name="attn_ctx",
    )(q, k, v)

    tq = 256
    n_q = DEC_SEQ // tq
    lat0 = T_CTX // DEC_SEQ
    attn_lat = pl.pallas_call(
        _attn_lat_kernel,
        out_shape=jax.ShapeDtypeStruct((T_LAT, V_W), BF16),
        grid=(DEC_BATCH, n_q),
        in_specs=[pl.BlockSpec((tq, QK_W), lambda b, i: (T_CTX // tq + b * n_q + i, 0)),
                  pl.BlockSpec((DEC_SEQ, QK_W), lambda b, i: (lat0 + b, 0)),
                  pl.BlockSpec((DEC_SEQ, V_W), lambda b, i: (lat0 + b, 0)),
                  pl.BlockSpec((PAST_LEN, QK_W), lambda b, i: (b, 0)),
                  pl.BlockSpec((PAST_LEN, V_W), lambda b, i: (b, 0))],
        out_specs=pl.BlockSpec((tq, V_W), lambda b, i: (b * n_q + i, 0)),
        compiler_params=_cparams("arbitrary", "arbitrary"),
        name="attn_lat",
    )(q, k, v, kc, vc)
    return attn_ctx, attn_lat


def _mix_kernel(with_router, attn_c_ref, attn_l_ref, pool_ref, xc_ref, xl_ref, g1_ref, sh2_ref, sc2_ref, n2_ref, wo_ref,
                *rest):
    if with_router:
        wr_ref, ltri_ref, utri_ref, x_out, xs_out, gs_out, route_out, meta_out = rest
    else:
        x_out, h_out = rest
    is_ctx = pl.program_id(0) < T_CTX // TILE_MIX
    attn = jnp.where(is_ctx, attn_c_ref[...], attn_l_ref[...])
    y = _dot(attn, wo_ref[:V_W, :]) + _dot(pool_ref[...], wo_ref[V_W:, :])
    x_new = jnp.where(is_ctx, xc_ref[...], xl_ref[...]) + g1_ref[...] * y
    x_out[...] = x_new
    h = (_rms(x_new) * n2_ref[...] * (1.0 + sc2_ref[...]) + sh2_ref[...]).astype(BF16)
    if not with_router:
        h_out[...] = h
        return

    logits = _dot(h, wr_ref[...])
    lane = lax.broadcasted_iota(jnp.int32, logits.shape, 1)
    neg = float(jnp.finfo(F32).min)
    lg = jnp.where(lane < N_EXPERTS, logits, neg)
    m1 = lg.max(axis=-1, keepdims=True)
    i1 = jnp.where(lg == m1, lane, LANES).min(axis=-1, keepdims=True)
    lg2 = jnp.where(lane == i1, neg, lg)
    m2 = lg2.max(axis=-1, keepdims=True)
    i2 = jnp.where(lg2 == m2, lane, LANES).min(axis=-1, keepdims=True)
    e = jnp.exp(m2 - m1)
    w1 = 1.0 / (1.0 + e)
    w2 = e / (1.0 + e)

    sel1 = lane == i1
    sel2 = lane == i2
    member = jnp.where(jnp.logical_or(sel1, sel2), 1.0, 0.0)
    rank = _dot(ltri_ref[...], member.astype(BF16))
    n_tok = member.sum(axis=0, keepdims=True)
    units = jnp.floor((n_tok + (SLOT_UNIT - 1)) * (1.0 / SLOT_UNIT))
    unit_off = _dot(jnp.broadcast_to(units, (8, LANES)).astype(BF16), utri_ref[...])[0:1, :]
    slot_of = SLOT_UNIT * unit_off + rank
    slot1 = jnp.where(sel1, slot_of, 0.0).sum(axis=-1, keepdims=True)
    slot2 = jnp.where(sel2, slot_of, 0.0).sum(axis=-1, keepdims=True)
    route = jnp.where(lane == 0, slot1, jnp.where(lane == 1, slot2,
                      jnp.where(lane == 2, w1, jnp.where(lane == 3, w2, 0.0))))
    route_out[...] = route
    sub = lax.broadcasted_iota(jnp.int32, (8, LANES), 0)
    meta_out[...] = jnp.where(sub == 0, units, jnp.where(sub == 1, unit_off, 0.0)).astype(jnp.int32)

    rt = route.T
    s1 = rt[0:1, :].astype(jnp.int32)
    s2 = rt[1:2, :].astype(jnp.int32)
    srow = lax.broadcasted_iota(jnp.int32, (SLOTS_PER_TILE, TILE_MIX), 0)
    hit1 = srow == s1
    hit2 = srow == s2
    perm = jnp.where(jnp.logical_or(hit1, hit2), 1.0, 0.0).astype(BF16)
    xs_out[...] = _dot(perm, h).astype(BF16)
    gate = (jnp.where(hit1, rt[2:3, :], 0.0) + jnp.where(hit2, rt[3:4, :], 0.0)).sum(axis=-1, keepdims=True)
    gs_out[...] = jnp.broadcast_to(gate, (SLOTS_PER_TILE, LANES))


def _mod_row(i, tile):
    n_ctx = T_CTX // tile
    per_seq = DEC_SEQ // tile
    return jnp.where(i < n_ctx, 0, 1 + (i - n_ctx) // per_seq)


def _mix_call(attn_ctx, attn_lat, pooled, x_ctx, x_lat, ada, n2, w_out, w_router):
    with_router = w_router is not None
    tm = TILE_MIX
    n_ctx = T_CTX // tm
    lat_tile0 = n_ctx if x_lat.shape[0] == T_ALL else 0
    row = lambda i: (i, 0)
    const2 = lambda i: (0, 0)
    mod = lambda k: pl.BlockSpec((None, 1, D_MODEL), lambda i: (_mod_row(i, tm), 0, k))
    ctx_blk = lambda i: (jnp.minimum(i, n_ctx - 1), 0)
    in_specs = [pl.BlockSpec((tm, V_W), ctx_blk),
                pl.BlockSpec((tm, V_W), lambda i: (jnp.maximum(i - n_ctx, 0), 0)),
                pl.BlockSpec((tm, POOL_W), row),
                pl.BlockSpec((tm, D_MODEL), ctx_blk),
                pl.BlockSpec((tm, D_MODEL), lambda i: (jnp.maximum(i - n_ctx, 0) + lat_tile0, 0)),
                mod(2), mod(3), mod(4),
                pl.BlockSpec((1, D_MODEL), const2), pl.BlockSpec((D_MODEL, D_MODEL), const2)]
    args = [attn_ctx, attn_lat, pooled, x_ctx, x_lat, ada, ada, ada, n2, w_out]
    if with_router:
        t_i = jnp.arange(tm)
        ltri = (t_i[None, :] < t_i[:, None]).astype(BF16)
        l_i = jnp.arange(LANES)
        utri = (l_i[:, None] < l_i[None, :]).astype(BF16)
        in_specs += [pl.BlockSpec((D_MODEL, LANES), const2), pl.BlockSpec((tm, tm), const2),
                     pl.BlockSpec((LANES, LANES), const2)]
        args += [w_router, ltri, utri]
        out_shape = [jax.ShapeDtypeStruct((T_ALL, D_MODEL), F32),
                     jax.ShapeDtypeStruct((N_ROUTE_TILES * SLOTS_PER_TILE, D_MODEL), BF16),
                     jax.ShapeDtypeStruct((N_ROUTE_TILES * SLOTS_PER_TILE, LANES), F32),
                     jax.ShapeDtypeStruct((T_ALL, LANES), F32),
                     jax.ShapeDtypeStruct((N_ROUTE_TILES, 8, LANES), jnp.int32)]
        out_specs = [pl.BlockSpec((tm, D_MODEL), row), pl.BlockSpec((SLOTS_PER_TILE, D_MODEL), row),
                     pl.BlockSpec((SLOTS_PER_TILE, LANES), row), pl.BlockSpec((tm, LANES), row),
                     pl.BlockSpec((None, 8, LANES), lambda i: (i, 0, 0))]
    else:
        out_shape = [jax.ShapeDtypeStruct((T_ALL, D_MODEL), F32), jax.ShapeDtypeStruct((T_ALL, D_MODEL), BF16)]
        out_specs = [pl.BlockSpec((tm, D_MODEL), row), pl.BlockSpec((tm, D_MODEL), row)]
    return pl.pallas_call(
        functools.partial(_mix_kernel, with_router),
        out_shape=tuple(out_shape),
        grid=(T_ALL // tm,),
        in_specs=in_specs,
        out_specs=tuple(out_specs),
        compiler_params=_cparams("arbitrary"),
        name="mix_router" if with_router else "mix",
    )(*args)


def _swiglu_hidden(h, wg_ref, wu_ref, a_buf):
    width = a_buf.shape[1]
    for c0 in range(0, width, FF_CHUNK):
        cols = slice(c0, min(c0 + FF_CHUNK, width))
        a_buf[:, cols] = (_silu(_dot(h, wg_ref[:, cols])) * _dot(h, wu_ref[:, cols])).astype(BF16)


def _ffn_kernel(h_ref, x_ref, g2_ref, wg_ref, wu_ref, wd_ref, o_ref, a_buf):
    _swiglu_hidden(h_ref[...], wg_ref, wu_ref, a_buf)
    o_ref[...] = x_ref[...] + g2_ref[...] * _dot(a_buf[...], wd_ref[...])


def _ffn_call(h, x, ada, wg, wu, wd):
    tm = TILE_FF
    resident = lambda shape: pl.BlockSpec(shape, lambda i: (0, 0), pipeline_mode=pl.Buffered(1))
    return pl.pallas_call(
        _ffn_kernel,
        out_shape=jax.ShapeDtypeStruct((T_ALL, D_MODEL), F32),
        grid=(T_ALL // tm,),
        in_specs=[pl.BlockSpec((tm, D_MODEL), lambda i: (i, 0)),
                  pl.BlockSpec((tm, D_MODEL), lambda i: (i, 0)),
                  pl.BlockSpec((None, 1, D_MODEL), lambda i: (_mod_row(i, tm), 0, 5)),
                  resident((D_MODEL, D_FF)), resident((D_MODEL, D_FF)), resident((D_FF, D_MODEL))],
        out_specs=pl.BlockSpec((tm, D_MODEL), lambda i: (i, 0)),
        scratch_shapes=[pltpu.VMEM((tm, D_FF), BF16)],
        compiler_params=_cparams("arbitrary"),
        name="ffn_dense",
    )(h, x, ada, wg, wu, wd)


def _route_tables(meta):
    units = meta[:, 0, :N_EXPERTS]
    offs = meta[:, 1, :N_EXPERTS]
    cum = jnp.cumsum(units, axis=0)
    total = cum[-1]
    tiles_e = (total + GEMM_UNITS - 1) // GEMM_UNITS
    tile_end = jnp.cumsum(tiles_e)
    n_act = tile_end[-1]
    m = jnp.arange(N_GEMM_TILES)
    m_eff = jnp.minimum(m, jnp.maximum(n_act - 1, 0))
    expert = jnp.minimum(jnp.sum(tile_end[None, :] <= m_eff[:, None], axis=1), N_EXPERTS - 1)
    is_e = (expert[:, None] == jnp.arange(N_EXPERTS)[None, :]).astype(jnp.int32)
    pick = lambda per_expert: jnp.sum(is_e * per_expert[None, :], axis=1)
    first_q = (m_eff - pick(tile_end - tiles_e)) * GEMM_UNITS
    count = jnp.where(m < n_act, jnp.clip(pick(total) - first_q, 0, GEMM_UNITS), 0)
    q = first_q[:, None] + jnp.arange(GEMM_UNITS)[None, :]
    rows_of = lambda table: jnp.sum(is_e[:, None, :] * table[None, :, :], axis=2)
    cum_e, units_e, offs_e = rows_of(cum), rows_of(units), rows_of(offs)
    src_tile = jnp.minimum(jnp.sum(cum_e[:, None, :] <= q[:, :, None], axis=2), N_ROUTE_TILES - 1)
    is_t = (src_tile[:, :, None] == jnp.arange(N_ROUTE_TILES)[None, None, :]).astype(jnp.int32)
    at_tile = lambda per_tile: jnp.sum(is_t * per_tile[:, None, :], axis=2)
    uid = src_tile * UNITS_PER_TILE + at_tile(offs_e) + (q - at_tile(cum_e - units_e))
    uid = jnp.clip(uid, 0, N_UNITS - 1)
    i32 = jnp.int32
    return expert.astype(i32), count.astype(i32), uid.reshape(-1).astype(i32), n_act.reshape(1).astype(i32)


def _gemm_kernel(em_ref, cnt_ref, ul_ref, nact_ref, xs_hbm, gs_hbm, wg_ref, wu_ref, wd_ref, ys_hbm,
                 xbuf, gbuf, obuf, acc_ref, a_buf, sem_in, sem_out):
    del em_ref
    m = pl.program_id(0)
    j = pl.program_id(1)
    last_j = pl.num_programs(1) - 1
    n_act = nact_ref[0]
    active = m < n_act
    slot = m % 2

    def rows(r):
        return pl.ds(pl.multiple_of(r * SLOT_UNIT, SLOT_UNIT), SLOT_UNIT)

    def in_copies(mm, sl, r):
        uid = ul_ref[mm * GEMM_UNITS + r]
        return (pltpu.make_async_copy(xs_hbm.at[uid], xbuf.at[sl, rows(r)], sem_in.at[sl]),
                pltpu.make_async_copy(gs_hbm.at[uid], gbuf.at[sl, rows(r)], sem_in.at[sl]))

    def out_copy(mm, r):
        uid = ul_ref[mm * GEMM_UNITS + r]
        return pltpu.make_async_copy(obuf.at[rows(r)], ys_hbm.at[uid], sem_out.at[0])

    def start_in(mm, sl):
        n = cnt_ref[mm]

        def issue(r, carry):
            for cp in in_copies(mm, sl, r):
                cp.start()
            return carry

        def clear(r, carry):
            xbuf[sl, rows(r), :] = jnp.zeros((SLOT_UNIT, D_MODEL), BF16)
            gbuf[sl, rows(r), :] = jnp.zeros((SLOT_UNIT, LANES), F32)
            return carry

        lax.fori_loop(0, n, issue, 0)
        lax.fori_loop(n, GEMM_UNITS, clear, 0)

    def wait_in(mm, sl):
        def body(r, carry):
            for cp in in_copies(mm, sl, r):
                cp.wait()
            return carry

        lax.fori_loop(0, cnt_ref[mm], body, 0)

    def start_out(mm):
        def body(r, carry):
            out_copy(mm, r).start()
            return carry

        lax.fori_loop(0, cnt_ref[mm], body, 0)

    def wait_out(mm):
        def body(r, carry):
            out_copy(mm, r).wait()
            return carry

        lax.fori_loop(0, cnt_ref[mm], body, 0)

    @pl.when(jnp.logical_and(active, j == 0))
    def _():
        @pl.when(m == 0)
        def _():
            start_in(0, 0)

        wait_in(m, slot)

        @pl.when(m + 1 < n_act)
        def _():
            start_in(m + 1, 1 - slot)

    n_valid = cnt_ref[m]
    for quarter in range(1, GEMM_QUARTERS + 1):
        n_rows = quarter * GEMM_ROWS // GEMM_QUARTERS
        lo_units = (quarter - 1) * GEMM_UNITS // GEMM_QUARTERS
        hi_units = quarter * GEMM_UNITS // GEMM_QUARTERS

        @pl.when(jnp.logical_and(active, jnp.logical_and(n_valid > lo_units, n_valid <= hi_units)))
        def _(n_rows=n_rows):
            _swiglu_hidden(xbuf[slot, :n_rows, :], wg_ref, wu_ref, a_buf.at[:n_rows, :])
            part = _dot(a_buf[:n_rows, :], wd_ref[...])

            @pl.when(j == 0)
            def _():
                acc_ref[:n_rows, :] = part

            @pl.when(j > 0)
            def _():
                acc_ref[:n_rows, :] += part

    @pl.when(jnp.logical_and(active, j == last_j))
    def _():
        @pl.when(m > 0)
        def _():
            wait_out(m - 1)

        for quarter in range(GEMM_QUARTERS):
            rows_q = slice(quarter * GEMM_ROWS // GEMM_QUARTERS, (quarter + 1) * GEMM_ROWS // GEMM_QUARTERS)

            @pl.when(n_valid > quarter * GEMM_UNITS // GEMM_QUARTERS)
            def _(rows_q=rows_q):
                gate = gbuf[slot, rows_q, :]
                for cb in range(D_MODEL // LANES):
                    cols = slice(cb * LANES, (cb + 1) * LANES)
                    obuf[rows_q, cols] = (acc_ref[rows_q, cols] * gate).astype(BF16)

        start_out(m)

        @pl.when(m == n_act - 1)
        def _():
            wait_out(m)


def _gemm_call(expert, count, uids, n_act, xs, gs, wg, wu, wd):
    tf = TILE_FE
    n_j = D_FF_EXPERT // tf

    def w_col(m, j, em, cnt, ul, nact):
        return (em[m], 0, jnp.where(m < nact[0], j, n_j - 1))

    def w_row(m, j, em, cnt, ul, nact):
        return (em[m], jnp.where(m < nact[0], j, n_j - 1), 0)

    grid_spec = pltpu.PrefetchScalarGridSpec(
        num_scalar_prefetch=4,
        grid=(N_GEMM_TILES, n_j),
        in_specs=[pl.BlockSpec(memory_space=pl.ANY),
                  pl.BlockSpec(memory_space=pl.ANY),
                  pl.BlockSpec((None, D_MODEL, tf), w_col),
                  pl.BlockSpec((None, D_MODEL, tf), w_col),
                  pl.BlockSpec((None, tf, D_MODEL), w_row)],
        out_specs=pl.BlockSpec(memory_space=pl.ANY),
        scratch_shapes=[pltpu.VMEM((2, GEMM_ROWS, D_MODEL), BF16),
                        pltpu.VMEM((2, GEMM_ROWS, LANES), F32),
                        pltpu.VMEM((GEMM_ROWS, D_MODEL), BF16),
                        pltpu.VMEM((GEMM_ROWS, D_MODEL), F32),
                        pltpu.VMEM((GEMM_ROWS, tf), BF16),
                        pltpu.SemaphoreType.DMA((2,)),
                        pltpu.SemaphoreType.DMA((1,))])
    return pl.pallas_call(
        _gemm_kernel,
        out_shape=jax.ShapeDtypeStruct((N_UNITS, SLOT_UNIT, D_MODEL), BF16),
        grid_spec=grid_spec,
        input_output_aliases={4: 0},
        compiler_params=_cparams("arbitrary", "arbitrary"),
        name="moe_experts",
    )(expert, count, uids, n_act,
      xs.reshape(N_UNITS, SLOT_UNIT, D_MODEL), gs.reshape(N_UNITS, SLOT_UNIT, LANES), wg, wu, wd)


def _combine_kernel(ys_ref, route_ref, x_ref, g2_ref, fg_ref, o_ref):
    route = route_ref[...]
    s1 = route[:, 0:1].astype(jnp.int32)
    s2 = route[:, 1:2].astype(jnp.int32)
    scol = lax.broadcasted_iota(jnp.int32, (TILE_MIX, SLOTS_PER_TILE), 1)
    unperm = jnp.where(jnp.logical_or(scol == s1, scol == s2), 1.0, 0.0).astype(BF16)
    y = _dot(unperm, ys_ref[...])
    o_ref[...] = _rms(x_ref[...] + g2_ref[...] * y) * fg_ref[...]


def _combine_call(ys, route, x, ada, final_g, tile0, n_tok):
    tm = TILE_MIX
    return pl.pallas_call(
        _combine_kernel,
        out_shape=jax.ShapeDtypeStruct((n_tok, D_MODEL), F32),
        grid=(n_tok // tm,),
        in_specs=[pl.BlockSpec((SLOTS_PER_TILE, D_MODEL), lambda i: (i + tile0, 0)),
                  pl.BlockSpec((tm, LANES), lambda i: (i + tile0, 0)),
                  pl.BlockSpec((tm, D_MODEL), lambda i: (i + tile0, 0)),
                  pl.BlockSpec((None, 1, D_MODEL), lambda i: (_mod_row(i + tile0, tm), 0, 5)),
                  pl.BlockSpec((1, D_MODEL), lambda i: (0, 0))],
        out_specs=pl.BlockSpec((tm, D_MODEL), lambda i: (i, 0)),
        compiler_params=_cparams("arbitrary"),
        name="moe_combine",
    )(ys.reshape(N_UNITS * SLOT_UNIT, D_MODEL), route, x, ada, final_g)


def _rot_cols(w):
    q = QK_ROPE // 4
    return jnp.concatenate([-w[:, q:2 * q], w[:, :q], -w[:, 3 * q:], w[:, 2 * q:3 * q]], axis=1)


def _rope_tables():
    t = jnp.arange(DEC_SEQ)
    rows = (t // GRID_W).astype(F32)
    cols = (t % GRID_W).astype(F32)
    half = QK_ROPE // 2
    freqs = ROPE_BASE ** (-jnp.arange(0, half, 2, dtype=F32) / half)
    ang_r = rows[:, None] * freqs
    ang_c = cols[:, None] * freqs
    ang = jnp.concatenate([ang_r, ang_r, ang_c, ang_c], axis=-1)
    cos, sin = jnp.cos(ang), jnp.sin(ang)
    one, zero = jnp.ones_like(cos), jnp.zeros_like(cos)
    tq = jnp.stack([jnp.concatenate([one, zero], -1), jnp.concatenate([cos, sin], -1)])
    tc = jnp.stack([jnp.concatenate([one, one], -1), jnp.concatenate([cos, cos], -1)])
    ts = jnp.stack([jnp.concatenate([zero, zero], -1), jnp.concatenate([sin, sin], -1)])
    return tq, tc, ts


def _layer_weights(l, w_in, w_uq, w_ukv):
    wi = w_in[l]
    w_kpe = wi[:, OFF_U:OFF_U + QK_ROPE]
    w_rot = _rot_cols(w_kpe)
    w_in_r = jnp.concatenate([wi[:, :OFF_U], wi[:, OFF_U + QK_ROPE:], w_kpe, w_kpe, w_rot, w_rot], axis=1)
    heads = []
    for h in range(N_HEADS):
        blk = w_uq[l][:, h * (QK_NOPE + QK_ROPE):(h + 1) * (QK_NOPE + QK_ROPE)]
        heads += [blk, _rot_cols(blk[:, QK_NOPE:])]
    w_uq_r = jnp.concatenate(heads, axis=1)
    kv = w_ukv[l].reshape(KV_LORA, N_HEADS, QK_NOPE + V_DIM)
    w_uk = kv[:, :, :QK_NOPE].reshape(KV_LORA, N_HEADS * QK_NOPE)
    w_uv = kv[:, :, QK_NOPE:].reshape(KV_LORA, V_W)
    return w_in_r.astype(BF16), w_uq_r.astype(BF16), w_uk.astype(BF16), w_uv.astype(BF16)


def kernel(x_prompt, x_sample, c, cache_ckv, cache_kpe, c_ctx, norm1_g, norm2_g, w_ada, b_ada, w_in, q_norm_g,
           kv_norm_g, w_uq, w_ukv, w_pool, pool_scale, w_out, ffn_w_gate, ffn_w_up, ffn_w_down, moe_w_router,
           moe_w_gate, moe_w_up, moe_w_down, final_norm_g):
    x_ctx = x_prompt.reshape(T_CTX, D_MODEL)
    x_lat = x_sample.reshape(T_LAT, D_MODEL)
    cond = jnp.concatenate([c_ctx[None, :], c, jnp.zeros((COND_ROWS - 1 - DEC_BATCH, D_MODEL), F32)], axis=0)
    ada_all = _ada_call(cond, w_ada, b_ada)
    tq, tc, ts = _rope_tables()

    lw = [_layer_weights(l, w_in, w_uq, w_ukv) for l in range(DEPTH)]
    dup = jnp.concatenate([jnp.eye(QK_ROPE, dtype=BF16)] * 2, axis=1)
    kc_all, vc_all = _cache_call(cache_ckv, cache_kpe, jnp.stack([w[2] for w in lw]),
                                 jnp.stack([w[3] for w in lw]), dup)

    assert DEPTH == 2
    new_ckv, new_kpe = [], []
    for l in range(DEPTH):
        w_in_r, w_uq_r, w_uk, w_uv = lw[l]
        ada = ada_all[l].reshape(COND_ROWS, 1, 6 * D_MODEL)
        q, k, v, ckv, kpe, pooled = _in_call(
            x_ctx, x_lat, ada, norm1_g[l][None, :], w_in_r, q_norm_g[l][None, :], kv_norm_g[l][None, :], w_uq_r,
            w_uk, w_uv, tq, tc, ts, w_pool[l].astype(BF16), pool_scale[l][None, :])
        new_ckv.append(ckv[:T_CTX].reshape(BATCH, SEQ, KV_LORA))
        new_kpe.append(kpe[:T_CTX].reshape(BATCH, SEQ, QK_ROPE))
        attn_c, attn_l = _attn_call(q, k, v, kc_all[l], vc_all[l])
        j = l // 2
        if l % 2 == 0:
            x, h2 = _mix_call(attn_c, attn_l, pooled, x_ctx, x_lat, ada, norm2_g[l][None, :],
                              w_out[l].astype(BF16), None)
            x = _ffn_call(h2, x, ada, ffn_w_gate[j].astype(BF16), ffn_w_up[j].astype(BF16),
                          ffn_w_down[j].astype(BF16))
            x_ctx = x_lat = x
        else:
            w_r = jnp.pad(moe_w_router[j], ((0, 0), (0, LANES - N_EXPERTS))).astype(BF16)
            x, xs, gs, route, meta = _mix_call(attn_c, attn_l, pooled, x_ctx, x_lat, ada, norm2_g[l][None, :],
                                               w_out[l].astype(BF16), w_r)
            expert, count, uids, n_act = _route_tables(meta)
            ys = _gemm_call(expert, count, uids, n_act, xs, gs, moe_w_gate[j].astype(BF16),
                            moe_w_up[j].astype(BF16), moe_w_down[j].astype(BF16))
            fg = final_norm_g[None, :]
            y_prompt = _combine_call(ys, route, x, ada, fg, 0, T_CTX).reshape(BATCH, SEQ, D_MODEL)
            y_sample = _combine_call(ys, route, x, ada, fg, T_CTX // TILE_MIX, T_LAT).reshape(
                DEC_BATCH, DEC_SEQ, D_MODEL)

    return y_prompt, y_sample, jnp.stack(new_ckv, axis=1), jnp.stack(new_kpe, axis=1)
```

```python
import functools

import jax
import jax.numpy as jnp
from jax import lax
from jax.experimental import pallas as pl
from jax.experimental.pallas import tpu as pltpu

D_MODEL = 1024
BATCH = 32
SEQ = 256
DEPTH = 2
DEC_BATCH = 8
DEC_SEQ = 1024
PAST_LEN = 512
GRID_W = 64
N_HEADS = 4
QK_NOPE = 128
QK_ROPE = 64
V_DIM = 128
Q_LORA = 384
KV_LORA = 256
POOL_W = 512
POOL_GROUPS = 4
POOL_WINDOWS = (2, 4, 8, 16)
POOL_CH = POOL_W // POOL_GROUPS
D_FF = 2816
N_EXPERTS = 8
D_FF_EXPERT = 3584
ROPE_BASE = 10000.0
EPS = 1e-6

T_CTX = BATCH * SEQ
T_LAT = DEC_BATCH * DEC_SEQ
T_ALL = T_CTX + T_LAT

LANES = 128
HEAD_PAD = 256
QK_W = N_HEADS * HEAD_PAD
V_W = N_HEADS * V_DIM
OFF_CKV = Q_LORA
OFF_U = Q_LORA + KV_LORA
OFF_KA = OFF_U + POOL_W
OFF_KB = OFF_KA + LANES
N_PROJ = OFF_KB + LANES

TILE_IN = 1024
CHUNK = 256
HALO = 8
REGION = CHUNK + 2 * HALO
CTX_SEQS_PER_STEP = 4
LAT_Q_TILE = 1024
TILE_MIX = 512
TILE_FF = 512
TOP_K = 2
SLOT_UNIT = 16
UNITS_PER_TILE = TILE_MIX * TOP_K // SLOT_UNIT + N_EXPERTS
SLOTS_PER_TILE = UNITS_PER_TILE * SLOT_UNIT
SLOT_BLOCK = SLOTS_PER_TILE // 4
N_ROUTE_TILES = T_ALL // TILE_MIX
N_UNITS = N_ROUTE_TILES * UNITS_PER_TILE
GEMM_UNITS = 64
GEMM_ROWS = GEMM_UNITS * SLOT_UNIT
GEMM_QUARTERS = 4
N_GEMM_TILES = -(-N_UNITS // GEMM_UNITS) + N_EXPERTS
TILE_FE = D_FF_EXPERT // 2
FF_CHUNK = 256
COND_ROWS = 16
VMEM_LIMIT = 56 * 1024 * 1024

F32 = jnp.float32
BF16 = jnp.bfloat16


def _rms(x):
    return x * lax.rsqrt(jnp.mean(x * x, axis=-1, keepdims=True) + EPS)


def _dot(a, b):
    return jnp.dot(a, b, preferred_element_type=F32)


def _dot_nt(a, b):
    return lax.dot_general(a, b, (((1,), (1,)), ((), ())), preferred_element_type=F32)


def _silu(x):
    return x * (1.0 / (1.0 + jnp.exp(-x)))


def _cparams(*sem):
    return pltpu.CompilerParams(dimension_semantics=sem, vmem_limit_bytes=VMEM_LIMIT)


def _ada_kernel(cond_ref, w_ref, b_ref, o_ref):
    c = cond_ref[...]
    o_ref[...] = jnp.dot(_silu(c), w_ref[...], preferred_element_type=F32,
                         precision=lax.Precision.HIGHEST) + b_ref[...]


def _ada_call(cond, w_ada, b_ada):
    n_blk = 4
    bw = 6 * D_MODEL // n_blk
    return pl.pallas_call(
        _ada_kernel,
        out_shape=jax.ShapeDtypeStruct((DEPTH, COND_ROWS, 6 * D_MODEL), F32),
        grid=(DEPTH, n_blk),
        in_specs=[
            pl.BlockSpec((COND_ROWS, D_MODEL), lambda l, j: (0, 0)),
            pl.BlockSpec((None, D_MODEL, bw), lambda l, j: (l, 0, j)),
            pl.BlockSpec((None, 1, bw), lambda l, j: (l, 0, j)),
        ],
        out_specs=pl.BlockSpec((None, COND_ROWS, bw), lambda l, j: (l, 0, j)),
        compiler_params=_cparams("arbitrary", "arbitrary"),
        name="ada_params",
    )(cond, w_ada, b_ada.reshape(DEPTH, 1, 6 * D_MODEL))


def _cache_kernel(ckv_ref, kpe_ref, wuk_ref, wuv_ref, dup_ref, k_ref, v_ref):
    ckv = ckv_ref[...].astype(BF16)
    knope = _dot(ckv, wuk_ref[...])
    v_ref[...] = _dot(ckv, wuv_ref[...]).astype(BF16)
    kdup = _dot(kpe_ref[...].astype(BF16), dup_ref[...]).astype(BF16)
    for h in range(N_HEADS):
        k_ref[:, h * HEAD_PAD:h * HEAD_PAD + QK_NOPE] = knope[:, h * QK_NOPE:(h + 1) * QK_NOPE].astype(BF16)
        k_ref[:, h * HEAD_PAD + QK_NOPE:(h + 1) * HEAD_PAD] = kdup


def _cache_call(cache_ckv, cache_kpe, w_uk, w_uv, dup):
    n_tok = DEC_BATCH * PAST_LEN
    return pl.pallas_call(
        _cache_kernel,
        out_shape=(jax.ShapeDtypeStruct((DEPTH, n_tok, QK_W), BF16),
                   jax.ShapeDtypeStruct((DEPTH, n_tok, V_W), BF16)),
        grid=(DEPTH, DEC_BATCH),
        in_specs=[
            pl.BlockSpec((None, None, PAST_LEN, KV_LORA), lambda l, b: (b, l, 0, 0)),
            pl.BlockSpec((None, None, PAST_LEN, QK_ROPE), lambda l, b: (b, l, 0, 0)),
            pl.BlockSpec((None, KV_LORA, N_HEADS * QK_NOPE), lambda l, b: (l, 0, 0)),
            pl.BlockSpec((None, KV_LORA, V_W), lambda l, b: (l, 0, 0)),
            pl.BlockSpec((QK_ROPE, LANES), lambda l, b: (0, 0)),
        ],
        out_specs=(pl.BlockSpec((None, PAST_LEN, QK_W), lambda l, b: (l, b, 0)),
                   pl.BlockSpec((None, PAST_LEN, V_W), lambda l, b: (l, b, 0))),
        compiler_params=_cparams("arbitrary", "arbitrary"),
        name="cache_kv",
    )(cache_ckv, cache_kpe, w_uk, w_uv, dup)


def _in_kernel(xc_ref, xl_ref, sh_ref, sc_ref, g_ref, win_ref, qg_ref, kvg_ref, wuq_ref, wuk_ref, wuv_ref,
               tq_ref, tc_ref, ts_ref, wpool_ref, ps_ref,
               q_out, k_out, v_out, ckv_out, kpe_out, pool_out, u_scr):
    is_ctx = pl.program_id(0) < T_CTX // TILE_IN
    shift = sh_ref[...]
    scale1 = 1.0 + sc_ref[...]
    qk_scale = (QK_NOPE + QK_ROPE) ** -0.5
    n_chunks = TILE_IN // CHUNK
    zeros = jnp.zeros((HALO, POOL_W), F32)
    seq_len = jnp.where(is_ctx, SEQ, DEC_SEQ)
    row = lax.broadcasted_iota(jnp.int32, (CHUNK, POOL_CH), 0)

    def pool_chunk(c):
        base = c * REGION + HALO
        rows = slice(c * CHUNK, (c + 1) * CHUNK)
        t = row + jnp.where(is_ctx, 0, c * CHUNK)
        for g, w in enumerate(POOL_WINDOWS):
            cols = slice(g * POOL_CH, (g + 1) * POOL_CH)
            acc = u_scr[base - w // 2:base - w // 2 + CHUNK, cols]
            for j in range(-w // 2 + 1, w // 2):
                acc = acc + u_scr[base + j:base + j + CHUNK, cols]
            cnt = jnp.minimum(t + w // 2, seq_len) - jnp.maximum(t - w // 2, 0)
            pooled = acc / cnt.astype(F32) - u_scr[base:base + CHUNK, cols]
            lin = _dot(pooled.astype(BF16), wpool_ref[g]) * ps_ref[:, cols]
            pool_out[rows, cols] = lin.astype(BF16)

    for c in range(n_chunks):
        r0 = c * CHUNK
        rows = slice(r0, r0 + CHUNK)
        x = jnp.where(is_ctx, xc_ref[rows, :], xl_ref[rows, :])
        h = (_rms(x) * g_ref[...] * scale1 + shift).astype(BF16)
        proj = _dot(h, win_ref[...])

        qn = (_rms(proj[:, :Q_LORA]) * qg_ref[...]).astype(BF16)
        q = _dot(qn, wuq_ref[...]) * qk_scale
        tq = tq_ref[rows, :]
        for hd in range(N_HEADS):
            lo = hd * HEAD_PAD
            q_out[rows, lo:lo + QK_NOPE] = q[:, lo:lo + QK_NOPE].astype(BF16)
            q_out[rows, lo + QK_NOPE:lo + HEAD_PAD] = (q[:, lo + QK_NOPE:lo + HEAD_PAD] * tq).astype(BF16)

        ckv = _rms(proj[:, OFF_CKV:OFF_U]) * kvg_ref[...]
        ckv_out[rows, :] = ckv
        ckv_b = ckv.astype(BF16)
        knope = _dot(ckv_b, wuk_ref[...])
        v_out[rows, :] = _dot(ckv_b, wuv_ref[...]).astype(BF16)

        k_a = proj[:, OFF_KA:OFF_KB]
        k_b = proj[:, OFF_KB:N_PROJ]
        kpe_out[rows, :] = k_a[:, :QK_ROPE]
        kr = (k_a * tc_ref[rows, :] + k_b * ts_ref[rows, :]).astype(BF16)
        for hd in range(N_HEADS):
            lo = hd * HEAD_PAD
            k_out[rows, lo:lo + QK_NOPE] = knope[:, hd * QK_NOPE:(hd + 1) * QK_NOPE].astype(BF16)
            k_out[rows, lo + QK_NOPE:lo + HEAD_PAD] = kr

        u = proj[:, OFF_U:OFF_KA]
        base = c * REGION + HALO
        u_scr[base:base + CHUNK, :] = u
        if c == 0:
            u_scr[0:HALO, :] = zeros
        else:
            u_scr[base - 2 * HALO:base - HALO, :] = jnp.where(is_ctx, zeros, u[:HALO, :])
        if c == n_chunks - 1:
            u_scr[base + CHUNK:base + CHUNK + HALO, :] = zeros
        else:
            u_scr[base + CHUNK + HALO:base + CHUNK + 2 * HALO, :] = jnp.where(is_ctx, zeros, u[CHUNK - HALO:, :])
        if c > 0:
            pool_chunk(c - 1)
    pool_chunk(n_chunks - 1)


def _in_call(x_ctx, x_lat, ada, g1, w_in, qg, kvg, w_uq, w_uk, w_uv, tq, tc, ts, w_pool, pscale):
    n_tiles = T_ALL // TILE_IN
    n_ctx = T_CTX // TILE_IN
    lat_tile0 = n_ctx if x_lat.shape[0] == T_ALL else 0

    def cond_row(i):
        return jnp.maximum(i - n_ctx + 1, 0)

    def tab(i):
        return (jnp.minimum(jnp.maximum(i - n_ctx + 1, 0), 1), 0, 0)

    ctx_or_spare = lambda i: (jnp.minimum(i, n_ctx), 0)
    const2 = lambda i: (0, 0)
    return pl.pallas_call(
        _in_kernel,
        out_shape=(jax.ShapeDtypeStruct((T_ALL, QK_W), BF16),
                   jax.ShapeDtypeStruct((T_ALL, QK_W), BF16),
                   jax.ShapeDtypeStruct((T_ALL, V_W), BF16),
                   jax.ShapeDtypeStruct((T_CTX + TILE_IN, KV_LORA), F32),
                   jax.ShapeDtypeStruct((T_CTX + TILE_IN, QK_ROPE), F32),
                   jax.ShapeDtypeStruct((T_ALL, POOL_W), BF16)),
        grid=(n_tiles,),
        in_specs=[
            pl.BlockSpec((TILE_IN, D_MODEL), lambda i: (jnp.minimum(i, n_ctx - 1), 0)),
            pl.BlockSpec((TILE_IN, D_MODEL), lambda i: (jnp.maximum(i - n_ctx, 0) + lat_tile0, 0)),
            pl.BlockSpec((None, 1, D_MODEL), lambda i: (cond_row(i), 0, 0)),
            pl.BlockSpec((None, 1, D_MODEL), lambda i: (cond_row(i), 0, 1)),
            pl.BlockSpec((1, D_MODEL), const2),
            pl.BlockSpec((D_MODEL, N_PROJ), const2),
            pl.BlockSpec((1, Q_LORA), const2),
            pl.BlockSpec((1, KV_LORA), const2),
            pl.BlockSpec((Q_LORA, QK_W), const2),
            pl.BlockSpec((KV_LORA, N_HEADS * QK_NOPE), const2),
            pl.BlockSpec((KV_LORA, V_W), const2),
            pl.BlockSpec((None, TILE_IN, LANES), tab),
            pl.BlockSpec((None, TILE_IN, LANES), tab),
            pl.BlockSpec((None, TILE_IN, LANES), tab),
            pl.BlockSpec((POOL_GROUPS, POOL_CH, POOL_CH), lambda i: (0, 0, 0)),
            pl.BlockSpec((1, POOL_W), const2),
        ],
        out_specs=(pl.BlockSpec((TILE_IN, QK_W), lambda i: (i, 0)),
                   pl.BlockSpec((TILE_IN, QK_W), lambda i: (i, 0)),
                   pl.BlockSpec((TILE_IN, V_W), lambda i: (i, 0)),
                   pl.BlockSpec((TILE_IN, KV_LORA), ctx_or_spare),
                   pl.BlockSpec((TILE_IN, QK_ROPE), ctx_or_spare),
                   pl.BlockSpec((TILE_IN, POOL_W), lambda i: (i, 0))),
        scratch_shapes=[pltpu.VMEM((TILE_IN // CHUNK * REGION, POOL_W), F32)],
        compiler_params=_cparams("arbitrary"),
        name="in_proj",
    )(x_ctx, x_lat, ada, ada, g1, w_in, qg, kvg, w_uq, w_uk, w_uv, tq, tc, ts, w_pool, pscale)


def _softmax_pv(scores, values):
    m = scores[0].max(axis=-1, keepdims=True)
    for s in scores[1:]:
        m = jnp.maximum(m, s.max(axis=-1, keepdims=True))
    den = None
    out = None
    for s, v in zip(scores, values):
        p = jnp.exp(s - m)
        d = p.sum(axis=-1, keepdims=True)
        o = _dot(p.astype(BF16), v)
        den = d if den is None else den + d
        out = o if out is None else out + o
    return out / den


def _attn_ctx_kernel(q_ref, k_ref, v_ref, o_ref):
    for b in range(CTX_SEQS_PER_STEP):
        rows = slice(b * SEQ, (b + 1) * SEQ)
        for hd in range(N_HEADS):
            qk = slice(hd * HEAD_PAD, (hd + 1) * HEAD_PAD)
            vv = slice(hd * V_DIM, (hd + 1) * V_DIM)
            s = _dot_nt(q_ref[rows, qk], k_ref[rows, qk])
            o_ref[rows, vv] = _softmax_pv([s], [v_ref[rows, vv]]).astype(BF16)


def _attn_lat_kernel(q_ref, k_ref, v_ref, kc_ref, vc_ref, o_ref):
    for hd in range(N_HEADS):
        qk = slice(hd * HEAD_PAD, (hd + 1) * HEAD_PAD)
        vv = slice(hd * V_DIM, (hd + 1) * V_DIM)
        q = q_ref[:, qk]
        s1 = _dot_nt(q, k_ref[:, qk])
        s2 = _dot_nt(q, kc_ref[:, qk])
        o_ref[:, vv] = _softmax_pv([s1, s2], [v_ref[:, vv], vc_ref[:, vv]]).astype(BF16)


def _attn_call(q, k, v, kc, vc):
    attn_ctx = pl.pallas_call(
        _attn_ctx_kernel,
        out_shape=jax.ShapeDtypeStruct((T_CTX, V_W), BF16),
        grid=(BATCH // CTX_SEQS_PER_STEP,),
        in_specs=[pl.BlockSpec((CTX_SEQS_PER_STEP * SEQ, QK_W), lambda b: (b, 0)),
                  pl.BlockSpec((CTX_SEQS_PER_STEP * SEQ, QK_W), lambda b: (b, 0)),
                  pl.BlockSpec((CTX_SEQS_PER_STEP * SEQ, V_W), lambda b: (b, 0))],
        out_specs=pl.BlockSpec((CTX_SEQS_PER_STEP * SEQ, V_W), lambda b: (b, 0)),
        compiler_params=_cparams("arbitrary"),
        name="attn_ctx",
    )(q, k, v)

    tq = LAT_Q_TILE
    n_q = DEC_SEQ // tq
    lat0 = T_CTX // DEC_SEQ
    attn_lat = pl.pallas_call(
        _attn_lat_kernel,
        out_shape=jax.ShapeDtypeStruct((T_LAT, V_W), BF16),
        grid=(DEC_BATCH, n_q),
        in_specs=[pl.BlockSpec((tq, QK_W), lambda b, i: (T_CTX // tq + b * n_q + i, 0)),
                  pl.BlockSpec((DEC_SEQ, QK_W), lambda b, i: (lat0 + b, 0)),
                  pl.BlockSpec((DEC_SEQ, V_W), lambda b, i: (lat0 + b, 0)),
                  pl.BlockSpec((PAST_LEN, QK_W), lambda b, i: (b, 0)),
                  pl.BlockSpec((PAST_LEN, V_W), lambda b, i: (b, 0))],
        out_specs=pl.BlockSpec((tq, V_W), lambda b, i: (b * n_q + i, 0)),
        compiler_params=_cparams("arbitrary", "arbitrary"),
        name="attn_lat",
    )(q, k, v, kc, vc)
    return attn_ctx, attn_lat


def _mix_kernel(with_router, attn_c_ref, attn_l_ref, pool_ref, xc_ref, xl_ref, g1_ref, sh2_ref, sc2_ref, n2_ref, wo_ref,
                *rest):
    if with_router:
        wr_ref, ltri_ref, utri_ref, x_out, xs_out, gs_out, route_out, meta_out = rest
    else:
        x_out, h_out = rest
    is_ctx = pl.program_id(0) < T_CTX // TILE_MIX
    halves = []
    for r0 in range(0, TILE_MIX, TILE_MIX // 2):
        rows = slice(r0, r0 + TILE_MIX // 2)
        attn = jnp.where(is_ctx, attn_c_ref[rows, :], attn_l_ref[rows, :])
        y = _dot(attn, wo_ref[:V_W, :]) + _dot(pool_ref[rows, :], wo_ref[V_W:, :])
        x_new = jnp.where(is_ctx, xc_ref[rows, :], xl_ref[rows, :]) + g1_ref[...] * y
        x_out[rows, :] = x_new
        halves.append((_rms(x_new) * n2_ref[...] * (1.0 + sc2_ref[...]) + sh2_ref[...]).astype(BF16))
        if not with_router:
            h_out[rows, :] = halves[-1]
    if not with_router:
        return
    h = jnp.concatenate(halves, axis=0)

    logits = _dot(h, wr_ref[...])
    lane = lax.broadcasted_iota(jnp.int32, logits.shape, 1)
    neg = float(jnp.finfo(F32).min)
    lg = jnp.where(lane < N_EXPERTS, logits, neg)
    m1 = lg.max(axis=-1, keepdims=True)
    i1 = jnp.where(lg == m1, lane, LANES).min(axis=-1, keepdims=True)
    lg2 = jnp.where(lane == i1, neg, lg)
    m2 = lg2.max(axis=-1, keepdims=True)
    i2 = jnp.where(lg2 == m2, lane, LANES).min(axis=-1, keepdims=True)
    e = jnp.exp(m2 - m1)
    w1 = 1.0 / (1.0 + e)
    w2 = e / (1.0 + e)

    sel1 = lane == i1
    sel2 = lane == i2
    member = jnp.where(jnp.logical_or(sel1, sel2), 1.0, 0.0)
    rank = _dot(ltri_ref[...], member.astype(BF16))
    n_tok = member.sum(axis=0, keepdims=True)
    units = jnp.floor((n_tok + (SLOT_UNIT - 1)) * (1.0 / SLOT_UNIT))
    unit_off = _dot(jnp.broadcast_to(units, (8, LANES)).astype(BF16), utri_ref[...])[0:1, :]
    slot_of = SLOT_UNIT * unit_off + rank
    slot1 = jnp.where(sel1, slot_of, 0.0).sum(axis=-1, keepdims=True)
    slot2 = jnp.where(sel2, slot_of, 0.0).sum(axis=-1, keepdims=True)
    route = jnp.where(lane == 0, slot1, jnp.where(lane == 1, slot2,
                      jnp.where(lane == 2, w1, jnp.where(lane == 3, w2, 0.0))))
    route_out[...] = route
    sub = lax.broadcasted_iota(jnp.int32, (8, LANES), 0)
    meta_out[...] = jnp.where(sub == 0, units, jnp.where(sub == 1, unit_off, 0.0)).astype(jnp.int32)

    rt = route.T
    s1 = rt[0:1, :].astype(jnp.int32)
    s2 = rt[1:2, :].astype(jnp.int32)
    def slot_block(r0):
        srow = r0 + lax.broadcasted_iota(jnp.int32, (SLOT_BLOCK, TILE_MIX), 0)
        hit1 = srow == s1
        hit2 = srow == s2
        perm = (jnp.where(hit1, 1.0, 0.0) + jnp.where(hit2, 1.0, 0.0)).astype(BF16)
        gate = (jnp.where(hit1, rt[2:3, :], 0.0) + jnp.where(hit2, rt[3:4, :], 0.0)).sum(axis=-1, keepdims=True)
        return perm, gate

    starts = list(range(0, SLOTS_PER_TILE, SLOT_BLOCK))
    nxt = slot_block(starts[0])
    for k, r0 in enumerate(starts):
        perm, gate = nxt
        if k + 1 < len(starts):
            nxt = slot_block(starts[k + 1])
        xs_out[r0:r0 + SLOT_BLOCK, :] = _dot(perm, h).astype(BF16)
        gs_out[r0:r0 + SLOT_BLOCK, :] = jnp.broadcast_to(gate, (SLOT_BLOCK, LANES))


def _mod_row(i, tile):
    n_ctx = T_CTX // tile
    per_seq = DEC_SEQ // tile
    return jnp.where(i < n_ctx, 0, 1 + (i - n_ctx) // per_seq)


def _mix_call(attn_ctx, attn_lat, pooled, x_ctx, x_lat, ada, n2, w_out, w_router):
    with_router = w_router is not None
    tm = TILE_MIX
    n_ctx = T_CTX // tm
    lat_tile0 = n_ctx if x_lat.shape[0] == T_ALL else 0
    row = lambda i: (i, 0)
    const2 = lambda i: (0, 0)
    mod = lambda k: pl.BlockSpec((None, 1, D_MODEL), lambda i: (_mod_row(i, tm), 0, k))
    ctx_blk = lambda i: (jnp.minimum(i, n_ctx - 1), 0)
    in_specs = [pl.BlockSpec((tm, V_W), ctx_blk),
                pl.BlockSpec((tm, V_W), lambda i: (jnp.maximum(i - n_ctx, 0), 0)),
                pl.BlockSpec((tm, POOL_W), row),
                pl.BlockSpec((tm, D_MODEL), ctx_blk),
                pl.BlockSpec((tm, D_MODEL), lambda i: (jnp.maximum(i - n_ctx, 0) + lat_tile0, 0)),
                mod(2), mod(3), mod(4),
                pl.BlockSpec((1, D_MODEL), const2), pl.BlockSpec((D_MODEL, D_MODEL), const2)]
    args = [attn_ctx, attn_lat, pooled, x_ctx, x_lat, ada, ada, ada, n2, w_out]
    if with_router:
        t_i = jnp.arange(tm)
        ltri = (t_i[None, :] < t_i[:, None]).astype(BF16)
        l_i = jnp.arange(LANES)
        utri = (l_i[:, None] < l_i[None, :]).astype(BF16)
        in_specs += [pl.BlockSpec((D_MODEL, LANES), const2), pl.BlockSpec((tm, tm), const2),
                     pl.BlockSpec((LANES, LANES), const2)]
        args += [w_router, ltri, utri]
        out_shape = [jax.ShapeDtypeStruct((T_ALL, D_MODEL), F32),
                     jax.ShapeDtypeStruct((N_ROUTE_TILES * SLOTS_PER_TILE, D_MODEL), BF16),
                     jax.ShapeDtypeStruct((N_ROUTE_TILES * SLOTS_PER_TILE, LANES), F32),
                     jax.ShapeDtypeStruct((T_ALL, LANES), F32),
                     jax.ShapeDtypeStruct((N_ROUTE_TILES, 8, LANES), jnp.int32)]
        out_specs = [pl.BlockSpec((tm, D_MODEL), row), pl.BlockSpec((SLOTS_PER_TILE, D_MODEL), row),
                     pl.BlockSpec((SLOTS_PER_TILE, LANES), row), pl.BlockSpec((tm, LANES), row),
                     pl.BlockSpec((None, 8, LANES), lambda i: (i, 0, 0))]
    else:
        out_shape = [jax.ShapeDtypeStruct((T_ALL, D_MODEL), F32), jax.ShapeDtypeStruct((T_ALL, D_MODEL), BF16)]
        out_specs = [pl.BlockSpec((tm, D_MODEL), row), pl.BlockSpec((tm, D_MODEL), row)]
    return pl.pallas_call(
        functools.partial(_mix_kernel, with_router),
        out_shape=tuple(out_shape),
        grid=(T_ALL // tm,),
        in_specs=in_specs,
        out_specs=tuple(out_specs),
        compiler_params=_cparams("arbitrary"),
        name="mix_router" if with_router else "mix",
    )(*args)


def _swiglu_hidden(h, wg_ref, wu_ref, a_buf):
    width = a_buf.shape[1]
    for c0 in range(0, width, FF_CHUNK):
        cols = slice(c0, min(c0 + FF_CHUNK, width))
        a_buf[:, cols] = (_silu(_dot(h, wg_ref[:, cols])) * _dot(h, wu_ref[:, cols])).astype(BF16)


def _ffn_kernel(h_ref, x_ref, g2_ref, wg_ref, wu_ref, wd_ref, o_ref, a_buf):
    _swiglu_hidden(h_ref[...], wg_ref, wu_ref, a_buf)
    o_ref[...] = x_ref[...] + g2_ref[...] * _dot(a_buf[...], wd_ref[...])


def _ffn_call(h, x, ada, wg, wu, wd):
    tm = TILE_FF
    resident = lambda shape: pl.BlockSpec(shape, lambda i: (0, 0), pipeline_mode=pl.Buffered(1))
    return pl.pallas_call(
        _ffn_kernel,
        out_shape=jax.ShapeDtypeStruct((T_ALL, D_MODEL), F32),
        grid=(T_ALL // tm,),
        in_specs=[pl.BlockSpec((tm, D_MODEL), lambda i: (i, 0)),
                  pl.BlockSpec((tm, D_MODEL), lambda i: (i, 0)),
                  pl.BlockSpec((None, 1, D_MODEL), lambda i: (_mod_row(i, tm), 0, 5)),
                  resident((D_MODEL, D_FF)), resident((D_MODEL, D_FF)), resident((D_FF, D_MODEL))],
        out_specs=pl.BlockSpec((tm, D_MODEL), lambda i: (i, 0)),
        scratch_shapes=[pltpu.VMEM((tm, D_FF), BF16)],
        compiler_params=_cparams("arbitrary"),
        name="ffn_dense",
    )(h, x, ada, wg, wu, wd)


def _route_tables(meta):
    units = meta[:, 0, :N_EXPERTS]
    offs = meta[:, 1, :N_EXPERTS]
    cum = jnp.cumsum(units, axis=0)
    total = cum[-1]
    tiles_e = (total + GEMM_UNITS - 1) // GEMM_UNITS
    tile_end = jnp.cumsum(tiles_e)
    n_act = tile_end[-1]
    m = jnp.arange(N_GEMM_TILES)
    m_eff = jnp.minimum(m, jnp.maximum(n_act - 1, 0))
    expert = jnp.minimum(jnp.sum(tile_end[None, :] <= m_eff[:, None], axis=1), N_EXPERTS - 1)
    is_e = (expert[:, None] == jnp.arange(N_EXPERTS)[None, :]).astype(jnp.int32)
    pick = lambda per_expert: jnp.sum(is_e * per_expert[None, :], axis=1)
    first_q = (m_eff - pick(tile_end - tiles_e)) * GEMM_UNITS
    count = jnp.where(m < n_act, jnp.clip(pick(total) - first_q, 0, GEMM_UNITS), 0)
    q = first_q[:, None] + jnp.arange(GEMM_UNITS)[None, :]
    rows_of = lambda table: jnp.sum(is_e[:, None, :] * table[None, :, :], axis=2)
    cum_e, units_e, offs_e = rows_of(cum), rows_of(units), rows_of(offs)
    src_tile = jnp.minimum(jnp.sum(cum_e[:, None, :] <= q[:, :, None], axis=2), N_ROUTE_TILES - 1)
    is_t = (src_tile[:, :, None] == jnp.arange(N_ROUTE_TILES)[None, None, :]).astype(jnp.int32)
    at_tile = lambda per_tile: jnp.sum(is_t * per_tile[:, None, :], axis=2)
    uid = src_tile * UNITS_PER_TILE + at_tile(offs_e) + (q - at_tile(cum_e - units_e))
    uid = jnp.clip(uid, 0, N_UNITS - 1)
    i32 = jnp.int32
    return expert.astype(i32), count.astype(i32), uid.reshape(-1).astype(i32), n_act.reshape(1).astype(i32)


def _gemm_kernel(em_ref, cnt_ref, ul_ref, nact_ref, xs_hbm, gs_hbm, wg_ref, wu_ref, wd_ref, ys_hbm,
                 xbuf, gbuf, obuf, acc_ref, a_buf, sem_in, sem_out):
    del em_ref
    m = pl.program_id(0)
    j = pl.program_id(1)
    last_j = pl.num_programs(1) - 1
    n_act = nact_ref[0]
    active = m < n_act
    slot = m % 2

    def rows(r):
        return pl.ds(pl.multiple_of(r * SLOT_UNIT, SLOT_UNIT), SLOT_UNIT)

    def in_copies(mm, sl, r):
        uid = ul_ref[mm * GEMM_UNITS + r]
        return (pltpu.make_async_copy(xs_hbm.at[uid], xbuf.at[sl, rows(r)], sem_in.at[sl]),
                pltpu.make_async_copy(gs_hbm.at[uid], gbuf.at[sl, rows(r)], sem_in.at[sl]))

    def out_copy(mm, r):
        uid = ul_ref[mm * GEMM_UNITS + r]
        return pltpu.make_async_copy(obuf.at[rows(r)], ys_hbm.at[uid], sem_out.at[0])

    def start_in(mm, sl):
        n = cnt_ref[mm]

        def issue(r, carry):
            for cp in in_copies(mm, sl, r):
                cp.start()
            return carry

        def clear(r, carry):
            xbuf[sl, rows(r), :] = jnp.zeros((SLOT_UNIT, D_MODEL), BF16)
            gbuf[sl, rows(r), :] = jnp.zeros((SLOT_UNIT, LANES), F32)
            return carry

        lax.fori_loop(0, n, issue, 0)
        lax.fori_loop(n, GEMM_UNITS, clear, 0)

    def wait_in(mm, sl):
        def body(r, carry):
            for cp in in_copies(mm, sl, r):
                cp.wait()
            return carry

        lax.fori_loop(0, cnt_ref[mm], body, 0)

    def start_out(mm):
        def body(r, carry):
            out_copy(mm, r).start()
            return carry

        lax.fori_loop(0, cnt_ref[mm], body, 0)

    def wait_out(mm):
        def body(r, carry):
            out_copy(mm, r).wait()
            return carry

        lax.fori_loop(0, cnt_ref[mm], body, 0)

    @pl.when(jnp.logical_and(active, j == 0))
    def _():
        @pl.when(m == 0)
        def _():
            start_in(0, 0)

        wait_in(m, slot)

        @pl.when(m + 1 < n_act)
        def _():
            start_in(m + 1, 1 - slot)

    n_valid = cnt_ref[m]
    for quarter in range(1, GEMM_QUARTERS + 1):
        n_rows = quarter * GEMM_ROWS // GEMM_QUARTERS
        lo_units = (quarter - 1) * GEMM_UNITS // GEMM_QUARTERS
        hi_units = quarter * GEMM_UNITS // GEMM_QUARTERS

        @pl.when(jnp.logical_and(active, jnp.logical_and(n_valid > lo_units, n_valid <= hi_units)))
        def _(n_rows=n_rows):
            _swiglu_hidden(xbuf[slot, :n_rows, :], wg_ref, wu_ref, a_buf.at[:n_rows, :])
            part = _dot(a_buf[:n_rows, :], wd_ref[...])

            @pl.when(j == 0)
            def _():
                acc_ref[:n_rows, :] = part

            @pl.when(j > 0)
            def _():
                acc_ref[:n_rows, :] += part

    @pl.when(jnp.logical_and(active, j == last_j))
    def _():
        @pl.when(m > 0)
        def _():
            wait_out(m - 1)

        for quarter in range(GEMM_QUARTERS):
            rows_q = slice(quarter * GEMM_ROWS // GEMM_QUARTERS, (quarter + 1) * GEMM_ROWS // GEMM_QUARTERS)

            @pl.when(n_valid > quarter * GEMM_UNITS // GEMM_QUARTERS)
            def _(rows_q=rows_q):
                gate = gbuf[slot, rows_q, :]
                for cb in range(D_MODEL // LANES):
                    cols = slice(cb * LANES, (cb + 1) * LANES)
                    obuf[rows_q, cols] = (acc_ref[rows_q, cols] * gate).astype(BF16)

        start_out(m)

        @pl.when(m == n_act - 1)
        def _():
            wait_out(m)


def _gemm_call(expert, count, uids, n_act, xs, gs, wg, wu, wd):
    tf = TILE_FE
    n_j = D_FF_EXPERT // tf

    def w_col(m, j, em, cnt, ul, nact):
        return (em[m], 0, jnp.where(m < nact[0], j, n_j - 1))

    def w_row(m, j, em, cnt, ul, nact):
        return (em[m], jnp.where(m < nact[0], j, n_j - 1), 0)

    grid_spec = pltpu.PrefetchScalarGridSpec(
        num_scalar_prefetch=4,
        grid=(N_GEMM_TILES, n_j),
        in_specs=[pl.BlockSpec(memory_space=pl.ANY),
                  pl.BlockSpec(memory_space=pl.ANY),
                  pl.BlockSpec((None, D_MODEL, tf), w_col),
                  pl.BlockSpec((None, D_MODEL, tf), w_col),
                  pl.BlockSpec((None, tf, D_MODEL), w_row)],
        out_specs=pl.BlockSpec(memory_space=pl.ANY),
        scratch_shapes=[pltpu.VMEM((2, GEMM_ROWS, D_MODEL), BF16),
                        pltpu.VMEM((2, GEMM_ROWS, LANES), F32),
                        pltpu.VMEM((GEMM_ROWS, D_MODEL), BF16),
                        pltpu.VMEM((GEMM_ROWS, D_MODEL), F32),
                        pltpu.VMEM((GEMM_ROWS, tf), BF16),
                        pltpu.SemaphoreType.DMA((2,)),
                        pltpu.SemaphoreType.DMA((1,))])
    return pl.pallas_call(
        _gemm_kernel,
        out_shape=jax.ShapeDtypeStruct((N_UNITS, SLOT_UNIT, D_MODEL), BF16),
        grid_spec=grid_spec,
        input_output_aliases={4: 0},
        compiler_params=_cparams("arbitrary", "arbitrary"),
        name="moe_experts",
    )(expert, count, uids, n_act,
      xs.reshape(N_UNITS, SLOT_UNIT, D_MODEL), gs.reshape(N_UNITS, SLOT_UNIT, LANES), wg, wu, wd)


def _combine_kernel(ys_ref, route_ref, x_ref, g2_ref, fg_ref, o_ref):
    route = route_ref[...]
    s1 = route[:, 0:1].astype(jnp.int32)
    s2 = route[:, 1:2].astype(jnp.int32)
    scol = lax.broadcasted_iota(jnp.int32, (TILE_MIX, SLOTS_PER_TILE), 1)
    unperm = (jnp.where(scol == s1, 1.0, 0.0) + jnp.where(scol == s2, 1.0, 0.0)).astype(BF16)
    y = _dot(unperm, ys_ref[...])
    o_ref[...] = _rms(x_ref[...] + g2_ref[...] * y) * fg_ref[...]


def _combine_call(ys, route, x, ada, final_g, tile0, n_tok):
    tm = TILE_MIX
    return pl.pallas_call(
        _combine_kernel,
        out_shape=jax.ShapeDtypeStruct((n_tok, D_MODEL), F32),
        grid=(n_tok // tm,),
        in_specs=[pl.BlockSpec((SLOTS_PER_TILE, D_MODEL), lambda i: (i + tile0, 0)),
                  pl.BlockSpec((tm, LANES), lambda i: (i + tile0, 0)),
                  pl.BlockSpec((tm, D_MODEL), lambda i: (i + tile0, 0)),
                  pl.BlockSpec((None, 1, D_MODEL), lambda i: (_mod_row(i + tile0, tm), 0, 5)),
                  pl.BlockSpec((1, D_MODEL), lambda i: (0, 0))],
        out_specs=pl.BlockSpec((tm, D_MODEL), lambda i: (i, 0)),
        compiler_params=_cparams("arbitrary"),
        name="moe_combine",
    )(ys.reshape(N_UNITS * SLOT_UNIT, D_MODEL), route, x, ada, final_g)


def _rot_cols(w):
    q = QK_ROPE // 4
    return jnp.concatenate([-w[:, q:2 * q], w[:, :q], -w[:, 3 * q:], w[:, 2 * q:3 * q]], axis=1)


def _rope_tables():
    t = jnp.arange(DEC_SEQ)
    rows = (t // GRID_W).astype(F32)
    cols = (t % GRID_W).astype(F32)
    half = QK_ROPE // 2
    freqs = ROPE_BASE ** (-jnp.arange(0, half, 2, dtype=F32) / half)
    ang_r = rows[:, None] * freqs
    ang_c = cols[:, None] * freqs
    ang = jnp.concatenate([ang_r, ang_r, ang_c, ang_c], axis=-1)
    cos, sin = jnp.cos(ang), jnp.sin(ang)
    one, zero = jnp.ones_like(cos), jnp.zeros_like(cos)
    tq = jnp.stack([jnp.concatenate([one, zero], -1), jnp.concatenate([cos, sin], -1)])
    tc = jnp.stack([jnp.concatenate([one, one], -1), jnp.concatenate([cos, cos], -1)])
    ts = jnp.stack([jnp.concatenate([zero, zero], -1), jnp.concatenate([sin, sin], -1)])
    return tq, tc, ts


def _layer_weights(l, w_in, w_uq, w_ukv):
    wi = w_in[l]
    w_kpe = wi[:, OFF_U:OFF_U + QK_ROPE]
    w_rot = _rot_cols(w_kpe)
    w_in_r = jnp.concatenate([wi[:, :OFF_U], wi[:, OFF_U + QK_ROPE:], w_kpe, w_kpe, w_rot, w_rot], axis=1)
    heads = []
    for h in range(N_HEADS):
        blk = w_uq[l][:, h * (QK_NOPE + QK_ROPE):(h + 1) * (QK_NOPE + QK_ROPE)]
        heads += [blk, _rot_cols(blk[:, QK_NOPE:])]
    w_uq_r = jnp.concatenate(heads, axis=1)
    kv = w_ukv[l].reshape(KV_LORA, N_HEADS, QK_NOPE + V_DIM)
    w_uk = kv[:, :, :QK_NOPE].reshape(KV_LORA, N_HEADS * QK_NOPE)
    w_uv = kv[:, :, QK_NOPE:].reshape(KV_LORA, V_W)
    return w_in_r.astype(BF16), w_uq_r.astype(BF16), w_uk.astype(BF16), w_uv.astype(BF16)


def kernel(x_prompt, x_sample, c, cache_ckv, cache_kpe, c_ctx, norm1_g, norm2_g, w_ada, b_ada, w_in, q_norm_g,
           kv_norm_g, w_uq, w_ukv, w_pool, pool_scale, w_out, ffn_w_gate, ffn_w_up, ffn_w_down, moe_w_router,
           moe_w_gate, moe_w_up, moe_w_down, final_norm_g):
    x_ctx = x_prompt.reshape(T_CTX, D_MODEL)
    x_lat = x_sample.reshape(T_LAT, D_MODEL)
    cond = jnp.concatenate([c_ctx[None, :], c, jnp.zeros((COND_ROWS - 1 - DEC_BATCH, D_MODEL), F32)], axis=0)
    ada_all = _ada_call(cond, w_ada, b_ada)
    tq, tc, ts = _rope_tables()

    lw = [_layer_weights(l, w_in, w_uq, w_ukv) for l in range(DEPTH)]
    dup = jnp.concatenate([jnp.eye(QK_ROPE, dtype=BF16)] * 2, axis=1)
    kc_all, vc_all = _cache_call(cache_ckv, cache_kpe, jnp.stack([w[2] for w in lw]),
                                 jnp.stack([w[3] for w in lw]), dup)

    assert DEPTH == 2
    new_ckv, new_kpe = [], []
    for l in range(DEPTH):
        w_in_r, w_uq_r, w_uk, w_uv = lw[l]
        ada = ada_all[l].reshape(COND_ROWS, 1, 6 * D_MODEL)
        q, k, v, ckv, kpe, pooled = _in_call(
            x_ctx, x_lat, ada, norm1_g[l][None, :], w_in_r, q_norm_g[l][None, :], kv_norm_g[l][None, :], w_uq_r,
            w_uk, w_uv, tq, tc, ts, w_pool[l].astype(BF16), pool_scale[l][None, :])
        new_ckv.append(ckv[:T_CTX].reshape(BATCH, SEQ, KV_LORA))
        new_kpe.append(kpe[:T_CTX].reshape(BATCH, SEQ, QK_ROPE))
        attn_c, attn_l = _attn_call(q, k, v, kc_all[l], vc_all[l])
        j = l // 2
        if l % 2 == 0:
            x, h2 = _mix_call(attn_c, attn_l, pooled, x_ctx, x_lat, ada, norm2_g[l][None, :],
                              w_out[l].astype(BF16), None)
            x = _ffn_call(h2, x, ada, ffn_w_gate[j].astype(BF16), ffn_w_up[j].astype(BF16),
                          ffn_w_down[j].astype(BF16))
            x_ctx = x_lat = x
        else:
            w_r = jnp.pad(moe_w_router[j], ((0, 0), (0, LANES - N_EXPERTS))).astype(BF16)
            x, xs, gs, route, meta = _mix_call(attn_c, attn_l, pooled, x_ctx, x_lat, ada, norm2_g[l][None, :],
                                               w_out[l].astype(BF16), w_r)
            expert, count, uids, n_act = _route_tables(meta)
            ys = _gemm_call(expert, count, uids, n_act, xs, gs, moe_w_gate[j].astype(BF16),
                            moe_w_up[j].astype(BF16), moe_w_down[j].astype(BF16))
            fg = final_norm_g[None, :]
            y_prompt = _combine_call(ys, route, x, ada, fg, 0, T_CTX).reshape(BATCH, SEQ, D_MODEL)
            y_sample = _combine_call(ys, route, x, ada, fg, T_CTX // TILE_MIX, T_LAT).reshape(
                DEC_BATCH, DEC_SEQ, D_MODEL)

    return y_prompt, y_sample, jnp.stack(new_ckv, axis=1), jnp.stack(new_kpe, axis=1)
```

```python
import functools

import jax
import jax.numpy as jnp
from jax import lax
from jax.experimental import pallas as pl
from jax.experimental.pallas import tpu as pltpu

D_MODEL = 1024
BATCH = 32
SEQ = 256
DEPTH = 2
DEC_BATCH = 8
DEC_SEQ = 1024
PAST_LEN = 512
GRID_W = 64
N_HEADS = 4
QK_NOPE = 128
QK_ROPE = 64
V_DIM = 128
Q_LORA = 384
KV_LORA = 256
POOL_W = 512
POOL_GROUPS = 4
POOL_WINDOWS = (2, 4, 8, 16)
POOL_CH = POOL_W // POOL_GROUPS
D_FF = 2816
N_EXPERTS = 8
D_FF_EXPERT = 3584
ROPE_BASE = 10000.0
EPS = 1e-6

T_CTX = BATCH * SEQ
T_LAT = DEC_BATCH * DEC_SEQ
T_ALL = T_CTX + T_LAT

LANES = 128
HEAD_PAD = 256
QK_W = N_HEADS * HEAD_PAD
V_W = N_HEADS * V_DIM
OFF_CKV = Q_LORA
OFF_U = Q_LORA + KV_LORA
OFF_KA = OFF_U + POOL_W
OFF_KB = OFF_KA + LANES
N_PROJ = OFF_KB + LANES

TILE_IN = 1024
CHUNK = 256
HALO = 8
REGION = CHUNK + 2 * HALO
CTX_SEQS_PER_STEP = 4
LAT_Q_TILE = 1024
TILE_MIX = 512
TOP_K = 2
SLOT_UNIT = 16
UNITS_PER_TILE = TILE_MIX * TOP_K // SLOT_UNIT + N_EXPERTS
SLOTS_PER_TILE = UNITS_PER_TILE * SLOT_UNIT
SLOT_BLOCK = SLOTS_PER_TILE // 4
N_ROUTE_TILES = T_ALL // TILE_MIX
N_UNITS = N_ROUTE_TILES * UNITS_PER_TILE
GEMM_UNITS = 64
GEMM_ROWS = GEMM_UNITS * SLOT_UNIT
GEMM_QUARTERS = 4
N_GEMM_TILES = -(-N_UNITS // GEMM_UNITS) + N_EXPERTS
TILE_FE = D_FF_EXPERT // 2
FF_CHUNK = 256
COND_ROWS = 16
VMEM_LIMIT = 56 * 1024 * 1024

F32 = jnp.float32
BF16 = jnp.bfloat16


def _rms(x):
    return x * lax.rsqrt(jnp.mean(x * x, axis=-1, keepdims=True) + EPS)


def _dot(a, b):
    return jnp.dot(a, b, preferred_element_type=F32)


def _dot_nt(a, b):
    return lax.dot_general(a, b, (((1,), (1,)), ((), ())), preferred_element_type=F32)


def _silu(x):
    return x * (1.0 / (1.0 + jnp.exp(-x)))


def _cparams(*sem):
    return pltpu.CompilerParams(dimension_semantics=sem, vmem_limit_bytes=VMEM_LIMIT)


def _ada_kernel(cond_ref, w_ref, b_ref, o_ref):
    c = cond_ref[...]
    o_ref[...] = jnp.dot(_silu(c), w_ref[...], preferred_element_type=F32,
                         precision=lax.Precision.HIGHEST) + b_ref[...]


def _ada_call(cond, w_ada, b_ada):
    n_blk = 4
    bw = 6 * D_MODEL // n_blk
    return pl.pallas_call(
        _ada_kernel,
        out_shape=jax.ShapeDtypeStruct((DEPTH, COND_ROWS, 6 * D_MODEL), F32),
        grid=(DEPTH, n_blk),
        in_specs=[
            pl.BlockSpec((COND_ROWS, D_MODEL), lambda l, j: (0, 0)),
            pl.BlockSpec((None, D_MODEL, bw), lambda l, j: (l, 0, j)),
            pl.BlockSpec((None, 1, bw), lambda l, j: (l, 0, j)),
        ],
        out_specs=pl.BlockSpec((None, COND_ROWS, bw), lambda l, j: (l, 0, j)),
        compiler_params=_cparams("arbitrary", "arbitrary"),
        name="ada_params",
    )(cond, w_ada, b_ada.reshape(DEPTH, 1, 6 * D_MODEL))


def _cache_kernel(ckv_ref, kpe_ref, wuk_ref, wuv_ref, dup_ref, k_ref, v_ref):
    ckv = ckv_ref[...].astype(BF16)
    knope = _dot(ckv, wuk_ref[...])
    v_ref[...] = _dot(ckv, wuv_ref[...]).astype(BF16)
    kdup = _dot(kpe_ref[...].astype(BF16), dup_ref[...]).astype(BF16)
    for h in range(N_HEADS):
        k_ref[:, h * HEAD_PAD:h * HEAD_PAD + QK_NOPE] = knope[:, h * QK_NOPE:(h + 1) * QK_NOPE].astype(BF16)
        k_ref[:, h * HEAD_PAD + QK_NOPE:(h + 1) * HEAD_PAD] = kdup


def _cache_call(cache_ckv, cache_kpe, w_uk, w_uv, dup):
    n_tok = DEC_BATCH * PAST_LEN
    return pl.pallas_call(
        _cache_kernel,
        out_shape=(jax.ShapeDtypeStruct((DEPTH, n_tok, QK_W), BF16),
                   jax.ShapeDtypeStruct((DEPTH, n_tok, V_W), BF16)),
        grid=(DEPTH, DEC_BATCH),
        in_specs=[
            pl.BlockSpec((None, None, PAST_LEN, KV_LORA), lambda l, b: (b, l, 0, 0)),
            pl.BlockSpec((None, None, PAST_LEN, QK_ROPE), lambda l, b: (b, l, 0, 0)),
            pl.BlockSpec((None, KV_LORA, N_HEADS * QK_NOPE), lambda l, b: (l, 0, 0)),
            pl.BlockSpec((None, KV_LORA, V_W), lambda l, b: (l, 0, 0)),
            pl.BlockSpec((QK_ROPE, LANES), lambda l, b: (0, 0)),
        ],
        out_specs=(pl.BlockSpec((None, PAST_LEN, QK_W), lambda l, b: (l, b, 0)),
                   pl.BlockSpec((None, PAST_LEN, V_W), lambda l, b: (l, b, 0))),
        compiler_params=_cparams("arbitrary", "arbitrary"),
        name="cache_kv",
    )(cache_ckv, cache_kpe, w_uk, w_uv, dup)


def _in_kernel(xc_ref, xl_ref, sh_ref, sc_ref, g_ref, win_ref, qg_ref, kvg_ref, wuq_ref, wuk_ref, wuv_ref,
               tq_ref, tc_ref, ts_ref, wpool_ref, ps_ref,
               q_out, k_out, v_out, ckv_out, kpe_out, pool_out, u_scr):
    is_ctx = pl.program_id(0) < T_CTX // TILE_IN
    shift = sh_ref[...]
    scale1 = 1.0 + sc_ref[...]
    qk_scale = (QK_NOPE + QK_ROPE) ** -0.5
    n_chunks = TILE_IN // CHUNK
    zeros = jnp.zeros((HALO, POOL_W), F32)
    seq_len = jnp.where(is_ctx, SEQ, DEC_SEQ)
    row = lax.broadcasted_iota(jnp.int32, (CHUNK, POOL_CH), 0)

    def pool_chunk(c):
        base = c * REGION + HALO
        rows = slice(c * CHUNK, (c + 1) * CHUNK)
        t = row + jnp.where(is_ctx, 0, c * CHUNK)
        for g, w in enumerate(POOL_WINDOWS):
            cols = slice(g * POOL_CH, (g + 1) * POOL_CH)
            acc = u_scr[base - w // 2:base - w // 2 + CHUNK, cols]
            for j in range(-w // 2 + 1, w // 2):
                acc = acc + u_scr[base + j:base + j + CHUNK, cols]
            cnt = jnp.minimum(t + w // 2, seq_len) - jnp.maximum(t - w // 2, 0)
            pooled = acc / cnt.astype(F32) - u_scr[base:base + CHUNK, cols]
            lin = _dot(pooled.astype(BF16), wpool_ref[g]) * ps_ref[:, cols]
            pool_out[rows, cols] = lin.astype(BF16)

    for c in range(n_chunks):
        r0 = c * CHUNK
        rows = slice(r0, r0 + CHUNK)
        x = jnp.where(is_ctx, xc_ref[rows, :], xl_ref[rows, :])
        h = (_rms(x) * g_ref[...] * scale1 + shift).astype(BF16)
        proj = _dot(h, win_ref[...])

        qn = (_rms(proj[:, :Q_LORA]) * qg_ref[...]).astype(BF16)
        q = _dot(qn, wuq_ref[...]) * qk_scale
        tq = tq_ref[rows, :]
        for hd in range(N_HEADS):
            lo = hd * HEAD_PAD
            q_out[rows, lo:lo + QK_NOPE] = q[:, lo:lo + QK_NOPE].astype(BF16)
            q_out[rows, lo + QK_NOPE:lo + HEAD_PAD] = (q[:, lo + QK_NOPE:lo + HEAD_PAD] * tq).astype(BF16)

        ckv = _rms(proj[:, OFF_CKV:OFF_U]) * kvg_ref[...]
        ckv_out[rows, :] = ckv
        ckv_b = ckv.astype(BF16)
        knope = _dot(ckv_b, wuk_ref[...])
        v_out[rows, :] = _dot(ckv_b, wuv_ref[...]).astype(BF16)

        k_a = proj[:, OFF_KA:OFF_KB]
        k_b = proj[:, OFF_KB:N_PROJ]
        kpe_out[rows, :] = k_a[:, :QK_ROPE]
        kr = (k_a * tc_ref[rows, :] + k_b * ts_ref[rows, :]).astype(BF16)
        for hd in range(N_HEADS):
            lo = hd * HEAD_PAD
            k_out[rows, lo:lo + QK_NOPE] = knope[:, hd * QK_NOPE:(hd + 1) * QK_NOPE].astype(BF16)
            k_out[rows, lo + QK_NOPE:lo + HEAD_PAD] = kr

        u = proj[:, OFF_U:OFF_KA]
        base = c * REGION + HALO
        u_scr[base:base + CHUNK, :] = u
        if c == 0:
            u_scr[0:HALO, :] = zeros
        else:
            u_scr[base - 2 * HALO:base - HALO, :] = jnp.where(is_ctx, zeros, u[:HALO, :])
        if c == n_chunks - 1:
            u_scr[base + CHUNK:base + CHUNK + HALO, :] = zeros
        else:
            u_scr[base + CHUNK + HALO:base + CHUNK + 2 * HALO, :] = jnp.where(is_ctx, zeros, u[CHUNK - HALO:, :])
        if c > 0:
            pool_chunk(c - 1)
    pool_chunk(n_chunks - 1)


def _in_call(x_ctx, x_lat, ada, g1, w_in, qg, kvg, w_uq, w_uk, w_uv, tq, tc, ts, w_pool, pscale):
    n_tiles = T_ALL // TILE_IN
    n_ctx = T_CTX // TILE_IN
    lat_tile0 = n_ctx if x_lat.shape[0] == T_ALL else 0

    def cond_row(i):
        return jnp.maximum(i - n_ctx + 1, 0)

    def tab(i):
        return (jnp.minimum(jnp.maximum(i - n_ctx + 1, 0), 1), 0, 0)

    ctx_or_spare = lambda i: (jnp.minimum(i, n_ctx), 0)
    const2 = lambda i: (0, 0)
    return pl.pallas_call(
        _in_kernel,
        out_shape=(jax.ShapeDtypeStruct((T_ALL, QK_W), BF16),
                   jax.ShapeDtypeStruct((T_ALL, QK_W), BF16),
                   jax.ShapeDtypeStruct((T_ALL, V_W), BF16),
                   jax.ShapeDtypeStruct((T_CTX + TILE_IN, KV_LORA), F32),
                   jax.ShapeDtypeStruct((T_CTX + TILE_IN, QK_ROPE), F32),
                   jax.ShapeDtypeStruct((T_ALL, POOL_W), BF16)),
        grid=(n_tiles,),
        in_specs=[
            pl.BlockSpec((TILE_IN, D_MODEL), lambda i: (jnp.minimum(i, n_ctx - 1), 0)),
            pl.BlockSpec((TILE_IN, D_MODEL), lambda i: (jnp.maximum(i - n_ctx, 0) + lat_tile0, 0)),
            pl.BlockSpec((None, 1, D_MODEL), lambda i: (cond_row(i), 0, 0)),
            pl.BlockSpec((None, 1, D_MODEL), lambda i: (cond_row(i), 0, 1)),
            pl.BlockSpec((1, D_MODEL), const2),
            pl.BlockSpec((D_MODEL, N_PROJ), const2),
            pl.BlockSpec((1, Q_LORA), const2),
            pl.BlockSpec((1, KV_LORA), const2),
            pl.BlockSpec((Q_LORA, QK_W), const2),
            pl.BlockSpec((KV_LORA, N_HEADS * QK_NOPE), const2),
            pl.BlockSpec((KV_LORA, V_W), const2),
            pl.BlockSpec((None, TILE_IN, LANES), tab),
            pl.BlockSpec((None, TILE_IN, LANES), tab),
            pl.BlockSpec((None, TILE_IN, LANES), tab),
            pl.BlockSpec((POOL_GROUPS, POOL_CH, POOL_CH), lambda i: (0, 0, 0)),
            pl.BlockSpec((1, POOL_W), const2),
        ],
        out_specs=(pl.BlockSpec((TILE_IN, QK_W), lambda i: (i, 0)),
                   pl.BlockSpec((TILE_IN, QK_W), lambda i: (i, 0)),
                   pl.BlockSpec((TILE_IN, V_W), lambda i: (i, 0)),
                   pl.BlockSpec((TILE_IN, KV_LORA), ctx_or_spare),
                   pl.BlockSpec((TILE_IN, QK_ROPE), ctx_or_spare),
                   pl.BlockSpec((TILE_IN, POOL_W), lambda i: (i, 0))),
        scratch_shapes=[pltpu.VMEM((TILE_IN // CHUNK * REGION, POOL_W), F32)],
        compiler_params=_cparams("arbitrary"),
        name="in_proj",
    )(x_ctx, x_lat, ada, ada, g1, w_in, qg, kvg, w_uq, w_uk, w_uv, tq, tc, ts, w_pool, pscale)


def _softmax_pv(scores, values):
    m = scores[0].max(axis=-1, keepdims=True)
    for s in scores[1:]:
        m = jnp.maximum(m, s.max(axis=-1, keepdims=True))
    den = None
    out = None
    for s, v in zip(scores, values):
        p = jnp.exp(s - m)
        d = p.sum(axis=-1, keepdims=True)
        o = _dot(p.astype(BF16), v)
        den = d if den is None else den + d
        out = o if out is None else out + o
    return out / den


def _attn_ctx_kernel(q_ref, k_ref, v_ref, o_ref):
    for b in range(CTX_SEQS_PER_STEP):
        rows = slice(b * SEQ, (b + 1) * SEQ)
        for hd in range(N_HEADS):
            qk = slice(hd * HEAD_PAD, (hd + 1) * HEAD_PAD)
            vv = slice(hd * V_DIM, (hd + 1) * V_DIM)
            s = _dot_nt(q_ref[rows, qk], k_ref[rows, qk])
            o_ref[rows, vv] = _softmax_pv([s], [v_ref[rows, vv]]).astype(BF16)


def _attn_lat_kernel(q_ref, k_ref, v_ref, kc_ref, vc_ref, o_ref):
    for hd in range(N_HEADS):
        qk = slice(hd * HEAD_PAD, (hd + 1) * HEAD_PAD)
        vv = slice(hd * V_DIM, (hd + 1) * V_DIM)
        q = q_ref[:, qk]
        s1 = _dot_nt(q, k_ref[:, qk])
        s2 = _dot_nt(q, kc_ref[:, qk])
        o_ref[:, vv] = _softmax_pv([s1, s2], [v_ref[:, vv], vc_ref[:, vv]]).astype(BF16)


def _attn_call(q, k, v, kc, vc):
    attn_ctx = pl.pallas_call(
        _attn_ctx_kernel,
        out_shape=jax.ShapeDtypeStruct((T_CTX, V_W), BF16),
        grid=(BATCH // CTX_SEQS_PER_STEP,),
        in_specs=[pl.BlockSpec((CTX_SEQS_PER_STEP * SEQ, QK_W), lambda b: (b, 0)),
                  pl.BlockSpec((CTX_SEQS_PER_STEP * SEQ, QK_W), lambda b: (b, 0)),
                  pl.BlockSpec((CTX_SEQS_PER_STEP * SEQ, V_W), lambda b: (b, 0))],
        out_specs=pl.BlockSpec((CTX_SEQS_PER_STEP * SEQ, V_W), lambda b: (b, 0)),
        compiler_params=_cparams("arbitrary"),
        name="attn_ctx",
    )(q, k, v)

    tq = LAT_Q_TILE
    n_q = DEC_SEQ // tq
    lat0 = T_CTX // DEC_SEQ
    attn_lat = pl.pallas_call(
        _attn_lat_kernel,
        out_shape=jax.ShapeDtypeStruct((T_LAT, V_W), BF16),
        grid=(DEC_BATCH, n_q),
        in_specs=[pl.BlockSpec((tq, QK_W), lambda b, i: (T_CTX // tq + b * n_q + i, 0)),
                  pl.BlockSpec((DEC_SEQ, QK_W), lambda b, i: (lat0 + b, 0)),
                  pl.BlockSpec((DEC_SEQ, V_W), lambda b, i: (lat0 + b, 0)),
                  pl.BlockSpec((PAST_LEN, QK_W), lambda b, i: (b, 0)),
                  pl.BlockSpec((PAST_LEN, V_W), lambda b, i: (b, 0))],
        out_specs=pl.BlockSpec((tq, V_W), lambda b, i: (b * n_q + i, 0)),
        compiler_params=_cparams("arbitrary", "arbitrary"),
        name="attn_lat",
    )(q, k, v, kc, vc)
    return attn_ctx, attn_lat


def _swiglu_hidden(h, wg_ref, wu_ref, a_buf):
    width = a_buf.shape[1]
    for c0 in range(0, width, FF_CHUNK):
        cols = slice(c0, min(c0 + FF_CHUNK, width))
        a_buf[:, cols] = (_silu(_dot(h, wg_ref[:, cols])) * _dot(h, wu_ref[:, cols])).astype(BF16)


def _mix_kernel(with_router, attn_c_ref, attn_l_ref, pool_ref, xc_ref, xl_ref, g1_ref, sh2_ref, sc2_ref, n2_ref, wo_ref,
                *rest):
    if with_router:
        wr_ref, ltri_ref, utri_ref, x_out, xs_out, gs_out, route_out, meta_out = rest
    else:
        g2_ref, wg_ref, wu_ref, wd_ref, x_out, a_buf = rest
    is_ctx = pl.program_id(0) < T_CTX // TILE_MIX
    halves = []
    for r0 in range(0, TILE_MIX, TILE_MIX // 2):
        rows = slice(r0, r0 + TILE_MIX // 2)
        attn = jnp.where(is_ctx, attn_c_ref[rows, :], attn_l_ref[rows, :])
        y = _dot(attn, wo_ref[:V_W, :]) + _dot(pool_ref[rows, :], wo_ref[V_W:, :])
        x_new = jnp.where(is_ctx, xc_ref[rows, :], xl_ref[rows, :]) + g1_ref[...] * y
        x_out[rows, :] = x_new
        halves.append((_rms(x_new) * n2_ref[...] * (1.0 + sc2_ref[...]) + sh2_ref[...]).astype(BF16))
    h = jnp.concatenate(halves, axis=0)
    if not with_router:
        _swiglu_hidden(h, wg_ref, wu_ref, a_buf)
        x_out[...] += g2_ref[...] * _dot(a_buf[...], wd_ref[...])
        return

    logits = _dot(h, wr_ref[...])
    lane = lax.broadcasted_iota(jnp.int32, logits.shape, 1)
    neg = float(jnp.finfo(F32).min)
    lg = jnp.where(lane < N_EXPERTS, logits, neg)
    m1 = lg.max(axis=-1, keepdims=True)
    i1 = jnp.where(lg == m1, lane, LANES).min(axis=-1, keepdims=True)
    lg2 = jnp.where(lane == i1, neg, lg)
    m2 = lg2.max(axis=-1, keepdims=True)
    i2 = jnp.where(lg2 == m2, lane, LANES).min(axis=-1, keepdims=True)
    e = jnp.exp(m2 - m1)
    w1 = 1.0 / (1.0 + e)
    w2 = e / (1.0 + e)

    sel1 = lane == i1
    sel2 = lane == i2
    member = jnp.where(jnp.logical_or(sel1, sel2), 1.0, 0.0)
    rank = _dot(ltri_ref[...], member.astype(BF16))
    n_tok = member.sum(axis=0, keepdims=True)
    units = jnp.floor((n_tok + (SLOT_UNIT - 1)) * (1.0 / SLOT_UNIT))
    unit_off = _dot(jnp.broadcast_to(units, (8, LANES)).astype(BF16), utri_ref[...])[0:1, :]
    slot_of = SLOT_UNIT * unit_off + rank
    slot1 = jnp.where(sel1, slot_of, 0.0).sum(axis=-1, keepdims=True)
    slot2 = jnp.where(sel2, slot_of, 0.0).sum(axis=-1, keepdims=True)
    route = jnp.where(lane == 0, slot1, jnp.where(lane == 1, slot2,
                      jnp.where(lane == 2, w1, jnp.where(lane == 3, w2, 0.0))))
    route_out[...] = route
    sub = lax.broadcasted_iota(jnp.int32, (8, LANES), 0)
    meta_out[...] = jnp.where(sub == 0, units, jnp.where(sub == 1, unit_off, 0.0)).astype(jnp.int32)

    rt = route.T
    s1 = rt[0:1, :].astype(jnp.int32)
    s2 = rt[1:2, :].astype(jnp.int32)
    def slot_block(r0):
        srow = r0 + lax.broadcasted_iota(jnp.int32, (SLOT_BLOCK, TILE_MIX), 0)
        hit1 = srow == s1
        hit2 = srow == s2
        perm = (jnp.where(hit1, 1.0, 0.0) + jnp.where(hit2, 1.0, 0.0)).astype(BF16)
        gate = (jnp.where(hit1, rt[2:3, :], 0.0) + jnp.where(hit2, rt[3:4, :], 0.0)).sum(axis=-1, keepdims=True)
        return perm, gate

    starts = list(range(0, SLOTS_PER_TILE, SLOT_BLOCK))
    nxt = slot_block(starts[0])
    for k, r0 in enumerate(starts):
        perm, gate = nxt
        if k + 1 < len(starts):
            nxt = slot_block(starts[k + 1])
        xs_out[r0:r0 + SLOT_BLOCK, :] = _dot(perm, h).astype(BF16)
        gs_out[r0:r0 + SLOT_BLOCK, :] = jnp.broadcast_to(gate, (SLOT_BLOCK, LANES))


def _mod_row(i, tile):
    n_ctx = T_CTX // tile
    per_seq = DEC_SEQ // tile
    return jnp.where(i < n_ctx, 0, 1 + (i - n_ctx) // per_seq)


def _mix_call(attn_ctx, attn_lat, pooled, x_ctx, x_lat, ada, n2, w_out, w_router=None, ffn_w=None):
    with_router = w_router is not None
    assert with_router != (ffn_w is not None)
    tm = TILE_MIX
    n_ctx = T_CTX // tm
    lat_tile0 = n_ctx if x_lat.shape[0] == T_ALL else 0
    row = lambda i: (i, 0)
    const2 = lambda i: (0, 0)
    mod = lambda k: pl.BlockSpec((None, 1, D_MODEL), lambda i: (_mod_row(i, tm), 0, k))
    ctx_blk = lambda i: (jnp.minimum(i, n_ctx - 1), 0)
    in_specs = [pl.BlockSpec((tm, V_W), ctx_blk),
                pl.BlockSpec((tm, V_W), lambda i: (jnp.maximum(i - n_ctx, 0), 0)),
                pl.BlockSpec((tm, POOL_W), row),
                pl.BlockSpec((tm, D_MODEL), ctx_blk),
                pl.BlockSpec((tm, D_MODEL), lambda i: (jnp.maximum(i - n_ctx, 0) + lat_tile0, 0)),
                mod(2), mod(3), mod(4),
                pl.BlockSpec((1, D_MODEL), const2), pl.BlockSpec((D_MODEL, D_MODEL), const2)]
    args = [attn_ctx, attn_lat, pooled, x_ctx, x_lat, ada, ada, ada, n2, w_out]
    if with_router:
        t_i = jnp.arange(tm)
        ltri = (t_i[None, :] < t_i[:, None]).astype(BF16)
        l_i = jnp.arange(LANES)
        utri = (l_i[:, None] < l_i[None, :]).astype(BF16)
        in_specs += [pl.BlockSpec((D_MODEL, LANES), const2), pl.BlockSpec((tm, tm), const2),
                     pl.BlockSpec((LANES, LANES), const2)]
        args += [w_router, ltri, utri]
        out_shape = [jax.ShapeDtypeStruct((T_ALL, D_MODEL), F32),
                     jax.ShapeDtypeStruct((N_ROUTE_TILES * SLOTS_PER_TILE, D_MODEL), BF16),
                     jax.ShapeDtypeStruct((N_ROUTE_TILES * SLOTS_PER_TILE, LANES), F32),
                     jax.ShapeDtypeStruct((T_ALL, LANES), F32),
                     jax.ShapeDtypeStruct((N_ROUTE_TILES, 8, LANES), jnp.int32)]
        out_specs = [pl.BlockSpec((tm, D_MODEL), row), pl.BlockSpec((SLOTS_PER_TILE, D_MODEL), row),
                     pl.BlockSpec((SLOTS_PER_TILE, LANES), row), pl.BlockSpec((tm, LANES), row),
                     pl.BlockSpec((None, 8, LANES), lambda i: (i, 0, 0))]
        scratch = []
    else:
        resident = lambda shape: pl.BlockSpec(shape, const2, pipeline_mode=pl.Buffered(1))
        in_specs += [mod(5), resident((D_MODEL, D_FF)), resident((D_MODEL, D_FF)), resident((D_FF, D_MODEL))]
        args += [ada, *ffn_w]
        out_shape = [jax.ShapeDtypeStruct((T_ALL, D_MODEL), F32)]
        out_specs = [pl.BlockSpec((tm, D_MODEL), row)]
        scratch = [pltpu.VMEM((tm, D_FF), BF16)]
    return pl.pallas_call(
        functools.partial(_mix_kernel, with_router),
        out_shape=tuple(out_shape),
        grid=(T_ALL // tm,),
        in_specs=in_specs,
        out_specs=tuple(out_specs),
        scratch_shapes=scratch,
        compiler_params=_cparams("arbitrary"),
        name="mix_router" if with_router else "mix_ffn",
    )(*args)


def _route_tables(meta):
    units = meta[:, 0, :N_EXPERTS]
    offs = meta[:, 1, :N_EXPERTS]
    cum = jnp.cumsum(units, axis=0)
    total = cum[-1]
    tiles_e = (total + GEMM_UNITS - 1) // GEMM_UNITS
    tile_end = jnp.cumsum(tiles_e)
    n_act = tile_end[-1]
    m = jnp.arange(N_GEMM_TILES)
    m_eff = jnp.minimum(m, jnp.maximum(n_act - 1, 0))
    expert = jnp.minimum(jnp.sum(tile_end[None, :] <= m_eff[:, None], axis=1), N_EXPERTS - 1)
    is_e = (expert[:, None] == jnp.arange(N_EXPERTS)[None, :]).astype(jnp.int32)
    pick = lambda per_expert: jnp.sum(is_e * per_expert[None, :], axis=1)
    first_q = (m_eff - pick(tile_end - tiles_e)) * GEMM_UNITS
    count = jnp.where(m < n_act, jnp.clip(pick(total) - first_q, 0, GEMM_UNITS), 0)
    q = first_q[:, None] + jnp.arange(GEMM_UNITS)[None, :]
    rows_of = lambda table: jnp.sum(is_e[:, None, :] * table[None, :, :], axis=2)
    cum_e, units_e, offs_e = rows_of(cum), rows_of(units), rows_of(offs)
    src_tile = jnp.minimum(jnp.sum(cum_e[:, None, :] <= q[:, :, None], axis=2), N_ROUTE_TILES - 1)
    is_t = (src_tile[:, :, None] == jnp.arange(N_ROUTE_TILES)[None, None, :]).astype(jnp.int32)
    at_tile = lambda per_tile: jnp.sum(is_t * per_tile[:, None, :], axis=2)
    uid = src_tile * UNITS_PER_TILE + at_tile(offs_e) + (q - at_tile(cum_e - units_e))
    uid = jnp.clip(uid, 0, N_UNITS - 1)
    i32 = jnp.int32
    return expert.astype(i32), count.astype(i32), uid.reshape(-1).astype(i32), n_act.reshape(1).astype(i32)


def _gemm_kernel(em_ref, cnt_ref, ul_ref, nact_ref, xs_hbm, gs_hbm, wg_ref, wu_ref, wd_ref, ys_hbm,
                 xbuf, gbuf, obuf, acc_ref, a_buf, sem_in, sem_out):
    del em_ref
    m = pl.program_id(0)
    j = pl.program_id(1)
    last_j = pl.num_programs(1) - 1
    n_act = nact_ref[0]
    active = m < n_act
    slot = m % 2

    def rows(r):
        return pl.ds(pl.multiple_of(r * SLOT_UNIT, SLOT_UNIT), SLOT_UNIT)

    def in_copies(mm, sl, r):
        uid = ul_ref[mm * GEMM_UNITS + r]
        return (pltpu.make_async_copy(xs_hbm.at[uid], xbuf.at[sl, rows(r)], sem_in.at[sl]),
                pltpu.make_async_copy(gs_hbm.at[uid], gbuf.at[sl, rows(r)], sem_in.at[sl]))

    def out_copy(mm, r):
        uid = ul_ref[mm * GEMM_UNITS + r]
        return pltpu.make_async_copy(obuf.at[rows(r)], ys_hbm.at[uid], sem_out.at[0])

    def start_in(mm, sl):
        n = cnt_ref[mm]

        def issue(r, carry):
            for cp in in_copies(mm, sl, r):
                cp.start()
            return carry

        def clear(r, carry):
            xbuf[sl, rows(r), :] = jnp.zeros((SLOT_UNIT, D_MODEL), BF16)
            gbuf[sl, rows(r), :] = jnp.zeros((SLOT_UNIT, LANES), F32)
            return carry

        lax.fori_loop(0, n, issue, 0)
        lax.fori_loop(n, GEMM_UNITS, clear, 0)

    def wait_in(mm, sl):
        def body(r, carry):
            for cp in in_copies(mm, sl, r):
                cp.wait()
            return carry

        lax.fori_loop(0, cnt_ref[mm], body, 0)

    def start_out(mm):
        def body(r, carry):
            out_copy(mm, r).start()
            return carry

        lax.fori_loop(0, cnt_ref[mm], body, 0)

    def wait_out(mm):
        def body(r, carry):
            out_copy(mm, r).wait()
            return carry

        lax.fori_loop(0, cnt_ref[mm], body, 0)

    @pl.when(jnp.logical_and(active, j == 0))
    def _():
        @pl.when(m == 0)
        def _():
            start_in(0, 0)

        wait_in(m, slot)

        @pl.when(m + 1 < n_act)
        def _():
            start_in(m + 1, 1 - slot)

    @pl.when(jnp.logical_and(jnp.logical_and(active, j == last_j), m > 0))
    def _():
        wait_out(m - 1)

    n_valid = cnt_ref[m]
    for quarter in range(1, GEMM_QUARTERS + 1):
        n_rows = quarter * GEMM_ROWS // GEMM_QUARTERS
        lo_units = (quarter - 1) * GEMM_UNITS // GEMM_QUARTERS
        hi_units = quarter * GEMM_UNITS // GEMM_QUARTERS

        @pl.when(jnp.logical_and(active, jnp.logical_and(n_valid > lo_units, n_valid <= hi_units)))
        def _(n_rows=n_rows):
            _swiglu_hidden(xbuf[slot, :n_rows, :], wg_ref, wu_ref, a_buf.at[:n_rows, :])
            part = _dot(a_buf[:n_rows, :], wd_ref[...])

            @pl.when(j == 0)
            def _():
                acc_ref[:n_rows, :] = part

            @pl.when(jnp.logical_and(j > 0, j < last_j))
            def _():
                acc_ref[:n_rows, :] += part

            @pl.when(j == last_j)
            def _():
                gate = gbuf[slot, :n_rows, :]
                for cb in range(D_MODEL // LANES):
                    cols = slice(cb * LANES, (cb + 1) * LANES)
                    obuf[:n_rows, cols] = ((acc_ref[:n_rows, cols] + part[:, cols]) * gate).astype(BF16)

    @pl.when(jnp.logical_and(active, j == last_j))
    def _():
        start_out(m)

        @pl.when(m == n_act - 1)
        def _():
            wait_out(m)


def _gemm_call(expert, count, uids, n_act, xs, gs, wg, wu, wd):
    tf = TILE_FE
    n_j = D_FF_EXPERT // tf
    assert n_j >= 2

    def w_col(m, j, em, cnt, ul, nact):
        return (em[m], 0, jnp.where(m < nact[0], j, n_j - 1))

    def w_row(m, j, em, cnt, ul, nact):
        return (em[m], jnp.where(m < nact[0], j, n_j - 1), 0)

    grid_spec = pltpu.PrefetchScalarGridSpec(
        num_scalar_prefetch=4,
        grid=(N_GEMM_TILES, n_j),
        in_specs=[pl.BlockSpec(memory_space=pl.ANY),
                  pl.BlockSpec(memory_space=pl.ANY),
                  pl.BlockSpec((None, D_MODEL, tf), w_col),
                  pl.BlockSpec((None, D_MODEL, tf), w_col),
                  pl.BlockSpec((None, tf, D_MODEL), w_row)],
        out_specs=pl.BlockSpec(memory_space=pl.ANY),
        scratch_shapes=[pltpu.VMEM((2, GEMM_ROWS, D_MODEL), BF16),
                        pltpu.VMEM((2, GEMM_ROWS, LANES), F32),
                        pltpu.VMEM((GEMM_ROWS, D_MODEL), BF16),
                        pltpu.VMEM((GEMM_ROWS, D_MODEL), F32),
                        pltpu.VMEM((GEMM_ROWS, tf), BF16),
                        pltpu.SemaphoreType.DMA((2,)),
                        pltpu.SemaphoreType.DMA((1,))])
    return pl.pallas_call(
        _gemm_kernel,
        out_shape=jax.ShapeDtypeStruct((N_UNITS, SLOT_UNIT, D_MODEL), BF16),
        grid_spec=grid_spec,
        input_output_aliases={4: 0},
        compiler_params=_cparams("arbitrary", "arbitrary"),
        name="moe_experts",
    )(expert, count, uids, n_act,
      xs.reshape(N_UNITS, SLOT_UNIT, D_MODEL), gs.reshape(N_UNITS, SLOT_UNIT, LANES), wg, wu, wd)


def _combine_kernel(ys_ref, route_ref, x_ref, g2_ref, fg_ref, o_ref):
    route = route_ref[...]
    s1 = route[:, 0:1].astype(jnp.int32)
    s2 = route[:, 1:2].astype(jnp.int32)
    scol = lax.broadcasted_iota(jnp.int32, (TILE_MIX, SLOTS_PER_TILE), 1)
    unperm = (jnp.where(scol == s1, 1.0, 0.0) + jnp.where(scol == s2, 1.0, 0.0)).astype(BF16)
    y = _dot(unperm, ys_ref[...])
    o_ref[...] = _rms(x_ref[...] + g2_ref[...] * y) * fg_ref[...]


def _combine_call(ys, route, x, ada, final_g, tile0, n_tok):
    tm = TILE_MIX
    return pl.pallas_call(
        _combine_kernel,
        out_shape=jax.ShapeDtypeStruct((n_tok, D_MODEL), F32),
        grid=(n_tok // tm,),
        in_specs=[pl.BlockSpec((SLOTS_PER_TILE, D_MODEL), lambda i: (i + tile0, 0)),
                  pl.BlockSpec((tm, LANES), lambda i: (i + tile0, 0)),
                  pl.BlockSpec((tm, D_MODEL), lambda i: (i + tile0, 0)),
                  pl.BlockSpec((None, 1, D_MODEL), lambda i: (_mod_row(i + tile0, tm), 0, 5)),
                  pl.BlockSpec((1, D_MODEL), lambda i: (0, 0))],
        out_specs=pl.BlockSpec((tm, D_MODEL), lambda i: (i, 0)),
        compiler_params=_cparams("arbitrary"),
        name="moe_combine",
    )(ys.reshape(N_UNITS * SLOT_UNIT, D_MODEL), route, x, ada, final_g)


def _rot_cols(w):
    q = QK_ROPE // 4
    return jnp.concatenate([-w[:, q:2 * q], w[:, :q], -w[:, 3 * q:], w[:, 2 * q:3 * q]], axis=1)


def _rope_tables():
    t = jnp.arange(DEC_SEQ)
    rows = (t // GRID_W).astype(F32)
    cols = (t % GRID_W).astype(F32)
    half = QK_ROPE // 2
    freqs = ROPE_BASE ** (-jnp.arange(0, half, 2, dtype=F32) / half)
    ang_r = rows[:, None] * freqs
    ang_c = cols[:, None] * freqs
    ang = jnp.concatenate([ang_r, ang_r, ang_c, ang_c], axis=-1)
    cos, sin = jnp.cos(ang), jnp.sin(ang)
    one, zero = jnp.ones_like(cos), jnp.zeros_like(cos)
    tq = jnp.stack([jnp.concatenate([one, zero], -1), jnp.concatenate([cos, sin], -1)])
    tc = jnp.stack([jnp.concatenate([one, one], -1), jnp.concatenate([cos, cos], -1)])
    ts = jnp.stack([jnp.concatenate([zero, zero], -1), jnp.concatenate([sin, sin], -1)])
    return tq, tc, ts


def _layer_weights(l, w_in, w_uq, w_ukv):
    wi = w_in[l]
    w_kpe = wi[:, OFF_U:OFF_U + QK_ROPE]
    w_rot = _rot_cols(w_kpe)
    w_in_r = jnp.concatenate([wi[:, :OFF_U], wi[:, OFF_U + QK_ROPE:], w_kpe, w_kpe, w_rot, w_rot], axis=1)
    heads = []
    for h in range(N_HEADS):
        blk = w_uq[l][:, h * (QK_NOPE + QK_ROPE):(h + 1) * (QK_NOPE + QK_ROPE)]
        heads += [blk, _rot_cols(blk[:, QK_NOPE:])]
    w_uq_r = jnp.concatenate(heads, axis=1)
    kv = w_ukv[l].reshape(KV_LORA, N_HEADS, QK_NOPE + V_DIM)
    w_uk = kv[:, :, :QK_NOPE].reshape(KV_LORA, N_HEADS * QK_NOPE)
    w_uv = kv[:, :, QK_NOPE:].reshape(KV_LORA, V_W)
    return w_in_r.astype(BF16), w_uq_r.astype(BF16), w_uk.astype(BF16), w_uv.astype(BF16)


def kernel(x_prompt, x_sample, c, cache_ckv, cache_kpe, c_ctx, norm1_g, norm2_g, w_ada, b_ada, w_in, q_norm_g,
           kv_norm_g, w_uq, w_ukv, w_pool, pool_scale, w_out, ffn_w_gate, ffn_w_up, ffn_w_down, moe_w_router,
           moe_w_gate, moe_w_up, moe_w_down, final_norm_g):
    x_ctx = x_prompt.reshape(T_CTX, D_MODEL)
    x_lat = x_sample.reshape(T_LAT, D_MODEL)
    cond = jnp.concatenate([c_ctx[None, :], c, jnp.zeros((COND_ROWS - 1 - DEC_BATCH, D_MODEL), F32)], axis=0)
    ada_all = _ada_call(cond, w_ada, b_ada)
    tq, tc, ts = _rope_tables()

    lw = [_layer_weights(l, w_in, w_uq, w_ukv) for l in range(DEPTH)]
    dup = jnp.concatenate([jnp.eye(QK_ROPE, dtype=BF16)] * 2, axis=1)
    kc_all, vc_all = _cache_call(cache_ckv, cache_kpe, jnp.stack([w[2] for w in lw]),
                                 jnp.stack([w[3] for w in lw]), dup)

    assert DEPTH == 2
    new_ckv, new_kpe = [], []
    for l in range(DEPTH):
        w_in_r, w_uq_r, w_uk, w_uv = lw[l]
        ada = ada_all[l].reshape(COND_ROWS, 1, 6 * D_MODEL)
        q, k, v, ckv, kpe, pooled = _in_call(
            x_ctx, x_lat, ada, norm1_g[l][None, :], w_in_r, q_norm_g[l][None, :], kv_norm_g[l][None, :], w_uq_r,
            w_uk, w_uv, tq, tc, ts, w_pool[l].astype(BF16), pool_scale[l][None, :])
        new_ckv.append(ckv[:T_CTX].reshape(BATCH, SEQ, KV_LORA))
        new_kpe.append(kpe[:T_CTX].reshape(BATCH, SEQ, QK_ROPE))
        attn_c, attn_l = _attn_call(q, k, v, kc_all[l], vc_all[l])
        j = l // 2
        if l % 2 == 0:
            ffn_w = (ffn_w_gate[j].astype(BF16), ffn_w_up[j].astype(BF16), ffn_w_down[j].astype(BF16))
            (x,) = _mix_call(attn_c, attn_l, pooled, x_ctx, x_lat, ada, norm2_g[l][None, :],
                             w_out[l].astype(BF16), ffn_w=ffn_w)
            x_ctx = x_lat = x
        else:
            w_r = jnp.pad(moe_w_router[j], ((0, 0), (0, LANES - N_EXPERTS))).astype(BF16)
            x, xs, gs, route, meta = _mix_call(attn_c, attn_l, pooled, x_ctx, x_lat, ada, norm2_g[l][None, :],
                                               w_out[l].astype(BF16), w_router=w_r)
            expert, count, uids, n_act = _route_tables(meta)
            ys = _gemm_call(expert, count, uids, n_act, xs, gs, moe_w_gate[j].astype(BF16),
                            moe_w_up[j].astype(BF16), moe_w_down[j].astype(BF16))
            fg = final_norm_g[None, :]
            y_prompt = _combine_call(ys, route, x, ada, fg, 0, T_CTX).reshape(BATCH, SEQ, D_MODEL)
            y_sample = _combine_call(ys, route, x, ada, fg, T_CTX // TILE_MIX, T_LAT).reshape(
                DEC_BATCH, DEC_SEQ, D_MODEL)

    return y_prompt, y_sample, jnp.stack(new_ckv, axis=1), jnp.stack(new_kpe, axis=1)
```

```python
import functools

import jax
import jax.numpy as jnp
from jax import lax
from jax.experimental import pallas as pl
from jax.experimental.pallas import tpu as pltpu

D_MODEL = 1024
BATCH = 32
SEQ = 256
DEPTH = 2
DEC_BATCH = 8
DEC_SEQ = 1024
PAST_LEN = 512
GRID_W = 64
N_HEADS = 4
QK_NOPE = 128
QK_ROPE = 64
V_DIM = 128
Q_LORA = 384
KV_LORA = 256
POOL_W = 512
POOL_GROUPS = 4
POOL_WINDOWS = (2, 4, 8, 16)
POOL_CH = POOL_W // POOL_GROUPS
D_FF = 2816
N_EXPERTS = 8
D_FF_EXPERT = 3584
ROPE_BASE = 10000.0
EPS = 1e-6

T_CTX = BATCH * SEQ
T_LAT = DEC_BATCH * DEC_SEQ
T_ALL = T_CTX + T_LAT

LANES = 128
HEAD_PAD = 256
QK_W = N_HEADS * HEAD_PAD
V_W = N_HEADS * V_DIM
OFF_CKV = Q_LORA
OFF_U = Q_LORA + KV_LORA
OFF_KA = OFF_U + POOL_W
OFF_KB = OFF_KA + LANES
N_PROJ = OFF_KB + LANES

TILE_IN = 1024
CHUNK = 256
HALO = 8
REGION = CHUNK + 2 * HALO
CTX_SEQS_PER_STEP = 4
LAT_Q_TILE = 1024
TILE_MIX = 512
TOP_K = 2
SLOT_UNIT = 16
UNITS_PER_TILE = TILE_MIX * TOP_K // SLOT_UNIT + N_EXPERTS
SLOTS_PER_TILE = UNITS_PER_TILE * SLOT_UNIT
SLOT_BLOCK = SLOTS_PER_TILE // 4
N_ROUTE_TILES = T_ALL // TILE_MIX
N_UNITS = N_ROUTE_TILES * UNITS_PER_TILE
GEMM_UNITS = 64
GEMM_ROWS = GEMM_UNITS * SLOT_UNIT
GEMM_QUARTERS = 4
N_GEMM_TILES = -(-N_UNITS // GEMM_UNITS) + N_EXPERTS
TILE_FE = D_FF_EXPERT // 2
FF_CHUNK = 256
COND_ROWS = 16
VMEM_LIMIT = 56 * 1024 * 1024

F32 = jnp.float32
BF16 = jnp.bfloat16


def _rms(x):
    return x * lax.rsqrt(jnp.mean(x * x, axis=-1, keepdims=True) + EPS)


def _dot(a, b):
    return jnp.dot(a, b, preferred_element_type=F32)


def _dot_nt(a, b):
    return lax.dot_general(a, b, (((1,), (1,)), ((), ())), preferred_element_type=F32)


def _silu(x):
    return x * (1.0 / (1.0 + jnp.exp(-x)))


def _cparams(*sem):
    return pltpu.CompilerParams(dimension_semantics=sem, vmem_limit_bytes=VMEM_LIMIT)


def _ada_kernel(cond_ref, w_ref, b_ref, o_ref):
    c = cond_ref[...]
    o_ref[...] = jnp.dot(_silu(c), w_ref[...], preferred_element_type=F32,
                         precision=lax.Precision.HIGHEST) + b_ref[...]


def _ada_call(cond, w_ada, b_ada):
    n_blk = 4
    bw = 6 * D_MODEL // n_blk
    return pl.pallas_call(
        _ada_kernel,
        out_shape=jax.ShapeDtypeStruct((DEPTH, COND_ROWS, 6 * D_MODEL), F32),
        grid=(DEPTH, n_blk),
        in_specs=[
            pl.BlockSpec((COND_ROWS, D_MODEL), lambda l, j: (0, 0)),
            pl.BlockSpec((None, D_MODEL, bw), lambda l, j: (l, 0, j)),
            pl.BlockSpec((None, 1, bw), lambda l, j: (l, 0, j)),
        ],
        out_specs=pl.BlockSpec((None, COND_ROWS, bw), lambda l, j: (l, 0, j)),
        compiler_params=_cparams("arbitrary", "arbitrary"),
        name="ada_params",
    )(cond, w_ada, b_ada.reshape(DEPTH, 1, 6 * D_MODEL))


def _cache_kernel(ckv_ref, kpe_ref, wuk_ref, wuv_ref, dup_ref, k_ref, v_ref):
    ckv = ckv_ref[...].astype(BF16)
    knope = _dot(ckv, wuk_ref[...])
    v_ref[...] = _dot(ckv, wuv_ref[...]).astype(BF16)
    kdup = _dot(kpe_ref[...].astype(BF16), dup_ref[...]).astype(BF16)
    for h in range(N_HEADS):
        k_ref[:, h * HEAD_PAD:h * HEAD_PAD + QK_NOPE] = knope[:, h * QK_NOPE:(h + 1) * QK_NOPE].astype(BF16)
        k_ref[:, h * HEAD_PAD + QK_NOPE:(h + 1) * HEAD_PAD] = kdup


def _cache_call(cache_ckv, cache_kpe, w_uk, w_uv, dup):
    n_tok = DEC_BATCH * PAST_LEN
    return pl.pallas_call(
        _cache_kernel,
        out_shape=(jax.ShapeDtypeStruct((DEPTH, n_tok, QK_W), BF16),
                   jax.ShapeDtypeStruct((DEPTH, n_tok, V_W), BF16)),
        grid=(DEPTH, DEC_BATCH),
        in_specs=[
            pl.BlockSpec((None, None, PAST_LEN, KV_LORA), lambda l, b: (b, l, 0, 0)),
            pl.BlockSpec((None, None, PAST_LEN, QK_ROPE), lambda l, b: (b, l, 0, 0)),
            pl.BlockSpec((None, KV_LORA, N_HEADS * QK_NOPE), lambda l, b: (l, 0, 0)),
            pl.BlockSpec((None, KV_LORA, V_W), lambda l, b: (l, 0, 0)),
            pl.BlockSpec((QK_ROPE, LANES), lambda l, b: (0, 0)),
        ],
        out_specs=(pl.BlockSpec((None, PAST_LEN, QK_W), lambda l, b: (l, b, 0)),
                   pl.BlockSpec((None, PAST_LEN, V_W), lambda l, b: (l, b, 0))),
        compiler_params=_cparams("arbitrary", "arbitrary"),
        name="cache_kv",
    )(cache_ckv, cache_kpe, w_uk, w_uv, dup)


def _in_kernel(n_prev, xc_ref, xl_ref, sh_ref, sc_ref, g_ref, win_ref, qg_ref, kvg_ref, wuq_ref, wuk_ref, wuv_ref,
               tq_ref, tc_ref, ts_ref, wpool_ref, ps_ref, *rest):
    if n_prev:
        prev_ckv_ref, prev_kpe_ref, *rest = rest
    q_out, k_out, v_out, ckv_out, kpe_out, pool_out, u_scr = rest
    is_ctx = pl.program_id(0) >= (T_ALL - T_CTX) // TILE_IN
    if n_prev:
        ckv_out[:, :n_prev] = prev_ckv_ref[...]
        kpe_out[:, :n_prev] = prev_kpe_ref[...]
    shift = sh_ref[...]
    scale1 = 1.0 + sc_ref[...]
    qk_scale = (QK_NOPE + QK_ROPE) ** -0.5
    n_chunks = TILE_IN // CHUNK
    zeros = jnp.zeros((HALO, POOL_W), F32)
    seq_len = jnp.where(is_ctx, SEQ, DEC_SEQ)
    row = lax.broadcasted_iota(jnp.int32, (CHUNK, POOL_CH), 0)

    def pool_chunk(c):
        base = c * REGION + HALO
        rows = slice(c * CHUNK, (c + 1) * CHUNK)
        t = row + jnp.where(is_ctx, 0, c * CHUNK)
        for g, w in enumerate(POOL_WINDOWS):
            cols = slice(g * POOL_CH, (g + 1) * POOL_CH)
            acc = u_scr[base - w // 2:base - w // 2 + CHUNK, cols]
            for j in range(-w // 2 + 1, w // 2):
                acc = acc + u_scr[base + j:base + j + CHUNK, cols]
            cnt = jnp.minimum(t + w // 2, seq_len) - jnp.maximum(t - w // 2, 0)
            pooled = acc / cnt.astype(F32) - u_scr[base:base + CHUNK, cols]
            lin = _dot(pooled.astype(BF16), wpool_ref[g]) * ps_ref[:, cols]
            pool_out[rows, cols] = lin.astype(BF16)

    for c in range(n_chunks):
        r0 = c * CHUNK
        rows = slice(r0, r0 + CHUNK)
        x = jnp.where(is_ctx, xc_ref[rows, :], xl_ref[rows, :])
        h = (_rms(x) * g_ref[...] * scale1 + shift).astype(BF16)
        proj = _dot(h, win_ref[...])

        qn = (_rms(proj[:, :Q_LORA]) * qg_ref[...]).astype(BF16)
        q = _dot(qn, wuq_ref[...]) * qk_scale
        tq = tq_ref[rows, :]
        for hd in range(N_HEADS):
            lo = hd * HEAD_PAD
            q_out[rows, lo:lo + QK_NOPE] = q[:, lo:lo + QK_NOPE].astype(BF16)
            q_out[rows, lo + QK_NOPE:lo + HEAD_PAD] = (q[:, lo + QK_NOPE:lo + HEAD_PAD] * tq).astype(BF16)

        ckv = _rms(proj[:, OFF_CKV:OFF_U]) * kvg_ref[...]
        ckv_out[c, n_prev] = ckv
        ckv_b = ckv.astype(BF16)
        knope = _dot(ckv_b, wuk_ref[...])
        v_out[rows, :] = _dot(ckv_b, wuv_ref[...]).astype(BF16)

        k_a = proj[:, OFF_KA:OFF_KB]
        k_b = proj[:, OFF_KB:N_PROJ]
        kpe_out[c, n_prev] = k_a[:, :QK_ROPE]
        kr = (k_a * tc_ref[rows, :] + k_b * ts_ref[rows, :]).astype(BF16)
        for hd in range(N_HEADS):
            lo = hd * HEAD_PAD
            k_out[rows, lo:lo + QK_NOPE] = knope[:, hd * QK_NOPE:(hd + 1) * QK_NOPE].astype(BF16)
            k_out[rows, lo + QK_NOPE:lo + HEAD_PAD] = kr

        u = proj[:, OFF_U:OFF_KA]
        base = c * REGION + HALO
        u_scr[base:base + CHUNK, :] = u
        if c == 0:
            u_scr[0:HALO, :] = zeros
        else:
            u_scr[base - 2 * HALO:base - HALO, :] = jnp.where(is_ctx, zeros, u[:HALO, :])
        if c == n_chunks - 1:
            u_scr[base + CHUNK:base + CHUNK + HALO, :] = zeros
        else:
            u_scr[base + CHUNK + HALO:base + CHUNK + 2 * HALO, :] = jnp.where(is_ctx, zeros, u[CHUNK - HALO:, :])
        if c > 0:
            pool_chunk(c - 1)
    pool_chunk(n_chunks - 1)


def _in_call(x_ctx, x_lat, ada, g1, w_in, qg, kvg, w_uq, w_uk, w_uv, tq, tc, ts, w_pool, pscale, prev_ckv, prev_kpe):
    assert CHUNK == SEQ
    n_tiles = T_ALL // TILE_IN
    n_ctx = T_CTX // TILE_IN
    n_prev = 0 if prev_ckv is None else prev_ckv.shape[1]
    seqs = TILE_IN // SEQ
    lat_tile0 = n_ctx if x_lat.shape[0] == T_ALL else 0

    def tile(i):
        return jnp.where(i < n_tiles - n_ctx, i + n_ctx, i - (n_tiles - n_ctx))

    def cond_row(i):
        return jnp.maximum(tile(i) - n_ctx + 1, 0)

    def tab(i):
        return (jnp.minimum(jnp.maximum(tile(i) - n_ctx + 1, 0), 1), 0, 0)

    rows = lambda i: (tile(i), 0)
    ctx_blk4 = lambda i: (jnp.where(tile(i) < n_ctx, tile(i), 0), 0, 0, 0)
    const2 = lambda i: (0, 0)
    prev_specs = [pl.BlockSpec((seqs, n_prev, SEQ, KV_LORA), ctx_blk4),
                  pl.BlockSpec((seqs, n_prev, SEQ, QK_ROPE), ctx_blk4)] if n_prev else []
    prev_args = [prev_ckv, prev_kpe] if n_prev else []
    return pl.pallas_call(
        functools.partial(_in_kernel, n_prev),
        out_shape=(jax.ShapeDtypeStruct((T_ALL, QK_W), BF16),
                   jax.ShapeDtypeStruct((T_ALL, QK_W), BF16),
                   jax.ShapeDtypeStruct((T_ALL, V_W), BF16),
                   jax.ShapeDtypeStruct((BATCH, n_prev + 1, SEQ, KV_LORA), F32),
                   jax.ShapeDtypeStruct((BATCH, n_prev + 1, SEQ, QK_ROPE), F32),
                   jax.ShapeDtypeStruct((T_ALL, POOL_W), BF16)),
        grid=(n_tiles,),
        in_specs=[
            pl.BlockSpec((TILE_IN, D_MODEL), lambda i: (jnp.minimum(tile(i), n_ctx - 1), 0)),
            pl.BlockSpec((TILE_IN, D_MODEL), lambda i: (jnp.maximum(tile(i) - n_ctx, 0) + lat_tile0, 0)),
            pl.BlockSpec((None, 1, D_MODEL), lambda i: (cond_row(i), 0, 0)),
            pl.BlockSpec((None, 1, D_MODEL), lambda i: (cond_row(i), 0, 1)),
            pl.BlockSpec((1, D_MODEL), const2),
            pl.BlockSpec((D_MODEL, N_PROJ), const2),
            pl.BlockSpec((1, Q_LORA), const2),
            pl.BlockSpec((1, KV_LORA), const2),
            pl.BlockSpec((Q_LORA, QK_W), const2),
            pl.BlockSpec((KV_LORA, N_HEADS * QK_NOPE), const2),
            pl.BlockSpec((KV_LORA, V_W), const2),
            pl.BlockSpec((None, TILE_IN, LANES), tab),
            pl.BlockSpec((None, TILE_IN, LANES), tab),
            pl.BlockSpec((None, TILE_IN, LANES), tab),
            pl.BlockSpec((POOL_GROUPS, POOL_CH, POOL_CH), lambda i: (0, 0, 0)),
            pl.BlockSpec((1, POOL_W), const2),
        ] + prev_specs,
        out_specs=(pl.BlockSpec((TILE_IN, QK_W), rows),
                   pl.BlockSpec((TILE_IN, QK_W), rows),
                   pl.BlockSpec((TILE_IN, V_W), rows),
                   pl.BlockSpec((seqs, n_prev + 1, SEQ, KV_LORA), ctx_blk4),
                   pl.BlockSpec((seqs, n_prev + 1, SEQ, QK_ROPE), ctx_blk4),
                   pl.BlockSpec((TILE_IN, POOL_W), rows)),
        scratch_shapes=[pltpu.VMEM((TILE_IN // CHUNK * REGION, POOL_W), F32)],
        compiler_params=_cparams("arbitrary"),
        name="in_proj",
    )(x_ctx, x_lat, ada, ada, g1, w_in, qg, kvg, w_uq, w_uk, w_uv, tq, tc, ts, w_pool, pscale, *prev_args)


def _softmax_pv(scores, values):
    m = scores[0].max(axis=-1, keepdims=True)
    for s in scores[1:]:
        m = jnp.maximum(m, s.max(axis=-1, keepdims=True))
    den = None
    out = None
    for s, v in zip(scores, values):
        p = jnp.exp(s - m)
        d = p.sum(axis=-1, keepdims=True)
        o = _dot(p.astype(BF16), v)
        den = d if den is None else den + d
        out = o if out is None else out + o
    return out / den


def _attn_ctx_kernel(q_ref, k_ref, v_ref, o_ref):
    for b in range(CTX_SEQS_PER_STEP):
        rows = slice(b * SEQ, (b + 1) * SEQ)
        for hd in range(N_HEADS):
            qk = slice(hd * HEAD_PAD, (hd + 1) * HEAD_PAD)
            vv = slice(hd * V_DIM, (hd + 1) * V_DIM)
            s = _dot_nt(q_ref[rows, qk], k_ref[rows, qk])
            o_ref[rows, vv] = _softmax_pv([s], [v_ref[rows, vv]]).astype(BF16)


def _attn_lat_kernel(q_ref, k_ref, v_ref, kc_ref, vc_ref, o_ref):
    for hd in range(N_HEADS):
        qk = slice(hd * HEAD_PAD, (hd + 1) * HEAD_PAD)
        vv = slice(hd * V_DIM, (hd + 1) * V_DIM)
        q = q_ref[:, qk]
        s1 = _dot_nt(q, k_ref[:, qk])
        s2 = _dot_nt(q, kc_ref[:, qk])
        o_ref[:, vv] = _softmax_pv([s1, s2], [v_ref[:, vv], vc_ref[:, vv]]).astype(BF16)


def _attn_call(q, k, v, kc, vc):
    attn_ctx = pl.pallas_call(
        _attn_ctx_kernel,
        out_shape=jax.ShapeDtypeStruct((T_CTX, V_W), BF16),
        grid=(BATCH // CTX_SEQS_PER_STEP,),
        in_specs=[pl.BlockSpec((CTX_SEQS_PER_STEP * SEQ, QK_W), lambda b: (b, 0)),
                  pl.BlockSpec((CTX_SEQS_PER_STEP * SEQ, QK_W), lambda b: (b, 0)),
                  pl.BlockSpec((CTX_SEQS_PER_STEP * SEQ, V_W), lambda b: (b, 0))],
        out_specs=pl.BlockSpec((CTX_SEQS_PER_STEP * SEQ, V_W), lambda b: (b, 0)),
        compiler_params=_cparams("arbitrary"),
        name="attn_ctx",
    )(q, k, v)

    tq = LAT_Q_TILE
    n_q = DEC_SEQ // tq
    lat0 = T_CTX // DEC_SEQ
    attn_lat = pl.pallas_call(
        _attn_lat_kernel,
        out_shape=jax.ShapeDtypeStruct((T_LAT, V_W), BF16),
        grid=(DEC_BATCH, n_q),
        in_specs=[pl.BlockSpec((tq, QK_W), lambda b, i: (T_CTX // tq + b * n_q + i, 0)),
                  pl.BlockSpec((DEC_SEQ, QK_W), lambda b, i: (lat0 + b, 0)),
                  pl.BlockSpec((DEC_SEQ, V_W), lambda b, i: (lat0 + b, 0)),
                  pl.BlockSpec((PAST_LEN, QK_W), lambda b, i: (b, 0)),
                  pl.BlockSpec((PAST_LEN, V_W), lambda b, i: (b, 0))],
        out_specs=pl.BlockSpec((tq, V_W), lambda b, i: (b * n_q + i, 0)),
        compiler_params=_cparams("arbitrary", "arbitrary"),
        name="attn_lat",
    )(q, k, v, kc, vc)
    return attn_ctx, attn_lat


def _swiglu_hidden(h, wg_ref, wu_ref, a_buf):
    width = a_buf.shape[1]
    for c0 in range(0, width, FF_CHUNK):
        cols = slice(c0, min(c0 + FF_CHUNK, width))
        a_buf[:, cols] = (_silu(_dot(h, wg_ref[:, cols])) * _dot(h, wu_ref[:, cols])).astype(BF16)


def _mix_kernel(with_router, attn_c_ref, attn_l_ref, pool_ref, xc_ref, xl_ref, g1_ref, sh2_ref, sc2_ref, n2_ref, wo_ref,
                *rest):
    if with_router:
        wr_ref, ltri_ref, utri_ref, x_out, xs_out, gs_out, route_out, meta_out = rest
    else:
        g2_ref, wg_ref, wu_ref, wd_ref, x_out, a_buf = rest
    is_ctx = pl.program_id(0) < T_CTX // TILE_MIX
    halves = []
    for r0 in range(0, TILE_MIX, TILE_MIX // 2):
        rows = slice(r0, r0 + TILE_MIX // 2)
        attn = jnp.where(is_ctx, attn_c_ref[rows, :], attn_l_ref[rows, :])
        y = _dot(attn, wo_ref[:V_W, :]) + _dot(pool_ref[rows, :], wo_ref[V_W:, :])
        x_new = jnp.where(is_ctx, xc_ref[rows, :], xl_ref[rows, :]) + g1_ref[...] * y
        x_out[rows, :] = x_new
        halves.append((_rms(x_new) * n2_ref[...] * (1.0 + sc2_ref[...]) + sh2_ref[...]).astype(BF16))
    h = jnp.concatenate(halves, axis=0)
    if not with_router:
        _swiglu_hidden(h, wg_ref, wu_ref, a_buf)
        x_out[...] += g2_ref[...] * _dot(a_buf[...], wd_ref[...])
        return

    logits = _dot(h, wr_ref[...])
    lane = lax.broadcasted_iota(jnp.int32, logits.shape, 1)
    neg = float(jnp.finfo(F32).min)
    lg = jnp.where(lane < N_EXPERTS, logits, neg)
    m1 = lg.max(axis=-1, keepdims=True)
    i1 = jnp.where(lg == m1, lane, LANES).min(axis=-1, keepdims=True)
    lg2 = jnp.where(lane == i1, neg, lg)
    m2 = lg2.max(axis=-1, keepdims=True)
    i2 = jnp.where(lg2 == m2, lane, LANES).min(axis=-1, keepdims=True)
    e = jnp.exp(m2 - m1)
    w1 = 1.0 / (1.0 + e)
    w2 = e / (1.0 + e)

    sel1 = lane == i1
    sel2 = lane == i2
    member = jnp.where(jnp.logical_or(sel1, sel2), 1.0, 0.0)
    rank = _dot(ltri_ref[...], member.astype(BF16))
    n_tok = member.sum(axis=0, keepdims=True)
    units = jnp.floor((n_tok + (SLOT_UNIT - 1)) * (1.0 / SLOT_UNIT))
    unit_off = _dot(jnp.broadcast_to(units, (8, LANES)).astype(BF16), utri_ref[...])[0:1, :]
    slot_of = SLOT_UNIT * unit_off + rank
    slot1 = jnp.where(sel1, slot_of, 0.0).sum(axis=-1, keepdims=True)
    slot2 = jnp.where(sel2, slot_of, 0.0).sum(axis=-1, keepdims=True)
    route = jnp.where(lane == 0, slot1, jnp.where(lane == 1, slot2,
                      jnp.where(lane == 2, w1, jnp.where(lane == 3, w2, 0.0))))
    route_out[...] = route
    sub = lax.broadcasted_iota(jnp.int32, (8, LANES), 0)
    meta_out[...] = jnp.where(sub == 0, units, jnp.where(sub == 1, unit_off, 0.0)).astype(jnp.int32)

    rt = route.T
    s1 = rt[0:1, :].astype(jnp.int32)
    s2 = rt[1:2, :].astype(jnp.int32)
    def slot_block(r0):
        srow = r0 + lax.broadcasted_iota(jnp.int32, (SLOT_BLOCK, TILE_MIX), 0)
        hit1 = srow == s1
        hit2 = srow == s2
        perm = (jnp.where(hit1, 1.0, 0.0) + jnp.where(hit2, 1.0, 0.0)).astype(BF16)
        gate = (jnp.where(hit1, rt[2:3, :], 0.0) + jnp.where(hit2, rt[3:4, :], 0.0)).sum(axis=-1, keepdims=True)
        return perm, gate

    starts = list(range(0, SLOTS_PER_TILE, SLOT_BLOCK))
    nxt = slot_block(starts[0])
    for k, r0 in enumerate(starts):
        perm, gate = nxt
        if k + 1 < len(starts):
            nxt = slot_block(starts[k + 1])
        xs_out[r0:r0 + SLOT_BLOCK, :] = _dot(perm, h).astype(BF16)
        gs_out[r0:r0 + SLOT_BLOCK, :] = jnp.broadcast_to(gate, (SLOT_BLOCK, LANES))


def _mod_row(i, tile):
    n_ctx = T_CTX // tile
    per_seq = DEC_SEQ // tile
    return jnp.where(i < n_ctx, 0, 1 + (i - n_ctx) // per_seq)


def _mix_call(attn_ctx, attn_lat, pooled, x_ctx, x_lat, ada, n2, w_out, w_router=None, ffn_w=None):
    with_router = w_router is not None
    assert with_router != (ffn_w is not None)
    tm = TILE_MIX
    n_ctx = T_CTX // tm
    lat_tile0 = n_ctx if x_lat.shape[0] == T_ALL else 0
    row = lambda i: (i, 0)
    const2 = lambda i: (0, 0)
    mod = lambda k: pl.BlockSpec((None, 1, D_MODEL), lambda i: (_mod_row(i, tm), 0, k))
    ctx_blk = lambda i: (jnp.minimum(i, n_ctx - 1), 0)
    in_specs = [pl.BlockSpec((tm, V_W), ctx_blk),
                pl.BlockSpec((tm, V_W), lambda i: (jnp.maximum(i - n_ctx, 0), 0)),
                pl.BlockSpec((tm, POOL_W), row),
                pl.BlockSpec((tm, D_MODEL), ctx_blk),
                pl.BlockSpec((tm, D_MODEL), lambda i: (jnp.maximum(i - n_ctx, 0) + lat_tile0, 0)),
                mod(2), mod(3), mod(4),
                pl.BlockSpec((1, D_MODEL), const2), pl.BlockSpec((D_MODEL, D_MODEL), const2)]
    args = [attn_ctx, attn_lat, pooled, x_ctx, x_lat, ada, ada, ada, n2, w_out]
    if with_router:
        t_i = jnp.arange(tm)
        ltri = (t_i[None, :] < t_i[:, None]).astype(BF16)
        l_i = jnp.arange(LANES)
        utri = (l_i[:, None] < l_i[None, :]).astype(BF16)
        in_specs += [pl.BlockSpec((D_MODEL, LANES), const2), pl.BlockSpec((tm, tm), const2),
                     pl.BlockSpec((LANES, LANES), const2)]
        args += [w_router, ltri, utri]
        out_shape = [jax.ShapeDtypeStruct((T_ALL, D_MODEL), F32),
                     jax.ShapeDtypeStruct((N_ROUTE_TILES * SLOTS_PER_TILE, D_MODEL), BF16),
                     jax.ShapeDtypeStruct((N_ROUTE_TILES * SLOTS_PER_TILE, LANES), F32),
                     jax.ShapeDtypeStruct((T_ALL, LANES), F32),
                     jax.ShapeDtypeStruct((N_ROUTE_TILES, 8, LANES), jnp.int32)]
        out_specs = [pl.BlockSpec((tm, D_MODEL), row), pl.BlockSpec((SLOTS_PER_TILE, D_MODEL), row),
                     pl.BlockSpec((SLOTS_PER_TILE, LANES), row), pl.BlockSpec((tm, LANES), row),
                     pl.BlockSpec((None, 8, LANES), lambda i: (i, 0, 0))]
        scratch = []
    else:
        resident = lambda shape: pl.BlockSpec(shape, const2, pipeline_mode=pl.Buffered(1))
        in_specs += [mod(5), resident((D_MODEL, D_FF)), resident((D_MODEL, D_FF)), resident((D_FF, D_MODEL))]
        args += [ada, *ffn_w]
        out_shape = [jax.ShapeDtypeStruct((T_ALL, D_MODEL), F32)]
        out_specs = [pl.BlockSpec((tm, D_MODEL), row)]
        scratch = [pltpu.VMEM((tm, D_FF), BF16)]
    return pl.pallas_call(
        functools.partial(_mix_kernel, with_router),
        out_shape=tuple(out_shape),
        grid=(T_ALL // tm,),
        in_specs=in_specs,
        out_specs=tuple(out_specs),
        scratch_shapes=scratch,
        compiler_params=_cparams("arbitrary"),
        name="mix_router" if with_router else "mix_ffn",
    )(*args)


def _route_tables(meta):
    units = meta[:, 0, :N_EXPERTS]
    offs = meta[:, 1, :N_EXPERTS]
    cum = jnp.cumsum(units, axis=0)
    total = cum[-1]
    tiles_e = (total + GEMM_UNITS - 1) // GEMM_UNITS
    tile_end = jnp.cumsum(tiles_e)
    n_act = tile_end[-1]
    m = jnp.arange(N_GEMM_TILES)
    m_eff = jnp.minimum(m, jnp.maximum(n_act - 1, 0))
    expert = jnp.minimum(jnp.sum(tile_end[None, :] <= m_eff[:, None], axis=1), N_EXPERTS - 1)
    is_e = (expert[:, None] == jnp.arange(N_EXPERTS)[None, :]).astype(jnp.int32)
    pick = lambda per_expert: jnp.sum(is_e * per_expert[None, :], axis=1)
    first_q = (m_eff - pick(tile_end - tiles_e)) * GEMM_UNITS
    count = jnp.where(m < n_act, jnp.clip(pick(total) - first_q, 0, GEMM_UNITS), 0)
    q = first_q[:, None] + jnp.arange(GEMM_UNITS)[None, :]
    rows_of = lambda table: jnp.sum(is_e[:, None, :] * table[None, :, :], axis=2)
    cum_e, units_e, offs_e = rows_of(cum), rows_of(units), rows_of(offs)
    src_tile = jnp.minimum(jnp.sum(cum_e[:, None, :] <= q[:, :, None], axis=2), N_ROUTE_TILES - 1)
    is_t = (src_tile[:, :, None] == jnp.arange(N_ROUTE_TILES)[None, None, :]).astype(jnp.int32)
    at_tile = lambda per_tile: jnp.sum(is_t * per_tile[:, None, :], axis=2)
    uid = src_tile * UNITS_PER_TILE + at_tile(offs_e) + (q - at_tile(cum_e - units_e))
    uid = jnp.clip(uid, 0, N_UNITS - 1)
    i32 = jnp.int32
    return expert.astype(i32), count.astype(i32), uid.reshape(-1).astype(i32), n_act.reshape(1).astype(i32)


def _gemm_kernel(em_ref, cnt_ref, ul_ref, nact_ref, xs_hbm, gs_hbm, wg_ref, wu_ref, wd_ref, ys_hbm,
                 xbuf, gbuf, obuf, acc_ref, a_buf, sem_in, sem_out):
    del em_ref
    m = pl.program_id(0)
    j = pl.program_id(1)
    last_j = pl.num_programs(1) - 1
    n_act = nact_ref[0]
    active = m < n_act
    slot = m % 2

    def rows(r):
        return pl.ds(pl.multiple_of(r * SLOT_UNIT, SLOT_UNIT), SLOT_UNIT)

    def in_copies(mm, sl, r):
        uid = ul_ref[mm * GEMM_UNITS + r]
        return (pltpu.make_async_copy(xs_hbm.at[uid], xbuf.at[sl, rows(r)], sem_in.at[sl]),
                pltpu.make_async_copy(gs_hbm.at[uid], gbuf.at[sl, rows(r)], sem_in.at[sl]))

    def out_copy(mm, r):
        uid = ul_ref[mm * GEMM_UNITS + r]
        return pltpu.make_async_copy(obuf.at[rows(r)], ys_hbm.at[uid], sem_out.at[0])

    def start_in(mm, sl):
        n = cnt_ref[mm]

        def issue(r, carry):
            for cp in in_copies(mm, sl, r):
                cp.start()
            return carry

        def clear(r, carry):
            xbuf[sl, rows(r), :] = jnp.zeros((SLOT_UNIT, D_MODEL), BF16)
            gbuf[sl, rows(r), :] = jnp.zeros((SLOT_UNIT, LANES), F32)
            return carry

        lax.fori_loop(0, n, issue, 0)
        lax.fori_loop(n, GEMM_UNITS, clear, 0)

    def wait_in(mm, sl):
        def body(r, carry):
            for cp in in_copies(mm, sl, r):
                cp.wait()
            return carry

        lax.fori_loop(0, cnt_ref[mm], body, 0)

    def start_out(mm):
        def body(r, carry):
            out_copy(mm, r).start()
            return carry

        lax.fori_loop(0, cnt_ref[mm], body, 0)

    def wait_out(mm):
        def body(r, carry):
            out_copy(mm, r).wait()
            return carry

        lax.fori_loop(0, cnt_ref[mm], body, 0)

    @pl.when(jnp.logical_and(active, j == 0))
    def _():
        @pl.when(m == 0)
        def _():
            start_in(0, 0)

        wait_in(m, slot)

        @pl.when(m + 1 < n_act)
        def _():
            start_in(m + 1, 1 - slot)

    @pl.when(jnp.logical_and(jnp.logical_and(active, j == last_j), m > 0))
    def _():
        wait_out(m - 1)

    n_valid = cnt_ref[m]
    for quarter in range(1, GEMM_QUARTERS + 1):
        n_rows = quarter * GEMM_ROWS // GEMM_QUARTERS
        lo_units = (quarter - 1) * GEMM_UNITS // GEMM_QUARTERS
        hi_units = quarter * GEMM_UNITS // GEMM_QUARTERS

        @pl.when(jnp.logical_and(active, jnp.logical_and(n_valid > lo_units, n_valid <= hi_units)))
        def _(n_rows=n_rows):
            _swiglu_hidden(xbuf[slot, :n_rows, :], wg_ref, wu_ref, a_buf.at[:n_rows, :])
            part = _dot(a_buf[:n_rows, :], wd_ref[...])

            @pl.when(j == 0)
            def _():
                acc_ref[:n_rows, :] = part

            @pl.when(jnp.logical_and(j > 0, j < last_j))
            def _():
                acc_ref[:n_rows, :] += part

            @pl.when(j == last_j)
            def _():
                gate = gbuf[slot, :n_rows, :]
                for cb in range(D_MODEL // LANES):
                    cols = slice(cb * LANES, (cb + 1) * LANES)
                    obuf[:n_rows, cols] = ((acc_ref[:n_rows, cols] + part[:, cols]) * gate).astype(BF16)

    @pl.when(jnp.logical_and(active, j == last_j))
    def _():
        start_out(m)

        @pl.when(m == n_act - 1)
        def _():
            wait_out(m)


def _gemm_call(expert, count, uids, n_act, xs, gs, wg, wu, wd):
    tf = TILE_FE
    n_j = D_FF_EXPERT // tf
    assert n_j >= 2

    def w_col(m, j, em, cnt, ul, nact):
        return (em[m], 0, jnp.where(m < nact[0], j, n_j - 1))

    def w_row(m, j, em, cnt, ul, nact):
        return (em[m], jnp.where(m < nact[0], j, n_j - 1), 0)

    grid_spec = pltpu.PrefetchScalarGridSpec(
        num_scalar_prefetch=4,
        grid=(N_GEMM_TILES, n_j),
        in_specs=[pl.BlockSpec(memory_space=pl.ANY),
                  pl.BlockSpec(memory_space=pl.ANY),
                  pl.BlockSpec((None, D_MODEL, tf), w_col),
                  pl.BlockSpec((None, D_MODEL, tf), w_col),
                  pl.BlockSpec((None, tf, D_MODEL), w_row)],
        out_specs=pl.BlockSpec(memory_space=pl.ANY),
        scratch_shapes=[pltpu.VMEM((2, GEMM_ROWS, D_MODEL), BF16),
                        pltpu.VMEM((2, GEMM_ROWS, LANES), F32),
                        pltpu.VMEM((GEMM_ROWS, D_MODEL), BF16),
                        pltpu.VMEM((GEMM_ROWS, D_MODEL), F32),
                        pltpu.VMEM((GEMM_ROWS, tf), BF16),
                        pltpu.SemaphoreType.DMA((2,)),
                        pltpu.SemaphoreType.DMA((1,))])
    return pl.pallas_call(
        _gemm_kernel,
        out_shape=jax.ShapeDtypeStruct((N_UNITS, SLOT_UNIT, D_MODEL), BF16),
        grid_spec=grid_spec,
        input_output_aliases={4: 0},
        compiler_params=_cparams("arbitrary", "arbitrary"),
        name="moe_experts",
    )(expert, count, uids, n_act,
      xs.reshape(N_UNITS, SLOT_UNIT, D_MODEL), gs.reshape(N_UNITS, SLOT_UNIT, LANES), wg, wu, wd)


def _combine_kernel(ys_ref, route_ref, x_ref, g2_ref, fg_ref, o_ref):
    route = route_ref[...]
    s1 = route[:, 0:1].astype(jnp.int32)
    s2 = route[:, 1:2].astype(jnp.int32)
    scol = lax.broadcasted_iota(jnp.int32, (TILE_MIX, SLOTS_PER_TILE), 1)
    unperm = (jnp.where(scol == s1, 1.0, 0.0) + jnp.where(scol == s2, 1.0, 0.0)).astype(BF16)
    y = _dot(unperm, ys_ref[...])
    o_ref[...] = _rms(x_ref[...] + g2_ref[...] * y) * fg_ref[...]


def _combine_call(ys, route, x, ada, final_g, tile0, n_tok):
    tm = TILE_MIX
    return pl.pallas_call(
        _combine_kernel,
        out_shape=jax.ShapeDtypeStruct((n_tok, D_MODEL), F32),
        grid=(n_tok // tm,),
        in_specs=[pl.BlockSpec((SLOTS_PER_TILE, D_MODEL), lambda i: (i + tile0, 0)),
                  pl.BlockSpec((tm, LANES), lambda i: (i + tile0, 0)),
                  pl.BlockSpec((tm, D_MODEL), lambda i: (i + tile0, 0)),
                  pl.BlockSpec((None, 1, D_MODEL), lambda i: (_mod_row(i + tile0, tm), 0, 5)),
                  pl.BlockSpec((1, D_MODEL), lambda i: (0, 0))],
        out_specs=pl.BlockSpec((tm, D_MODEL), lambda i: (i, 0)),
        compiler_params=_cparams("arbitrary"),
        name="moe_combine",
    )(ys.reshape(N_UNITS * SLOT_UNIT, D_MODEL), route, x, ada, final_g)


def _rot_cols(w):
    q = QK_ROPE // 4
    return jnp.concatenate([-w[:, q:2 * q], w[:, :q], -w[:, 3 * q:], w[:, 2 * q:3 * q]], axis=1)


def _rope_tables():
    t = jnp.arange(DEC_SEQ)
    rows = (t // GRID_W).astype(F32)
    cols = (t % GRID_W).astype(F32)
    half = QK_ROPE // 2
    freqs = ROPE_BASE ** (-jnp.arange(0, half, 2, dtype=F32) / half)
    ang_r = rows[:, None] * freqs
    ang_c = cols[:, None] * freqs
    ang = jnp.concatenate([ang_r, ang_r, ang_c, ang_c], axis=-1)
    cos, sin = jnp.cos(ang), jnp.sin(ang)
    one, zero = jnp.ones_like(cos), jnp.zeros_like(cos)
    tq = jnp.stack([jnp.concatenate([one, zero], -1), jnp.concatenate([cos, sin], -1)])
    tc = jnp.stack([jnp.concatenate([one, one], -1), jnp.concatenate([cos, cos], -1)])
    ts = jnp.stack([jnp.concatenate([zero, zero], -1), jnp.concatenate([sin, sin], -1)])
    return tq, tc, ts


def _layer_weights(l, w_in, w_uq, w_ukv):
    wi = w_in[l]
    w_kpe = wi[:, OFF_U:OFF_U + QK_ROPE]
    w_rot = _rot_cols(w_kpe)
    w_in_r = jnp.concatenate([wi[:, :OFF_U], wi[:, OFF_U + QK_ROPE:], w_kpe, w_kpe, w_rot, w_rot], axis=1)
    heads = []
    for h in range(N_HEADS):
        blk = w_uq[l][:, h * (QK_NOPE + QK_ROPE):(h + 1) * (QK_NOPE + QK_ROPE)]
        heads += [blk, _rot_cols(blk[:, QK_NOPE:])]
    w_uq_r = jnp.concatenate(heads, axis=1)
    kv = w_ukv[l].reshape(KV_LORA, N_HEADS, QK_NOPE + V_DIM)
    w_uk = kv[:, :, :QK_NOPE].reshape(KV_LORA, N_HEADS * QK_NOPE)
    w_uv = kv[:, :, QK_NOPE:].reshape(KV_LORA, V_W)
    return w_in_r.astype(BF16), w_uq_r.astype(BF16), w_uk.astype(BF16), w_uv.astype(BF16)


def kernel(x_prompt, x_sample, c, cache_ckv, cache_kpe, c_ctx, norm1_g, norm2_g, w_ada, b_ada, w_in, q_norm_g,
           kv_norm_g, w_uq, w_ukv, w_pool, pool_scale, w_out, ffn_w_gate, ffn_w_up, ffn_w_down, moe_w_router,
           moe_w_gate, moe_w_up, moe_w_down, final_norm_g):
    x_ctx = x_prompt.reshape(T_CTX, D_MODEL)
    x_lat = x_sample.reshape(T_LAT, D_MODEL)
    cond = jnp.concatenate([c_ctx[None, :], c, jnp.zeros((COND_ROWS - 1 - DEC_BATCH, D_MODEL), F32)], axis=0)
    ada_all = _ada_call(cond, w_ada, b_ada)
    tq, tc, ts = _rope_tables()

    lw = [_layer_weights(l, w_in, w_uq, w_ukv) for l in range(DEPTH)]
    dup = jnp.concatenate([jnp.eye(QK_ROPE, dtype=BF16)] * 2, axis=1)
    kc_all, vc_all = _cache_call(cache_ckv, cache_kpe, jnp.stack([w[2] for w in lw]),
                                 jnp.stack([w[3] for w in lw]), dup)

    assert DEPTH == 2
    new_ckv = new_kpe = None
    for l in range(DEPTH):
        w_in_r, w_uq_r, w_uk, w_uv = lw[l]
        ada = ada_all[l].reshape(COND_ROWS, 1, 6 * D_MODEL)
        q, k, v, new_ckv, new_kpe, pooled = _in_call(
            x_ctx, x_lat, ada, norm1_g[l][None, :], w_in_r, q_norm_g[l][None, :], kv_norm_g[l][None, :], w_uq_r,
            w_uk, w_uv, tq, tc, ts, w_pool[l].astype(BF16), pool_scale[l][None, :], new_ckv, new_kpe)
        attn_c, attn_l = _attn_call(q, k, v, kc_all[l], vc_all[l])
        j = l // 2
        if l % 2 == 0:
            ffn_w = (ffn_w_gate[j].astype(BF16), ffn_w_up[j].astype(BF16), ffn_w_down[j].astype(BF16))
            (x,) = _mix_call(attn_c, attn_l, pooled, x_ctx, x_lat, ada, norm2_g[l][None, :],
                             w_out[l].astype(BF16), ffn_w=ffn_w)
            x_ctx = x_lat = x
        else:
            w_r = jnp.pad(moe_w_router[j], ((0, 0), (0, LANES - N_EXPERTS))).astype(BF16)
            x, xs, gs, route, meta = _mix_call(attn_c, attn_l, pooled, x_ctx, x_lat, ada, norm2_g[l][None, :],
                                               w_out[l].astype(BF16), w_router=w_r)
            expert, count, uids, n_act = _route_tables(meta)
            ys = _gemm_call(expert, count, uids, n_act, xs, gs, moe_w_gate[j].astype(BF16),
                            moe_w_up[j].astype(BF16), moe_w_down[j].astype(BF16))
            fg = final_norm_g[None, :]
            y_prompt = _combine_call(ys, route, x, ada, fg, 0, T_CTX).reshape(BATCH, SEQ, D_MODEL)
            y_sample = _combine_call(ys, route, x, ada, fg, T_CTX // TILE_MIX, T_LAT).reshape(
                DEC_BATCH, DEC_SEQ, D_MODEL)

    return y_prompt, y_sample, new_ckv, new_kpe
```

```python
import functools

import jax
import jax.numpy as jnp
from jax import lax
from jax.experimental import pallas as pl
from jax.experimental.pallas import tpu as pltpu

D_MODEL = 1024
BATCH = 32
SEQ = 256
DEPTH = 2
DEC_BATCH = 8
DEC_SEQ = 1024
PAST_LEN = 512
GRID_W = 64
N_HEADS = 4
QK_NOPE = 128
QK_ROPE = 64
V_DIM = 128
Q_LORA = 384
KV_LORA = 256
POOL_W = 512
POOL_GROUPS = 4
POOL_WINDOWS = (2, 4, 8, 16)
POOL_CH = POOL_W // POOL_GROUPS
D_FF = 2816
N_EXPERTS = 8
D_FF_EXPERT = 3584
ROPE_BASE = 10000.0
EPS = 1e-6

T_CTX = BATCH * SEQ
T_LAT = DEC_BATCH * DEC_SEQ
T_ALL = T_CTX + T_LAT

LANES = 128
HEAD_PAD = 256
QK_W = N_HEADS * HEAD_PAD
V_W = N_HEADS * V_DIM
OFF_CKV = Q_LORA
OFF_U = Q_LORA + KV_LORA
OFF_KA = OFF_U + POOL_W
OFF_KB = OFF_KA + LANES
N_PROJ = OFF_KB + LANES

TILE_IN = 1024
CHUNK = 256
HALO = 8
REGION = CHUNK + 2 * HALO
CTX_SEQS_PER_STEP = 4
LAT_Q_TILE = 1024
TILE_MIX = 512
TOP_K = 2
SLOT_UNIT = 16
UNITS_PER_TILE = TILE_MIX * TOP_K // SLOT_UNIT + N_EXPERTS
SLOTS_PER_TILE = UNITS_PER_TILE * SLOT_UNIT
SLOT_BLOCK = SLOTS_PER_TILE // 4
N_ROUTE_TILES = T_ALL // TILE_MIX
N_UNITS = N_ROUTE_TILES * UNITS_PER_TILE
ZERO_UNIT = UNITS_PER_TILE - 1
GEMM_UNITS = 64
GEMM_ROWS = GEMM_UNITS * SLOT_UNIT
GEMM_QUARTERS = 4
N_GEMM_TILES = -(-N_UNITS // GEMM_UNITS) + N_EXPERTS
TILE_FE = D_FF_EXPERT // 2
FF_CHUNK = 256
COND_ROWS = 16
VMEM_LIMIT = 56 * 1024 * 1024

F32 = jnp.float32
BF16 = jnp.bfloat16


def _rms(x):
    return x * lax.rsqrt(jnp.mean(x * x, axis=-1, keepdims=True) + EPS)


def _dot(a, b):
    return jnp.dot(a, b, preferred_element_type=F32)


def _dot_nt(a, b):
    return lax.dot_general(a, b, (((1,), (1,)), ((), ())), preferred_element_type=F32)


def _silu(x):
    return x * (1.0 / (1.0 + jnp.exp(-x)))


def _cparams(*sem):
    return pltpu.CompilerParams(dimension_semantics=sem, vmem_limit_bytes=VMEM_LIMIT)


def _ada_kernel(cond_ref, w_ref, b_ref, o_ref):
    c = cond_ref[...]
    o_ref[...] = jnp.dot(_silu(c), w_ref[...], preferred_element_type=F32,
                         precision=lax.Precision.HIGHEST) + b_ref[...]


def _ada_call(cond, w_ada, b_ada):
    n_blk = 4
    bw = 6 * D_MODEL // n_blk
    return pl.pallas_call(
        _ada_kernel,
        out_shape=jax.ShapeDtypeStruct((DEPTH, COND_ROWS, 6 * D_MODEL), F32),
        grid=(DEPTH, n_blk),
        in_specs=[
            pl.BlockSpec((COND_ROWS, D_MODEL), lambda l, j: (0, 0)),
            pl.BlockSpec((None, D_MODEL, bw), lambda l, j: (l, 0, j)),
            pl.BlockSpec((None, 1, bw), lambda l, j: (l, 0, j)),
        ],
        out_specs=pl.BlockSpec((None, COND_ROWS, bw), lambda l, j: (l, 0, j)),
        compiler_params=_cparams("arbitrary", "arbitrary"),
        name="ada_params",
    )(cond, w_ada, b_ada.reshape(DEPTH, 1, 6 * D_MODEL))


def _cache_kernel(ckv_ref, kpe_ref, wuk_ref, wuv_ref, dup_ref, k_ref, v_ref):
    ckv = ckv_ref[...].astype(BF16)
    knope = _dot(ckv, wuk_ref[...])
    v_ref[...] = _dot(ckv, wuv_ref[...]).astype(BF16)
    kdup = _dot(kpe_ref[...].astype(BF16), dup_ref[...]).astype(BF16)
    for h in range(N_HEADS):
        k_ref[:, h * HEAD_PAD:h * HEAD_PAD + QK_NOPE] = knope[:, h * QK_NOPE:(h + 1) * QK_NOPE].astype(BF16)
        k_ref[:, h * HEAD_PAD + QK_NOPE:(h + 1) * HEAD_PAD] = kdup


def _cache_call(cache_ckv, cache_kpe, w_uk, w_uv, dup):
    n_tok = DEC_BATCH * PAST_LEN
    return pl.pallas_call(
        _cache_kernel,
        out_shape=(jax.ShapeDtypeStruct((DEPTH, n_tok, QK_W), BF16),
                   jax.ShapeDtypeStruct((DEPTH, n_tok, V_W), BF16)),
        grid=(DEPTH, DEC_BATCH),
        in_specs=[
            pl.BlockSpec((None, None, PAST_LEN, KV_LORA), lambda l, b: (b, l, 0, 0)),
            pl.BlockSpec((None, None, PAST_LEN, QK_ROPE), lambda l, b: (b, l, 0, 0)),
            pl.BlockSpec((None, KV_LORA, N_HEADS * QK_NOPE), lambda l, b: (l, 0, 0)),
            pl.BlockSpec((None, KV_LORA, V_W), lambda l, b: (l, 0, 0)),
            pl.BlockSpec((QK_ROPE, LANES), lambda l, b: (0, 0)),
        ],
        out_specs=(pl.BlockSpec((None, PAST_LEN, QK_W), lambda l, b: (l, b, 0)),
                   pl.BlockSpec((None, PAST_LEN, V_W), lambda l, b: (l, b, 0))),
        compiler_params=_cparams("arbitrary", "arbitrary"),
        name="cache_kv",
    )(cache_ckv, cache_kpe, w_uk, w_uv, dup)


def _in_kernel(n_prev, xc_ref, xl_ref, sh_ref, sc_ref, g_ref, win_ref, qg_ref, kvg_ref, wuq_ref, wuk_ref, wuv_ref,
               tq_ref, tc_ref, ts_ref, wpool_ref, ps_ref, *rest):
    if n_prev:
        prev_ckv_ref, prev_kpe_ref, *rest = rest
    q_out, k_out, v_out, ckv_out, kpe_out, pool_out, u_scr = rest
    is_ctx = pl.program_id(0) >= (T_ALL - T_CTX) // TILE_IN
    if n_prev:
        ckv_out[:, :n_prev] = prev_ckv_ref[...]
        kpe_out[:, :n_prev] = prev_kpe_ref[...]
    shift = sh_ref[...]
    scale1 = 1.0 + sc_ref[...]
    qk_scale = (QK_NOPE + QK_ROPE) ** -0.5
    n_chunks = TILE_IN // CHUNK
    zeros = jnp.zeros((HALO, POOL_W), F32)
    seq_len = jnp.where(is_ctx, SEQ, DEC_SEQ)
    row = lax.broadcasted_iota(jnp.int32, (CHUNK, POOL_CH), 0)

    def pool_chunk(c):
        base = c * REGION + HALO
        rows = slice(c * CHUNK, (c + 1) * CHUNK)
        t = row + jnp.where(is_ctx, 0, c * CHUNK)
        for g, w in enumerate(POOL_WINDOWS):
            cols = slice(g * POOL_CH, (g + 1) * POOL_CH)
            acc = u_scr[base - w // 2:base - w // 2 + CHUNK, cols]
            for j in range(-w // 2 + 1, w // 2):
                acc = acc + u_scr[base + j:base + j + CHUNK, cols]
            cnt = jnp.minimum(t + w // 2, seq_len) - jnp.maximum(t - w // 2, 0)
            pooled = acc / cnt.astype(F32) - u_scr[base:base + CHUNK, cols]
            lin = _dot(pooled.astype(BF16), wpool_ref[g]) * ps_ref[:, cols]
            pool_out[rows, cols] = lin.astype(BF16)

    for c in range(n_chunks):
        r0 = c * CHUNK
        rows = slice(r0, r0 + CHUNK)
        x = jnp.where(is_ctx, xc_ref[rows, :], xl_ref[rows, :])
        h = (_rms(x) * g_ref[...] * scale1 + shift).astype(BF16)
        proj = _dot(h, win_ref[...])

        qn = (_rms(proj[:, :Q_LORA]) * qg_ref[...]).astype(BF16)
        q = _dot(qn, wuq_ref[...]) * qk_scale
        tq = tq_ref[rows, :]
        for hd in range(N_HEADS):
            lo = hd * HEAD_PAD
            q_out[rows, lo:lo + QK_NOPE] = q[:, lo:lo + QK_NOPE].astype(BF16)
            q_out[rows, lo + QK_NOPE:lo + HEAD_PAD] = (q[:, lo + QK_NOPE:lo + HEAD_PAD] * tq).astype(BF16)

        ckv = _rms(proj[:, OFF_CKV:OFF_U]) * kvg_ref[...]
        ckv_out[c, n_prev] = ckv
        ckv_b = ckv.astype(BF16)
        knope = _dot(ckv_b, wuk_ref[...])
        v_out[rows, :] = _dot(ckv_b, wuv_ref[...]).astype(BF16)

        k_a = proj[:, OFF_KA:OFF_KB]
        k_b = proj[:, OFF_KB:N_PROJ]
        kpe_out[c, n_prev] = k_a[:, :QK_ROPE]
        kr = (k_a * tc_ref[rows, :] + k_b * ts_ref[rows, :]).astype(BF16)
        for hd in range(N_HEADS):
            lo = hd * HEAD_PAD
            k_out[rows, lo:lo + QK_NOPE] = knope[:, hd * QK_NOPE:(hd + 1) * QK_NOPE].astype(BF16)
            k_out[rows, lo + QK_NOPE:lo + HEAD_PAD] = kr

        u = proj[:, OFF_U:OFF_KA]
        base = c * REGION + HALO
        u_scr[base:base + CHUNK, :] = u
        if c == 0:
            u_scr[0:HALO, :] = zeros
        else:
            u_scr[base - 2 * HALO:base - HALO, :] = jnp.where(is_ctx, zeros, u[:HALO, :])
        if c == n_chunks - 1:
            u_scr[base + CHUNK:base + CHUNK + HALO, :] = zeros
        else:
            u_scr[base + CHUNK + HALO:base + CHUNK + 2 * HALO, :] = jnp.where(is_ctx, zeros, u[CHUNK - HALO:, :])
        if c > 0:
            pool_chunk(c - 1)
    pool_chunk(n_chunks - 1)


def _in_call(x_ctx, x_lat, ada, g1, w_in, qg, kvg, w_uq, w_uk, w_uv, tq, tc, ts, w_pool, pscale, prev_ckv, prev_kpe):
    assert CHUNK == SEQ
    n_tiles = T_ALL // TILE_IN
    n_ctx = T_CTX // TILE_IN
    n_prev = 0 if prev_ckv is None else prev_ckv.shape[1]
    seqs = TILE_IN // SEQ
    lat_tile0 = n_ctx if x_lat.shape[0] == T_ALL else 0

    def tile(i):
        return jnp.where(i < n_tiles - n_ctx, i + n_ctx, i - (n_tiles - n_ctx))

    def cond_row(i):
        return jnp.maximum(tile(i) - n_ctx + 1, 0)

    def tab(i):
        return (jnp.minimum(jnp.maximum(tile(i) - n_ctx + 1, 0), 1), 0, 0)

    rows = lambda i: (tile(i), 0)
    ctx_blk4 = lambda i: (jnp.where(tile(i) < n_ctx, tile(i), 0), 0, 0, 0)
    const2 = lambda i: (0, 0)
    prev_specs = [pl.BlockSpec((seqs, n_prev, SEQ, KV_LORA), ctx_blk4),
                  pl.BlockSpec((seqs, n_prev, SEQ, QK_ROPE), ctx_blk4)] if n_prev else []
    prev_args = [prev_ckv, prev_kpe] if n_prev else []
    return pl.pallas_call(
        functools.partial(_in_kernel, n_prev),
        out_shape=(jax.ShapeDtypeStruct((T_ALL, QK_W), BF16),
                   jax.ShapeDtypeStruct((T_ALL, QK_W), BF16),
                   jax.ShapeDtypeStruct((T_ALL, V_W), BF16),
                   jax.ShapeDtypeStruct((BATCH, n_prev + 1, SEQ, KV_LORA), F32),
                   jax.ShapeDtypeStruct((BATCH, n_prev + 1, SEQ, QK_ROPE), F32),
                   jax.ShapeDtypeStruct((T_ALL, POOL_W), BF16)),
        grid=(n_tiles,),
        in_specs=[
            pl.BlockSpec((TILE_IN, D_MODEL), lambda i: (jnp.minimum(tile(i), n_ctx - 1), 0)),
            pl.BlockSpec((TILE_IN, D_MODEL), lambda i: (jnp.maximum(tile(i) - n_ctx, 0) + lat_tile0, 0)),
            pl.BlockSpec((None, 1, D_MODEL), lambda i: (cond_row(i), 0, 0)),
            pl.BlockSpec((None, 1, D_MODEL), lambda i: (cond_row(i), 0, 1)),
            pl.BlockSpec((1, D_MODEL), const2),
            pl.BlockSpec((D_MODEL, N_PROJ), const2),
            pl.BlockSpec((1, Q_LORA), const2),
            pl.BlockSpec((1, KV_LORA), const2),
            pl.BlockSpec((Q_LORA, QK_W), const2),
            pl.BlockSpec((KV_LORA, N_HEADS * QK_NOPE), const2),
            pl.BlockSpec((KV_LORA, V_W), const2),
            pl.BlockSpec((None, TILE_IN, LANES), tab),
            pl.BlockSpec((None, TILE_IN, LANES), tab),
            pl.BlockSpec((None, TILE_IN, LANES), tab),
            pl.BlockSpec((POOL_GROUPS, POOL_CH, POOL_CH), lambda i: (0, 0, 0)),
            pl.BlockSpec((1, POOL_W), const2),
        ] + prev_specs,
        out_specs=(pl.BlockSpec((TILE_IN, QK_W), rows),
                   pl.BlockSpec((TILE_IN, QK_W), rows),
                   pl.BlockSpec((TILE_IN, V_W), rows),
                   pl.BlockSpec((seqs, n_prev + 1, SEQ, KV_LORA), ctx_blk4),
                   pl.BlockSpec((seqs, n_prev + 1, SEQ, QK_ROPE), ctx_blk4),
                   pl.BlockSpec((TILE_IN, POOL_W), rows)),
        scratch_shapes=[pltpu.VMEM((TILE_IN // CHUNK * REGION, POOL_W), F32)],
        compiler_params=_cparams("arbitrary"),
        name="in_proj",
    )(x_ctx, x_lat, ada, ada, g1, w_in, qg, kvg, w_uq, w_uk, w_uv, tq, tc, ts, w_pool, pscale, *prev_args)


def _softmax_pv(scores, values):
    m = scores[0].max(axis=-1, keepdims=True)
    for s in scores[1:]:
        m = jnp.maximum(m, s.max(axis=-1, keepdims=True))
    den = None
    out = None
    for s, v in zip(scores, values):
        p = jnp.exp(s - m)
        d = p.sum(axis=-1, keepdims=True)
        o = _dot(p.astype(BF16), v)
        den = d if den is None else den + d
        out = o if out is None else out + o
    return out / den


def _attn_ctx_kernel(q_ref, k_ref, v_ref, o_ref):
    for b in range(CTX_SEQS_PER_STEP):
        rows = slice(b * SEQ, (b + 1) * SEQ)
        for hd in range(N_HEADS):
            qk = slice(hd * HEAD_PAD, (hd + 1) * HEAD_PAD)
            vv = slice(hd * V_DIM, (hd + 1) * V_DIM)
            s = _dot_nt(q_ref[rows, qk], k_ref[rows, qk])
            o_ref[rows, vv] = _softmax_pv([s], [v_ref[rows, vv]]).astype(BF16)


def _attn_lat_kernel(q_ref, k_ref, v_ref, kc_ref, vc_ref, o_ref):
    for hd in range(N_HEADS):
        qk = slice(hd * HEAD_PAD, (hd + 1) * HEAD_PAD)
        vv = slice(hd * V_DIM, (hd + 1) * V_DIM)
        q = q_ref[:, qk]
        s1 = _dot_nt(q, k_ref[:, qk])
        s2 = _dot_nt(q, kc_ref[:, qk])
        o_ref[:, vv] = _softmax_pv([s1, s2], [v_ref[:, vv], vc_ref[:, vv]]).astype(BF16)


def _attn_call(q, k, v, kc, vc):
    attn_ctx = pl.pallas_call(
        _attn_ctx_kernel,
        out_shape=jax.ShapeDtypeStruct((T_CTX, V_W), BF16),
        grid=(BATCH // CTX_SEQS_PER_STEP,),
        in_specs=[pl.BlockSpec((CTX_SEQS_PER_STEP * SEQ, QK_W), lambda b: (b, 0)),
                  pl.BlockSpec((CTX_SEQS_PER_STEP * SEQ, QK_W), lambda b: (b, 0)),
                  pl.BlockSpec((CTX_SEQS_PER_STEP * SEQ, V_W), lambda b: (b, 0))],
        out_specs=pl.BlockSpec((CTX_SEQS_PER_STEP * SEQ, V_W), lambda b: (b, 0)),
        compiler_params=_cparams("arbitrary"),
        name="attn_ctx",
    )(q, k, v)

    tq = LAT_Q_TILE
    n_q = DEC_SEQ // tq
    lat0 = T_CTX // DEC_SEQ
    attn_lat = pl.pallas_call(
        _attn_lat_kernel,
        out_shape=jax.ShapeDtypeStruct((T_LAT, V_W), BF16),
        grid=(DEC_BATCH, n_q),
        in_specs=[pl.BlockSpec((tq, QK_W), lambda b, i: (T_CTX // tq + b * n_q + i, 0)),
                  pl.BlockSpec((DEC_SEQ, QK_W), lambda b, i: (lat0 + b, 0)),
                  pl.BlockSpec((DEC_SEQ, V_W), lambda b, i: (lat0 + b, 0)),
                  pl.BlockSpec((PAST_LEN, QK_W), lambda b, i: (b, 0)),
                  pl.BlockSpec((PAST_LEN, V_W), lambda b, i: (b, 0))],
        out_specs=pl.BlockSpec((tq, V_W), lambda b, i: (b * n_q + i, 0)),
        compiler_params=_cparams("arbitrary", "arbitrary"),
        name="attn_lat",
    )(q, k, v, kc, vc)
    return attn_ctx, attn_lat


def _swiglu_hidden(h, wg_ref, wu_ref, a_buf):
    width = a_buf.shape[1]
    for c0 in range(0, width, FF_CHUNK):
        cols = slice(c0, min(c0 + FF_CHUNK, width))
        a_buf[:, cols] = (_silu(_dot(h, wg_ref[:, cols])) * _dot(h, wu_ref[:, cols])).astype(BF16)


def _mix_kernel(with_router, attn_c_ref, attn_l_ref, pool_ref, xc_ref, xl_ref, g1_ref, sh2_ref, sc2_ref, n2_ref, wo_ref,
                *rest):
    if with_router:
        wr_ref, ltri_ref, utri_ref, x_out, xs_out, gs_out, route_out, meta_out = rest
    else:
        g2_ref, wg_ref, wu_ref, wd_ref, x_out, a_buf = rest
    is_ctx = pl.program_id(0) < T_CTX // TILE_MIX
    halves = []
    for r0 in range(0, TILE_MIX, TILE_MIX // 2):
        rows = slice(r0, r0 + TILE_MIX // 2)
        attn = jnp.where(is_ctx, attn_c_ref[rows, :], attn_l_ref[rows, :])
        y = _dot(attn, wo_ref[:V_W, :]) + _dot(pool_ref[rows, :], wo_ref[V_W:, :])
        x_new = jnp.where(is_ctx, xc_ref[rows, :], xl_ref[rows, :]) + g1_ref[...] * y
        x_out[rows, :] = x_new
        halves.append((_rms(x_new) * n2_ref[...] * (1.0 + sc2_ref[...]) + sh2_ref[...]).astype(BF16))
    h = jnp.concatenate(halves, axis=0)
    if not with_router:
        _swiglu_hidden(h, wg_ref, wu_ref, a_buf)
        x_out[...] += g2_ref[...] * _dot(a_buf[...], wd_ref[...])
        return

    logits = _dot(h, wr_ref[...])
    lane = lax.broadcasted_iota(jnp.int32, logits.shape, 1)
    neg = float(jnp.finfo(F32).min)
    lg = jnp.where(lane < N_EXPERTS, logits, neg)
    m1 = lg.max(axis=-1, keepdims=True)
    i1 = jnp.where(lg == m1, lane, LANES).min(axis=-1, keepdims=True)
    lg2 = jnp.where(lane == i1, neg, lg)
    m2 = lg2.max(axis=-1, keepdims=True)
    i2 = jnp.where(lg2 == m2, lane, LANES).min(axis=-1, keepdims=True)
    e = jnp.exp(m2 - m1)
    w1 = 1.0 / (1.0 + e)
    w2 = e / (1.0 + e)

    sel1 = lane == i1
    sel2 = lane == i2
    member = jnp.where(jnp.logical_or(sel1, sel2), 1.0, 0.0)
    rank = _dot(ltri_ref[...], member.astype(BF16))
    n_tok = member.sum(axis=0, keepdims=True)
    units = jnp.floor((n_tok + (SLOT_UNIT - 1)) * (1.0 / SLOT_UNIT))
    unit_off = _dot(jnp.broadcast_to(units, (8, LANES)).astype(BF16), utri_ref[...])[0:1, :]
    slot_of = SLOT_UNIT * unit_off + rank
    slot1 = jnp.where(sel1, slot_of, 0.0).sum(axis=-1, keepdims=True)
    slot2 = jnp.where(sel2, slot_of, 0.0).sum(axis=-1, keepdims=True)
    route = jnp.where(lane == 0, slot1, jnp.where(lane == 1, slot2,
                      jnp.where(lane == 2, w1, jnp.where(lane == 3, w2, 0.0))))
    route_out[...] = route
    sub = lax.broadcasted_iota(jnp.int32, (8, LANES), 0)
    meta_out[...] = jnp.where(sub == 0, units, jnp.where(sub == 1, unit_off, 0.0)).astype(jnp.int32)

    rt = route.T
    s1 = rt[0:1, :].astype(jnp.int32)
    s2 = rt[1:2, :].astype(jnp.int32)
    def slot_block(r0):
        srow = r0 + lax.broadcasted_iota(jnp.int32, (SLOT_BLOCK, TILE_MIX), 0)
        hit1 = srow == s1
        hit2 = srow == s2
        perm = (jnp.where(hit1, 1.0, 0.0) + jnp.where(hit2, 1.0, 0.0)).astype(BF16)
        gate = (jnp.where(hit1, rt[2:3, :], 0.0) + jnp.where(hit2, rt[3:4, :], 0.0)).sum(axis=-1, keepdims=True)
        return perm, gate

    starts = list(range(0, SLOTS_PER_TILE, SLOT_BLOCK))
    nxt = slot_block(starts[0])
    for k, r0 in enumerate(starts):
        perm, gate = nxt
        if k + 1 < len(starts):
            nxt = slot_block(starts[k + 1])
        xs_out[r0:r0 + SLOT_BLOCK, :] = _dot(perm, h).astype(BF16)
        gs_out[r0:r0 + SLOT_BLOCK, :] = jnp.broadcast_to(gate, (SLOT_BLOCK, LANES))


def _mod_row(i, tile):
    n_ctx = T_CTX // tile
    per_seq = DEC_SEQ // tile
    return jnp.where(i < n_ctx, 0, 1 + (i - n_ctx) // per_seq)


def _mix_call(attn_ctx, attn_lat, pooled, x_ctx, x_lat, ada, n2, w_out, w_router=None, ffn_w=None):
    with_router = w_router is not None
    assert with_router != (ffn_w is not None)
    tm = TILE_MIX
    n_ctx = T_CTX // tm
    lat_tile0 = n_ctx if x_lat.shape[0] == T_ALL else 0
    row = lambda i: (i, 0)
    const2 = lambda i: (0, 0)
    mod = lambda k: pl.BlockSpec((None, 1, D_MODEL), lambda i: (_mod_row(i, tm), 0, k))
    ctx_blk = lambda i: (jnp.minimum(i, n_ctx - 1), 0)
    in_specs = [pl.BlockSpec((tm, V_W), ctx_blk),
                pl.BlockSpec((tm, V_W), lambda i: (jnp.maximum(i - n_ctx, 0), 0)),
                pl.BlockSpec((tm, POOL_W), row),
                pl.BlockSpec((tm, D_MODEL), ctx_blk),
                pl.BlockSpec((tm, D_MODEL), lambda i: (jnp.maximum(i - n_ctx, 0) + lat_tile0, 0)),
                mod(2), mod(3), mod(4),
                pl.BlockSpec((1, D_MODEL), const2), pl.BlockSpec((D_MODEL, D_MODEL), const2)]
    args = [attn_ctx, attn_lat, pooled, x_ctx, x_lat, ada, ada, ada, n2, w_out]
    if with_router:
        t_i = jnp.arange(tm)
        ltri = (t_i[None, :] < t_i[:, None]).astype(BF16)
        l_i = jnp.arange(LANES)
        utri = (l_i[:, None] < l_i[None, :]).astype(BF16)
        in_specs += [pl.BlockSpec((D_MODEL, LANES), const2), pl.BlockSpec((tm, tm), const2),
                     pl.BlockSpec((LANES, LANES), const2)]
        args += [w_router, ltri, utri]
        out_shape = [jax.ShapeDtypeStruct((T_ALL, D_MODEL), F32),
                     jax.ShapeDtypeStruct((N_ROUTE_TILES * SLOTS_PER_TILE, D_MODEL), BF16),
                     jax.ShapeDtypeStruct((N_ROUTE_TILES * SLOTS_PER_TILE, LANES), F32),
                     jax.ShapeDtypeStruct((T_ALL, LANES), F32),
                     jax.ShapeDtypeStruct((N_ROUTE_TILES, 8, LANES), jnp.int32)]
        out_specs = [pl.BlockSpec((tm, D_MODEL), row), pl.BlockSpec((SLOTS_PER_TILE, D_MODEL), row),
                     pl.BlockSpec((SLOTS_PER_TILE, LANES), row), pl.BlockSpec((tm, LANES), row),
                     pl.BlockSpec((None, 8, LANES), lambda i: (i, 0, 0))]
        scratch = []
    else:
        resident = lambda shape: pl.BlockSpec(shape, const2, pipeline_mode=pl.Buffered(1))
        in_specs += [mod(5), resident((D_MODEL, D_FF)), resident((D_MODEL, D_FF)), resident((D_FF, D_MODEL))]
        args += [ada, *ffn_w]
        out_shape = [jax.ShapeDtypeStruct((T_ALL, D_MODEL), F32)]
        out_specs = [pl.BlockSpec((tm, D_MODEL), row)]
        scratch = [pltpu.VMEM((tm, D_FF), BF16)]
    return pl.pallas_call(
        functools.partial(_mix_kernel, with_router),
        out_shape=tuple(out_shape),
        grid=(T_ALL // tm,),
        in_specs=in_specs,
        out_specs=tuple(out_specs),
        scratch_shapes=scratch,
        compiler_params=_cparams("arbitrary"),
        name="mix_router" if with_router else "mix_ffn",
    )(*args)


def _route_tables(meta):
    units = meta[:, 0, :N_EXPERTS]
    offs = meta[:, 1, :N_EXPERTS]
    cum = jnp.cumsum(units, axis=0)
    total = cum[-1]
    tiles_e = (total + GEMM_UNITS - 1) // GEMM_UNITS
    tile_end = jnp.cumsum(tiles_e)
    n_act = tile_end[-1]
    m = jnp.arange(N_GEMM_TILES)
    m_eff = jnp.minimum(m, jnp.maximum(n_act - 1, 0))
    expert = jnp.minimum(jnp.sum(tile_end[None, :] <= m_eff[:, None], axis=1), N_EXPERTS - 1)
    is_e = (expert[:, None] == jnp.arange(N_EXPERTS)[None, :]).astype(jnp.int32)
    pick = lambda per_expert: jnp.sum(is_e * per_expert[None, :], axis=1)
    first_q = (m_eff - pick(tile_end - tiles_e)) * GEMM_UNITS
    count = jnp.where(m < n_act, jnp.clip(pick(total) - first_q, 0, GEMM_UNITS), 0)
    q = first_q[:, None] + jnp.arange(GEMM_UNITS)[None, :]
    rows_of = lambda table: jnp.sum(is_e[:, None, :] * table[None, :, :], axis=2)
    cum_e, units_e, offs_e = rows_of(cum), rows_of(units), rows_of(offs)
    src_tile = jnp.minimum(jnp.sum(cum_e[:, None, :] <= q[:, :, None], axis=2), N_ROUTE_TILES - 1)
    is_t = (src_tile[:, :, None] == jnp.arange(N_ROUTE_TILES)[None, None, :]).astype(jnp.int32)
    at_tile = lambda per_tile: jnp.sum(is_t * per_tile[:, None, :], axis=2)
    uid = src_tile * UNITS_PER_TILE + at_tile(offs_e) + (q - at_tile(cum_e - units_e))
    uid = jnp.where(jnp.arange(GEMM_UNITS)[None, :] < count[:, None], uid, ZERO_UNIT)
    i32 = jnp.int32
    return expert.astype(i32), count.astype(i32), uid.reshape(-1).astype(i32), n_act.reshape(1).astype(i32)


def _gemm_kernel(em_ref, cnt_ref, ul_ref, nact_ref, xs_hbm, gs_hbm, wg_ref, wu_ref, wd_ref, ys_hbm,
                 xbuf, gbuf, obuf, acc_ref, a_buf, sem_in, sem_out):
    del em_ref
    m = pl.program_id(0)
    j = pl.program_id(1)
    last_j = pl.num_programs(1) - 1
    n_act = nact_ref[0]
    active = m < n_act
    slot = m % 2

    def rows(r):
        return pl.ds(r * SLOT_UNIT if isinstance(r, int) else pl.multiple_of(r * SLOT_UNIT, SLOT_UNIT), SLOT_UNIT)

    def in_copies(mm, sl, r):
        uid = ul_ref[mm * GEMM_UNITS + r]
        return (pltpu.make_async_copy(xs_hbm.at[uid], xbuf.at[sl, rows(r)], sem_in.at[sl]),
                pltpu.make_async_copy(gs_hbm.at[uid], gbuf.at[sl, rows(r)], sem_in.at[sl]))

    def out_copy(mm, r):
        uid = ul_ref[mm * GEMM_UNITS + r]
        return pltpu.make_async_copy(obuf.at[rows(r)], ys_hbm.at[uid], sem_out.at[0])

    def for_slot(sl_dyn, fn):
        if isinstance(sl_dyn, int):
            fn(sl_dyn)
            return
        for sl in range(2):
            pl.when(sl_dyn == sl)(functools.partial(fn, sl))

    def start_in(mm, sl_dyn):
        def issue(sl):
            for r in range(GEMM_UNITS):
                for cp in in_copies(mm, sl, r):
                    cp.start()

        for_slot(sl_dyn, issue)

    def wait_in(mm, sl_dyn):
        def wait(sl):
            for r in range(GEMM_UNITS):
                for cp in in_copies(mm, sl, r):
                    cp.wait()

        for_slot(sl_dyn, wait)

    def start_out(mm):
        n = cnt_ref[mm]
        for r in range(GEMM_UNITS):
            pl.when(r < n)(lambda r=r: out_copy(mm, r).start())

    def wait_out(mm):
        def body(r, carry):
            out_copy(mm, r).wait()
            return carry

        lax.fori_loop(0, cnt_ref[mm], body, 0)

    @pl.when(jnp.logical_and(active, j == 0))
    def _():
        @pl.when(m == 0)
        def _():
            start_in(0, 0)

        wait_in(m, slot)

        @pl.when(m + 1 < n_act)
        def _():
            start_in(m + 1, 1 - slot)

    @pl.when(jnp.logical_and(jnp.logical_and(active, j == last_j), m > 0))
    def _():
        wait_out(m - 1)

    n_valid = cnt_ref[m]
    for quarter in range(1, GEMM_QUARTERS + 1):
        n_rows = quarter * GEMM_ROWS // GEMM_QUARTERS
        lo_units = (quarter - 1) * GEMM_UNITS // GEMM_QUARTERS
        hi_units = quarter * GEMM_UNITS // GEMM_QUARTERS

        @pl.when(jnp.logical_and(active, jnp.logical_and(n_valid > lo_units, n_valid <= hi_units)))
        def _(n_rows=n_rows):
            _swiglu_hidden(xbuf[slot, :n_rows, :], wg_ref, wu_ref, a_buf.at[:n_rows, :])
            part = _dot(a_buf[:n_rows, :], wd_ref[...])

            @pl.when(j == 0)
            def _():
                acc_ref[:n_rows, :] = part

            @pl.when(jnp.logical_and(j > 0, j < last_j))
            def _():
                acc_ref[:n_rows, :] += part

            @pl.when(j == last_j)
            def _():
                gate = gbuf[slot, :n_rows, :]
                for cb in range(D_MODEL // LANES):
                    cols = slice(cb * LANES, (cb + 1) * LANES)
                    obuf[:n_rows, cols] = ((acc_ref[:n_rows, cols] + part[:, cols]) * gate).astype(BF16)

    @pl.when(jnp.logical_and(active, j == last_j))
    def _():
        start_out(m)

        @pl.when(m == n_act - 1)
        def _():
            wait_out(m)


def _gemm_call(expert, count, uids, n_act, xs, gs, wg, wu, wd):
    tf = TILE_FE
    n_j = D_FF_EXPERT // tf
    assert n_j >= 2

    def w_col(m, j, em, cnt, ul, nact):
        return (em[m], 0, jnp.where(m < nact[0], j, n_j - 1))

    def w_row(m, j, em, cnt, ul, nact):
        return (em[m], jnp.where(m < nact[0], j, n_j - 1), 0)

    grid_spec = pltpu.PrefetchScalarGridSpec(
        num_scalar_prefetch=4,
        grid=(N_GEMM_TILES, n_j),
        in_specs=[pl.BlockSpec(memory_space=pl.ANY),
                  pl.BlockSpec(memory_space=pl.ANY),
                  pl.BlockSpec((None, D_MODEL, tf), w_col),
                  pl.BlockSpec((None, D_MODEL, tf), w_col),
                  pl.BlockSpec((None, tf, D_MODEL), w_row)],
        out_specs=pl.BlockSpec(memory_space=pl.ANY),
        scratch_shapes=[pltpu.VMEM((2, GEMM_ROWS, D_MODEL), BF16),
                        pltpu.VMEM((2, GEMM_ROWS, LANES), F32),
                        pltpu.VMEM((GEMM_ROWS, D_MODEL), BF16),
                        pltpu.VMEM((GEMM_ROWS, D_MODEL), F32),
                        pltpu.VMEM((GEMM_ROWS, tf), BF16),
                        pltpu.SemaphoreType.DMA((2,)),
                        pltpu.SemaphoreType.DMA((1,))])
    return pl.pallas_call(
        _gemm_kernel,
        out_shape=jax.ShapeDtypeStruct((N_UNITS, SLOT_UNIT, D_MODEL), BF16),
        grid_spec=grid_spec,
        input_output_aliases={4: 0},
        compiler_params=_cparams("arbitrary", "arbitrary"),
        name="moe_experts",
    )(expert, count, uids, n_act,
      xs.reshape(N_UNITS, SLOT_UNIT, D_MODEL), gs.reshape(N_UNITS, SLOT_UNIT, LANES), wg, wu, wd)


def _combine_kernel(ys_ref, route_ref, x_ref, g2_ref, fg_ref, o_ref):
    route = route_ref[...]
    s1 = route[:, 0:1].astype(jnp.int32)
    s2 = route[:, 1:2].astype(jnp.int32)
    scol = lax.broadcasted_iota(jnp.int32, (TILE_MIX, SLOTS_PER_TILE), 1)
    unperm = (jnp.where(scol == s1, 1.0, 0.0) + jnp.where(scol == s2, 1.0, 0.0)).astype(BF16)
    y = _dot(unperm, ys_ref[...])
    o_ref[...] = _rms(x_ref[...] + g2_ref[...] * y) * fg_ref[...]


def _combine_call(ys, route, x, ada, final_g, tile0, n_tok):
    tm = TILE_MIX
    return pl.pallas_call(
        _combine_kernel,
        out_shape=jax.ShapeDtypeStruct((n_tok, D_MODEL), F32),
        grid=(n_tok // tm,),
        in_specs=[pl.BlockSpec((SLOTS_PER_TILE, D_MODEL), lambda i: (i + tile0, 0)),
                  pl.BlockSpec((tm, LANES), lambda i: (i + tile0, 0)),
                  pl.BlockSpec((tm, D_MODEL), lambda i: (i + tile0, 0)),
                  pl.BlockSpec((None, 1, D_MODEL), lambda i: (_mod_row(i + tile0, tm), 0, 5)),
                  pl.BlockSpec((1, D_MODEL), lambda i: (0, 0))],
        out_specs=pl.BlockSpec((tm, D_MODEL), lambda i: (i, 0)),
        compiler_params=_cparams("arbitrary"),
        name="moe_combine",
    )(ys.reshape(N_UNITS * SLOT_UNIT, D_MODEL), route, x, ada, final_g)


def _rot_cols(w):
    q = QK_ROPE // 4
    return jnp.concatenate([-w[:, q:2 * q], w[:, :q], -w[:, 3 * q:], w[:, 2 * q:3 * q]], axis=1)


def _rope_tables():
    t = jnp.arange(DEC_SEQ)
    rows = (t // GRID_W).astype(F32)
    cols = (t % GRID_W).astype(F32)
    half = QK_ROPE // 2
    freqs = ROPE_BASE ** (-jnp.arange(0, half, 2, dtype=F32) / half)
    ang_r = rows[:, None] * freqs
    ang_c = cols[:, None] * freqs
    ang = jnp.concatenate([ang_r, ang_r, ang_c, ang_c], axis=-1)
    cos, sin = jnp.cos(ang), jnp.sin(ang)
    one, zero = jnp.ones_like(cos), jnp.zeros_like(cos)
    tq = jnp.stack([jnp.concatenate([one, zero], -1), jnp.concatenate([cos, sin], -1)])
    tc = jnp.stack([jnp.concatenate([one, one], -1), jnp.concatenate([cos, cos], -1)])
    ts = jnp.stack([jnp.concatenate([zero, zero], -1), jnp.concatenate([sin, sin], -1)])
    return tq, tc, ts


def _layer_weights(l, w_in, w_uq, w_ukv):
    wi = w_in[l]
    w_kpe = wi[:, OFF_U:OFF_U + QK_ROPE]
    w_rot = _rot_cols(w_kpe)
    w_in_r = jnp.concatenate([wi[:, :OFF_U], wi[:, OFF_U + QK_ROPE:], w_kpe, w_kpe, w_rot, w_rot], axis=1)
    heads = []
    for h in range(N_HEADS):
        blk = w_uq[l][:, h * (QK_NOPE + QK_ROPE):(h + 1) * (QK_NOPE + QK_ROPE)]
        heads += [blk, _rot_cols(blk[:, QK_NOPE:])]
    w_uq_r = jnp.concatenate(heads, axis=1)
    kv = w_ukv[l].reshape(KV_LORA, N_HEADS, QK_NOPE + V_DIM)
    w_uk = kv[:, :, :QK_NOPE].reshape(KV_LORA, N_HEADS * QK_NOPE)
    w_uv = kv[:, :, QK_NOPE:].reshape(KV_LORA, V_W)
    return w_in_r.astype(BF16), w_uq_r.astype(BF16), w_uk.astype(BF16), w_uv.astype(BF16)


def kernel(x_prompt, x_sample, c, cache_ckv, cache_kpe, c_ctx, norm1_g, norm2_g, w_ada, b_ada, w_in, q_norm_g,
           kv_norm_g, w_uq, w_ukv, w_pool, pool_scale, w_out, ffn_w_gate, ffn_w_up, ffn_w_down, moe_w_router,
           moe_w_gate, moe_w_up, moe_w_down, final_norm_g):
    x_ctx = x_prompt.reshape(T_CTX, D_MODEL)
    x_lat = x_sample.reshape(T_LAT, D_MODEL)
    cond = jnp.concatenate([c_ctx[None, :], c, jnp.zeros((COND_ROWS - 1 - DEC_BATCH, D_MODEL), F32)], axis=0)
    ada_all = _ada_call(cond, w_ada, b_ada)
    tq, tc, ts = _rope_tables()

    lw = [_layer_weights(l, w_in, w_uq, w_ukv) for l in range(DEPTH)]
    dup = jnp.concatenate([jnp.eye(QK_ROPE, dtype=BF16)] * 2, axis=1)
    kc_all, vc_all = _cache_call(cache_ckv, cache_kpe, jnp.stack([w[2] for w in lw]),
                                 jnp.stack([w[3] for w in lw]), dup)

    assert DEPTH == 2
    new_ckv = new_kpe = None
    for l in range(DEPTH):
        w_in_r, w_uq_r, w_uk, w_uv = lw[l]
        ada = ada_all[l].reshape(COND_ROWS, 1, 6 * D_MODEL)
        q, k, v, new_ckv, new_kpe, pooled = _in_call(
            x_ctx, x_lat, ada, norm1_g[l][None, :], w_in_r, q_norm_g[l][None, :], kv_norm_g[l][None, :], w_uq_r,
            w_uk, w_uv, tq, tc, ts, w_pool[l].astype(BF16), pool_scale[l][None, :], new_ckv, new_kpe)
        attn_c, attn_l = _attn_call(q, k, v, kc_all[l], vc_all[l])
        j = l // 2
        if l % 2 == 0:
            ffn_w = (ffn_w_gate[j].astype(BF16), ffn_w_up[j].astype(BF16), ffn_w_down[j].astype(BF16))
            (x,) = _mix_call(attn_c, attn_l, pooled, x_ctx, x_lat, ada, norm2_g[l][None, :],
                             w_out[l].astype(BF16), ffn_w=ffn_w)
            x_ctx = x_lat = x
        else:
            w_r = jnp.pad(moe_w_router[j], ((0, 0), (0, LANES - N_EXPERTS))).astype(BF16)
            x, xs, gs, route, meta = _mix_call(attn_c, attn_l, pooled, x_ctx, x_lat, ada, norm2_g[l][None, :],
                                               w_out[l].astype(BF16), w_router=w_r)
            expert, count, uids, n_act = _route_tables(meta)
            ys = _gemm_call(expert, count, uids, n_act, xs, gs, moe_w_gate[j].astype(BF16),
                            moe_w_up[j].astype(BF16), moe_w_down[j].astype(BF16))
            fg = final_norm_g[None, :]
            y_prompt = _combine_call(ys, route, x, ada, fg, 0, T_CTX).reshape(BATCH, SEQ, D_MODEL)
            y_sample = _combine_call(ys, route, x, ada, fg, T_CTX // TILE_MIX, T_LAT).reshape(
                DEC_BATCH, DEC_SEQ, D_MODEL)

    return y_prompt, y_sample, new_ckv, new_kpe
```

```python
import functools

import jax
import jax.numpy as jnp
from jax import lax
from jax.experimental import pallas as pl
from jax.experimental.pallas import tpu as pltpu

D_MODEL = 1024
BATCH = 32
SEQ = 256
DEPTH = 2
DEC_BATCH = 8
DEC_SEQ = 1024
PAST_LEN = 512
GRID_W = 64
N_HEADS = 4
QK_NOPE = 128
QK_ROPE = 64
V_DIM = 128
Q_LORA = 384
KV_LORA = 256
POOL_W = 512
POOL_GROUPS = 4
POOL_WINDOWS = (2, 4, 8, 16)
POOL_CH = POOL_W // POOL_GROUPS
D_FF = 2816
N_EXPERTS = 8
D_FF_EXPERT = 3584
ROPE_BASE = 10000.0
EPS = 1e-6

T_CTX = BATCH * SEQ
T_LAT = DEC_BATCH * DEC_SEQ
T_ALL = T_CTX + T_LAT

LANES = 128
HEAD_PAD = 256
QK_W = N_HEADS * HEAD_PAD
V_W = N_HEADS * V_DIM
OFF_CKV = Q_LORA
OFF_U = Q_LORA + KV_LORA
OFF_KA = OFF_U + POOL_W
OFF_KB = OFF_KA + LANES
N_PROJ = OFF_KB + LANES

TILE_IN = 1024
CHUNK = 256
HALO = 8
REGION = CHUNK + 2 * HALO
CTX_SEQS_PER_STEP = 4
LAT_Q_TILE = 1024
TILE_MIX = 512
TOP_K = 2
SLOT_UNIT = 16
UNITS_PER_TILE = TILE_MIX * TOP_K // SLOT_UNIT + N_EXPERTS
SLOTS_PER_TILE = UNITS_PER_TILE * SLOT_UNIT
SLOT_BLOCK = SLOTS_PER_TILE // 4
N_ROUTE_TILES = T_ALL // TILE_MIX
N_UNITS = N_ROUTE_TILES * UNITS_PER_TILE
ZERO_UNIT = UNITS_PER_TILE - 1
GEMM_UNITS = 64
GEMM_ROWS = GEMM_UNITS * SLOT_UNIT
GEMM_QUARTERS = 4
N_GEMM_TILES = -(-N_UNITS // GEMM_UNITS) + N_EXPERTS
TILE_FE = D_FF_EXPERT // 2
FF_CHUNK = 256
COND_ROWS = 16
VMEM_LIMIT = 56 * 1024 * 1024

F32 = jnp.float32
BF16 = jnp.bfloat16


def _rms(x):
    return x * lax.rsqrt(jnp.mean(x * x, axis=-1, keepdims=True) + EPS)


def _dot(a, b):
    return jnp.dot(a, b, preferred_element_type=F32)


def _dot_nt(a, b):
    return lax.dot_general(a, b, (((1,), (1,)), ((), ())), preferred_element_type=F32)


def _silu(x):
    return x * (1.0 / (1.0 + jnp.exp(-x)))


def _cparams(*sem):
    return pltpu.CompilerParams(dimension_semantics=sem, vmem_limit_bytes=VMEM_LIMIT)


def _ada_kernel(cond_ref, w_ref, b_ref, o_ref):
    c = cond_ref[...]
    o_ref[...] = jnp.dot(_silu(c), w_ref[...], preferred_element_type=F32,
                         precision=lax.Precision.HIGHEST) + b_ref[...]


def _ada_call(cond, w_ada, b_ada):
    n_blk = 4
    bw = 6 * D_MODEL // n_blk
    return pl.pallas_call(
        _ada_kernel,
        out_shape=jax.ShapeDtypeStruct((DEPTH, COND_ROWS, 6 * D_MODEL), F32),
        grid=(DEPTH, n_blk),
        in_specs=[
            pl.BlockSpec((COND_ROWS, D_MODEL), lambda l, j: (0, 0)),
            pl.BlockSpec((None, D_MODEL, bw), lambda l, j: (l, 0, j)),
            pl.BlockSpec((None, 1, bw), lambda l, j: (l, 0, j)),
        ],
        out_specs=pl.BlockSpec((None, COND_ROWS, bw), lambda l, j: (l, 0, j)),
        compiler_params=_cparams("arbitrary", "arbitrary"),
        name="ada_params",
    )(cond, w_ada, b_ada.reshape(DEPTH, 1, 6 * D_MODEL))


def _cache_kernel(ckv_ref, kpe_ref, wuk_ref, wuv_ref, dup_ref, k_ref, v_ref):
    ckv = ckv_ref[...].astype(BF16)
    knope = _dot(ckv, wuk_ref[...])
    v_ref[...] = _dot(ckv, wuv_ref[...]).astype(BF16)
    kdup = _dot(kpe_ref[...].astype(BF16), dup_ref[...]).astype(BF16)
    for h in range(N_HEADS):
        k_ref[:, h * HEAD_PAD:h * HEAD_PAD + QK_NOPE] = knope[:, h * QK_NOPE:(h + 1) * QK_NOPE].astype(BF16)
        k_ref[:, h * HEAD_PAD + QK_NOPE:(h + 1) * HEAD_PAD] = kdup


def _cache_call(cache_ckv, cache_kpe, w_uk, w_uv, dup):
    n_tok = DEC_BATCH * PAST_LEN
    return pl.pallas_call(
        _cache_kernel,
        out_shape=(jax.ShapeDtypeStruct((DEPTH, n_tok, QK_W), BF16),
                   jax.ShapeDtypeStruct((DEPTH, n_tok, V_W), BF16)),
        grid=(DEPTH, DEC_BATCH),
        in_specs=[
            pl.BlockSpec((None, None, PAST_LEN, KV_LORA), lambda l, b: (b, l, 0, 0)),
            pl.BlockSpec((None, None, PAST_LEN, QK_ROPE), lambda l, b: (b, l, 0, 0)),
            pl.BlockSpec((None, KV_LORA, N_HEADS * QK_NOPE), lambda l, b: (l, 0, 0)),
            pl.BlockSpec((None, KV_LORA, V_W), lambda l, b: (l, 0, 0)),
            pl.BlockSpec((QK_ROPE, LANES), lambda l, b: (0, 0)),
        ],
        out_specs=(pl.BlockSpec((None, PAST_LEN, QK_W), lambda l, b: (l, b, 0)),
                   pl.BlockSpec((None, PAST_LEN, V_W), lambda l, b: (l, b, 0))),
        compiler_params=_cparams("arbitrary", "arbitrary"),
        name="cache_kv",
    )(cache_ckv, cache_kpe, w_uk, w_uv, dup)


def _in_kernel(n_prev, xc_ref, xl_ref, sh_ref, sc_ref, g_ref, win_ref, qg_ref, kvg_ref, wuq_ref, wuk_ref, wuv_ref,
               tq_ref, tc_ref, ts_ref, wpool_ref, ps_ref, *rest):
    if n_prev:
        prev_ckv_ref, prev_kpe_ref, *rest = rest
    q_out, k_out, v_out, ckv_out, kpe_out, pool_out, u_scr = rest
    is_ctx = pl.program_id(0) >= (T_ALL - T_CTX) // TILE_IN
    if n_prev:
        ckv_out[:, :n_prev] = prev_ckv_ref[...]
        kpe_out[:, :n_prev] = prev_kpe_ref[...]
    shift = sh_ref[...]
    scale1 = 1.0 + sc_ref[...]
    qk_scale = (QK_NOPE + QK_ROPE) ** -0.5
    n_chunks = TILE_IN // CHUNK
    zeros = jnp.zeros((HALO, POOL_W), F32)
    seq_len = jnp.where(is_ctx, SEQ, DEC_SEQ)
    row = lax.broadcasted_iota(jnp.int32, (CHUNK, POOL_CH), 0)

    def pool_chunk(c):
        base = c * REGION + HALO
        rows = slice(c * CHUNK, (c + 1) * CHUNK)
        t = row + jnp.where(is_ctx, 0, c * CHUNK)
        for g, w in enumerate(POOL_WINDOWS):
            cols = slice(g * POOL_CH, (g + 1) * POOL_CH)
            acc = u_scr[base - w // 2:base - w // 2 + CHUNK, cols]
            for j in range(-w // 2 + 1, w // 2):
                acc = acc + u_scr[base + j:base + j + CHUNK, cols]
            cnt = jnp.minimum(t + w // 2, seq_len) - jnp.maximum(t - w // 2, 0)
            pooled = acc / cnt.astype(F32) - u_scr[base:base + CHUNK, cols]
            lin = _dot(pooled.astype(BF16), wpool_ref[g]) * ps_ref[:, cols]
            pool_out[rows, cols] = lin.astype(BF16)

    for c in range(n_chunks):
        r0 = c * CHUNK
        rows = slice(r0, r0 + CHUNK)
        x = jnp.where(is_ctx, xc_ref[rows, :], xl_ref[rows, :])
        h = (_rms(x) * g_ref[...] * scale1 + shift).astype(BF16)
        proj = _dot(h, win_ref[...])

        qn = (_rms(proj[:, :Q_LORA]) * qg_ref[...]).astype(BF16)
        q = _dot(qn, wuq_ref[...]) * qk_scale
        tq = tq_ref[rows, :]
        for hd in range(N_HEADS):
            lo = hd * HEAD_PAD
            q_out[rows, lo:lo + QK_NOPE] = q[:, lo:lo + QK_NOPE].astype(BF16)
            q_out[rows, lo + QK_NOPE:lo + HEAD_PAD] = (q[:, lo + QK_NOPE:lo + HEAD_PAD] * tq).astype(BF16)

        ckv = _rms(proj[:, OFF_CKV:OFF_U]) * kvg_ref[...]
        ckv_out[c, n_prev] = ckv
        ckv_b = ckv.astype(BF16)
        knope = _dot(ckv_b, wuk_ref[...])
        v_out[rows, :] = _dot(ckv_b, wuv_ref[...]).astype(BF16)

        k_a = proj[:, OFF_KA:OFF_KB]
        k_b = proj[:, OFF_KB:N_PROJ]
        kpe_out[c, n_prev] = k_a[:, :QK_ROPE]
        kr = (k_a * tc_ref[rows, :] + k_b * ts_ref[rows, :]).astype(BF16)
        for hd in range(N_HEADS):
            lo = hd * HEAD_PAD
            k_out[rows, lo:lo + QK_NOPE] = knope[:, hd * QK_NOPE:(hd + 1) * QK_NOPE].astype(BF16)
            k_out[rows, lo + QK_NOPE:lo + HEAD_PAD] = kr

        u = proj[:, OFF_U:OFF_KA]
        base = c * REGION + HALO
        u_scr[base:base + CHUNK, :] = u
        if c == 0:
            u_scr[0:HALO, :] = zeros
        else:
            u_scr[base - 2 * HALO:base - HALO, :] = jnp.where(is_ctx, zeros, u[:HALO, :])
        if c == n_chunks - 1:
            u_scr[base + CHUNK:base + CHUNK + HALO, :] = zeros
        else:
            u_scr[base + CHUNK + HALO:base + CHUNK + 2 * HALO, :] = jnp.where(is_ctx, zeros, u[CHUNK - HALO:, :])
        if c > 0:
            pool_chunk(c - 1)
    pool_chunk(n_chunks - 1)


def _in_call(x_ctx, x_lat, ada, g1, w_in, qg, kvg, w_uq, w_uk, w_uv, tq, tc, ts, w_pool, pscale, prev_ckv, prev_kpe):
    assert CHUNK == SEQ
    n_tiles = T_ALL // TILE_IN
    n_ctx = T_CTX // TILE_IN
    n_prev = 0 if prev_ckv is None else prev_ckv.shape[1]
    seqs = TILE_IN // SEQ
    lat_tile0 = n_ctx if x_lat.shape[0] == T_ALL else 0

    def tile(i):
        return jnp.where(i < n_tiles - n_ctx, i + n_ctx, i - (n_tiles - n_ctx))

    def cond_row(i):
        return jnp.maximum(tile(i) - n_ctx + 1, 0)

    def tab(i):
        return (jnp.minimum(jnp.maximum(tile(i) - n_ctx + 1, 0), 1), 0, 0)

    rows = lambda i: (tile(i), 0)
    ctx_blk4 = lambda i: (jnp.where(tile(i) < n_ctx, tile(i), 0), 0, 0, 0)
    const2 = lambda i: (0, 0)
    prev_specs = [pl.BlockSpec((seqs, n_prev, SEQ, KV_LORA), ctx_blk4),
                  pl.BlockSpec((seqs, n_prev, SEQ, QK_ROPE), ctx_blk4)] if n_prev else []
    prev_args = [prev_ckv, prev_kpe] if n_prev else []
    return pl.pallas_call(
        functools.partial(_in_kernel, n_prev),
        out_shape=(jax.ShapeDtypeStruct((T_ALL, QK_W), BF16),
                   jax.ShapeDtypeStruct((T_ALL, QK_W), BF16),
                   jax.ShapeDtypeStruct((T_ALL, V_W), BF16),
                   jax.ShapeDtypeStruct((BATCH, n_prev + 1, SEQ, KV_LORA), F32),
                   jax.ShapeDtypeStruct((BATCH, n_prev + 1, SEQ, QK_ROPE), F32),
                   jax.ShapeDtypeStruct((T_ALL, POOL_W), BF16)),
        grid=(n_tiles,),
        in_specs=[
            pl.BlockSpec((TILE_IN, D_MODEL), lambda i: (jnp.minimum(tile(i), n_ctx - 1), 0)),
            pl.BlockSpec((TILE_IN, D_MODEL), lambda i: (jnp.maximum(tile(i) - n_ctx, 0) + lat_tile0, 0)),
            pl.BlockSpec((None, 1, D_MODEL), lambda i: (cond_row(i), 0, 0)),
            pl.BlockSpec((None, 1, D_MODEL), lambda i: (cond_row(i), 0, 1)),
            pl.BlockSpec((1, D_MODEL), const2),
            pl.BlockSpec((D_MODEL, N_PROJ), const2),
            pl.BlockSpec((1, Q_LORA), const2),
            pl.BlockSpec((1, KV_LORA), const2),
            pl.BlockSpec((Q_LORA, QK_W), const2),
            pl.BlockSpec((KV_LORA, N_HEADS * QK_NOPE), const2),
            pl.BlockSpec((KV_LORA, V_W), const2),
            pl.BlockSpec((None, TILE_IN, LANES), tab),
            pl.BlockSpec((None, TILE_IN, LANES), tab),
            pl.BlockSpec((None, TILE_IN, LANES), tab),
            pl.BlockSpec((POOL_GROUPS, POOL_CH, POOL_CH), lambda i: (0, 0, 0)),
            pl.BlockSpec((1, POOL_W), const2),
        ] + prev_specs,
        out_specs=(pl.BlockSpec((TILE_IN, QK_W), rows),
                   pl.BlockSpec((TILE_IN, QK_W), rows),
                   pl.BlockSpec((TILE_IN, V_W), rows),
                   pl.BlockSpec((seqs, n_prev + 1, SEQ, KV_LORA), ctx_blk4),
                   pl.BlockSpec((seqs, n_prev + 1, SEQ, QK_ROPE), ctx_blk4),
                   pl.BlockSpec((TILE_IN, POOL_W), rows)),
        scratch_shapes=[pltpu.VMEM((TILE_IN // CHUNK * REGION, POOL_W), F32)],
        compiler_params=_cparams("arbitrary"),
        name="in_proj",
    )(x_ctx, x_lat, ada, ada, g1, w_in, qg, kvg, w_uq, w_uk, w_uv, tq, tc, ts, w_pool, pscale, *prev_args)


def _softmax_pv(scores, values):
    m = scores[0].max(axis=-1, keepdims=True)
    for s in scores[1:]:
        m = jnp.maximum(m, s.max(axis=-1, keepdims=True))
    den = None
    out = None
    for s, v in zip(scores, values):
        p = jnp.exp(s - m)
        d = p.sum(axis=-1, keepdims=True)
        o = _dot(p.astype(BF16), v)
        den = d if den is None else den + d
        out = o if out is None else out + o
    return out / den


def _attn_ctx_kernel(q_ref, k_ref, v_ref, o_ref):
    for b in range(CTX_SEQS_PER_STEP):
        rows = slice(b * SEQ, (b + 1) * SEQ)
        for hd in range(N_HEADS):
            qk = slice(hd * HEAD_PAD, (hd + 1) * HEAD_PAD)
            vv = slice(hd * V_DIM, (hd + 1) * V_DIM)
            s = _dot_nt(q_ref[rows, qk], k_ref[rows, qk])
            o_ref[rows, vv] = _softmax_pv([s], [v_ref[rows, vv]]).astype(BF16)


def _cast_slab(n_steps, w):
    n_rows, n_cols = w.shape
    slab = n_rows // n_steps
    assert slab * n_steps == n_rows
    in_spec = lambda step_of: pl.BlockSpec((slab, n_cols), lambda *g: (step_of(*g), 0))
    return w, in_spec, jax.ShapeDtypeStruct(w.shape, BF16), in_spec


def _attn_lat_kernel(q_ref, k_ref, v_ref, kc_ref, vc_ref, *rest):
    if len(rest) == 3:
        w_ref, o_ref, wb_ref = rest
        wb_ref[...] = w_ref[...].astype(BF16)
    else:
        (o_ref,) = rest
    for hd in range(N_HEADS):
        qk = slice(hd * HEAD_PAD, (hd + 1) * HEAD_PAD)
        vv = slice(hd * V_DIM, (hd + 1) * V_DIM)
        q = q_ref[:, qk]
        s1 = _dot_nt(q, k_ref[:, qk])
        s2 = _dot_nt(q, kc_ref[:, qk])
        o_ref[:, vv] = _softmax_pv([s1, s2], [v_ref[:, vv], vc_ref[:, vv]]).astype(BF16)


def _attn_call(q, k, v, kc, vc, cast_w=None):
    attn_ctx = pl.pallas_call(
        _attn_ctx_kernel,
        out_shape=jax.ShapeDtypeStruct((T_CTX, V_W), BF16),
        grid=(BATCH // CTX_SEQS_PER_STEP,),
        in_specs=[pl.BlockSpec((CTX_SEQS_PER_STEP * SEQ, QK_W), lambda b: (b, 0)),
                  pl.BlockSpec((CTX_SEQS_PER_STEP * SEQ, QK_W), lambda b: (b, 0)),
                  pl.BlockSpec((CTX_SEQS_PER_STEP * SEQ, V_W), lambda b: (b, 0))],
        out_specs=pl.BlockSpec((CTX_SEQS_PER_STEP * SEQ, V_W), lambda b: (b, 0)),
        compiler_params=_cparams("arbitrary"),
        name="attn_ctx",
    )(q, k, v)

    tq = LAT_Q_TILE if cast_w is None else LAT_Q_TILE // 2
    n_q = DEC_SEQ // tq
    lat0 = T_CTX // DEC_SEQ
    in_specs = [pl.BlockSpec((tq, QK_W), lambda b, i: (T_CTX // tq + b * n_q + i, 0)),
                pl.BlockSpec((DEC_SEQ, QK_W), lambda b, i: (lat0 + b, 0)),
                pl.BlockSpec((DEC_SEQ, V_W), lambda b, i: (lat0 + b, 0)),
                pl.BlockSpec((PAST_LEN, QK_W), lambda b, i: (b, 0)),
                pl.BlockSpec((PAST_LEN, V_W), lambda b, i: (b, 0))]
    out_shape = [jax.ShapeDtypeStruct((T_LAT, V_W), BF16)]
    out_specs = [pl.BlockSpec((tq, V_W), lambda b, i: (b * n_q + i, 0))]
    args = [q, k, v, kc, vc]
    if cast_w is not None:
        w, w_in_spec, wb_shape, wb_spec = _cast_slab(DEC_BATCH * n_q, cast_w)
        step_of = lambda b, i: b * n_q + i
        in_specs.append(w_in_spec(step_of))
        out_shape.append(wb_shape)
        out_specs.append(wb_spec(step_of))
        args.append(w)
    outs = pl.pallas_call(
        _attn_lat_kernel,
        out_shape=tuple(out_shape),
        grid=(DEC_BATCH, n_q),
        in_specs=in_specs,
        out_specs=tuple(out_specs),
        compiler_params=_cparams("arbitrary", "arbitrary"),
        name="attn_lat",
    )(*args)
    return (attn_ctx, *outs)


def _swiglu_hidden(h, wg_ref, wu_ref, a_buf):
    width = a_buf.shape[1]
    for c0 in range(0, width, FF_CHUNK):
        cols = slice(c0, min(c0 + FF_CHUNK, width))
        a_buf[:, cols] = (_silu(_dot(h, wg_ref[:, cols])) * _dot(h, wu_ref[:, cols])).astype(BF16)


def _mix_kernel(with_router, attn_c_ref, attn_l_ref, pool_ref, xc_ref, xl_ref, g1_ref, sh2_ref, sc2_ref, n2_ref, wo_ref,
                *rest):
    if with_router:
        wr_ref, ltri_ref, utri_ref, w_ref, x_out, xs_out, gs_out, route_out, meta_out, wb_ref = rest
    else:
        g2_ref, wg_ref, wu_ref, wd_ref, w_ref, x_out, wb_ref, a_buf = rest
    wb_ref[...] = w_ref[...].astype(BF16)
    is_ctx = pl.program_id(0) < T_CTX // TILE_MIX
    halves = []
    for r0 in range(0, TILE_MIX, TILE_MIX // 2):
        rows = slice(r0, r0 + TILE_MIX // 2)
        attn = jnp.where(is_ctx, attn_c_ref[rows, :], attn_l_ref[rows, :])
        y = _dot(attn, wo_ref[:V_W, :]) + _dot(pool_ref[rows, :], wo_ref[V_W:, :])
        x_new = jnp.where(is_ctx, xc_ref[rows, :], xl_ref[rows, :]) + g1_ref[...] * y
        x_out[rows, :] = x_new
        halves.append((_rms(x_new) * n2_ref[...] * (1.0 + sc2_ref[...]) + sh2_ref[...]).astype(BF16))
    h = jnp.concatenate(halves, axis=0)
    if not with_router:
        _swiglu_hidden(h, wg_ref, wu_ref, a_buf)
        x_out[...] += g2_ref[...] * _dot(a_buf[...], wd_ref[...])
        return

    logits = _dot(h, wr_ref[...])
    lane = lax.broadcasted_iota(jnp.int32, logits.shape, 1)
    neg = float(jnp.finfo(F32).min)
    lg = jnp.where(lane < N_EXPERTS, logits, neg)
    m1 = lg.max(axis=-1, keepdims=True)
    i1 = jnp.where(lg == m1, lane, LANES).min(axis=-1, keepdims=True)
    lg2 = jnp.where(lane == i1, neg, lg)
    m2 = lg2.max(axis=-1, keepdims=True)
    i2 = jnp.where(lg2 == m2, lane, LANES).min(axis=-1, keepdims=True)
    e = jnp.exp(m2 - m1)
    w1 = 1.0 / (1.0 + e)
    w2 = e / (1.0 + e)

    sel1 = lane == i1
    sel2 = lane == i2
    member = jnp.where(jnp.logical_or(sel1, sel2), 1.0, 0.0)
    rank = _dot(ltri_ref[...], member.astype(BF16))
    n_tok = member.sum(axis=0, keepdims=True)
    units = jnp.floor((n_tok + (SLOT_UNIT - 1)) * (1.0 / SLOT_UNIT))
    unit_off = _dot(jnp.broadcast_to(units, (8, LANES)).astype(BF16), utri_ref[...])[0:1, :]
    slot_of = SLOT_UNIT * unit_off + rank
    slot1 = jnp.where(sel1, slot_of, 0.0).sum(axis=-1, keepdims=True)
    slot2 = jnp.where(sel2, slot_of, 0.0).sum(axis=-1, keepdims=True)
    route = jnp.where(lane == 0, slot1, jnp.where(lane == 1, slot2,
                      jnp.where(lane == 2, w1, jnp.where(lane == 3, w2, 0.0))))
    route_out[...] = route
    sub = lax.broadcasted_iota(jnp.int32, (8, LANES), 0)
    meta_out[...] = jnp.where(sub == 0, units, jnp.where(sub == 1, unit_off, 0.0)).astype(jnp.int32)

    rt = route.T
    s1 = rt[0:1, :].astype(jnp.int32)
    s2 = rt[1:2, :].astype(jnp.int32)
    def slot_block(r0):
        srow = r0 + lax.broadcasted_iota(jnp.int32, (SLOT_BLOCK, TILE_MIX), 0)
        hit1 = srow == s1
        hit2 = srow == s2
        perm = (jnp.where(hit1, 1.0, 0.0) + jnp.where(hit2, 1.0, 0.0)).astype(BF16)
        gate = (jnp.where(hit1, rt[2:3, :], 0.0) + jnp.where(hit2, rt[3:4, :], 0.0)).sum(axis=-1, keepdims=True)
        return perm, gate

    starts = list(range(0, SLOTS_PER_TILE, SLOT_BLOCK))
    nxt = slot_block(starts[0])
    for k, r0 in enumerate(starts):
        perm, gate = nxt
        if k + 1 < len(starts):
            nxt = slot_block(starts[k + 1])
        xs_out[r0:r0 + SLOT_BLOCK, :] = _dot(perm, h).astype(BF16)
        gs_out[r0:r0 + SLOT_BLOCK, :] = jnp.broadcast_to(gate, (SLOT_BLOCK, LANES))


def _mod_row(i, tile):
    n_ctx = T_CTX // tile
    per_seq = DEC_SEQ // tile
    return jnp.where(i < n_ctx, 0, 1 + (i - n_ctx) // per_seq)


def _mix_call(attn_ctx, attn_lat, pooled, x_ctx, x_lat, ada, n2, w_out, cast_w, w_router=None, ffn_w=None):
    with_router = w_router is not None
    assert with_router != (ffn_w is not None)
    tm = TILE_MIX
    n_ctx = T_CTX // tm
    lat_tile0 = n_ctx if x_lat.shape[0] == T_ALL else 0
    row = lambda i: (i, 0)
    const2 = lambda i: (0, 0)
    mod = lambda k: pl.BlockSpec((None, 1, D_MODEL), lambda i: (_mod_row(i, tm), 0, k))
    ctx_blk = lambda i: (jnp.minimum(i, n_ctx - 1), 0)
    in_specs = [pl.BlockSpec((tm, V_W), ctx_blk),
                pl.BlockSpec((tm, V_W), lambda i: (jnp.maximum(i - n_ctx, 0), 0)),
                pl.BlockSpec((tm, POOL_W), row),
                pl.BlockSpec((tm, D_MODEL), ctx_blk),
                pl.BlockSpec((tm, D_MODEL), lambda i: (jnp.maximum(i - n_ctx, 0) + lat_tile0, 0)),
                mod(2), mod(3), mod(4),
                pl.BlockSpec((1, D_MODEL), const2), pl.BlockSpec((D_MODEL, D_MODEL), const2)]
    args = [attn_ctx, attn_lat, pooled, x_ctx, x_lat, ada, ada, ada, n2, w_out]
    if with_router:
        t_i = jnp.arange(tm)
        ltri = (t_i[None, :] < t_i[:, None]).astype(BF16)
        l_i = jnp.arange(LANES)
        utri = (l_i[:, None] < l_i[None, :]).astype(BF16)
        in_specs += [pl.BlockSpec((D_MODEL, LANES), const2), pl.BlockSpec((tm, tm), const2),
                     pl.BlockSpec((LANES, LANES), const2)]
        args += [w_router, ltri, utri]
        out_shape = [jax.ShapeDtypeStruct((T_ALL, D_MODEL), F32),
                     jax.ShapeDtypeStruct((N_ROUTE_TILES * SLOTS_PER_TILE, D_MODEL), BF16),
                     jax.ShapeDtypeStruct((N_ROUTE_TILES * SLOTS_PER_TILE, LANES), F32),
                     jax.ShapeDtypeStruct((T_ALL, LANES), F32),
                     jax.ShapeDtypeStruct((N_ROUTE_TILES, 8, LANES), jnp.int32)]
        out_specs = [pl.BlockSpec((tm, D_MODEL), row), pl.BlockSpec((SLOTS_PER_TILE, D_MODEL), row),
                     pl.BlockSpec((SLOTS_PER_TILE, LANES), row), pl.BlockSpec((tm, LANES), row),
                     pl.BlockSpec((None, 8, LANES), lambda i: (i, 0, 0))]
        scratch = []
    else:
        resident = lambda shape: pl.BlockSpec(shape, const2, pipeline_mode=pl.Buffered(1))
        in_specs += [mod(5), resident((D_MODEL, D_FF)), resident((D_MODEL, D_FF)), resident((D_FF, D_MODEL))]
        args += [ada, *ffn_w]
        out_shape = [jax.ShapeDtypeStruct((T_ALL, D_MODEL), F32)]
        out_specs = [pl.BlockSpec((tm, D_MODEL), row)]
        scratch = [pltpu.VMEM((tm, D_FF), BF16)]
    w, w_spec, wb_shape, wb_spec = _cast_slab(T_ALL // tm, cast_w)
    in_specs.append(w_spec(lambda i: i))
    args.append(w)
    out_shape.append(wb_shape)
    out_specs.append(wb_spec(lambda i: i))
    return pl.pallas_call(
        functools.partial(_mix_kernel, with_router),
        out_shape=tuple(out_shape),
        grid=(T_ALL // tm,),
        in_specs=in_specs,
        out_specs=tuple(out_specs),
        scratch_shapes=scratch,
        compiler_params=_cparams("arbitrary"),
        name="mix_router" if with_router else "mix_ffn",
    )(*args)


def _route_tables(meta):
    units = meta[:, 0, :N_EXPERTS]
    offs = meta[:, 1, :N_EXPERTS]
    cum = jnp.cumsum(units, axis=0)
    total = cum[-1]
    tiles_e = (total + GEMM_UNITS - 1) // GEMM_UNITS
    tile_end = jnp.cumsum(tiles_e)
    n_act = tile_end[-1]
    m = jnp.arange(N_GEMM_TILES)
    m_eff = jnp.minimum(m, jnp.maximum(n_act - 1, 0))
    expert = jnp.minimum(jnp.sum(tile_end[None, :] <= m_eff[:, None], axis=1), N_EXPERTS - 1)
    is_e = (expert[:, None] == jnp.arange(N_EXPERTS)[None, :]).astype(jnp.int32)
    pick = lambda per_expert: jnp.sum(is_e * per_expert[None, :], axis=1)
    first_q = (m_eff - pick(tile_end - tiles_e)) * GEMM_UNITS
    count = jnp.where(m < n_act, jnp.clip(pick(total) - first_q, 0, GEMM_UNITS), 0)
    q = first_q[:, None] + jnp.arange(GEMM_UNITS)[None, :]
    rows_of = lambda table: jnp.sum(is_e[:, None, :] * table[None, :, :], axis=2)
    cum_e, units_e, offs_e = rows_of(cum), rows_of(units), rows_of(offs)
    src_tile = jnp.minimum(jnp.sum(cum_e[:, None, :] <= q[:, :, None], axis=2), N_ROUTE_TILES - 1)
    is_t = (src_tile[:, :, None] == jnp.arange(N_ROUTE_TILES)[None, None, :]).astype(jnp.int32)
    at_tile = lambda per_tile: jnp.sum(is_t * per_tile[:, None, :], axis=2)
    uid = src_tile * UNITS_PER_TILE + at_tile(offs_e) + (q - at_tile(cum_e - units_e))
    uid = jnp.where(jnp.arange(GEMM_UNITS)[None, :] < count[:, None], uid, ZERO_UNIT)
    i32 = jnp.int32
    return expert.astype(i32), count.astype(i32), uid.reshape(-1).astype(i32), n_act.reshape(1).astype(i32)


def _gemm_kernel(em_ref, cnt_ref, ul_ref, nact_ref, xs_hbm, gs_hbm, wg_ref, wu_ref, wd_ref, ys_hbm,
                 xbuf, gbuf, obuf, acc_ref, a_buf, sem_in, sem_out):
    del em_ref
    m = pl.program_id(0)
    j = pl.program_id(1)
    last_j = pl.num_programs(1) - 1
    n_act = nact_ref[0]
    active = m < n_act
    slot = m % 2

    def rows(r):
        return pl.ds(r * SLOT_UNIT if isinstance(r, int) else pl.multiple_of(r * SLOT_UNIT, SLOT_UNIT), SLOT_UNIT)

    def in_copies(mm, sl, r):
        uid = ul_ref[mm * GEMM_UNITS + r]
        return (pltpu.make_async_copy(xs_hbm.at[uid], xbuf.at[sl, rows(r)], sem_in.at[sl]),
                pltpu.make_async_copy(gs_hbm.at[uid], gbuf.at[sl, rows(r)], sem_in.at[sl]))

    def out_copy(mm, r):
        uid = ul_ref[mm * GEMM_UNITS + r]
        return pltpu.make_async_copy(obuf.at[rows(r)], ys_hbm.at[uid], sem_out.at[0])

    def for_slot(sl_dyn, fn):
        if isinstance(sl_dyn, int):
            fn(sl_dyn)
            return
        for sl in range(2):
            pl.when(sl_dyn == sl)(functools.partial(fn, sl))

    def start_in(mm, sl_dyn):
        def issue(sl):
            for r in range(GEMM_UNITS):
                for cp in in_copies(mm, sl, r):
                    cp.start()

        for_slot(sl_dyn, issue)

    def wait_in(mm, sl_dyn):
        def wait(sl):
            for r in range(GEMM_UNITS):
                for cp in in_copies(mm, sl, r):
                    cp.wait()

        for_slot(sl_dyn, wait)

    def start_out(mm):
        n = cnt_ref[mm]
        for r in range(GEMM_UNITS):
            pl.when(r < n)(lambda r=r: out_copy(mm, r).start())

    def wait_out(mm):
        def body(r, carry):
            out_copy(mm, r).wait()
            return carry

        lax.fori_loop(0, cnt_ref[mm], body, 0)

    @pl.when(jnp.logical_and(active, j == 0))
    def _():
        @pl.when(m == 0)
        def _():
            start_in(0, 0)

        wait_in(m, slot)

        @pl.when(m + 1 < n_act)
        def _():
            start_in(m + 1, 1 - slot)

    @pl.when(jnp.logical_and(jnp.logical_and(active, j == last_j), m > 0))
    def _():
        wait_out(m - 1)

    n_valid = cnt_ref[m]
    for quarter in range(1, GEMM_QUARTERS + 1):
        n_rows = quarter * GEMM_ROWS // GEMM_QUARTERS
        lo_units = (quarter - 1) * GEMM_UNITS // GEMM_QUARTERS
        hi_units = quarter * GEMM_UNITS // GEMM_QUARTERS

        @pl.when(jnp.logical_and(active, jnp.logical_and(n_valid > lo_units, n_valid <= hi_units)))
        def _(n_rows=n_rows):
            _swiglu_hidden(xbuf[slot, :n_rows, :], wg_ref, wu_ref, a_buf.at[:n_rows, :])
            part = _dot(a_buf[:n_rows, :], wd_ref[...])

            @pl.when(j == 0)
            def _():
                acc_ref[:n_rows, :] = part

            @pl.when(jnp.logical_and(j > 0, j < last_j))
            def _():
                acc_ref[:n_rows, :] += part

            @pl.when(j == last_j)
            def _():
                gate = gbuf[slot, :n_rows, :]
                for cb in range(D_MODEL // LANES):
                    cols = slice(cb * LANES, (cb + 1) * LANES)
                    obuf[:n_rows, cols] = ((acc_ref[:n_rows, cols] + part[:, cols]) * gate).astype(BF16)

    @pl.when(jnp.logical_and(active, j == last_j))
    def _():
        start_out(m)

        @pl.when(m == n_act - 1)
        def _():
            wait_out(m)


def _gemm_call(expert, count, uids, n_act, xs, gs, wg, wu, wd):
    tf = TILE_FE
    n_j = D_FF_EXPERT // tf
    assert n_j >= 2

    def w_col(m, j, em, cnt, ul, nact):
        return (em[m], 0, jnp.where(m < nact[0], j, n_j - 1))

    def w_row(m, j, em, cnt, ul, nact):
        return (em[m], jnp.where(m < nact[0], j, n_j - 1), 0)

    grid_spec = pltpu.PrefetchScalarGridSpec(
        num_scalar_prefetch=4,
        grid=(N_GEMM_TILES, n_j),
        in_specs=[pl.BlockSpec(memory_space=pl.ANY),
                  pl.BlockSpec(memory_space=pl.ANY),
                  pl.BlockSpec((None, D_MODEL, tf), w_col),
                  pl.BlockSpec((None, D_MODEL, tf), w_col),
                  pl.BlockSpec((None, tf, D_MODEL), w_row)],
        out_specs=pl.BlockSpec(memory_space=pl.ANY),
        scratch_shapes=[pltpu.VMEM((2, GEMM_ROWS, D_MODEL), BF16),
                        pltpu.VMEM((2, GEMM_ROWS, LANES), F32),
                        pltpu.VMEM((GEMM_ROWS, D_MODEL), BF16),
                        pltpu.VMEM((GEMM_ROWS, D_MODEL), F32),
                        pltpu.VMEM((GEMM_ROWS, tf), BF16),
                        pltpu.SemaphoreType.DMA((2,)),
                        pltpu.SemaphoreType.DMA((1,))])
    return pl.pallas_call(
        _gemm_kernel,
        out_shape=jax.ShapeDtypeStruct((N_UNITS, SLOT_UNIT, D_MODEL), BF16),
        grid_spec=grid_spec,
        input_output_aliases={4: 0},
        compiler_params=_cparams("arbitrary", "arbitrary"),
        name="moe_experts",
    )(expert, count, uids, n_act,
      xs.reshape(N_UNITS, SLOT_UNIT, D_MODEL), gs.reshape(N_UNITS, SLOT_UNIT, LANES), wg, wu, wd)


def _combine_kernel(ys_ref, route_ref, x_ref, g2_ref, fg_ref, o_ref):
    route = route_ref[...]
    s1 = route[:, 0:1].astype(jnp.int32)
    s2 = route[:, 1:2].astype(jnp.int32)
    scol = lax.broadcasted_iota(jnp.int32, (TILE_MIX, SLOTS_PER_TILE), 1)
    unperm = (jnp.where(scol == s1, 1.0, 0.0) + jnp.where(scol == s2, 1.0, 0.0)).astype(BF16)
    y = _dot(unperm, ys_ref[...])
    o_ref[...] = _rms(x_ref[...] + g2_ref[...] * y) * fg_ref[...]


def _combine_call(ys, route, x, ada, final_g, tile0, n_tok):
    tm = TILE_MIX
    return pl.pallas_call(
        _combine_kernel,
        out_shape=jax.ShapeDtypeStruct((n_tok, D_MODEL), F32),
        grid=(n_tok // tm,),
        in_specs=[pl.BlockSpec((SLOTS_PER_TILE, D_MODEL), lambda i: (i + tile0, 0)),
                  pl.BlockSpec((tm, LANES), lambda i: (i + tile0, 0)),
                  pl.BlockSpec((tm, D_MODEL), lambda i: (i + tile0, 0)),
                  pl.BlockSpec((None, 1, D_MODEL), lambda i: (_mod_row(i + tile0, tm), 0, 5)),
                  pl.BlockSpec((1, D_MODEL), lambda i: (0, 0))],
        out_specs=pl.BlockSpec((tm, D_MODEL), lambda i: (i, 0)),
        compiler_params=_cparams("arbitrary"),
        name="moe_combine",
    )(ys.reshape(N_UNITS * SLOT_UNIT, D_MODEL), route, x, ada, final_g)


def _rot_cols(w):
    q = QK_ROPE // 4
    return jnp.concatenate([-w[:, q:2 * q], w[:, :q], -w[:, 3 * q:], w[:, 2 * q:3 * q]], axis=1)


def _rope_tables():
    t = jnp.arange(DEC_SEQ)
    rows = (t // GRID_W).astype(F32)
    cols = (t % GRID_W).astype(F32)
    half = QK_ROPE // 2
    freqs = ROPE_BASE ** (-jnp.arange(0, half, 2, dtype=F32) / half)
    ang_r = rows[:, None] * freqs
    ang_c = cols[:, None] * freqs
    ang = jnp.concatenate([ang_r, ang_r, ang_c, ang_c], axis=-1)
    cos, sin = jnp.cos(ang), jnp.sin(ang)
    one, zero = jnp.ones_like(cos), jnp.zeros_like(cos)
    tq = jnp.stack([jnp.concatenate([one, zero], -1), jnp.concatenate([cos, sin], -1)])
    tc = jnp.stack([jnp.concatenate([one, one], -1), jnp.concatenate([cos, cos], -1)])
    ts = jnp.stack([jnp.concatenate([zero, zero], -1), jnp.concatenate([sin, sin], -1)])
    return tq, tc, ts


def _layer_weights(l, w_in, w_uq, w_ukv):
    wi = w_in[l]
    w_kpe = wi[:, OFF_U:OFF_U + QK_ROPE]
    w_rot = _rot_cols(w_kpe)
    w_in_r = jnp.concatenate([wi[:, :OFF_U], wi[:, OFF_U + QK_ROPE:], w_kpe, w_kpe, w_rot, w_rot], axis=1)
    heads = []
    for h in range(N_HEADS):
        blk = w_uq[l][:, h * (QK_NOPE + QK_ROPE):(h + 1) * (QK_NOPE + QK_ROPE)]
        heads += [blk, _rot_cols(blk[:, QK_NOPE:])]
    w_uq_r = jnp.concatenate(heads, axis=1)
    kv = w_ukv[l].reshape(KV_LORA, N_HEADS, QK_NOPE + V_DIM)
    w_uk = kv[:, :, :QK_NOPE].reshape(KV_LORA, N_HEADS * QK_NOPE)
    w_uv = kv[:, :, QK_NOPE:].reshape(KV_LORA, V_W)
    return w_in_r.astype(BF16), w_uq_r.astype(BF16), w_uk.astype(BF16), w_uv.astype(BF16)


def kernel(x_prompt, x_sample, c, cache_ckv, cache_kpe, c_ctx, norm1_g, norm2_g, w_ada, b_ada, w_in, q_norm_g,
           kv_norm_g, w_uq, w_ukv, w_pool, pool_scale, w_out, ffn_w_gate, ffn_w_up, ffn_w_down, moe_w_router,
           moe_w_gate, moe_w_up, moe_w_down, final_norm_g):
    x_ctx = x_prompt.reshape(T_CTX, D_MODEL)
    x_lat = x_sample.reshape(T_LAT, D_MODEL)
    cond = jnp.concatenate([c_ctx[None, :], c, jnp.zeros((COND_ROWS - 1 - DEC_BATCH, D_MODEL), F32)], axis=0)
    ada_all = _ada_call(cond, w_ada, b_ada)
    tq, tc, ts = _rope_tables()

    lw = [_layer_weights(l, w_in, w_uq, w_ukv) for l in range(DEPTH)]
    dup = jnp.concatenate([jnp.eye(QK_ROPE, dtype=BF16)] * 2, axis=1)
    kc_all, vc_all = _cache_call(cache_ckv, cache_kpe, jnp.stack([w[2] for w in lw]),
                                 jnp.stack([w[3] for w in lw]), dup)

    assert DEPTH == 2
    moe_gate_2d = moe_w_gate[0].reshape(N_EXPERTS * D_MODEL, D_FF_EXPERT)
    moe_up_2d = moe_w_up[0].reshape(N_EXPERTS * D_MODEL, D_FF_EXPERT)
    moe_down_2d = moe_w_down[0].reshape(N_EXPERTS * D_FF_EXPERT, D_MODEL)
    new_ckv = new_kpe = None
    for l in range(DEPTH):
        w_in_r, w_uq_r, w_uk, w_uv = lw[l]
        ada = ada_all[l].reshape(COND_ROWS, 1, 6 * D_MODEL)
        q, k, v, new_ckv, new_kpe, pooled = _in_call(
            x_ctx, x_lat, ada, norm1_g[l][None, :], w_in_r, q_norm_g[l][None, :], kv_norm_g[l][None, :], w_uq_r,
            w_uk, w_uv, tq, tc, ts, w_pool[l].astype(BF16), pool_scale[l][None, :], new_ckv, new_kpe)
        j = l // 2
        if l % 2 == 0:
            attn_c, attn_l, moe_wd = _attn_call(q, k, v, kc_all[l], vc_all[l], cast_w=moe_down_2d)
            ffn_w = (ffn_w_gate[j].astype(BF16), ffn_w_up[j].astype(BF16), ffn_w_down[j].astype(BF16))
            x, moe_wg = _mix_call(attn_c, attn_l, pooled, x_ctx, x_lat, ada, norm2_g[l][None, :],
                                  w_out[l].astype(BF16), moe_gate_2d, ffn_w=ffn_w)
            x_ctx = x_lat = x
        else:
            attn_c, attn_l = _attn_call(q, k, v, kc_all[l], vc_all[l])
            w_r = jnp.pad(moe_w_router[j], ((0, 0), (0, LANES - N_EXPERTS))).astype(BF16)
            x, xs, gs, route, meta, moe_wu = _mix_call(attn_c, attn_l, pooled, x_ctx, x_lat, ada,
                                                       norm2_g[l][None, :], w_out[l].astype(BF16), moe_up_2d,
                                                       w_router=w_r)
            expert, count, uids, n_act = _route_tables(meta)
            ys = _gemm_call(expert, count, uids, n_act, xs, gs,
                            moe_wg.reshape(N_EXPERTS, D_MODEL, D_FF_EXPERT),
                            moe_wu.reshape(N_EXPERTS, D_MODEL, D_FF_EXPERT),
                            moe_wd.reshape(N_EXPERTS, D_FF_EXPERT, D_MODEL))
            fg = final_norm_g[None, :]
            y_prompt = _combine_call(ys, route, x, ada, fg, 0, T_CTX).reshape(BATCH, SEQ, D_MODEL)
            y_sample = _combine_call(ys, route, x, ada, fg, T_CTX // TILE_MIX, T_LAT).reshape(
                DEC_BATCH, DEC_SEQ, D_MODEL)

    return y_prompt, y_sample, new_ckv, new_kpe
```

```python
import functools

import jax
import jax.numpy as jnp
from jax import lax
from jax.experimental import pallas as pl
from jax.experimental.pallas import tpu as pltpu

D_MODEL = 1024
BATCH = 32
SEQ = 256
DEPTH = 2
DEC_BATCH = 8
DEC_SEQ = 1024
PAST_LEN = 512
GRID_W = 64
N_HEADS = 4
QK_NOPE = 128
QK_ROPE = 64
V_DIM = 128
Q_LORA = 384
KV_LORA = 256
POOL_W = 512
POOL_GROUPS = 4
POOL_WINDOWS = (2, 4, 8, 16)
POOL_CH = POOL_W // POOL_GROUPS
D_FF = 2816
N_EXPERTS = 8
D_FF_EXPERT = 3584
ROPE_BASE = 10000.0
EPS = 1e-6

T_CTX = BATCH * SEQ
T_LAT = DEC_BATCH * DEC_SEQ
T_ALL = T_CTX + T_LAT

LANES = 128
HEAD_PAD = 256
QK_W = N_HEADS * HEAD_PAD
V_W = N_HEADS * V_DIM
OFF_CKV = Q_LORA
OFF_U = Q_LORA + KV_LORA
OFF_KA = OFF_U + POOL_W
OFF_KB = OFF_KA + LANES
N_PROJ = OFF_KB + LANES

TILE_IN = 1024
CHUNK = 256
HALO = 8
REGION = CHUNK + 2 * HALO
CTX_SEQS_PER_STEP = 4
LAT_Q_TILE = 1024
TILE_MIX = 512
TOP_K = 2
SLOT_UNIT = 16
UNITS_PER_TILE = TILE_MIX * TOP_K // SLOT_UNIT + N_EXPERTS
SLOTS_PER_TILE = UNITS_PER_TILE * SLOT_UNIT
SLOT_BLOCK = SLOTS_PER_TILE // 4
N_ROUTE_TILES = T_ALL // TILE_MIX
N_UNITS = N_ROUTE_TILES * UNITS_PER_TILE
ZERO_UNIT = UNITS_PER_TILE - 1
GEMM_UNITS = 64
GEMM_ROWS = GEMM_UNITS * SLOT_UNIT
GEMM_QUARTERS = 4
N_GEMM_TILES = -(-N_UNITS // GEMM_UNITS) + N_EXPERTS
TILE_FE = D_FF_EXPERT // 2
FF_CHUNK = 256
COND_ROWS = 16
VMEM_LIMIT = 56 * 1024 * 1024

F32 = jnp.float32
BF16 = jnp.bfloat16


def _rms(x):
    return x * lax.rsqrt(jnp.mean(x * x, axis=-1, keepdims=True) + EPS)


def _dot(a, b):
    return jnp.dot(a, b, preferred_element_type=F32)


def _dot_nt(a, b):
    return lax.dot_general(a, b, (((1,), (1,)), ((), ())), preferred_element_type=F32)


def _silu(x):
    return x * (1.0 / (1.0 + jnp.exp(-x)))


def _cparams(*sem):
    return pltpu.CompilerParams(dimension_semantics=sem, vmem_limit_bytes=VMEM_LIMIT)


def _ada_kernel(cond_ref, w_ref, b_ref, o_ref):
    c = cond_ref[...]
    o_ref[...] = jnp.dot(_silu(c), w_ref[...], preferred_element_type=F32,
                         precision=lax.Precision.HIGHEST) + b_ref[...]


def _ada_call(cond, w_ada, b_ada):
    n_blk = 4
    bw = 6 * D_MODEL // n_blk
    return pl.pallas_call(
        _ada_kernel,
        out_shape=jax.ShapeDtypeStruct((DEPTH, COND_ROWS, 6 * D_MODEL), F32),
        grid=(DEPTH, n_blk),
        in_specs=[
            pl.BlockSpec((COND_ROWS, D_MODEL), lambda l, j: (0, 0)),
            pl.BlockSpec((None, D_MODEL, bw), lambda l, j: (l, 0, j)),
            pl.BlockSpec((None, 1, bw), lambda l, j: (l, 0, j)),
        ],
        out_specs=pl.BlockSpec((None, COND_ROWS, bw), lambda l, j: (l, 0, j)),
        compiler_params=_cparams("arbitrary", "arbitrary"),
        name="ada_params",
    )(cond, w_ada, b_ada.reshape(DEPTH, 1, 6 * D_MODEL))


def _cache_kernel(ckv_ref, kpe_ref, wuk_ref, wuv_ref, dup_ref, k_ref, v_ref):
    ckv = ckv_ref[...].astype(BF16)
    knope = _dot(ckv, wuk_ref[...])
    v_ref[...] = _dot(ckv, wuv_ref[...]).astype(BF16)
    kdup = _dot(kpe_ref[...].astype(BF16), dup_ref[...]).astype(BF16)
    for h in range(N_HEADS):
        k_ref[:, h * HEAD_PAD:h * HEAD_PAD + QK_NOPE] = knope[:, h * QK_NOPE:(h + 1) * QK_NOPE].astype(BF16)
        k_ref[:, h * HEAD_PAD + QK_NOPE:(h + 1) * HEAD_PAD] = kdup


def _cache_call(cache_ckv, cache_kpe, w_uk, w_uv, dup):
    n_tok = DEC_BATCH * PAST_LEN
    return pl.pallas_call(
        _cache_kernel,
        out_shape=(jax.ShapeDtypeStruct((DEPTH, n_tok, QK_W), BF16),
                   jax.ShapeDtypeStruct((DEPTH, n_tok, V_W), BF16)),
        grid=(DEPTH, DEC_BATCH),
        in_specs=[
            pl.BlockSpec((None, None, PAST_LEN, KV_LORA), lambda l, b: (b, l, 0, 0)),
            pl.BlockSpec((None, None, PAST_LEN, QK_ROPE), lambda l, b: (b, l, 0, 0)),
            pl.BlockSpec((None, KV_LORA, N_HEADS * QK_NOPE), lambda l, b: (l, 0, 0)),
            pl.BlockSpec((None, KV_LORA, V_W), lambda l, b: (l, 0, 0)),
            pl.BlockSpec((QK_ROPE, LANES), lambda l, b: (0, 0)),
        ],
        out_specs=(pl.BlockSpec((None, PAST_LEN, QK_W), lambda l, b: (l, b, 0)),
                   pl.BlockSpec((None, PAST_LEN, V_W), lambda l, b: (l, b, 0))),
        compiler_params=_cparams("arbitrary", "arbitrary"),
        name="cache_kv",
    )(cache_ckv, cache_kpe, w_uk, w_uv, dup)


def _in_kernel(n_prev, n_cast, xc_ref, xl_ref, sh_ref, sc_ref, g_ref, win_ref, qg_ref, kvg_ref, wuq_ref, wuk_ref,
               wuv_ref, tq_ref, tc_ref, ts_ref, wpool_ref, ps_ref, *rest):
    if n_prev:
        prev_ckv_ref, prev_kpe_ref, *rest = rest
    cast_in, rest = rest[:n_cast], rest[n_cast:]
    q_out, k_out, v_out, ckv_out, kpe_out, pool_out, *rest = rest
    cast_out, (u_scr,) = rest[:n_cast], rest[n_cast:]
    for w_ref, wb_ref in zip(cast_in, cast_out):
        wb_ref[...] = w_ref[...].astype(BF16)
    is_ctx = pl.program_id(0) >= (T_ALL - T_CTX) // TILE_IN
    if n_prev:
        ckv_out[:, :n_prev] = prev_ckv_ref[...]
        kpe_out[:, :n_prev] = prev_kpe_ref[...]
    shift = sh_ref[...]
    scale1 = 1.0 + sc_ref[...]
    qk_scale = (QK_NOPE + QK_ROPE) ** -0.5
    n_chunks = TILE_IN // CHUNK
    zeros = jnp.zeros((HALO, POOL_W), F32)
    seq_len = jnp.where(is_ctx, SEQ, DEC_SEQ)
    row = lax.broadcasted_iota(jnp.int32, (CHUNK, POOL_CH), 0)

    def pool_chunk(c):
        base = c * REGION + HALO
        rows = slice(c * CHUNK, (c + 1) * CHUNK)
        t = row + jnp.where(is_ctx, 0, c * CHUNK)
        for g, w in enumerate(POOL_WINDOWS):
            cols = slice(g * POOL_CH, (g + 1) * POOL_CH)
            acc = u_scr[base - w // 2:base - w // 2 + CHUNK, cols]
            for j in range(-w // 2 + 1, w // 2):
                acc = acc + u_scr[base + j:base + j + CHUNK, cols]
            cnt = jnp.minimum(t + w // 2, seq_len) - jnp.maximum(t - w // 2, 0)
            pooled = acc / cnt.astype(F32) - u_scr[base:base + CHUNK, cols]
            lin = _dot(pooled.astype(BF16), wpool_ref[g]) * ps_ref[:, cols]
            pool_out[rows, cols] = lin.astype(BF16)

    for c in range(n_chunks):
        r0 = c * CHUNK
        rows = slice(r0, r0 + CHUNK)
        x = jnp.where(is_ctx, xc_ref[rows, :], xl_ref[rows, :])
        h = (_rms(x) * g_ref[...] * scale1 + shift).astype(BF16)
        proj = _dot(h, win_ref[...])

        qn = (_rms(proj[:, :Q_LORA]) * qg_ref[...]).astype(BF16)
        q = _dot(qn, wuq_ref[...]) * qk_scale
        tq = tq_ref[rows, :]
        for hd in range(N_HEADS):
            lo = hd * HEAD_PAD
            q_out[rows, lo:lo + QK_NOPE] = q[:, lo:lo + QK_NOPE].astype(BF16)
            q_out[rows, lo + QK_NOPE:lo + HEAD_PAD] = (q[:, lo + QK_NOPE:lo + HEAD_PAD] * tq).astype(BF16)

        ckv = _rms(proj[:, OFF_CKV:OFF_U]) * kvg_ref[...]
        ckv_out[c, n_prev] = ckv
        ckv_b = ckv.astype(BF16)
        knope = _dot(ckv_b, wuk_ref[...])
        v_out[rows, :] = _dot(ckv_b, wuv_ref[...]).astype(BF16)

        k_a = proj[:, OFF_KA:OFF_KB]
        k_b = proj[:, OFF_KB:N_PROJ]
        kpe_out[c, n_prev] = k_a[:, :QK_ROPE]
        kr = (k_a * tc_ref[rows, :] + k_b * ts_ref[rows, :]).astype(BF16)
        for hd in range(N_HEADS):
            lo = hd * HEAD_PAD
            k_out[rows, lo:lo + QK_NOPE] = knope[:, hd * QK_NOPE:(hd + 1) * QK_NOPE].astype(BF16)
            k_out[rows, lo + QK_NOPE:lo + HEAD_PAD] = kr

        u = proj[:, OFF_U:OFF_KA]
        base = c * REGION + HALO
        u_scr[base:base + CHUNK, :] = u
        if c == 0:
            u_scr[0:HALO, :] = zeros
        else:
            u_scr[base - 2 * HALO:base - HALO, :] = jnp.where(is_ctx, zeros, u[:HALO, :])
        if c == n_chunks - 1:
            u_scr[base + CHUNK:base + CHUNK + HALO, :] = zeros
        else:
            u_scr[base + CHUNK + HALO:base + CHUNK + 2 * HALO, :] = jnp.where(is_ctx, zeros, u[CHUNK - HALO:, :])
        if c > 0:
            pool_chunk(c - 1)
    pool_chunk(n_chunks - 1)


def _in_call(x_ctx, x_lat, ada, g1, w_in, qg, kvg, w_uq, w_uk, w_uv, tq, tc, ts, w_pool, pscale, prev_ckv, prev_kpe,
             cast_ws=()):
    assert CHUNK == SEQ
    n_tiles = T_ALL // TILE_IN
    n_ctx = T_CTX // TILE_IN
    n_prev = 0 if prev_ckv is None else prev_ckv.shape[1]
    seqs = TILE_IN // SEQ
    lat_tile0 = n_ctx if x_lat.shape[0] == T_ALL else 0

    def tile(i):
        return jnp.where(i < n_tiles - n_ctx, i + n_ctx, i - (n_tiles - n_ctx))

    def cond_row(i):
        return jnp.maximum(tile(i) - n_ctx + 1, 0)

    def tab(i):
        return (jnp.minimum(jnp.maximum(tile(i) - n_ctx + 1, 0), 1), 0, 0)

    rows = lambda i: (tile(i), 0)
    ctx_blk4 = lambda i: (jnp.where(tile(i) < n_ctx, tile(i), 0), 0, 0, 0)
    const2 = lambda i: (0, 0)
    prev_specs = [pl.BlockSpec((seqs, n_prev, SEQ, KV_LORA), ctx_blk4),
                  pl.BlockSpec((seqs, n_prev, SEQ, QK_ROPE), ctx_blk4)] if n_prev else []
    prev_args = [prev_ckv, prev_kpe] if n_prev else []
    casts = [_cast_slab(n_tiles, w) for w in cast_ws]
    step = lambda i: i
    return pl.pallas_call(
        functools.partial(_in_kernel, n_prev, len(casts)),
        out_shape=(jax.ShapeDtypeStruct((T_ALL, QK_W), BF16),
                   jax.ShapeDtypeStruct((T_ALL, QK_W), BF16),
                   jax.ShapeDtypeStruct((T_ALL, V_W), BF16),
                   jax.ShapeDtypeStruct((BATCH, n_prev + 1, SEQ, KV_LORA), F32),
                   jax.ShapeDtypeStruct((BATCH, n_prev + 1, SEQ, QK_ROPE), F32),
                   jax.ShapeDtypeStruct((T_ALL, POOL_W), BF16)) + tuple(c[2] for c in casts),
        grid=(n_tiles,),
        in_specs=[
            pl.BlockSpec((TILE_IN, D_MODEL), lambda i: (jnp.minimum(tile(i), n_ctx - 1), 0)),
            pl.BlockSpec((TILE_IN, D_MODEL), lambda i: (jnp.maximum(tile(i) - n_ctx, 0) + lat_tile0, 0)),
            pl.BlockSpec((None, 1, D_MODEL), lambda i: (cond_row(i), 0, 0)),
            pl.BlockSpec((None, 1, D_MODEL), lambda i: (cond_row(i), 0, 1)),
            pl.BlockSpec((1, D_MODEL), const2),
            pl.BlockSpec((D_MODEL, N_PROJ), const2),
            pl.BlockSpec((1, Q_LORA), const2),
            pl.BlockSpec((1, KV_LORA), const2),
            pl.BlockSpec((Q_LORA, QK_W), const2),
            pl.BlockSpec((KV_LORA, N_HEADS * QK_NOPE), const2),
            pl.BlockSpec((KV_LORA, V_W), const2),
            pl.BlockSpec((None, TILE_IN, LANES), tab),
            pl.BlockSpec((None, TILE_IN, LANES), tab),
            pl.BlockSpec((None, TILE_IN, LANES), tab),
            pl.BlockSpec((POOL_GROUPS, POOL_CH, POOL_CH), lambda i: (0, 0, 0)),
            pl.BlockSpec((1, POOL_W), const2),
        ] + prev_specs + [c[1](step) for c in casts],
        out_specs=(pl.BlockSpec((TILE_IN, QK_W), rows),
                   pl.BlockSpec((TILE_IN, QK_W), rows),
                   pl.BlockSpec((TILE_IN, V_W), rows),
                   pl.BlockSpec((seqs, n_prev + 1, SEQ, KV_LORA), ctx_blk4),
                   pl.BlockSpec((seqs, n_prev + 1, SEQ, QK_ROPE), ctx_blk4),
                   pl.BlockSpec((TILE_IN, POOL_W), rows)) + tuple(c[3](step) for c in casts),
        scratch_shapes=[pltpu.VMEM((TILE_IN // CHUNK * REGION, POOL_W), F32)],
        compiler_params=_cparams("arbitrary"),
        name="in_proj",
    )(x_ctx, x_lat, ada, ada, g1, w_in, qg, kvg, w_uq, w_uk, w_uv, tq, tc, ts, w_pool, pscale, *prev_args,
      *(c[0] for c in casts))


def _softmax_pv(scores, values):
    m = scores[0].max(axis=-1, keepdims=True)
    for s in scores[1:]:
        m = jnp.maximum(m, s.max(axis=-1, keepdims=True))
    den = None
    out = None
    for s, v in zip(scores, values):
        p = jnp.exp(s - m)
        d = p.sum(axis=-1, keepdims=True)
        o = _dot(p.astype(BF16), v)
        den = d if den is None else den + d
        out = o if out is None else out + o
    return out / den


def _attn_ctx_kernel(q_ref, k_ref, v_ref, o_ref):
    for b in range(CTX_SEQS_PER_STEP):
        rows = slice(b * SEQ, (b + 1) * SEQ)
        for hd in range(N_HEADS):
            qk = slice(hd * HEAD_PAD, (hd + 1) * HEAD_PAD)
            vv = slice(hd * V_DIM, (hd + 1) * V_DIM)
            s = _dot_nt(q_ref[rows, qk], k_ref[rows, qk])
            o_ref[rows, vv] = _softmax_pv([s], [v_ref[rows, vv]]).astype(BF16)


def _cast_slab(n_steps, w):
    n_rows, n_cols = w.shape
    slab = n_rows // n_steps
    assert slab * n_steps == n_rows
    in_spec = lambda step_of: pl.BlockSpec((slab, n_cols), lambda *g: (step_of(*g), 0))
    return w, in_spec, jax.ShapeDtypeStruct(w.shape, BF16), in_spec


def _attn_lat_kernel(q_ref, k_ref, v_ref, kc_ref, vc_ref, *rest):
    if len(rest) == 3:
        w_ref, o_ref, wb_ref = rest
        wb_ref[...] = w_ref[...].astype(BF16)
    else:
        (o_ref,) = rest
    for hd in range(N_HEADS):
        qk = slice(hd * HEAD_PAD, (hd + 1) * HEAD_PAD)
        vv = slice(hd * V_DIM, (hd + 1) * V_DIM)
        q = q_ref[:, qk]
        s1 = _dot_nt(q, k_ref[:, qk])
        s2 = _dot_nt(q, kc_ref[:, qk])
        o_ref[:, vv] = _softmax_pv([s1, s2], [v_ref[:, vv], vc_ref[:, vv]]).astype(BF16)


def _attn_call(q, k, v, kc, vc, cast_w=None):
    attn_ctx = pl.pallas_call(
        _attn_ctx_kernel,
        out_shape=jax.ShapeDtypeStruct((T_CTX, V_W), BF16),
        grid=(BATCH // CTX_SEQS_PER_STEP,),
        in_specs=[pl.BlockSpec((CTX_SEQS_PER_STEP * SEQ, QK_W), lambda b: (b, 0)),
                  pl.BlockSpec((CTX_SEQS_PER_STEP * SEQ, QK_W), lambda b: (b, 0)),
                  pl.BlockSpec((CTX_SEQS_PER_STEP * SEQ, V_W), lambda b: (b, 0))],
        out_specs=pl.BlockSpec((CTX_SEQS_PER_STEP * SEQ, V_W), lambda b: (b, 0)),
        compiler_params=_cparams("arbitrary"),
        name="attn_ctx",
    )(q, k, v)

    tq = LAT_Q_TILE if cast_w is None else LAT_Q_TILE // 2
    n_q = DEC_SEQ // tq
    lat0 = T_CTX // DEC_SEQ
    in_specs = [pl.BlockSpec((tq, QK_W), lambda b, i: (T_CTX // tq + b * n_q + i, 0)),
                pl.BlockSpec((DEC_SEQ, QK_W), lambda b, i: (lat0 + b, 0)),
                pl.BlockSpec((DEC_SEQ, V_W), lambda b, i: (lat0 + b, 0)),
                pl.BlockSpec((PAST_LEN, QK_W), lambda b, i: (b, 0)),
                pl.BlockSpec((PAST_LEN, V_W), lambda b, i: (b, 0))]
    out_shape = [jax.ShapeDtypeStruct((T_LAT, V_W), BF16)]
    out_specs = [pl.BlockSpec((tq, V_W), lambda b, i: (b * n_q + i, 0))]
    args = [q, k, v, kc, vc]
    if cast_w is not None:
        w, w_in_spec, wb_shape, wb_spec = _cast_slab(DEC_BATCH * n_q, cast_w)
        step_of = lambda b, i: b * n_q + i
        in_specs.append(w_in_spec(step_of))
        out_shape.append(wb_shape)
        out_specs.append(wb_spec(step_of))
        args.append(w)
    outs = pl.pallas_call(
        _attn_lat_kernel,
        out_shape=tuple(out_shape),
        grid=(DEC_BATCH, n_q),
        in_specs=in_specs,
        out_specs=tuple(out_specs),
        compiler_params=_cparams("arbitrary", "arbitrary"),
        name="attn_lat",
    )(*args)
    return (attn_ctx, *outs)


def _swiglu_hidden(h, wg_ref, wu_ref, a_buf):
    width = a_buf.shape[1]
    for c0 in range(0, width, FF_CHUNK):
        cols = slice(c0, min(c0 + FF_CHUNK, width))
        a_buf[:, cols] = (_silu(_dot(h, wg_ref[:, cols])) * _dot(h, wu_ref[:, cols])).astype(BF16)


def _mix_kernel(with_router, attn_c_ref, attn_l_ref, pool_ref, xc_ref, xl_ref, g1_ref, sh2_ref, sc2_ref, n2_ref, wo_ref,
                *rest):
    if with_router:
        wr_ref, ltri_ref, utri_ref, w_ref, x_out, xs_out, gs_out, route_out, meta_out, wb_ref = rest
    else:
        g2_ref, wg_ref, wu_ref, wd_ref, w_ref, x_out, wb_ref, a_buf = rest
    wb_ref[...] = w_ref[...].astype(BF16)
    is_ctx = pl.program_id(0) < T_CTX // TILE_MIX
    halves = []
    for r0 in range(0, TILE_MIX, TILE_MIX // 2):
        rows = slice(r0, r0 + TILE_MIX // 2)
        attn = jnp.where(is_ctx, attn_c_ref[rows, :], attn_l_ref[rows, :])
        y = _dot(attn, wo_ref[:V_W, :]) + _dot(pool_ref[rows, :], wo_ref[V_W:, :])
        x_new = jnp.where(is_ctx, xc_ref[rows, :], xl_ref[rows, :]) + g1_ref[...] * y
        x_out[rows, :] = x_new
        halves.append((_rms(x_new) * n2_ref[...] * (1.0 + sc2_ref[...]) + sh2_ref[...]).astype(BF16))
    h = jnp.concatenate(halves, axis=0)
    if not with_router:
        _swiglu_hidden(h, wg_ref, wu_ref, a_buf)
        x_out[...] += g2_ref[...] * _dot(a_buf[...], wd_ref[...])
        return

    logits = _dot(h, wr_ref[...])
    lane = lax.broadcasted_iota(jnp.int32, logits.shape, 1)
    neg = float(jnp.finfo(F32).min)
    lg = jnp.where(lane < N_EXPERTS, logits, neg)
    m1 = lg.max(axis=-1, keepdims=True)
    i1 = jnp.where(lg == m1, lane, LANES).min(axis=-1, keepdims=True)
    lg2 = jnp.where(lane == i1, neg, lg)
    m2 = lg2.max(axis=-1, keepdims=True)
    i2 = jnp.where(lg2 == m2, lane, LANES).min(axis=-1, keepdims=True)
    e = jnp.exp(m2 - m1)
    w1 = 1.0 / (1.0 + e)
    w2 = e / (1.0 + e)

    sel1 = lane == i1
    sel2 = lane == i2
    member = jnp.where(jnp.logical_or(sel1, sel2), 1.0, 0.0)
    rank = _dot(ltri_ref[...], member.astype(BF16))
    n_tok = member.sum(axis=0, keepdims=True)
    units = jnp.floor((n_tok + (SLOT_UNIT - 1)) * (1.0 / SLOT_UNIT))
    unit_off = _dot(jnp.broadcast_to(units, (8, LANES)).astype(BF16), utri_ref[...])[0:1, :]
    slot_of = SLOT_UNIT * unit_off + rank
    slot1 = jnp.where(sel1, slot_of, 0.0).sum(axis=-1, keepdims=True)
    slot2 = jnp.where(sel2, slot_of, 0.0).sum(axis=-1, keepdims=True)
    route = jnp.where(lane == 0, slot1, jnp.where(lane == 1, slot2,
                      jnp.where(lane == 2, w1, jnp.where(lane == 3, w2, 0.0))))
    route_out[...] = route
    sub = lax.broadcasted_iota(jnp.int32, (8, LANES), 0)
    meta_out[...] = jnp.where(sub == 0, units, jnp.where(sub == 1, unit_off, 0.0)).astype(jnp.int32)

    rt = route.T
    s1 = rt[0:1, :].astype(jnp.int32)
    s2 = rt[1:2, :].astype(jnp.int32)
    for r0 in range(0, SLOTS_PER_TILE, SLOT_BLOCK):
        blk = slice(r0, r0 + SLOT_BLOCK)
        srow = r0 + lax.broadcasted_iota(jnp.int32, (SLOT_BLOCK, TILE_MIX), 0)
        hit1 = srow == s1
        hit2 = srow == s2
        perm = (jnp.where(hit1, 1.0, 0.0) + jnp.where(hit2, 1.0, 0.0)).astype(BF16)
        xs_out[blk, :] = _dot(perm, h).astype(BF16)
        gate = (jnp.where(hit1, rt[2:3, :], 0.0) + jnp.where(hit2, rt[3:4, :], 0.0)).sum(axis=-1, keepdims=True)
        gs_out[blk, :] = jnp.broadcast_to(gate, (SLOT_BLOCK, LANES))


def _mod_row(i, tile):
    n_ctx = T_CTX // tile
    per_seq = DEC_SEQ // tile
    return jnp.where(i < n_ctx, 0, 1 + (i - n_ctx) // per_seq)


def _mix_call(attn_ctx, attn_lat, pooled, x_ctx, x_lat, ada, n2, w_out, cast_w, w_router=None, ffn_w=None):
    with_router = w_router is not None
    assert with_router != (ffn_w is not None)
    tm = TILE_MIX
    n_ctx = T_CTX // tm
    lat_tile0 = n_ctx if x_lat.shape[0] == T_ALL else 0
    row = lambda i: (i, 0)
    const2 = lambda i: (0, 0)
    mod = lambda k: pl.BlockSpec((None, 1, D_MODEL), lambda i: (_mod_row(i, tm), 0, k))
    ctx_blk = lambda i: (jnp.minimum(i, n_ctx - 1), 0)
    in_specs = [pl.BlockSpec((tm, V_W), ctx_blk),
                pl.BlockSpec((tm, V_W), lambda i: (jnp.maximum(i - n_ctx, 0), 0)),
                pl.BlockSpec((tm, POOL_W), row),
                pl.BlockSpec((tm, D_MODEL), ctx_blk),
                pl.BlockSpec((tm, D_MODEL), lambda i: (jnp.maximum(i - n_ctx, 0) + lat_tile0, 0)),
                mod(2), mod(3), mod(4),
                pl.BlockSpec((1, D_MODEL), const2), pl.BlockSpec((D_MODEL, D_MODEL), const2)]
    args = [attn_ctx, attn_lat, pooled, x_ctx, x_lat, ada, ada, ada, n2, w_out]
    if with_router:
        t_i = jnp.arange(tm)
        ltri = (t_i[None, :] < t_i[:, None]).astype(BF16)
        l_i = jnp.arange(LANES)
        utri = (l_i[:, None] < l_i[None, :]).astype(BF16)
        in_specs += [pl.BlockSpec((D_MODEL, LANES), const2), pl.BlockSpec((tm, tm), const2),
                     pl.BlockSpec((LANES, LANES), const2)]
        args += [w_router, ltri, utri]
        out_shape = [jax.ShapeDtypeStruct((T_ALL, D_MODEL), F32),
                     jax.ShapeDtypeStruct((N_ROUTE_TILES * SLOTS_PER_TILE, D_MODEL), BF16),
                     jax.ShapeDtypeStruct((N_ROUTE_TILES * SLOTS_PER_TILE, LANES), F32),
                     jax.ShapeDtypeStruct((T_ALL, LANES), F32),
                     jax.ShapeDtypeStruct((N_ROUTE_TILES, 8, LANES), jnp.int32)]
        out_specs = [pl.BlockSpec((tm, D_MODEL), row), pl.BlockSpec((SLOTS_PER_TILE, D_MODEL), row),
                     pl.BlockSpec((SLOTS_PER_TILE, LANES), row), pl.BlockSpec((tm, LANES), row),
                     pl.BlockSpec((None, 8, LANES), lambda i: (i, 0, 0))]
        scratch = []
    else:
        resident = lambda shape: pl.BlockSpec(shape, const2, pipeline_mode=pl.Buffered(1))
        in_specs += [mod(5), resident((D_MODEL, D_FF)), resident((D_MODEL, D_FF)), resident((D_FF, D_MODEL))]
        args += [ada, *ffn_w]
        out_shape = [jax.ShapeDtypeStruct((T_ALL, D_MODEL), F32)]
        out_specs = [pl.BlockSpec((tm, D_MODEL), row)]
        scratch = [pltpu.VMEM((tm, D_FF), BF16)]
    w, w_spec, wb_shape, wb_spec = _cast_slab(T_ALL // tm, cast_w)
    in_specs.append(w_spec(lambda i: i))
    args.append(w)
    out_shape.append(wb_shape)
    out_specs.append(wb_spec(lambda i: i))
    return pl.pallas_call(
        functools.partial(_mix_kernel, with_router),
        out_shape=tuple(out_shape),
        grid=(T_ALL // tm,),
        in_specs=in_specs,
        out_specs=tuple(out_specs),
        scratch_shapes=scratch,
        compiler_params=_cparams("arbitrary"),
        name="mix_router" if with_router else "mix_ffn",
    )(*args)


def _route_tables(meta):
    units = meta[:, 0, :N_EXPERTS]
    offs = meta[:, 1, :N_EXPERTS]
    cum = jnp.cumsum(units, axis=0)
    total = cum[-1]
    tiles_e = (total + GEMM_UNITS - 1) // GEMM_UNITS
    tile_end = jnp.cumsum(tiles_e)
    n_act = tile_end[-1]
    m = jnp.arange(N_GEMM_TILES)
    m_eff = jnp.minimum(m, jnp.maximum(n_act - 1, 0))
    expert = jnp.minimum(jnp.sum(tile_end[None, :] <= m_eff[:, None], axis=1), N_EXPERTS - 1)
    is_e = (expert[:, None] == jnp.arange(N_EXPERTS)[None, :]).astype(jnp.int32)
    pick = lambda per_expert: jnp.sum(is_e * per_expert[None, :], axis=1)
    first_q = (m_eff - pick(tile_end - tiles_e)) * GEMM_UNITS
    count = jnp.where(m < n_act, jnp.clip(pick(total) - first_q, 0, GEMM_UNITS), 0)
    q = first_q[:, None] + jnp.arange(GEMM_UNITS)[None, :]
    rows_of = lambda table: jnp.sum(is_e[:, None, :] * table[None, :, :], axis=2)
    cum_e, units_e, offs_e = rows_of(cum), rows_of(units), rows_of(offs)
    src_tile = jnp.minimum(jnp.sum(cum_e[:, None, :] <= q[:, :, None], axis=2), N_ROUTE_TILES - 1)
    is_t = (src_tile[:, :, None] == jnp.arange(N_ROUTE_TILES)[None, None, :]).astype(jnp.int32)
    at_tile = lambda per_tile: jnp.sum(is_t * per_tile[:, None, :], axis=2)
    uid = src_tile * UNITS_PER_TILE + at_tile(offs_e) + (q - at_tile(cum_e - units_e))
    uid = jnp.where(jnp.arange(GEMM_UNITS)[None, :] < count[:, None], uid, ZERO_UNIT)
    i32 = jnp.int32
    return expert.astype(i32), count.astype(i32), uid.reshape(-1).astype(i32), n_act.reshape(1).astype(i32)


def _gemm_kernel(em_ref, cnt_ref, ul_ref, nact_ref, xs_hbm, gs_hbm, wg_ref, wu_ref, wd_ref, ys_hbm,
                 xbuf, gbuf, obuf, acc_ref, a_buf, sem_in, sem_out):
    del em_ref
    m = pl.program_id(0)
    j = pl.program_id(1)
    last_j = pl.num_programs(1) - 1
    n_act = nact_ref[0]
    active = m < n_act
    slot = m % 2

    def rows(r):
        return pl.ds(r * SLOT_UNIT if isinstance(r, int) else pl.multiple_of(r * SLOT_UNIT, SLOT_UNIT), SLOT_UNIT)

    def in_copies(mm, sl, r):
        uid = ul_ref[mm * GEMM_UNITS + r]
        return (pltpu.make_async_copy(xs_hbm.at[uid], xbuf.at[sl, rows(r)], sem_in.at[sl]),
                pltpu.make_async_copy(gs_hbm.at[uid], gbuf.at[sl, rows(r)], sem_in.at[sl]))

    def out_copy(mm, r):
        uid = ul_ref[mm * GEMM_UNITS + r]
        return pltpu.make_async_copy(obuf.at[rows(r)], ys_hbm.at[uid], sem_out.at[0])

    def for_slot(sl_dyn, fn):
        if isinstance(sl_dyn, int):
            fn(sl_dyn)
            return
        for sl in range(2):
            pl.when(sl_dyn == sl)(functools.partial(fn, sl))

    def start_in(mm, sl_dyn):
        def issue(sl):
            for r in range(GEMM_UNITS):
                for cp in in_copies(mm, sl, r):
                    cp.start()

        for_slot(sl_dyn, issue)

    def wait_in(mm, sl_dyn):
        def wait(sl):
            for r in range(GEMM_UNITS):
                for cp in in_copies(mm, sl, r):
                    cp.wait()

        for_slot(sl_dyn, wait)

    def start_out(mm):
        n = cnt_ref[mm]
        for r in range(GEMM_UNITS):
            pl.when(r < n)(lambda r=r: out_copy(mm, r).start())

    def wait_out(mm):
        def body(r, carry):
            out_copy(mm, r).wait()
            return carry

        lax.fori_loop(0, cnt_ref[mm], body, 0)

    @pl.when(jnp.logical_and(active, j == 0))
    def _():
        @pl.when(m == 0)
        def _():
            start_in(0, 0)

        wait_in(m, slot)

        @pl.when(m + 1 < n_act)
        def _():
            start_in(m + 1, 1 - slot)

    @pl.when(jnp.logical_and(jnp.logical_and(active, j == last_j), m > 0))
    def _():
        wait_out(m - 1)

    n_valid = cnt_ref[m]
    for quarter in range(1, GEMM_QUARTERS + 1):
        n_rows = quarter * GEMM_ROWS // GEMM_QUARTERS
        lo_units = (quarter - 1) * GEMM_UNITS // GEMM_QUARTERS
        hi_units = quarter * GEMM_UNITS // GEMM_QUARTERS

        @pl.when(jnp.logical_and(active, jnp.logical_and(n_valid > lo_units, n_valid <= hi_units)))
        def _(n_rows=n_rows):
            _swiglu_hidden(xbuf[slot, :n_rows, :], wg_ref, wu_ref, a_buf.at[:n_rows, :])
            part = _dot(a_buf[:n_rows, :], wd_ref[...])

            @pl.when(j == 0)
            def _():
                acc_ref[:n_rows, :] = part

            @pl.when(jnp.logical_and(j > 0, j < last_j))
            def _():
                acc_ref[:n_rows, :] += part

            @pl.when(j == last_j)
            def _():
                gate = gbuf[slot, :n_rows, :]
                for cb in range(D_MODEL // LANES):
                    cols = slice(cb * LANES, (cb + 1) * LANES)
                    obuf[:n_rows, cols] = ((acc_ref[:n_rows, cols] + part[:, cols]) * gate).astype(BF16)

    @pl.when(jnp.logical_and(active, j == last_j))
    def _():
        start_out(m)

        @pl.when(m == n_act - 1)
        def _():
            wait_out(m)


def _gemm_call(expert, count, uids, n_act, xs, gs, wg, wu, wd):
    tf = TILE_FE
    n_j = D_FF_EXPERT // tf
    assert n_j >= 2

    def w_col(m, j, em, cnt, ul, nact):
        return (em[m], 0, jnp.where(m < nact[0], j, n_j - 1))

    def w_row(m, j, em, cnt, ul, nact):
        return (em[m], jnp.where(m < nact[0], j, n_j - 1), 0)

    grid_spec = pltpu.PrefetchScalarGridSpec(
        num_scalar_prefetch=4,
        grid=(N_GEMM_TILES, n_j),
        in_specs=[pl.BlockSpec(memory_space=pl.ANY),
                  pl.BlockSpec(memory_space=pl.ANY),
                  pl.BlockSpec((None, D_MODEL, tf), w_col),
                  pl.BlockSpec((None, D_MODEL, tf), w_col),
                  pl.BlockSpec((None, tf, D_MODEL), w_row)],
        out_specs=pl.BlockSpec(memory_space=pl.ANY),
        scratch_shapes=[pltpu.VMEM((2, GEMM_ROWS, D_MODEL), BF16),
                        pltpu.VMEM((2, GEMM_ROWS, LANES), F32),
                        pltpu.VMEM((GEMM_ROWS, D_MODEL), BF16),
                        pltpu.VMEM((GEMM_ROWS, D_MODEL), F32),
                        pltpu.VMEM((GEMM_ROWS, tf), BF16),
                        pltpu.SemaphoreType.DMA((2,)),
                        pltpu.SemaphoreType.DMA((1,))])
    return pl.pallas_call(
        _gemm_kernel,
        out_shape=jax.ShapeDtypeStruct((N_UNITS, SLOT_UNIT, D_MODEL), BF16),
        grid_spec=grid_spec,
        input_output_aliases={4: 0},
        compiler_params=_cparams("arbitrary", "arbitrary"),
        name="moe_experts",
    )(expert, count, uids, n_act,
      xs.reshape(N_UNITS, SLOT_UNIT, D_MODEL), gs.reshape(N_UNITS, SLOT_UNIT, LANES), wg, wu, wd)


def _combine_kernel(ys_ref, route_ref, x_ref, g2_ref, fg_ref, o_ref):
    route = route_ref[...]
    s1 = route[:, 0:1].astype(jnp.int32)
    s2 = route[:, 1:2].astype(jnp.int32)
    scol = lax.broadcasted_iota(jnp.int32, (TILE_MIX, SLOTS_PER_TILE), 1)
    unperm = (jnp.where(scol == s1, 1.0, 0.0) + jnp.where(scol == s2, 1.0, 0.0)).astype(BF16)
    y = _dot(unperm, ys_ref[...])
    o_ref[...] = _rms(x_ref[...] + g2_ref[...] * y) * fg_ref[...]


def _combine_call(ys, route, x, ada, final_g, tile0, n_tok):
    tm = TILE_MIX
    return pl.pallas_call(
        _combine_kernel,
        out_shape=jax.ShapeDtypeStruct((n_tok, D_MODEL), F32),
        grid=(n_tok // tm,),
        in_specs=[pl.BlockSpec((SLOTS_PER_TILE, D_MODEL), lambda i: (i + tile0, 0)),
                  pl.BlockSpec((tm, LANES), lambda i: (i + tile0, 0)),
                  pl.BlockSpec((tm, D_MODEL), lambda i: (i + tile0, 0)),
                  pl.BlockSpec((None, 1, D_MODEL), lambda i: (_mod_row(i + tile0, tm), 0, 5)),
                  pl.BlockSpec((1, D_MODEL), lambda i: (0, 0))],
        out_specs=pl.BlockSpec((tm, D_MODEL), lambda i: (i, 0)),
        compiler_params=_cparams("arbitrary"),
        name="moe_combine",
    )(ys.reshape(N_UNITS * SLOT_UNIT, D_MODEL), route, x, ada, final_g)


def _rot_cols(w):
    q = QK_ROPE // 4
    return jnp.concatenate([-w[..., q:2 * q], w[..., :q], -w[..., 3 * q:], w[..., 2 * q:3 * q]], axis=-1)


def _rope_tables():
    t = jnp.arange(DEC_SEQ)
    rows = (t // GRID_W).astype(F32)
    cols = (t % GRID_W).astype(F32)
    half = QK_ROPE // 2
    freqs = ROPE_BASE ** (-jnp.arange(0, half, 2, dtype=F32) / half)
    ang_r = rows[:, None] * freqs
    ang_c = cols[:, None] * freqs
    ang = jnp.concatenate([ang_r, ang_r, ang_c, ang_c], axis=-1)
    cos, sin = jnp.cos(ang), jnp.sin(ang)
    one, zero = jnp.ones_like(cos), jnp.zeros_like(cos)
    tq = jnp.stack([jnp.concatenate([one, zero], -1), jnp.concatenate([cos, sin], -1)])
    tc = jnp.stack([jnp.concatenate([one, one], -1), jnp.concatenate([cos, cos], -1)])
    ts = jnp.stack([jnp.concatenate([zero, zero], -1), jnp.concatenate([sin, sin], -1)])
    return tq, tc, ts


def _layer_weights(w_in, w_uq, w_ukv):
    wi = w_in.astype(BF16)
    w_kpe = wi[..., OFF_U:OFF_U + QK_ROPE]
    w_rot = _rot_cols(w_kpe)
    w_in_r = jnp.concatenate([wi[..., :OFF_U], wi[..., OFF_U + QK_ROPE:], w_kpe, w_kpe, w_rot, w_rot], axis=-1)
    wq = w_uq.astype(BF16).reshape(DEPTH, Q_LORA, N_HEADS, QK_NOPE + QK_ROPE)
    w_uq_r = jnp.concatenate([wq, _rot_cols(wq[..., QK_NOPE:])], axis=-1).reshape(DEPTH, Q_LORA, QK_W)
    kv = w_ukv.astype(BF16).reshape(DEPTH, KV_LORA, N_HEADS, QK_NOPE + V_DIM)
    w_uk = kv[..., :QK_NOPE].reshape(DEPTH, KV_LORA, N_HEADS * QK_NOPE)
    w_uv = kv[..., QK_NOPE:].reshape(DEPTH, KV_LORA, V_W)
    return w_in_r, w_uq_r, w_uk, w_uv


def kernel(x_prompt, x_sample, c, cache_ckv, cache_kpe, c_ctx, norm1_g, norm2_g, w_ada, b_ada, w_in, q_norm_g,
           kv_norm_g, w_uq, w_ukv, w_pool, pool_scale, w_out, ffn_w_gate, ffn_w_up, ffn_w_down, moe_w_router,
           moe_w_gate, moe_w_up, moe_w_down, final_norm_g):
    x_ctx = x_prompt.reshape(T_CTX, D_MODEL)
    x_lat = x_sample.reshape(T_LAT, D_MODEL)
    cond = jnp.concatenate([c_ctx[None, :], c, jnp.zeros((COND_ROWS - 1 - DEC_BATCH, D_MODEL), F32)], axis=0)
    ada_all = _ada_call(cond, w_ada, b_ada)
    tq, tc, ts = _rope_tables()

    w_in_all, w_uq_all, w_uk_all, w_uv_all = _layer_weights(w_in, w_uq, w_ukv)
    dup = jnp.concatenate([jnp.eye(QK_ROPE, dtype=BF16)] * 2, axis=1)
    kc_all, vc_all = _cache_call(cache_ckv, cache_kpe, w_uk_all, w_uv_all, dup)

    assert DEPTH == 2
    moe_gate_2d = moe_w_gate[0].reshape(N_EXPERTS * D_MODEL, D_FF_EXPERT)
    moe_up_2d = moe_w_up[0].reshape(N_EXPERTS * D_MODEL, D_FF_EXPERT)
    moe_down_2d = moe_w_down[0].reshape(N_EXPERTS * D_FF_EXPERT, D_MODEL)
    new_ckv = new_kpe = None
    for l in range(DEPTH):
        w_in_r, w_uq_r, w_uk, w_uv = w_in_all[l], w_uq_all[l], w_uk_all[l], w_uv_all[l]
        ada = ada_all[l].reshape(COND_ROWS, 1, 6 * D_MODEL)
        cast_ws = (ffn_w_gate[0], ffn_w_up[0], ffn_w_down[0], *w_out) if l == 0 else ()
        q, k, v, new_ckv, new_kpe, pooled, *cast = _in_call(
            x_ctx, x_lat, ada, norm1_g[l][None, :], w_in_r, q_norm_g[l][None, :], kv_norm_g[l][None, :], w_uq_r,
            w_uk, w_uv, tq, tc, ts, w_pool[l].astype(BF16), pool_scale[l][None, :], new_ckv, new_kpe, cast_ws)
        if l == 0:
            ffn_w, w_out_b = cast[:3], cast[3:]
        j = l // 2
        if l % 2 == 0:
            attn_c, attn_l, moe_wd = _attn_call(q, k, v, kc_all[l], vc_all[l], cast_w=moe_down_2d)
            x, moe_wg = _mix_call(attn_c, attn_l, pooled, x_ctx, x_lat, ada, norm2_g[l][None, :],
                                  w_out_b[l], moe_gate_2d, ffn_w=ffn_w)
            x_ctx = x_lat = x
        else:
            attn_c, attn_l = _attn_call(q, k, v, kc_all[l], vc_all[l])
            w_r = jnp.pad(moe_w_router[j], ((0, 0), (0, LANES - N_EXPERTS))).astype(BF16)
            x, xs, gs, route, meta, moe_wu = _mix_call(attn_c, attn_l, pooled, x_ctx, x_lat, ada,
                                                       norm2_g[l][None, :], w_out_b[l], moe_up_2d,
                                                       w_router=w_r)
            expert, count, uids, n_act = _route_tables(meta)
            ys = _gemm_call(expert, count, uids, n_act, xs, gs,
                            moe_wg.reshape(N_EXPERTS, D_MODEL, D_FF_EXPERT),
                            moe_wu.reshape(N_EXPERTS, D_MODEL, D_FF_EXPERT),
                            moe_wd.reshape(N_EXPERTS, D_FF_EXPERT, D_MODEL))
            fg = final_norm_g[None, :]
            y_prompt = _combine_call(ys, route, x, ada, fg, 0, T_CTX).reshape(BATCH, SEQ, D_MODEL)
            y_sample = _combine_call(ys, route, x, ada, fg, T_CTX // TILE_MIX, T_LAT).reshape(
                DEC_BATCH, DEC_SEQ, D_MODEL)

    return y_prompt, y_sample, new_ckv, new_kpe
```

```python
import functools

import jax
import jax.numpy as jnp
from jax import lax
from jax.experimental import pallas as pl
from jax.experimental.pallas import tpu as pltpu

D_MODEL = 1024
BATCH = 32
SEQ = 256
DEPTH = 2
DEC_BATCH = 8
DEC_SEQ = 1024
PAST_LEN = 512
GRID_W = 64
N_HEADS = 4
QK_NOPE = 128
QK_ROPE = 64
V_DIM = 128
Q_LORA = 384
KV_LORA = 256
POOL_W = 512
POOL_GROUPS = 4
POOL_WINDOWS = (2, 4, 8, 16)
POOL_CH = POOL_W // POOL_GROUPS
D_FF = 2816
N_EXPERTS = 8
D_FF_EXPERT = 3584
ROPE_BASE = 10000.0
EPS = 1e-6

T_CTX = BATCH * SEQ
T_LAT = DEC_BATCH * DEC_SEQ
T_ALL = T_CTX + T_LAT

LANES = 128
HEAD_PAD = 256
QK_W = N_HEADS * HEAD_PAD
V_W = N_HEADS * V_DIM
OFF_CKV = Q_LORA
OFF_U = Q_LORA + KV_LORA
OFF_KA = OFF_U + POOL_W
OFF_KB = OFF_KA + LANES
N_PROJ = OFF_KB + LANES

TILE_IN = 1024
CHUNK = 256
HALO = 8
REGION = CHUNK + 2 * HALO
CTX_SEQS_PER_STEP = 4
LAT_Q_TILE = 1024
TILE_MIX = 512
TOP_K = 2
SLOT_UNIT = 16
UNITS_PER_TILE = TILE_MIX * TOP_K // SLOT_UNIT + N_EXPERTS
SLOTS_PER_TILE = UNITS_PER_TILE * SLOT_UNIT
SLOT_BLOCK = SLOTS_PER_TILE // 4
N_ROUTE_TILES = T_ALL // TILE_MIX
N_UNITS = N_ROUTE_TILES * UNITS_PER_TILE
ZERO_UNIT = UNITS_PER_TILE - 1
GEMM_UNITS = 64
GEMM_ROWS = GEMM_UNITS * SLOT_UNIT
GEMM_QUARTERS = 4
N_GEMM_TILES = -(-N_UNITS // GEMM_UNITS) + N_EXPERTS
TILE_FE = D_FF_EXPERT // 2
FF_CHUNK = 256
COND_ROWS = 16
VMEM_LIMIT = 56 * 1024 * 1024

F32 = jnp.float32
BF16 = jnp.bfloat16


def _rms(x):
    return x * lax.rsqrt(jnp.mean(x * x, axis=-1, keepdims=True) + EPS)


def _dot(a, b):
    return jnp.dot(a, b, preferred_element_type=F32)


def _dot_nt(a, b):
    return lax.dot_general(a, b, (((1,), (1,)), ((), ())), preferred_element_type=F32)


def _silu(x):
    return x * (1.0 / (1.0 + jnp.exp(-x)))


def _cparams(*sem):
    return pltpu.CompilerParams(dimension_semantics=sem, vmem_limit_bytes=VMEM_LIMIT)


def _ada_kernel(cond_ref, w_ref, b_ref, o_ref):
    c = cond_ref[...]
    o_ref[...] = jnp.dot(_silu(c), w_ref[...], preferred_element_type=F32,
                         precision=lax.Precision.HIGHEST) + b_ref[...]


def _ada_call(cond, w_ada, b_ada):
    n_blk = 4
    bw = 6 * D_MODEL // n_blk
    return pl.pallas_call(
        _ada_kernel,
        out_shape=jax.ShapeDtypeStruct((DEPTH, COND_ROWS, 6 * D_MODEL), F32),
        grid=(DEPTH, n_blk),
        in_specs=[
            pl.BlockSpec((COND_ROWS, D_MODEL), lambda l, j: (0, 0)),
            pl.BlockSpec((None, D_MODEL, bw), lambda l, j: (l, 0, j)),
            pl.BlockSpec((None, 1, bw), lambda l, j: (l, 0, j)),
        ],
        out_specs=pl.BlockSpec((None, COND_ROWS, bw), lambda l, j: (l, 0, j)),
        compiler_params=_cparams("arbitrary", "arbitrary"),
        name="ada_params",
    )(cond, w_ada, b_ada.reshape(DEPTH, 1, 6 * D_MODEL))


def _cache_kernel(ckv_ref, kpe_ref, wuk_ref, wuv_ref, dup_ref, k_ref, v_ref):
    ckv = ckv_ref[...].astype(BF16)
    knope = _dot(ckv, wuk_ref[...])
    v_ref[...] = _dot(ckv, wuv_ref[...]).astype(BF16)
    kdup = _dot(kpe_ref[...].astype(BF16), dup_ref[...]).astype(BF16)
    for h in range(N_HEADS):
        k_ref[:, h * HEAD_PAD:h * HEAD_PAD + QK_NOPE] = knope[:, h * QK_NOPE:(h + 1) * QK_NOPE].astype(BF16)
        k_ref[:, h * HEAD_PAD + QK_NOPE:(h + 1) * HEAD_PAD] = kdup


def _cache_call(cache_ckv, cache_kpe, w_uk, w_uv, dup):
    n_tok = DEC_BATCH * PAST_LEN
    return pl.pallas_call(
        _cache_kernel,
        out_shape=(jax.ShapeDtypeStruct((DEPTH, n_tok, QK_W), BF16),
                   jax.ShapeDtypeStruct((DEPTH, n_tok, V_W), BF16)),
        grid=(DEPTH, DEC_BATCH),
        in_specs=[
            pl.BlockSpec((None, None, PAST_LEN, KV_LORA), lambda l, b: (b, l, 0, 0)),
            pl.BlockSpec((None, None, PAST_LEN, QK_ROPE), lambda l, b: (b, l, 0, 0)),
            pl.BlockSpec((None, KV_LORA, N_HEADS * QK_NOPE), lambda l, b: (l, 0, 0)),
            pl.BlockSpec((None, KV_LORA, V_W), lambda l, b: (l, 0, 0)),
            pl.BlockSpec((QK_ROPE, LANES), lambda l, b: (0, 0)),
        ],
        out_specs=(pl.BlockSpec((None, PAST_LEN, QK_W), lambda l, b: (l, b, 0)),
                   pl.BlockSpec((None, PAST_LEN, V_W), lambda l, b: (l, b, 0))),
        compiler_params=_cparams("arbitrary", "arbitrary"),
        name="cache_kv",
    )(cache_ckv, cache_kpe, w_uk, w_uv, dup)


def _in_kernel(n_prev, n_cast, xc_ref, xl_ref, sh_ref, sc_ref, g_ref, win_ref, qg_ref, kvg_ref, wuq_ref, wuk_ref,
               wuv_ref, tq_ref, tc_ref, ts_ref, wpool_ref, ps_ref, *rest):
    if n_prev:
        prev_ckv_ref, prev_kpe_ref, *rest = rest
    cast_in, rest = rest[:n_cast], rest[n_cast:]
    q_out, k_out, v_out, ckv_out, kpe_out, pool_out, *rest = rest
    cast_out, (u_scr,) = rest[:n_cast], rest[n_cast:]
    for w_ref, wb_ref in zip(cast_in, cast_out):
        wb_ref[...] = w_ref[...].astype(BF16)
    is_ctx = pl.program_id(0) >= (T_ALL - T_CTX) // TILE_IN
    if n_prev:
        ckv_out[:, :n_prev] = prev_ckv_ref[...]
        kpe_out[:, :n_prev] = prev_kpe_ref[...]
    shift = sh_ref[...]
    scale1 = 1.0 + sc_ref[...]
    qk_scale = (QK_NOPE + QK_ROPE) ** -0.5
    n_chunks = TILE_IN // CHUNK
    zeros = jnp.zeros((HALO, POOL_W), F32)
    seq_len = jnp.where(is_ctx, SEQ, DEC_SEQ)
    row = lax.broadcasted_iota(jnp.int32, (CHUNK, POOL_CH), 0)

    def pool_chunk(c):
        base = c * REGION + HALO
        rows = slice(c * CHUNK, (c + 1) * CHUNK)
        t = row + jnp.where(is_ctx, 0, c * CHUNK)
        for g, w in enumerate(POOL_WINDOWS):
            cols = slice(g * POOL_CH, (g + 1) * POOL_CH)
            acc = u_scr[base - w // 2:base - w // 2 + CHUNK, cols]
            for j in range(-w // 2 + 1, w // 2):
                acc = acc + u_scr[base + j:base + j + CHUNK, cols]
            cnt = jnp.minimum(t + w // 2, seq_len) - jnp.maximum(t - w // 2, 0)
            pooled = acc / cnt.astype(F32) - u_scr[base:base + CHUNK, cols]
            lin = _dot(pooled.astype(BF16), wpool_ref[g]) * ps_ref[:, cols]
            pool_out[rows, cols] = lin.astype(BF16)

    for c in range(n_chunks):
        r0 = c * CHUNK
        rows = slice(r0, r0 + CHUNK)
        x = jnp.where(is_ctx, xc_ref[rows, :], xl_ref[rows, :])
        h = (_rms(x) * g_ref[...] * scale1 + shift).astype(BF16)
        proj = _dot(h, win_ref[...])

        qn = (_rms(proj[:, :Q_LORA]) * qg_ref[...]).astype(BF16)
        q = _dot(qn, wuq_ref[...]) * qk_scale
        tq = tq_ref[rows, :]
        for hd in range(N_HEADS):
            lo = hd * HEAD_PAD
            q_out[rows, lo:lo + QK_NOPE] = q[:, lo:lo + QK_NOPE].astype(BF16)
            q_out[rows, lo + QK_NOPE:lo + HEAD_PAD] = (q[:, lo + QK_NOPE:lo + HEAD_PAD] * tq).astype(BF16)

        ckv = _rms(proj[:, OFF_CKV:OFF_U]) * kvg_ref[...]
        ckv_out[c, n_prev] = ckv
        ckv_b = ckv.astype(BF16)
        knope = _dot(ckv_b, wuk_ref[...])
        v_out[rows, :] = _dot(ckv_b, wuv_ref[...]).astype(BF16)

        k_a = proj[:, OFF_KA:OFF_KB]
        k_b = proj[:, OFF_KB:N_PROJ]
        kpe_out[c, n_prev] = k_a[:, :QK_ROPE]
        kr = (k_a * tc_ref[rows, :] + k_b * ts_ref[rows, :]).astype(BF16)
        for hd in range(N_HEADS):
            lo = hd * HEAD_PAD
            k_out[rows, lo:lo + QK_NOPE] = knope[:, hd * QK_NOPE:(hd + 1) * QK_NOPE].astype(BF16)
            k_out[rows, lo + QK_NOPE:lo + HEAD_PAD] = kr

        u = proj[:, OFF_U:OFF_KA]
        base = c * REGION + HALO
        u_scr[base:base + CHUNK, :] = u
        if c == 0:
            u_scr[0:HALO, :] = zeros
        else:
            u_scr[base - 2 * HALO:base - HALO, :] = jnp.where(is_ctx, zeros, u[:HALO, :])
        if c == n_chunks - 1:
            u_scr[base + CHUNK:base + CHUNK + HALO, :] = zeros
        else:
            u_scr[base + CHUNK + HALO:base + CHUNK + 2 * HALO, :] = jnp.where(is_ctx, zeros, u[CHUNK - HALO:, :])
        if c > 0:
            pool_chunk(c - 1)
    pool_chunk(n_chunks - 1)


def _in_call(layer, x_ctx, x_lat, ada, g1, w_in, qg, kvg, w_uq, w_uk, w_uv, tq, tc, ts, w_pool, pscale, prev_ckv, prev_kpe,
             cast_ws=()):
    assert CHUNK == SEQ
    n_tiles = T_ALL // TILE_IN
    n_ctx = T_CTX // TILE_IN
    n_prev = 0 if prev_ckv is None else prev_ckv.shape[1]
    seqs = TILE_IN // SEQ
    lat_tile0 = n_ctx if x_lat.shape[0] == T_ALL else 0

    def tile(i):
        return jnp.where(i < n_tiles - n_ctx, i + n_ctx, i - (n_tiles - n_ctx))

    def cond_row(i):
        return jnp.maximum(tile(i) - n_ctx + 1, 0)

    def tab(i):
        return (jnp.minimum(jnp.maximum(tile(i) - n_ctx + 1, 0), 1), 0, 0)

    rows = lambda i: (tile(i), 0)
    ctx_blk4 = lambda i: (jnp.where(tile(i) < n_ctx, tile(i), 0), 0, 0, 0)
    const2 = lambda i: (0, 0)
    of_layer = lambda i: (layer, 0, 0)
    prev_specs = [pl.BlockSpec((seqs, n_prev, SEQ, KV_LORA), ctx_blk4),
                  pl.BlockSpec((seqs, n_prev, SEQ, QK_ROPE), ctx_blk4)] if n_prev else []
    prev_args = [prev_ckv, prev_kpe] if n_prev else []
    casts = [_cast_slab(n_tiles, w) for w in cast_ws]
    step = lambda i: i
    return pl.pallas_call(
        functools.partial(_in_kernel, n_prev, len(casts)),
        out_shape=(jax.ShapeDtypeStruct((T_ALL, QK_W), BF16),
                   jax.ShapeDtypeStruct((T_ALL, QK_W), BF16),
                   jax.ShapeDtypeStruct((T_ALL, V_W), BF16),
                   jax.ShapeDtypeStruct((BATCH, n_prev + 1, SEQ, KV_LORA), F32),
                   jax.ShapeDtypeStruct((BATCH, n_prev + 1, SEQ, QK_ROPE), F32),
                   jax.ShapeDtypeStruct((T_ALL, POOL_W), BF16)) + tuple(c[2] for c in casts),
        grid=(n_tiles,),
        in_specs=[
            pl.BlockSpec((TILE_IN, D_MODEL), lambda i: (jnp.minimum(tile(i), n_ctx - 1), 0)),
            pl.BlockSpec((TILE_IN, D_MODEL), lambda i: (jnp.maximum(tile(i) - n_ctx, 0) + lat_tile0, 0)),
            pl.BlockSpec((None, 1, D_MODEL), lambda i: (cond_row(i), 0, 0)),
            pl.BlockSpec((None, 1, D_MODEL), lambda i: (cond_row(i), 0, 1)),
            pl.BlockSpec((1, D_MODEL), const2),
            pl.BlockSpec((None, D_MODEL, N_PROJ), of_layer),
            pl.BlockSpec((1, Q_LORA), const2),
            pl.BlockSpec((1, KV_LORA), const2),
            pl.BlockSpec((None, Q_LORA, QK_W), of_layer),
            pl.BlockSpec((None, KV_LORA, N_HEADS * QK_NOPE), of_layer),
            pl.BlockSpec((None, KV_LORA, V_W), of_layer),
            pl.BlockSpec((None, TILE_IN, LANES), tab),
            pl.BlockSpec((None, TILE_IN, LANES), tab),
            pl.BlockSpec((None, TILE_IN, LANES), tab),
            pl.BlockSpec((POOL_GROUPS, POOL_CH, POOL_CH), lambda i: (0, 0, 0)),
            pl.BlockSpec((1, POOL_W), const2),
        ] + prev_specs + [c[1](step) for c in casts],
        out_specs=(pl.BlockSpec((TILE_IN, QK_W), rows),
                   pl.BlockSpec((TILE_IN, QK_W), rows),
                   pl.BlockSpec((TILE_IN, V_W), rows),
                   pl.BlockSpec((seqs, n_prev + 1, SEQ, KV_LORA), ctx_blk4),
                   pl.BlockSpec((seqs, n_prev + 1, SEQ, QK_ROPE), ctx_blk4),
                   pl.BlockSpec((TILE_IN, POOL_W), rows)) + tuple(c[3](step) for c in casts),
        scratch_shapes=[pltpu.VMEM((TILE_IN // CHUNK * REGION, POOL_W), F32)],
        compiler_params=_cparams("arbitrary"),
        name="in_proj",
    )(x_ctx, x_lat, ada, ada, g1, w_in, qg, kvg, w_uq, w_uk, w_uv, tq, tc, ts, w_pool, pscale, *prev_args,
      *(c[0] for c in casts))


def _softmax_pv(scores, values):
    m = scores[0].max(axis=-1, keepdims=True)
    for s in scores[1:]:
        m = jnp.maximum(m, s.max(axis=-1, keepdims=True))
    den = None
    out = None
    for s, v in zip(scores, values):
        p = jnp.exp(s - m)
        d = p.sum(axis=-1, keepdims=True)
        o = _dot(p.astype(BF16), v)
        den = d if den is None else den + d
        out = o if out is None else out + o
    return out / den


def _attn_ctx_kernel(q_ref, k_ref, v_ref, o_ref):
    for b in range(CTX_SEQS_PER_STEP):
        rows = slice(b * SEQ, (b + 1) * SEQ)
        for hd in range(N_HEADS):
            qk = slice(hd * HEAD_PAD, (hd + 1) * HEAD_PAD)
            vv = slice(hd * V_DIM, (hd + 1) * V_DIM)
            s = _dot_nt(q_ref[rows, qk], k_ref[rows, qk])
            o_ref[rows, vv] = _softmax_pv([s], [v_ref[rows, vv]]).astype(BF16)


def _cast_slab(n_steps, w):
    n_rows, n_cols = w.shape
    slab = n_rows // n_steps
    assert slab * n_steps == n_rows
    in_spec = lambda step_of: pl.BlockSpec((slab, n_cols), lambda *g: (step_of(*g), 0))
    return w, in_spec, jax.ShapeDtypeStruct(w.shape, BF16), in_spec


def _attn_lat_kernel(q_ref, k_ref, v_ref, kc_ref, vc_ref, *rest):
    if len(rest) == 3:
        w_ref, o_ref, wb_ref = rest
        wb_ref[...] = w_ref[...].astype(BF16)
    else:
        (o_ref,) = rest
    for hd in range(N_HEADS):
        qk = slice(hd * HEAD_PAD, (hd + 1) * HEAD_PAD)
        vv = slice(hd * V_DIM, (hd + 1) * V_DIM)
        q = q_ref[:, qk]
        s1 = _dot_nt(q, k_ref[:, qk])
        s2 = _dot_nt(q, kc_ref[:, qk])
        o_ref[:, vv] = _softmax_pv([s1, s2], [v_ref[:, vv], vc_ref[:, vv]]).astype(BF16)


def _attn_call(layer, q, k, v, kc, vc, cast_w=None):
    attn_ctx = pl.pallas_call(
        _attn_ctx_kernel,
        out_shape=jax.ShapeDtypeStruct((T_CTX, V_W), BF16),
        grid=(BATCH // CTX_SEQS_PER_STEP,),
        in_specs=[pl.BlockSpec((CTX_SEQS_PER_STEP * SEQ, QK_W), lambda b: (b, 0)),
                  pl.BlockSpec((CTX_SEQS_PER_STEP * SEQ, QK_W), lambda b: (b, 0)),
                  pl.BlockSpec((CTX_SEQS_PER_STEP * SEQ, V_W), lambda b: (b, 0))],
        out_specs=pl.BlockSpec((CTX_SEQS_PER_STEP * SEQ, V_W), lambda b: (b, 0)),
        compiler_params=_cparams("arbitrary"),
        name="attn_ctx",
    )(q, k, v)

    tq = LAT_Q_TILE if cast_w is None else LAT_Q_TILE // 2
    n_q = DEC_SEQ // tq
    lat0 = T_CTX // DEC_SEQ
    in_specs = [pl.BlockSpec((tq, QK_W), lambda b, i: (T_CTX // tq + b * n_q + i, 0)),
                pl.BlockSpec((DEC_SEQ, QK_W), lambda b, i: (lat0 + b, 0)),
                pl.BlockSpec((DEC_SEQ, V_W), lambda b, i: (lat0 + b, 0)),
                pl.BlockSpec((None, PAST_LEN, QK_W), lambda b, i: (layer, b, 0)),
                pl.BlockSpec((None, PAST_LEN, V_W), lambda b, i: (layer, b, 0))]
    out_shape = [jax.ShapeDtypeStruct((T_LAT, V_W), BF16)]
    out_specs = [pl.BlockSpec((tq, V_W), lambda b, i: (b * n_q + i, 0))]
    args = [q, k, v, kc, vc]
    if cast_w is not None:
        w, w_in_spec, wb_shape, wb_spec = _cast_slab(DEC_BATCH * n_q, cast_w)
        step_of = lambda b, i: b * n_q + i
        in_specs.append(w_in_spec(step_of))
        out_shape.append(wb_shape)
        out_specs.append(wb_spec(step_of))
        args.append(w)
    outs = pl.pallas_call(
        _attn_lat_kernel,
        out_shape=tuple(out_shape),
        grid=(DEC_BATCH, n_q),
        in_specs=in_specs,
        out_specs=tuple(out_specs),
        compiler_params=_cparams("arbitrary", "arbitrary"),
        name="attn_lat",
    )(*args)
    return (attn_ctx, *outs)


def _swiglu_hidden(h, wg_ref, wu_ref, a_buf):
    width = a_buf.shape[1]
    for c0 in range(0, width, FF_CHUNK):
        cols = slice(c0, min(c0 + FF_CHUNK, width))
        a_buf[:, cols] = (_silu(_dot(h, wg_ref[:, cols])) * _dot(h, wu_ref[:, cols])).astype(BF16)


def _mix_kernel(with_router, attn_c_ref, attn_l_ref, pool_ref, xc_ref, xl_ref, g1_ref, sh2_ref, sc2_ref, n2_ref, wo_ref,
                *rest):
    if with_router:
        wr_ref, ltri_ref, utri_ref, w_ref, x_out, xs_out, gs_out, route_out, meta_out, wb_ref = rest
    else:
        g2_ref, wg_ref, wu_ref, wd_ref, w_ref, x_out, wb_ref, a_buf = rest
    wb_ref[...] = w_ref[...].astype(BF16)
    is_ctx = pl.program_id(0) < T_CTX // TILE_MIX
    halves = []
    for r0 in range(0, TILE_MIX, TILE_MIX // 2):
        rows = slice(r0, r0 + TILE_MIX // 2)
        attn = jnp.where(is_ctx, attn_c_ref[rows, :], attn_l_ref[rows, :])
        y = _dot(attn, wo_ref[:V_W, :]) + _dot(pool_ref[rows, :], wo_ref[V_W:, :])
        x_new = jnp.where(is_ctx, xc_ref[rows, :], xl_ref[rows, :]) + g1_ref[...] * y
        x_out[rows, :] = x_new
        halves.append((_rms(x_new) * n2_ref[...] * (1.0 + sc2_ref[...]) + sh2_ref[...]).astype(BF16))
    h = jnp.concatenate(halves, axis=0)
    if not with_router:
        _swiglu_hidden(h, wg_ref, wu_ref, a_buf)
        x_out[...] += g2_ref[...] * _dot(a_buf[...], wd_ref[...])
        return

    logits = _dot(h, wr_ref[...])
    lane = lax.broadcasted_iota(jnp.int32, logits.shape, 1)
    neg = float(jnp.finfo(F32).min)
    lg = jnp.where(lane < N_EXPERTS, logits, neg)
    m1 = lg.max(axis=-1, keepdims=True)
    i1 = jnp.where(lg == m1, lane, LANES).min(axis=-1, keepdims=True)
    lg2 = jnp.where(lane == i1, neg, lg)
    m2 = lg2.max(axis=-1, keepdims=True)
    i2 = jnp.where(lg2 == m2, lane, LANES).min(axis=-1, keepdims=True)
    e = jnp.exp(m2 - m1)
    w1 = 1.0 / (1.0 + e)
    w2 = e / (1.0 + e)

    sel1 = lane == i1
    sel2 = lane == i2
    member = jnp.where(jnp.logical_or(sel1, sel2), 1.0, 0.0)
    rank = _dot(ltri_ref[...], member.astype(BF16))
    n_tok = member.sum(axis=0, keepdims=True)
    units = jnp.floor((n_tok + (SLOT_UNIT - 1)) * (1.0 / SLOT_UNIT))
    unit_off = _dot(jnp.broadcast_to(units, (8, LANES)).astype(BF16), utri_ref[...])[0:1, :]
    slot_of = SLOT_UNIT * unit_off + rank
    slot1 = jnp.where(sel1, slot_of, 0.0).sum(axis=-1, keepdims=True)
    slot2 = jnp.where(sel2, slot_of, 0.0).sum(axis=-1, keepdims=True)
    route = jnp.where(lane == 0, slot1, jnp.where(lane == 1, slot2,
                      jnp.where(lane == 2, w1, jnp.where(lane == 3, w2, 0.0))))
    route_out[...] = route
    sub = lax.broadcasted_iota(jnp.int32, (8, LANES), 0)
    meta_out[...] = jnp.where(sub == 0, units, jnp.where(sub == 1, unit_off, 0.0)).astype(jnp.int32)

    rt = route.T
    s1 = rt[0:1, :].astype(jnp.int32)
    s2 = rt[1:2, :].astype(jnp.int32)
    for r0 in range(0, SLOTS_PER_TILE, SLOT_BLOCK):
        blk = slice(r0, r0 + SLOT_BLOCK)
        srow = r0 + lax.broadcasted_iota(jnp.int32, (SLOT_BLOCK, TILE_MIX), 0)
        hit1 = srow == s1
        hit2 = srow == s2
        perm = (jnp.where(hit1, 1.0, 0.0) + jnp.where(hit2, 1.0, 0.0)).astype(BF16)
        xs_out[blk, :] = _dot(perm, h).astype(BF16)
        gate = (jnp.where(hit1, rt[2:3, :], 0.0) + jnp.where(hit2, rt[3:4, :], 0.0)).sum(axis=-1, keepdims=True)
        gs_out[blk, :] = jnp.broadcast_to(gate, (SLOT_BLOCK, LANES))


def _mod_row(i, tile):
    n_ctx = T_CTX // tile
    per_seq = DEC_SEQ // tile
    return jnp.where(i < n_ctx, 0, 1 + (i - n_ctx) // per_seq)


def _mix_call(attn_ctx, attn_lat, pooled, x_ctx, x_lat, ada, n2, w_out, cast_w, w_router=None, ffn_w=None):
    with_router = w_router is not None
    assert with_router != (ffn_w is not None)
    tm = TILE_MIX
    n_ctx = T_CTX // tm
    lat_tile0 = n_ctx if x_lat.shape[0] == T_ALL else 0
    row = lambda i: (i, 0)
    const2 = lambda i: (0, 0)
    mod = lambda k: pl.BlockSpec((None, 1, D_MODEL), lambda i: (_mod_row(i, tm), 0, k))
    ctx_blk = lambda i: (jnp.minimum(i, n_ctx - 1), 0)
    in_specs = [pl.BlockSpec((tm, V_W), ctx_blk),
                pl.BlockSpec((tm, V_W), lambda i: (jnp.maximum(i - n_ctx, 0), 0)),
                pl.BlockSpec((tm, POOL_W), row),
                pl.BlockSpec((tm, D_MODEL), ctx_blk),
                pl.BlockSpec((tm, D_MODEL), lambda i: (jnp.maximum(i - n_ctx, 0) + lat_tile0, 0)),
                mod(2), mod(3), mod(4),
                pl.BlockSpec((1, D_MODEL), const2), pl.BlockSpec((D_MODEL, D_MODEL), const2)]
    args = [attn_ctx, attn_lat, pooled, x_ctx, x_lat, ada, ada, ada, n2, w_out]
    if with_router:
        t_i = jnp.arange(tm)
        ltri = (t_i[None, :] < t_i[:, None]).astype(BF16)
        l_i = jnp.arange(LANES)
        utri = (l_i[:, None] < l_i[None, :]).astype(BF16)
        in_specs += [pl.BlockSpec((D_MODEL, LANES), const2), pl.BlockSpec((tm, tm), const2),
                     pl.BlockSpec((LANES, LANES), const2)]
        args += [w_router, ltri, utri]
        out_shape = [jax.ShapeDtypeStruct((T_ALL, D_MODEL), F32),
                     jax.ShapeDtypeStruct((N_ROUTE_TILES * SLOTS_PER_TILE, D_MODEL), BF16),
                     jax.ShapeDtypeStruct((N_ROUTE_TILES * SLOTS_PER_TILE, LANES), F32),
                     jax.ShapeDtypeStruct((T_ALL, LANES), F32),
                     jax.ShapeDtypeStruct((N_ROUTE_TILES, 8, LANES), jnp.int32)]
        out_specs = [pl.BlockSpec((tm, D_MODEL), row), pl.BlockSpec((SLOTS_PER_TILE, D_MODEL), row),
                     pl.BlockSpec((SLOTS_PER_TILE, LANES), row), pl.BlockSpec((tm, LANES), row),
                     pl.BlockSpec((None, 8, LANES), lambda i: (i, 0, 0))]
        scratch = []
    else:
        resident = lambda shape: pl.BlockSpec(shape, const2, pipeline_mode=pl.Buffered(1))
        in_specs += [mod(5), resident((D_MODEL, D_FF)), resident((D_MODEL, D_FF)), resident((D_FF, D_MODEL))]
        args += [ada, *ffn_w]
        out_shape = [jax.ShapeDtypeStruct((T_ALL, D_MODEL), F32)]
        out_specs = [pl.BlockSpec((tm, D_MODEL), row)]
        scratch = [pltpu.VMEM((tm, D_FF), BF16)]
    w, w_spec, wb_shape, wb_spec = _cast_slab(T_ALL // tm, cast_w)
    in_specs.append(w_spec(lambda i: i))
    args.append(w)
    out_shape.append(wb_shape)
    out_specs.append(wb_spec(lambda i: i))
    return pl.pallas_call(
        functools.partial(_mix_kernel, with_router),
        out_shape=tuple(out_shape),
        grid=(T_ALL // tm,),
        in_specs=in_specs,
        out_specs=tuple(out_specs),
        scratch_shapes=scratch,
        compiler_params=_cparams("arbitrary"),
        name="mix_router" if with_router else "mix_ffn",
    )(*args)


def _route_tables(meta):
    units = meta[:, 0, :N_EXPERTS]
    offs = meta[:, 1, :N_EXPERTS]
    cum = jnp.cumsum(units, axis=0)
    total = cum[-1]
    tiles_e = (total + GEMM_UNITS - 1) // GEMM_UNITS
    tile_end = jnp.cumsum(tiles_e)
    n_act = tile_end[-1]
    m = jnp.arange(N_GEMM_TILES)
    m_eff = jnp.minimum(m, jnp.maximum(n_act - 1, 0))
    expert = jnp.minimum(jnp.sum(tile_end[None, :] <= m_eff[:, None], axis=1), N_EXPERTS - 1)
    is_e = (expert[:, None] == jnp.arange(N_EXPERTS)[None, :]).astype(jnp.int32)
    pick = lambda per_expert: jnp.sum(is_e * per_expert[None, :], axis=1)
    first_q = (m_eff - pick(tile_end - tiles_e)) * GEMM_UNITS
    count = jnp.where(m < n_act, jnp.clip(pick(total) - first_q, 0, GEMM_UNITS), 0)
    q = first_q[:, None] + jnp.arange(GEMM_UNITS)[None, :]
    rows_of = lambda table: jnp.sum(is_e[:, None, :] * table[None, :, :], axis=2)
    cum_e, units_e, offs_e = rows_of(cum), rows_of(units), rows_of(offs)
    src_tile = jnp.minimum(jnp.sum(cum_e[:, None, :] <= q[:, :, None], axis=2), N_ROUTE_TILES - 1)
    is_t = (src_tile[:, :, None] == jnp.arange(N_ROUTE_TILES)[None, None, :]).astype(jnp.int32)
    at_tile = lambda per_tile: jnp.sum(is_t * per_tile[:, None, :], axis=2)
    uid = src_tile * UNITS_PER_TILE + at_tile(offs_e) + (q - at_tile(cum_e - units_e))
    uid = jnp.where(jnp.arange(GEMM_UNITS)[None, :] < count[:, None], uid, ZERO_UNIT)
    i32 = jnp.int32
    return expert.astype(i32), count.astype(i32), uid.reshape(-1).astype(i32), n_act.reshape(1).astype(i32)


def _gemm_kernel(em_ref, cnt_ref, ul_ref, nact_ref, xs_hbm, gs_hbm, wg_ref, wu_ref, wd_ref, ys_hbm,
                 xbuf, gbuf, obuf, acc_ref, a_buf, sem_in, sem_out):
    del em_ref
    m = pl.program_id(0)
    j = pl.program_id(1)
    last_j = pl.num_programs(1) - 1
    n_act = nact_ref[0]
    active = m < n_act
    slot = m % 2

    def rows(r):
        return pl.ds(r * SLOT_UNIT if isinstance(r, int) else pl.multiple_of(r * SLOT_UNIT, SLOT_UNIT), SLOT_UNIT)

    def in_copies(mm, sl, r):
        uid = ul_ref[mm * GEMM_UNITS + r]
        return (pltpu.make_async_copy(xs_hbm.at[uid], xbuf.at[sl, rows(r)], sem_in.at[sl]),
                pltpu.make_async_copy(gs_hbm.at[uid], gbuf.at[sl, rows(r)], sem_in.at[sl]))

    def out_copy(mm, r):
        uid = ul_ref[mm * GEMM_UNITS + r]
        return pltpu.make_async_copy(obuf.at[rows(r)], ys_hbm.at[uid], sem_out.at[0])

    def for_slot(sl_dyn, fn):
        if isinstance(sl_dyn, int):
            fn(sl_dyn)
            return
        for sl in range(2):
            pl.when(sl_dyn == sl)(functools.partial(fn, sl))

    def start_in(mm, sl_dyn):
        def issue(sl):
            for r in range(GEMM_UNITS):
                for cp in in_copies(mm, sl, r):
                    cp.start()

        for_slot(sl_dyn, issue)

    def wait_in(mm, sl_dyn):
        def wait(sl):
            for r in range(GEMM_UNITS):
                for cp in in_copies(mm, sl, r):
                    cp.wait()

        for_slot(sl_dyn, wait)

    def start_out(mm):
        n = cnt_ref[mm]
        for r in range(GEMM_UNITS):
            pl.when(r < n)(lambda r=r: out_copy(mm, r).start())

    def wait_out(mm):
        def body(r, carry):
            out_copy(mm, r).wait()
            return carry

        lax.fori_loop(0, cnt_ref[mm], body, 0)

    @pl.when(jnp.logical_and(active, j == 0))
    def _():
        @pl.when(m == 0)
        def _():
            start_in(0, 0)

        wait_in(m, slot)

        @pl.when(m + 1 < n_act)
        def _():
            start_in(m + 1, 1 - slot)

    @pl.when(jnp.logical_and(jnp.logical_and(active, j == last_j), m > 0))
    def _():
        wait_out(m - 1)

    n_valid = cnt_ref[m]
    for quarter in range(1, GEMM_QUARTERS + 1):
        n_rows = quarter * GEMM_ROWS // GEMM_QUARTERS
        lo_units = (quarter - 1) * GEMM_UNITS // GEMM_QUARTERS
        hi_units = quarter * GEMM_UNITS // GEMM_QUARTERS

        @pl.when(jnp.logical_and(active, jnp.logical_and(n_valid > lo_units, n_valid <= hi_units)))
        def _(n_rows=n_rows):
            _swiglu_hidden(xbuf[slot, :n_rows, :], wg_ref, wu_ref, a_buf.at[:n_rows, :])
            part = _dot(a_buf[:n_rows, :], wd_ref[...])

            @pl.when(j == 0)
            def _():
                acc_ref[:n_rows, :] = part

            @pl.when(jnp.logical_and(j > 0, j < last_j))
            def _():
                acc_ref[:n_rows, :] += part

            @pl.when(j == last_j)
            def _():
                gate = gbuf[slot, :n_rows, :]
                for cb in range(D_MODEL // LANES):
                    cols = slice(cb * LANES, (cb + 1) * LANES)
                    obuf[:n_rows, cols] = ((acc_ref[:n_rows, cols] + part[:, cols]) * gate).astype(BF16)

    @pl.when(jnp.logical_and(active, j == last_j))
    def _():
        start_out(m)

        @pl.when(m == n_act - 1)
        def _():
            wait_out(m)


def _gemm_call(expert, count, uids, n_act, xs, gs, wg, wu, wd):
    tf = TILE_FE
    n_j = D_FF_EXPERT // tf
    assert n_j >= 2

    def w_col(m, j, em, cnt, ul, nact):
        return (em[m], 0, jnp.where(m < nact[0], j, n_j - 1))

    def w_row(m, j, em, cnt, ul, nact):
        return (em[m], jnp.where(m < nact[0], j, n_j - 1), 0)

    grid_spec = pltpu.PrefetchScalarGridSpec(
        num_scalar_prefetch=4,
        grid=(N_GEMM_TILES, n_j),
        in_specs=[pl.BlockSpec(memory_space=pl.ANY),
                  pl.BlockSpec(memory_space=pl.ANY),
                  pl.BlockSpec((None, D_MODEL, tf), w_col),
                  pl.BlockSpec((None, D_MODEL, tf), w_col),
                  pl.BlockSpec((None, tf, D_MODEL), w_row)],
        out_specs=pl.BlockSpec(memory_space=pl.ANY),
        scratch_shapes=[pltpu.VMEM((2, GEMM_ROWS, D_MODEL), BF16),
                        pltpu.VMEM((2, GEMM_ROWS, LANES), F32),
                        pltpu.VMEM((GEMM_ROWS, D_MODEL), BF16),
                        pltpu.VMEM((GEMM_ROWS, D_MODEL), F32),
                        pltpu.VMEM((GEMM_ROWS, tf), BF16),
                        pltpu.SemaphoreType.DMA((2,)),
                        pltpu.SemaphoreType.DMA((1,))])
    return pl.pallas_call(
        _gemm_kernel,
        out_shape=jax.ShapeDtypeStruct((N_UNITS, SLOT_UNIT, D_MODEL), BF16),
        grid_spec=grid_spec,
        input_output_aliases={4: 0},
        compiler_params=_cparams("arbitrary", "arbitrary"),
        name="moe_experts",
    )(expert, count, uids, n_act,
      xs.reshape(N_UNITS, SLOT_UNIT, D_MODEL), gs.reshape(N_UNITS, SLOT_UNIT, LANES), wg, wu, wd)


def _combine_kernel(ys_ref, route_ref, x_ref, g2_ref, fg_ref, o_ref):
    route = route_ref[...]
    s1 = route[:, 0:1].astype(jnp.int32)
    s2 = route[:, 1:2].astype(jnp.int32)
    scol = lax.broadcasted_iota(jnp.int32, (TILE_MIX, SLOTS_PER_TILE), 1)
    unperm = (jnp.where(scol == s1, 1.0, 0.0) + jnp.where(scol == s2, 1.0, 0.0)).astype(BF16)
    y = _dot(unperm, ys_ref[...])
    o_ref[...] = _rms(x_ref[...] + g2_ref[...] * y) * fg_ref[...]


def _combine_call(ys, route, x, ada, final_g, tile0, n_tok):
    tm = TILE_MIX
    return pl.pallas_call(
        _combine_kernel,
        out_shape=jax.ShapeDtypeStruct((n_tok, D_MODEL), F32),
        grid=(n_tok // tm,),
        in_specs=[pl.BlockSpec((SLOTS_PER_TILE, D_MODEL), lambda i: (i + tile0, 0)),
                  pl.BlockSpec((tm, LANES), lambda i: (i + tile0, 0)),
                  pl.BlockSpec((tm, D_MODEL), lambda i: (i + tile0, 0)),
                  pl.BlockSpec((None, 1, D_MODEL), lambda i: (_mod_row(i + tile0, tm), 0, 5)),
                  pl.BlockSpec((1, D_MODEL), lambda i: (0, 0))],
        out_specs=pl.BlockSpec((tm, D_MODEL), lambda i: (i, 0)),
        compiler_params=_cparams("arbitrary"),
        name="moe_combine",
    )(ys.reshape(N_UNITS * SLOT_UNIT, D_MODEL), route, x, ada, final_g)


def _rot_cols(w):
    q = QK_ROPE // 4
    return jnp.concatenate([-w[..., q:2 * q], w[..., :q], -w[..., 3 * q:], w[..., 2 * q:3 * q]], axis=-1)


def _rope_tables():
    t = jnp.arange(DEC_SEQ)
    rows = (t // GRID_W).astype(F32)
    cols = (t % GRID_W).astype(F32)
    half = QK_ROPE // 2
    freqs = ROPE_BASE ** (-jnp.arange(0, half, 2, dtype=F32) / half)
    ang_r = rows[:, None] * freqs
    ang_c = cols[:, None] * freqs
    ang = jnp.concatenate([ang_r, ang_r, ang_c, ang_c], axis=-1)
    cos, sin = jnp.cos(ang), jnp.sin(ang)
    one, zero = jnp.ones_like(cos), jnp.zeros_like(cos)
    tq = jnp.stack([jnp.concatenate([one, zero], -1), jnp.concatenate([cos, sin], -1)])
    tc = jnp.stack([jnp.concatenate([one, one], -1), jnp.concatenate([cos, cos], -1)])
    ts = jnp.stack([jnp.concatenate([zero, zero], -1), jnp.concatenate([sin, sin], -1)])
    return tq, tc, ts


def _layer_weights(w_in, w_uq, w_ukv):
    wi = w_in.astype(BF16)
    w_kpe = wi[..., OFF_U:OFF_U + QK_ROPE]
    w_rot = _rot_cols(w_kpe)
    w_in_r = jnp.concatenate([wi[..., :OFF_U], wi[..., OFF_U + QK_ROPE:], w_kpe, w_kpe, w_rot, w_rot], axis=-1)
    wq = w_uq.astype(BF16).reshape(DEPTH, Q_LORA, N_HEADS, QK_NOPE + QK_ROPE)
    w_uq_r = jnp.concatenate([wq, _rot_cols(wq[..., QK_NOPE:])], axis=-1).reshape(DEPTH, Q_LORA, QK_W)
    kv = w_ukv.astype(BF16).reshape(DEPTH, KV_LORA, N_HEADS, QK_NOPE + V_DIM)
    w_uk = kv[..., :QK_NOPE].reshape(DEPTH, KV_LORA, N_HEADS * QK_NOPE)
    w_uv = kv[..., QK_NOPE:].reshape(DEPTH, KV_LORA, V_W)
    return w_in_r, w_uq_r, w_uk, w_uv


def kernel(x_prompt, x_sample, c, cache_ckv, cache_kpe, c_ctx, norm1_g, norm2_g, w_ada, b_ada, w_in, q_norm_g,
           kv_norm_g, w_uq, w_ukv, w_pool, pool_scale, w_out, ffn_w_gate, ffn_w_up, ffn_w_down, moe_w_router,
           moe_w_gate, moe_w_up, moe_w_down, final_norm_g):
    x_ctx = x_prompt.reshape(T_CTX, D_MODEL)
    x_lat = x_sample.reshape(T_LAT, D_MODEL)
    cond = jnp.concatenate([c_ctx[None, :], c, jnp.zeros((COND_ROWS - 1 - DEC_BATCH, D_MODEL), F32)], axis=0)
    ada_all = _ada_call(cond, w_ada, b_ada)
    tq, tc, ts = _rope_tables()

    w_in_all, w_uq_all, w_uk_all, w_uv_all = _layer_weights(w_in, w_uq, w_ukv)
    dup = jnp.concatenate([jnp.eye(QK_ROPE, dtype=BF16)] * 2, axis=1)
    kc_all, vc_all = _cache_call(cache_ckv, cache_kpe, w_uk_all, w_uv_all, dup)

    assert DEPTH == 2
    moe_gate_2d = moe_w_gate[0].reshape(N_EXPERTS * D_MODEL, D_FF_EXPERT)
    moe_up_2d = moe_w_up[0].reshape(N_EXPERTS * D_MODEL, D_FF_EXPERT)
    moe_down_2d = moe_w_down[0].reshape(N_EXPERTS * D_FF_EXPERT, D_MODEL)
    new_ckv = new_kpe = None
    for l in range(DEPTH):
        ada = ada_all[l].reshape(COND_ROWS, 1, 6 * D_MODEL)
        cast_ws = (ffn_w_gate[0], ffn_w_up[0], ffn_w_down[0], *w_out) if l == 0 else ()
        q, k, v, new_ckv, new_kpe, pooled, *cast = _in_call(
            l, x_ctx, x_lat, ada, norm1_g[l][None, :], w_in_all, q_norm_g[l][None, :], kv_norm_g[l][None, :],
            w_uq_all, w_uk_all, w_uv_all, tq, tc, ts, w_pool[l].astype(BF16), pool_scale[l][None, :], new_ckv,
            new_kpe, cast_ws)
        if l == 0:
            ffn_w, w_out_b = cast[:3], cast[3:]
        j = l // 2
        if l % 2 == 0:
            attn_c, attn_l, moe_wd = _attn_call(l, q, k, v, kc_all, vc_all, cast_w=moe_down_2d)
            x, moe_wg = _mix_call(attn_c, attn_l, pooled, x_ctx, x_lat, ada, norm2_g[l][None, :],
                                  w_out_b[l], moe_gate_2d, ffn_w=ffn_w)
            x_ctx = x_lat = x
        else:
            attn_c, attn_l = _attn_call(l, q, k, v, kc_all, vc_all)
            w_r = jnp.pad(moe_w_router[j], ((0, 0), (0, LANES - N_EXPERTS))).astype(BF16)
            x, xs, gs, route, meta, moe_wu = _mix_call(attn_c, attn_l, pooled, x_ctx, x_lat, ada,
                                                       norm2_g[l][None, :], w_out_b[l], moe_up_2d,
                                                       w_router=w_r)
            expert, count, uids, n_act = _route_tables(meta)
            ys = _gemm_call(expert, count, uids, n_act, xs, gs,
                            moe_wg.reshape(N_EXPERTS, D_MODEL, D_FF_EXPERT),
                            moe_wu.reshape(N_EXPERTS, D_MODEL, D_FF_EXPERT),
                            moe_wd.reshape(N_EXPERTS, D_FF_EXPERT, D_MODEL))
            fg = final_norm_g[None, :]
            y_prompt = _combine_call(ys, route, x, ada, fg, 0, T_CTX).reshape(BATCH, SEQ, D_MODEL)
            y_sample = _combine_call(ys, route, x, ada, fg, T_CTX // TILE_MIX, T_LAT).reshape(
                DEC_BATCH, DEC_SEQ, D_MODEL)

    return y_prompt, y_sample, new_ckv, new_kpe
```

```python
import functools

import jax
import jax.numpy as jnp
from jax import lax
from jax.experimental import pallas as pl
from jax.experimental.pallas import tpu as pltpu

D_MODEL = 1024
BATCH = 32
SEQ = 256
DEPTH = 2
DEC_BATCH = 8
DEC_SEQ = 1024
PAST_LEN = 512
GRID_W = 64
N_HEADS = 4
QK_NOPE = 128
QK_ROPE = 64
V_DIM = 128
Q_LORA = 384
KV_LORA = 256
POOL_W = 512
POOL_GROUPS = 4
POOL_WINDOWS = (2, 4, 8, 16)
POOL_CH = POOL_W // POOL_GROUPS
D_FF = 2816
N_EXPERTS = 8
D_FF_EXPERT = 3584
ROPE_BASE = 10000.0
EPS = 1e-6

T_CTX = BATCH * SEQ
T_LAT = DEC_BATCH * DEC_SEQ
T_ALL = T_CTX + T_LAT

LANES = 128
HEAD_PAD = 256
QK_W = N_HEADS * HEAD_PAD
V_W = N_HEADS * V_DIM
OFF_CKV = Q_LORA
OFF_U = Q_LORA + KV_LORA
OFF_KA = OFF_U + POOL_W
OFF_KB = OFF_KA + LANES
N_PROJ = OFF_KB + LANES

TILE_IN = 1024
CHUNK = 256
HALO = 8
REGION = CHUNK + 2 * HALO
CTX_SEQS_PER_STEP = 4
LAT_Q_TILE = 1024
TILE_MIX = 512
TOP_K = 2
SLOT_UNIT = 16
UNITS_PER_TILE = TILE_MIX * TOP_K // SLOT_UNIT + N_EXPERTS
SLOTS_PER_TILE = UNITS_PER_TILE * SLOT_UNIT
SLOT_BLOCK = SLOTS_PER_TILE // 4
N_ROUTE_TILES = T_ALL // TILE_MIX
N_UNITS = N_ROUTE_TILES * UNITS_PER_TILE
ZERO_UNIT = UNITS_PER_TILE - 1
GEMM_UNITS = 64
GEMM_ROWS = GEMM_UNITS * SLOT_UNIT
GEMM_QUARTERS = 4
N_GEMM_TILES = -(-N_UNITS // GEMM_UNITS) + N_EXPERTS
TILE_FE = D_FF_EXPERT // 2
FF_CHUNK = 256
COND_ROWS = 16
VMEM_LIMIT = 56 * 1024 * 1024

F32 = jnp.float32
BF16 = jnp.bfloat16


def _rms(x):
    return x * lax.rsqrt(jnp.mean(x * x, axis=-1, keepdims=True) + EPS)


def _dot(a, b):
    return jnp.dot(a, b, preferred_element_type=F32)


def _dot_nt(a, b):
    return lax.dot_general(a, b, (((1,), (1,)), ((), ())), preferred_element_type=F32)


def _silu(x):
    return x * (1.0 / (1.0 + jnp.exp(-x)))


def _cparams(*sem):
    return pltpu.CompilerParams(dimension_semantics=sem, vmem_limit_bytes=VMEM_LIMIT)


def _ada_kernel(cond_ref, w_ref, b_ref, o_ref):
    c = cond_ref[...]
    o_ref[...] = jnp.dot(_silu(c), w_ref[...], preferred_element_type=F32,
                         precision=lax.Precision.HIGHEST) + b_ref[...]


def _ada_call(cond, w_ada, b_ada):
    n_blk = 4
    bw = 6 * D_MODEL // n_blk
    return pl.pallas_call(
        _ada_kernel,
        out_shape=jax.ShapeDtypeStruct((DEPTH, COND_ROWS, 6 * D_MODEL), F32),
        grid=(DEPTH, n_blk),
        in_specs=[
            pl.BlockSpec((COND_ROWS, D_MODEL), lambda l, j: (0, 0)),
            pl.BlockSpec((None, D_MODEL, bw), lambda l, j: (l, 0, j)),
            pl.BlockSpec((None, 1, bw), lambda l, j: (l, 0, j)),
        ],
        out_specs=pl.BlockSpec((None, COND_ROWS, bw), lambda l, j: (l, 0, j)),
        compiler_params=_cparams("arbitrary", "arbitrary"),
        name="ada_params",
    )(cond, w_ada, b_ada.reshape(DEPTH, 1, 6 * D_MODEL))


def _cache_kernel(ckv_ref, kpe_ref, wuk_ref, wuv_ref, dup_ref, k_ref, v_ref):
    ckv = ckv_ref[...].astype(BF16)
    knope = _dot(ckv, wuk_ref[...])
    v_ref[...] = _dot(ckv, wuv_ref[...]).astype(BF16)
    kpe = kpe_ref[...].T
    kdup = _dot(kpe.astype(BF16), dup_ref[...]).astype(BF16)
    for h in range(N_HEADS):
        k_ref[:, h * HEAD_PAD:h * HEAD_PAD + QK_NOPE] = knope[:, h * QK_NOPE:(h + 1) * QK_NOPE].astype(BF16)
        k_ref[:, h * HEAD_PAD + QK_NOPE:(h + 1) * HEAD_PAD] = kdup


def _cache_call(cache_ckv, cache_kpe, w_uk, w_uv, dup):
    n_tok = DEC_BATCH * PAST_LEN
    return pl.pallas_call(
        _cache_kernel,
        out_shape=(jax.ShapeDtypeStruct((DEPTH, n_tok, QK_W), BF16),
                   jax.ShapeDtypeStruct((DEPTH, n_tok, V_W), BF16)),
        grid=(DEPTH, DEC_BATCH),
        in_specs=[
            pl.BlockSpec((None, None, PAST_LEN, KV_LORA), lambda l, b: (b, l, 0, 0)),
            pl.BlockSpec((None, None, QK_ROPE, PAST_LEN), lambda l, b: (b, l, 0, 0)),
            pl.BlockSpec((None, KV_LORA, N_HEADS * QK_NOPE), lambda l, b: (l, 0, 0)),
            pl.BlockSpec((None, KV_LORA, V_W), lambda l, b: (l, 0, 0)),
            pl.BlockSpec((QK_ROPE, LANES), lambda l, b: (0, 0)),
        ],
        out_specs=(pl.BlockSpec((None, PAST_LEN, QK_W), lambda l, b: (l, b, 0)),
                   pl.BlockSpec((None, PAST_LEN, V_W), lambda l, b: (l, b, 0))),
        compiler_params=_cparams("arbitrary", "arbitrary"),
        name="cache_kv",
    )(cache_ckv, jnp.swapaxes(cache_kpe, -1, -2), w_uk, w_uv, dup)


def _in_kernel(n_prev, n_cast, xc_ref, xl_ref, sh_ref, sc_ref, g_ref, win_ref, qg_ref, kvg_ref, wuq_ref, wuk_ref,
               wuv_ref, tq_ref, tc_ref, ts_ref, wpool_ref, ps_ref, *rest):
    if n_prev:
        prev_ckv_ref, prev_kpe_ref, *rest = rest
    cast_in, rest = rest[:n_cast], rest[n_cast:]
    q_out, k_out, v_out, ckv_out, kpe_out, pool_out, *rest = rest
    cast_out, (u_scr,) = rest[:n_cast], rest[n_cast:]
    for w_ref, wb_ref in zip(cast_in, cast_out):
        wb_ref[...] = w_ref[...].astype(BF16)
    is_ctx = pl.program_id(0) >= (T_ALL - T_CTX) // TILE_IN
    if n_prev:
        ckv_out[:, :n_prev] = prev_ckv_ref[...]
        kpe_out[:, :n_prev] = prev_kpe_ref[...]
    shift = sh_ref[...]
    scale1 = 1.0 + sc_ref[...]
    qk_scale = (QK_NOPE + QK_ROPE) ** -0.5
    n_chunks = TILE_IN // CHUNK
    zeros = jnp.zeros((HALO, POOL_W), F32)
    seq_len = jnp.where(is_ctx, SEQ, DEC_SEQ)
    row = lax.broadcasted_iota(jnp.int32, (CHUNK, POOL_CH), 0)

    def pool_chunk(c):
        base = c * REGION + HALO
        rows = slice(c * CHUNK, (c + 1) * CHUNK)
        t = row + jnp.where(is_ctx, 0, c * CHUNK)
        for g, w in enumerate(POOL_WINDOWS):
            cols = slice(g * POOL_CH, (g + 1) * POOL_CH)
            acc = u_scr[base - w // 2:base - w // 2 + CHUNK, cols]
            for j in range(-w // 2 + 1, w // 2):
                acc = acc + u_scr[base + j:base + j + CHUNK, cols]
            cnt = jnp.minimum(t + w // 2, seq_len) - jnp.maximum(t - w // 2, 0)
            pooled = acc / cnt.astype(F32) - u_scr[base:base + CHUNK, cols]
            lin = _dot(pooled.astype(BF16), wpool_ref[g]) * ps_ref[:, cols]
            pool_out[rows, cols] = lin.astype(BF16)

    for c in range(n_chunks):
        r0 = c * CHUNK
        rows = slice(r0, r0 + CHUNK)
        x = jnp.where(is_ctx, xc_ref[rows, :], xl_ref[rows, :])
        h = (_rms(x) * g_ref[...] * scale1 + shift).astype(BF16)
        proj = _dot(h, win_ref[...])

        qn = (_rms(proj[:, :Q_LORA]) * qg_ref[...]).astype(BF16)
        q = _dot(qn, wuq_ref[...]) * qk_scale
        tq = tq_ref[rows, :]
        for hd in range(N_HEADS):
            lo = hd * HEAD_PAD
            q_out[rows, lo:lo + QK_NOPE] = q[:, lo:lo + QK_NOPE].astype(BF16)
            q_out[rows, lo + QK_NOPE:lo + HEAD_PAD] = (q[:, lo + QK_NOPE:lo + HEAD_PAD] * tq).astype(BF16)

        ckv = _rms(proj[:, OFF_CKV:OFF_U]) * kvg_ref[...]
        ckv_out[c, n_prev] = ckv
        ckv_b = ckv.astype(BF16)
        knope = _dot(ckv_b, wuk_ref[...])
        v_out[rows, :] = _dot(ckv_b, wuv_ref[...]).astype(BF16)

        k_a = proj[:, OFF_KA:OFF_KB]
        k_b = proj[:, OFF_KB:N_PROJ]
        kpe_out[c, n_prev] = k_a.T[:QK_ROPE, :]
        kr = (k_a * tc_ref[rows, :] + k_b * ts_ref[rows, :]).astype(BF16)
        for hd in range(N_HEADS):
            lo = hd * HEAD_PAD
            k_out[rows, lo:lo + QK_NOPE] = knope[:, hd * QK_NOPE:(hd + 1) * QK_NOPE].astype(BF16)
            k_out[rows, lo + QK_NOPE:lo + HEAD_PAD] = kr

        u = proj[:, OFF_U:OFF_KA]
        base = c * REGION + HALO
        u_scr[base:base + CHUNK, :] = u
        if c == 0:
            u_scr[0:HALO, :] = zeros
        else:
            u_scr[base - 2 * HALO:base - HALO, :] = jnp.where(is_ctx, zeros, u[:HALO, :])
        if c == n_chunks - 1:
            u_scr[base + CHUNK:base + CHUNK + HALO, :] = zeros
        else:
            u_scr[base + CHUNK + HALO:base + CHUNK + 2 * HALO, :] = jnp.where(is_ctx, zeros, u[CHUNK - HALO:, :])
        if c > 0:
            pool_chunk(c - 1)
    pool_chunk(n_chunks - 1)


def _in_call(layer, x_ctx, x_lat, ada, g1, w_in, qg, kvg, w_uq, w_uk, w_uv, tq, tc, ts, w_pool, pscale, prev_ckv, prev_kpe,
             cast_ws=()):
    assert CHUNK == SEQ
    n_tiles = T_ALL // TILE_IN
    n_ctx = T_CTX // TILE_IN
    n_prev = 0 if prev_ckv is None else prev_ckv.shape[1]
    seqs = TILE_IN // SEQ
    lat_tile0 = n_ctx if x_lat.shape[0] == T_ALL else 0

    def tile(i):
        return jnp.where(i < n_tiles - n_ctx, i + n_ctx, i - (n_tiles - n_ctx))

    def cond_row(i):
        return jnp.maximum(tile(i) - n_ctx + 1, 0)

    def tab(i):
        return (jnp.minimum(jnp.maximum(tile(i) - n_ctx + 1, 0), 1), 0, 0)

    rows = lambda i: (tile(i), 0)
    ctx_blk4 = lambda i: (jnp.where(tile(i) < n_ctx, tile(i), 0), 0, 0, 0)
    const2 = lambda i: (0, 0)
    of_layer = lambda i: (layer, 0, 0)
    prev_specs = [pl.BlockSpec((seqs, n_prev, SEQ, KV_LORA), ctx_blk4),
                  pl.BlockSpec((seqs, n_prev, QK_ROPE, SEQ), ctx_blk4)] if n_prev else []
    prev_args = [prev_ckv, prev_kpe] if n_prev else []
    casts = [_cast_slab(n_tiles, w) for w in cast_ws]
    step = lambda i: i
    return pl.pallas_call(
        functools.partial(_in_kernel, n_prev, len(casts)),
        out_shape=(jax.ShapeDtypeStruct((T_ALL, QK_W), BF16),
                   jax.ShapeDtypeStruct((T_ALL, QK_W), BF16),
                   jax.ShapeDtypeStruct((T_ALL, V_W), BF16),
                   jax.ShapeDtypeStruct((BATCH, n_prev + 1, SEQ, KV_LORA), F32),
                   jax.ShapeDtypeStruct((BATCH, n_prev + 1, QK_ROPE, SEQ), F32),
                   jax.ShapeDtypeStruct((T_ALL, POOL_W), BF16)) + tuple(c[2] for c in casts),
        grid=(n_tiles,),
        in_specs=[
            pl.BlockSpec((TILE_IN, D_MODEL), lambda i: (jnp.minimum(tile(i), n_ctx - 1), 0)),
            pl.BlockSpec((TILE_IN, D_MODEL), lambda i: (jnp.maximum(tile(i) - n_ctx, 0) + lat_tile0, 0)),
            pl.BlockSpec((None, 1, D_MODEL), lambda i: (cond_row(i), 0, 0)),
            pl.BlockSpec((None, 1, D_MODEL), lambda i: (cond_row(i), 0, 1)),
            pl.BlockSpec((1, D_MODEL), const2),
            pl.BlockSpec((None, D_MODEL, N_PROJ), of_layer),
            pl.BlockSpec((1, Q_LORA), const2),
            pl.BlockSpec((1, KV_LORA), const2),
            pl.BlockSpec((None, Q_LORA, QK_W), of_layer),
            pl.BlockSpec((None, KV_LORA, N_HEADS * QK_NOPE), of_layer),
            pl.BlockSpec((None, KV_LORA, V_W), of_layer),
            pl.BlockSpec((None, TILE_IN, LANES), tab),
            pl.BlockSpec((None, TILE_IN, LANES), tab),
            pl.BlockSpec((None, TILE_IN, LANES), tab),
            pl.BlockSpec((POOL_GROUPS, POOL_CH, POOL_CH), lambda i: (0, 0, 0)),
            pl.BlockSpec((1, POOL_W), const2),
        ] + prev_specs + [c[1](step) for c in casts],
        out_specs=(pl.BlockSpec((TILE_IN, QK_W), rows),
                   pl.BlockSpec((TILE_IN, QK_W), rows),
                   pl.BlockSpec((TILE_IN, V_W), rows),
                   pl.BlockSpec((seqs, n_prev + 1, SEQ, KV_LORA), ctx_blk4),
                   pl.BlockSpec((seqs, n_prev + 1, QK_ROPE, SEQ), ctx_blk4),
                   pl.BlockSpec((TILE_IN, POOL_W), rows)) + tuple(c[3](step) for c in casts),
        scratch_shapes=[pltpu.VMEM((TILE_IN // CHUNK * REGION, POOL_W), F32)],
        compiler_params=_cparams("arbitrary"),
        name="in_proj",
    )(x_ctx, x_lat, ada, ada, g1, w_in, qg, kvg, w_uq, w_uk, w_uv, tq, tc, ts, w_pool, pscale, *prev_args,
      *(c[0] for c in casts))


def _softmax_pv(scores, values):
    m = scores[0].max(axis=-1, keepdims=True)
    for s in scores[1:]:
        m = jnp.maximum(m, s.max(axis=-1, keepdims=True))
    den = None
    out = None
    for s, v in zip(scores, values):
        p = jnp.exp(s - m)
        d = p.sum(axis=-1, keepdims=True)
        o = _dot(p.astype(BF16), v)
        den = d if den is None else den + d
        out = o if out is None else out + o
    return out / den


def _attn_ctx_kernel(q_ref, k_ref, v_ref, o_ref):
    for b in range(CTX_SEQS_PER_STEP):
        rows = slice(b * SEQ, (b + 1) * SEQ)
        for hd in range(N_HEADS):
            qk = slice(hd * HEAD_PAD, (hd + 1) * HEAD_PAD)
            vv = slice(hd * V_DIM, (hd + 1) * V_DIM)
            s = _dot_nt(q_ref[rows, qk], k_ref[rows, qk])
            o_ref[rows, vv] = _softmax_pv([s], [v_ref[rows, vv]]).astype(BF16)


def _cast_slab(n_steps, w):
    n_rows, n_cols = w.shape
    slab = n_rows // n_steps
    assert slab * n_steps == n_rows
    in_spec = lambda step_of: pl.BlockSpec((slab, n_cols), lambda *g: (step_of(*g), 0))
    return w, in_spec, jax.ShapeDtypeStruct(w.shape, BF16), in_spec


def _attn_lat_kernel(q_ref, k_ref, v_ref, kc_ref, vc_ref, *rest):
    if len(rest) == 3:
        w_ref, o_ref, wb_ref = rest
        wb_ref[...] = w_ref[...].astype(BF16)
    else:
        (o_ref,) = rest
    for hd in range(N_HEADS):
        qk = slice(hd * HEAD_PAD, (hd + 1) * HEAD_PAD)
        vv = slice(hd * V_DIM, (hd + 1) * V_DIM)
        q = q_ref[:, qk]
        s1 = _dot_nt(q, k_ref[:, qk])
        s2 = _dot_nt(q, kc_ref[:, qk])
        o_ref[:, vv] = _softmax_pv([s1, s2], [v_ref[:, vv], vc_ref[:, vv]]).astype(BF16)


def _attn_call(layer, q, k, v, kc, vc, cast_w=None):
    attn_ctx = pl.pallas_call(
        _attn_ctx_kernel,
        out_shape=jax.ShapeDtypeStruct((T_CTX, V_W), BF16),
        grid=(BATCH // CTX_SEQS_PER_STEP,),
        in_specs=[pl.BlockSpec((CTX_SEQS_PER_STEP * SEQ, QK_W), lambda b: (b, 0)),
                  pl.BlockSpec((CTX_SEQS_PER_STEP * SEQ, QK_W), lambda b: (b, 0)),
                  pl.BlockSpec((CTX_SEQS_PER_STEP * SEQ, V_W), lambda b: (b, 0))],
        out_specs=pl.BlockSpec((CTX_SEQS_PER_STEP * SEQ, V_W), lambda b: (b, 0)),
        compiler_params=_cparams("arbitrary"),
        name="attn_ctx",
    )(q, k, v)

    tq = LAT_Q_TILE if cast_w is None else LAT_Q_TILE // 2
    n_q = DEC_SEQ // tq
    lat0 = T_CTX // DEC_SEQ
    in_specs = [pl.BlockSpec((tq, QK_W), lambda b, i: (T_CTX // tq + b * n_q + i, 0)),
                pl.BlockSpec((DEC_SEQ, QK_W), lambda b, i: (lat0 + b, 0)),
                pl.BlockSpec((DEC_SEQ, V_W), lambda b, i: (lat0 + b, 0)),
                pl.BlockSpec((None, PAST_LEN, QK_W), lambda b, i: (layer, b, 0)),
                pl.BlockSpec((None, PAST_LEN, V_W), lambda b, i: (layer, b, 0))]
    out_shape = [jax.ShapeDtypeStruct((T_LAT, V_W), BF16)]
    out_specs = [pl.BlockSpec((tq, V_W), lambda b, i: (b * n_q + i, 0))]
    args = [q, k, v, kc, vc]
    if cast_w is not None:
        w, w_in_spec, wb_shape, wb_spec = _cast_slab(DEC_BATCH * n_q, cast_w)
        step_of = lambda b, i: b * n_q + i
        in_specs.append(w_in_spec(step_of))
        out_shape.append(wb_shape)
        out_specs.append(wb_spec(step_of))
        args.append(w)
    outs = pl.pallas_call(
        _attn_lat_kernel,
        out_shape=tuple(out_shape),
        grid=(DEC_BATCH, n_q),
        in_specs=in_specs,
        out_specs=tuple(out_specs),
        compiler_params=_cparams("arbitrary", "arbitrary"),
        name="attn_lat",
    )(*args)
    return (attn_ctx, *outs)


def _swiglu_hidden(h, wg_ref, wu_ref, a_buf):
    width = a_buf.shape[1]
    for c0 in range(0, width, FF_CHUNK):
        cols = slice(c0, min(c0 + FF_CHUNK, width))
        a_buf[:, cols] = (_silu(_dot(h, wg_ref[:, cols])) * _dot(h, wu_ref[:, cols])).astype(BF16)


def _mix_kernel(with_router, attn_c_ref, attn_l_ref, pool_ref, xc_ref, xl_ref, g1_ref, sh2_ref, sc2_ref, n2_ref, wo_ref,
                *rest):
    if with_router:
        wr_ref, ltri_ref, utri_ref, w_ref, x_out, xs_out, gs_out, route_out, meta_out, wb_ref = rest
    else:
        g2_ref, wg_ref, wu_ref, wd_ref, w_ref, x_out, wb_ref, a_buf = rest
    wb_ref[...] = w_ref[...].astype(BF16)
    is_ctx = pl.program_id(0) < T_CTX // TILE_MIX
    halves = []
    for r0 in range(0, TILE_MIX, TILE_MIX // 2):
        rows = slice(r0, r0 + TILE_MIX // 2)
        attn = jnp.where(is_ctx, attn_c_ref[rows, :], attn_l_ref[rows, :])
        y = _dot(attn, wo_ref[:V_W, :]) + _dot(pool_ref[rows, :], wo_ref[V_W:, :])
        x_new = jnp.where(is_ctx, xc_ref[rows, :], xl_ref[rows, :]) + g1_ref[...] * y
        x_out[rows, :] = x_new
        halves.append((_rms(x_new) * n2_ref[...] * (1.0 + sc2_ref[...]) + sh2_ref[...]).astype(BF16))
    h = jnp.concatenate(halves, axis=0)
    if not with_router:
        _swiglu_hidden(h, wg_ref, wu_ref, a_buf)
        x_out[...] += g2_ref[...] * _dot(a_buf[...], wd_ref[...])
        return

    logits = _dot(h, wr_ref[...])
    lane = lax.broadcasted_iota(jnp.int32, logits.shape, 1)
    neg = float(jnp.finfo(F32).min)
    lg = jnp.where(lane < N_EXPERTS, logits, neg)
    m1 = lg.max(axis=-1, keepdims=True)
    i1 = jnp.where(lg == m1, lane, LANES).min(axis=-1, keepdims=True)
    lg2 = jnp.where(lane == i1, neg, lg)
    m2 = lg2.max(axis=-1, keepdims=True)
    i2 = jnp.where(lg2 == m2, lane, LANES).min(axis=-1, keepdims=True)
    e = jnp.exp(m2 - m1)
    w1 = 1.0 / (1.0 + e)
    w2 = e / (1.0 + e)

    sel1 = lane == i1
    sel2 = lane == i2
    member = jnp.where(jnp.logical_or(sel1, sel2), 1.0, 0.0)
    rank = _dot(ltri_ref[...], member.astype(BF16))
    n_tok = member.sum(axis=0, keepdims=True)
    units = jnp.floor((n_tok + (SLOT_UNIT - 1)) * (1.0 / SLOT_UNIT))
    unit_off = _dot(jnp.broadcast_to(units, (8, LANES)).astype(BF16), utri_ref[...])[0:1, :]
    slot_of = SLOT_UNIT * unit_off + rank
    slot1 = jnp.where(sel1, slot_of, 0.0).sum(axis=-1, keepdims=True)
    slot2 = jnp.where(sel2, slot_of, 0.0).sum(axis=-1, keepdims=True)
    route = jnp.where(lane == 0, slot1, jnp.where(lane == 1, slot2,
                      jnp.where(lane == 2, w1, jnp.where(lane == 3, w2, 0.0))))
    route_out[...] = route
    sub = lax.broadcasted_iota(jnp.int32, (8, LANES), 0)
    meta_out[...] = jnp.where(sub == 0, units, jnp.where(sub == 1, unit_off, 0.0)).astype(jnp.int32)

    rt = route.T
    s1 = rt[0:1, :].astype(jnp.int32)
    s2 = rt[1:2, :].astype(jnp.int32)
    for r0 in range(0, SLOTS_PER_TILE, SLOT_BLOCK):
        blk = slice(r0, r0 + SLOT_BLOCK)
        srow = r0 + lax.broadcasted_iota(jnp.int32, (SLOT_BLOCK, TILE_MIX), 0)
        hit1 = srow == s1
        hit2 = srow == s2
        perm = (jnp.where(hit1, 1.0, 0.0) + jnp.where(hit2, 1.0, 0.0)).astype(BF16)
        xs_out[blk, :] = _dot(perm, h).astype(BF16)
        gate = (jnp.where(hit1, rt[2:3, :], 0.0) + jnp.where(hit2, rt[3:4, :], 0.0)).sum(axis=-1, keepdims=True)
        gs_out[blk, :] = jnp.broadcast_to(gate, (SLOT_BLOCK, LANES))


def _mod_row(i, tile):
    n_ctx = T_CTX // tile
    per_seq = DEC_SEQ // tile
    return jnp.where(i < n_ctx, 0, 1 + (i - n_ctx) // per_seq)


def _mix_call(attn_ctx, attn_lat, pooled, x_ctx, x_lat, ada, n2, w_out, cast_w, w_router=None, ffn_w=None):
    with_router = w_router is not None
    assert with_router != (ffn_w is not None)
    tm = TILE_MIX
    n_ctx = T_CTX // tm
    lat_tile0 = n_ctx if x_lat.shape[0] == T_ALL else 0
    row = lambda i: (i, 0)
    const2 = lambda i: (0, 0)
    mod = lambda k: pl.BlockSpec((None, 1, D_MODEL), lambda i: (_mod_row(i, tm), 0, k))
    ctx_blk = lambda i: (jnp.minimum(i, n_ctx - 1), 0)
    in_specs = [pl.BlockSpec((tm, V_W), ctx_blk),
                pl.BlockSpec((tm, V_W), lambda i: (jnp.maximum(i - n_ctx, 0), 0)),
                pl.BlockSpec((tm, POOL_W), row),
                pl.BlockSpec((tm, D_MODEL), ctx_blk),
                pl.BlockSpec((tm, D_MODEL), lambda i: (jnp.maximum(i - n_ctx, 0) + lat_tile0, 0)),
                mod(2), mod(3), mod(4),
                pl.BlockSpec((1, D_MODEL), const2), pl.BlockSpec((D_MODEL, D_MODEL), const2)]
    args = [attn_ctx, attn_lat, pooled, x_ctx, x_lat, ada, ada, ada, n2, w_out]
    if with_router:
        t_i = jnp.arange(tm)
        ltri = (t_i[None, :] < t_i[:, None]).astype(BF16)
        l_i = jnp.arange(LANES)
        utri = (l_i[:, None] < l_i[None, :]).astype(BF16)
        in_specs += [pl.BlockSpec((D_MODEL, LANES), const2), pl.BlockSpec((tm, tm), const2),
                     pl.BlockSpec((LANES, LANES), const2)]
        args += [w_router, ltri, utri]
        out_shape = [jax.ShapeDtypeStruct((T_ALL, D_MODEL), F32),
                     jax.ShapeDtypeStruct((N_ROUTE_TILES * SLOTS_PER_TILE, D_MODEL), BF16),
                     jax.ShapeDtypeStruct((N_ROUTE_TILES * SLOTS_PER_TILE, LANES), F32),
                     jax.ShapeDtypeStruct((T_ALL, LANES), F32),
                     jax.ShapeDtypeStruct((N_ROUTE_TILES, 8, LANES), jnp.int32)]
        out_specs = [pl.BlockSpec((tm, D_MODEL), row), pl.BlockSpec((SLOTS_PER_TILE, D_MODEL), row),
                     pl.BlockSpec((SLOTS_PER_TILE, LANES), row), pl.BlockSpec((tm, LANES), row),
                     pl.BlockSpec((None, 8, LANES), lambda i: (i, 0, 0))]
        scratch = []
    else:
        resident = lambda shape: pl.BlockSpec(shape, const2, pipeline_mode=pl.Buffered(1))
        in_specs += [mod(5), resident((D_MODEL, D_FF)), resident((D_MODEL, D_FF)), resident((D_FF, D_MODEL))]
        args += [ada, *ffn_w]
        out_shape = [jax.ShapeDtypeStruct((T_ALL, D_MODEL), F32)]
        out_specs = [pl.BlockSpec((tm, D_MODEL), row)]
        scratch = [pltpu.VMEM((tm, D_FF), BF16)]
    w, w_spec, wb_shape, wb_spec = _cast_slab(T_ALL // tm, cast_w)
    in_specs.append(w_spec(lambda i: i))
    args.append(w)
    out_shape.append(wb_shape)
    out_specs.append(wb_spec(lambda i: i))
    return pl.pallas_call(
        functools.partial(_mix_kernel, with_router),
        out_shape=tuple(out_shape),
        grid=(T_ALL // tm,),
        in_specs=in_specs,
        out_specs=tuple(out_specs),
        scratch_shapes=scratch,
        compiler_params=_cparams("arbitrary"),
        name="mix_router" if with_router else "mix_ffn",
    )(*args)


def _route_tables(meta):
    units = meta[:, 0, :N_EXPERTS]
    offs = meta[:, 1, :N_EXPERTS]
    cum = jnp.cumsum(units, axis=0)
    total = cum[-1]
    tiles_e = (total + GEMM_UNITS - 1) // GEMM_UNITS
    tile_end = jnp.cumsum(tiles_e)
    n_act = tile_end[-1]
    m = jnp.arange(N_GEMM_TILES)
    m_eff = jnp.minimum(m, jnp.maximum(n_act - 1, 0))
    expert = jnp.minimum(jnp.sum(tile_end[None, :] <= m_eff[:, None], axis=1), N_EXPERTS - 1)
    is_e = (expert[:, None] == jnp.arange(N_EXPERTS)[None, :]).astype(jnp.int32)
    pick = lambda per_expert: jnp.sum(is_e * per_expert[None, :], axis=1)
    first_q = (m_eff - pick(tile_end - tiles_e)) * GEMM_UNITS
    count = jnp.where(m < n_act, jnp.clip(pick(total) - first_q, 0, GEMM_UNITS), 0)
    q = first_q[:, None] + jnp.arange(GEMM_UNITS)[None, :]
    rows_of = lambda table: jnp.sum(is_e[:, None, :] * table[None, :, :], axis=2)
    cum_e, units_e, offs_e = rows_of(cum), rows_of(units), rows_of(offs)
    src_tile = jnp.minimum(jnp.sum(cum_e[:, None, :] <= q[:, :, None], axis=2), N_ROUTE_TILES - 1)
    is_t = (src_tile[:, :, None] == jnp.arange(N_ROUTE_TILES)[None, None, :]).astype(jnp.int32)
    at_tile = lambda per_tile: jnp.sum(is_t * per_tile[:, None, :], axis=2)
    uid = src_tile * UNITS_PER_TILE + at_tile(offs_e) + (q - at_tile(cum_e - units_e))
    uid = jnp.where(jnp.arange(GEMM_UNITS)[None, :] < count[:, None], uid, ZERO_UNIT)
    i32 = jnp.int32
    return expert.astype(i32), count.astype(i32), uid.reshape(-1).astype(i32), n_act.reshape(1).astype(i32)


def _gemm_kernel(em_ref, cnt_ref, ul_ref, nact_ref, xs_hbm, gs_hbm, wg_ref, wu_ref, wd_ref, ys_hbm,
                 xbuf, gbuf, obuf, acc_ref, a_buf, sem_in, sem_out):
    del em_ref
    m = pl.program_id(0)
    j = pl.program_id(1)
    last_j = pl.num_programs(1) - 1
    n_act = nact_ref[0]
    active = m < n_act
    slot = m % 2

    def rows(r):
        return pl.ds(r * SLOT_UNIT if isinstance(r, int) else pl.multiple_of(r * SLOT_UNIT, SLOT_UNIT), SLOT_UNIT)

    def in_copies(mm, sl, r):
        uid = ul_ref[mm * GEMM_UNITS + r]
        return (pltpu.make_async_copy(xs_hbm.at[uid], xbuf.at[sl, rows(r)], sem_in.at[sl]),
                pltpu.make_async_copy(gs_hbm.at[uid], gbuf.at[sl, rows(r)], sem_in.at[sl]))

    def out_copy(mm, r):
        uid = ul_ref[mm * GEMM_UNITS + r]
        return pltpu.make_async_copy(obuf.at[rows(r)], ys_hbm.at[uid], sem_out.at[0])

    def for_slot(sl_dyn, fn):
        if isinstance(sl_dyn, int):
            fn(sl_dyn)
            return
        for sl in range(2):
            pl.when(sl_dyn == sl)(functools.partial(fn, sl))

    def start_in(mm, sl_dyn):
        def issue(sl):
            for r in range(GEMM_UNITS):
                for cp in in_copies(mm, sl, r):
                    cp.start()

        for_slot(sl_dyn, issue)

    def wait_in(mm, sl_dyn):
        def wait(sl):
            for r in range(GEMM_UNITS):
                for cp in in_copies(mm, sl, r):
                    cp.wait()

        for_slot(sl_dyn, wait)

    def start_out(mm):
        n = cnt_ref[mm]
        for r in range(GEMM_UNITS):
            pl.when(r < n)(lambda r=r: out_copy(mm, r).start())

    def wait_out(mm):
        def body(r, carry):
            out_copy(mm, r).wait()
            return carry

        lax.fori_loop(0, cnt_ref[mm], body, 0)

    @pl.when(jnp.logical_and(active, j == 0))
    def _():
        @pl.when(m == 0)
        def _():
            start_in(0, 0)

        wait_in(m, slot)

        @pl.when(m + 1 < n_act)
        def _():
            start_in(m + 1, 1 - slot)

    @pl.when(jnp.logical_and(jnp.logical_and(active, j == last_j), m > 0))
    def _():
        wait_out(m - 1)

    n_valid = cnt_ref[m]
    for quarter in range(1, GEMM_QUARTERS + 1):
        n_rows = quarter * GEMM_ROWS // GEMM_QUARTERS
        lo_units = (quarter - 1) * GEMM_UNITS // GEMM_QUARTERS
        hi_units = quarter * GEMM_UNITS // GEMM_QUARTERS

        @pl.when(jnp.logical_and(active, jnp.logical_and(n_valid > lo_units, n_valid <= hi_units)))
        def _(n_rows=n_rows):
            _swiglu_hidden(xbuf[slot, :n_rows, :], wg_ref, wu_ref, a_buf.at[:n_rows, :])
            part = _dot(a_buf[:n_rows, :], wd_ref[...])

            @pl.when(j == 0)
            def _():
                acc_ref[:n_rows, :] = part

            @pl.when(jnp.logical_and(j > 0, j < last_j))
            def _():
                acc_ref[:n_rows, :] += part

            @pl.when(j == last_j)
            def _():
                gate = gbuf[slot, :n_rows, :]
                for cb in range(D_MODEL // LANES):
                    cols = slice(cb * LANES, (cb + 1) * LANES)
                    obuf[:n_rows, cols] = ((acc_ref[:n_rows, cols] + part[:, cols]) * gate).astype(BF16)

    @pl.when(jnp.logical_and(active, j == last_j))
    def _():
        start_out(m)

        @pl.when(m == n_act - 1)
        def _():
            wait_out(m)


def _gemm_call(expert, count, uids, n_act, xs, gs, wg, wu, wd):
    tf = TILE_FE
    n_j = D_FF_EXPERT // tf
    assert n_j >= 2

    def w_col(m, j, em, cnt, ul, nact):
        return (em[m], 0, jnp.where(m < nact[0], j, n_j - 1))

    def w_row(m, j, em, cnt, ul, nact):
        return (em[m], jnp.where(m < nact[0], j, n_j - 1), 0)

    grid_spec = pltpu.PrefetchScalarGridSpec(
        num_scalar_prefetch=4,
        grid=(N_GEMM_TILES, n_j),
        in_specs=[pl.BlockSpec(memory_space=pl.ANY),
                  pl.BlockSpec(memory_space=pl.ANY),
                  pl.BlockSpec((None, D_MODEL, tf), w_col),
                  pl.BlockSpec((None, D_MODEL, tf), w_col),
                  pl.BlockSpec((None, tf, D_MODEL), w_row)],
        out_specs=pl.BlockSpec(memory_space=pl.ANY),
        scratch_shapes=[pltpu.VMEM((2, GEMM_ROWS, D_MODEL), BF16),
                        pltpu.VMEM((2, GEMM_ROWS, LANES), F32),
                        pltpu.VMEM((GEMM_ROWS, D_MODEL), BF16),
                        pltpu.VMEM((GEMM_ROWS, D_MODEL), F32),
                        pltpu.VMEM((GEMM_ROWS, tf), BF16),
                        pltpu.SemaphoreType.DMA((2,)),
                        pltpu.SemaphoreType.DMA((1,))])
    return pl.pallas_call(
        _gemm_kernel,
        out_shape=jax.ShapeDtypeStruct((N_UNITS, SLOT_UNIT, D_MODEL), BF16),
        grid_spec=grid_spec,
        input_output_aliases={4: 0},
        compiler_params=_cparams("arbitrary", "arbitrary"),
        name="moe_experts",
    )(expert, count, uids, n_act,
      xs.reshape(N_UNITS, SLOT_UNIT, D_MODEL), gs.reshape(N_UNITS, SLOT_UNIT, LANES), wg, wu, wd)


def _combine_kernel(ys_ref, route_ref, x_ref, g2_ref, fg_ref, o_ref):
    route = route_ref[...]
    s1 = route[:, 0:1].astype(jnp.int32)
    s2 = route[:, 1:2].astype(jnp.int32)
    scol = lax.broadcasted_iota(jnp.int32, (TILE_MIX, SLOTS_PER_TILE), 1)
    unperm = (jnp.where(scol == s1, 1.0, 0.0) + jnp.where(scol == s2, 1.0, 0.0)).astype(BF16)
    y = _dot(unperm, ys_ref[...])
    o_ref[...] = _rms(x_ref[...] + g2_ref[...] * y) * fg_ref[...]


def _combine_call(ys, route, x, ada, final_g, tile0, n_tok):
    tm = TILE_MIX
    return pl.pallas_call(
        _combine_kernel,
        out_shape=jax.ShapeDtypeStruct((n_tok, D_MODEL), F32),
        grid=(n_tok // tm,),
        in_specs=[pl.BlockSpec((SLOTS_PER_TILE, D_MODEL), lambda i: (i + tile0, 0)),
                  pl.BlockSpec((tm, LANES), lambda i: (i + tile0, 0)),
                  pl.BlockSpec((tm, D_MODEL), lambda i: (i + tile0, 0)),
                  pl.BlockSpec((None, 1, D_MODEL), lambda i: (_mod_row(i + tile0, tm), 0, 5)),
                  pl.BlockSpec((1, D_MODEL), lambda i: (0, 0))],
        out_specs=pl.BlockSpec((tm, D_MODEL), lambda i: (i, 0)),
        compiler_params=_cparams("arbitrary"),
        name="moe_combine",
    )(ys.reshape(N_UNITS * SLOT_UNIT, D_MODEL), route, x, ada, final_g)


def _rot_cols(w):
    q = QK_ROPE // 4
    src = jnp.arange(QK_ROPE)[:, None]
    dst = jnp.arange(QK_ROPE)[None, :]
    first_half = (dst // q) % 2 == 0
    rot = jnp.where(first_half, -(src == dst + q).astype(F32), (src == dst - q).astype(F32))
    return jnp.matmul(w, rot.astype(w.dtype))


def _rope_tables():
    t = jnp.arange(DEC_SEQ)
    rows = (t // GRID_W).astype(F32)
    cols = (t % GRID_W).astype(F32)
    half = QK_ROPE // 2
    freqs = ROPE_BASE ** (-jnp.arange(0, half, 2, dtype=F32) / half)
    ang_r = rows[:, None] * freqs
    ang_c = cols[:, None] * freqs
    ang = jnp.concatenate([ang_r, ang_r, ang_c, ang_c], axis=-1)
    cos, sin = jnp.cos(ang), jnp.sin(ang)
    one, zero = jnp.ones_like(cos), jnp.zeros_like(cos)
    tq = jnp.stack([jnp.concatenate([one, zero], -1), jnp.concatenate([cos, sin], -1)])
    tc = jnp.stack([jnp.concatenate([one, one], -1), jnp.concatenate([cos, cos], -1)])
    ts = jnp.stack([jnp.concatenate([zero, zero], -1), jnp.concatenate([sin, sin], -1)])
    return tq, tc, ts


def _layer_weights(w_in, w_uq, w_ukv):
    wi = w_in.astype(BF16)
    w_kpe = wi[..., OFF_U:OFF_U + QK_ROPE]
    w_rot = _rot_cols(w_kpe)
    w_in_r = jnp.concatenate([wi[..., :OFF_U], wi[..., OFF_U + QK_ROPE:], w_kpe, w_kpe, w_rot, w_rot], axis=-1)
    wq = w_uq.astype(BF16).reshape(DEPTH, Q_LORA, N_HEADS, QK_NOPE + QK_ROPE)
    w_uq_r = jnp.concatenate([wq, _rot_cols(wq[..., QK_NOPE:])], axis=-1).reshape(DEPTH, Q_LORA, QK_W)
    kv = w_ukv.astype(BF16).reshape(DEPTH, KV_LORA, N_HEADS, QK_NOPE + V_DIM)
    w_uk = kv[..., :QK_NOPE].reshape(DEPTH, KV_LORA, N_HEADS * QK_NOPE)
    w_uv = kv[..., QK_NOPE:].reshape(DEPTH, KV_LORA, V_W)
    return w_in_r, w_uq_r, w_uk, w_uv


def kernel(x_prompt, x_sample, c, cache_ckv, cache_kpe, c_ctx, norm1_g, norm2_g, w_ada, b_ada, w_in, q_norm_g,
           kv_norm_g, w_uq, w_ukv, w_pool, pool_scale, w_out, ffn_w_gate, ffn_w_up, ffn_w_down, moe_w_router,
           moe_w_gate, moe_w_up, moe_w_down, final_norm_g):
    x_ctx = x_prompt.reshape(T_CTX, D_MODEL)
    x_lat = x_sample.reshape(T_LAT, D_MODEL)
    cond = jnp.concatenate([c_ctx[None, :], c, jnp.zeros((COND_ROWS - 1 - DEC_BATCH, D_MODEL), F32)], axis=0)
    ada_all = _ada_call(cond, w_ada, b_ada)
    tq, tc, ts = _rope_tables()

    w_in_all, w_uq_all, w_uk_all, w_uv_all = _layer_weights(w_in, w_uq, w_ukv)
    dup = jnp.concatenate([jnp.eye(QK_ROPE, dtype=BF16)] * 2, axis=1)
    kc_all, vc_all = _cache_call(cache_ckv, cache_kpe, w_uk_all, w_uv_all, dup)

    assert DEPTH == 2
    moe_gate_2d = moe_w_gate[0].reshape(N_EXPERTS * D_MODEL, D_FF_EXPERT)
    moe_up_2d = moe_w_up[0].reshape(N_EXPERTS * D_MODEL, D_FF_EXPERT)
    moe_down_2d = moe_w_down[0].reshape(N_EXPERTS * D_FF_EXPERT, D_MODEL)
    new_ckv = new_kpe = None
    for l in range(DEPTH):
        ada = ada_all[l].reshape(COND_ROWS, 1, 6 * D_MODEL)
        cast_ws = (ffn_w_gate[0], ffn_w_up[0], ffn_w_down[0],
                   w_out.reshape(DEPTH * (V_W + POOL_W), D_MODEL)) if l == 0 else ()
        q, k, v, new_ckv, new_kpe, pooled, *cast = _in_call(
            l, x_ctx, x_lat, ada, norm1_g[l][None, :], w_in_all, q_norm_g[l][None, :], kv_norm_g[l][None, :],
            w_uq_all, w_uk_all, w_uv_all, tq, tc, ts, w_pool[l].astype(BF16), pool_scale[l][None, :], new_ckv,
            new_kpe, cast_ws)
        if l == 0:
            ffn_w, w_out_b = cast[:3], cast[3].reshape(DEPTH, V_W + POOL_W, D_MODEL)
        j = l // 2
        if l % 2 == 0:
            attn_c, attn_l, moe_wd = _attn_call(l, q, k, v, kc_all, vc_all, cast_w=moe_down_2d)
            x, moe_wg = _mix_call(attn_c, attn_l, pooled, x_ctx, x_lat, ada, norm2_g[l][None, :],
                                  w_out_b[l], moe_gate_2d, ffn_w=ffn_w)
            x_ctx = x_lat = x
        else:
            attn_c, attn_l = _attn_call(l, q, k, v, kc_all, vc_all)
            w_r = jnp.pad(moe_w_router[j], ((0, 0), (0, LANES - N_EXPERTS))).astype(BF16)
            x, xs, gs, route, meta, moe_wu = _mix_call(attn_c, attn_l, pooled, x_ctx, x_lat, ada,
                                                       norm2_g[l][None, :], w_out_b[l], moe_up_2d,
                                                       w_router=w_r)
            expert, count, uids, n_act = _route_tables(meta)
            ys = _gemm_call(expert, count, uids, n_act, xs, gs,
                            moe_wg.reshape(N_EXPERTS, D_MODEL, D_FF_EXPERT),
                            moe_wu.reshape(N_EXPERTS, D_MODEL, D_FF_EXPERT),
                            moe_wd.reshape(N_EXPERTS, D_FF_EXPERT, D_MODEL))
            fg = final_norm_g[None, :]
            y_prompt = _combine_call(ys, route, x, ada, fg, 0, T_CTX).reshape(BATCH, SEQ, D_MODEL)
            y_sample = _combine_call(ys, route, x, ada, fg, T_CTX // TILE_MIX, T_LAT).reshape(
                DEC_BATCH, DEC_SEQ, D_MODEL)

    return y_prompt, y_sample, new_ckv, jnp.swapaxes(new_kpe, -1, -2)
```

```python
import functools

import jax
import jax.numpy as jnp
from jax import lax
from jax.experimental import pallas as pl
from jax.experimental.pallas import tpu as pltpu

D_MODEL = 1024
BATCH = 32
SEQ = 256
DEPTH = 2
DEC_BATCH = 8
DEC_SEQ = 1024
PAST_LEN = 512
GRID_W = 64
N_HEADS = 4
QK_NOPE = 128
QK_ROPE = 64
V_DIM = 128
Q_LORA = 384
KV_LORA = 256
POOL_W = 512
POOL_GROUPS = 4
POOL_WINDOWS = (2, 4, 8, 16)
POOL_CH = POOL_W // POOL_GROUPS
D_FF = 2816
N_EXPERTS = 8
D_FF_EXPERT = 3584
ROPE_BASE = 10000.0
EPS = 1e-6

T_CTX = BATCH * SEQ
T_LAT = DEC_BATCH * DEC_SEQ
T_ALL = T_CTX + T_LAT

LANES = 128
HEAD_PAD = 256
QK_W = N_HEADS * HEAD_PAD
V_W = N_HEADS * V_DIM
OFF_CKV = Q_LORA
OFF_U = Q_LORA + KV_LORA
OFF_KA = OFF_U + POOL_W
OFF_KB = OFF_KA + LANES
N_PROJ = OFF_KB + LANES

TILE_IN = 1024
CHUNK = 256
HALO = 8
REGION = CHUNK + 2 * HALO
CTX_SEQS_PER_STEP = 4
LAT_Q_TILE = 1024
TILE_MIX = 512
TOP_K = 2
SLOT_UNIT = 16
UNITS_PER_TILE = TILE_MIX * TOP_K // SLOT_UNIT + N_EXPERTS
SLOTS_PER_TILE = UNITS_PER_TILE * SLOT_UNIT
SLOT_BLOCK = SLOTS_PER_TILE // 4
N_ROUTE_TILES = T_ALL // TILE_MIX
N_UNITS = N_ROUTE_TILES * UNITS_PER_TILE
ZERO_UNIT = UNITS_PER_TILE - 1
GEMM_UNITS = 64
GEMM_ROWS = GEMM_UNITS * SLOT_UNIT
GEMM_QUARTERS = 4
N_GEMM_TILES = -(-N_UNITS // GEMM_UNITS) + N_EXPERTS
TILE_FE = D_FF_EXPERT // 2
FF_CHUNK = 256
COND_ROWS = 16
VMEM_LIMIT = 56 * 1024 * 1024

F32 = jnp.float32
BF16 = jnp.bfloat16


def _rms(x):
    return x * lax.rsqrt(jnp.mean(x * x, axis=-1, keepdims=True) + EPS)


def _dot(a, b):
    return jnp.dot(a, b, preferred_element_type=F32)


def _dot_nt(a, b):
    return lax.dot_general(a, b, (((1,), (1,)), ((), ())), preferred_element_type=F32)


def _silu(x):
    return x * (1.0 / (1.0 + jnp.exp(-x)))


def _cparams(*sem):
    return pltpu.CompilerParams(dimension_semantics=sem, vmem_limit_bytes=VMEM_LIMIT)


def _ada_kernel(cond_ref, w_ref, b_ref, o_ref):
    k = pl.program_id(1)
    part = _dot(_silu(cond_ref[...]).astype(BF16), w_ref[...].astype(BF16))

    @pl.when(k == 0)
    def _():
        o_ref[...] = part + b_ref[...]

    @pl.when(k > 0)
    def _():
        o_ref[...] += part


def _ada_call(cond, w_ada, b_ada):
    n_blk = 4
    kb = D_MODEL // n_blk
    return pl.pallas_call(
        _ada_kernel,
        out_shape=jax.ShapeDtypeStruct((DEPTH, COND_ROWS, 6 * D_MODEL), F32),
        grid=(DEPTH, n_blk),
        in_specs=[
            pl.BlockSpec((COND_ROWS, kb), lambda l, k: (0, k)),
            pl.BlockSpec((None, kb, 6 * D_MODEL), lambda l, k: (l, k, 0)),
            pl.BlockSpec((None, 1, 6 * D_MODEL), lambda l, k: (l, 0, 0)),
        ],
        out_specs=pl.BlockSpec((None, COND_ROWS, 6 * D_MODEL), lambda l, k: (l, 0, 0)),
        compiler_params=_cparams("arbitrary", "arbitrary"),
        name="ada_params",
    )(cond, w_ada, b_ada.reshape(DEPTH, 1, 6 * D_MODEL))


def _cache_kernel(ckv_ref, kpe_ref, wuk_ref, wuv_ref, dup_ref, k_ref, v_ref):
    ckv = ckv_ref[...].astype(BF16)
    knope = _dot(ckv, wuk_ref[...])
    v_ref[...] = _dot(ckv, wuv_ref[...]).astype(BF16)
    kpe = kpe_ref[...].T
    kdup = _dot(kpe.astype(BF16), dup_ref[...]).astype(BF16)
    for h in range(N_HEADS):
        k_ref[:, h * HEAD_PAD:h * HEAD_PAD + QK_NOPE] = knope[:, h * QK_NOPE:(h + 1) * QK_NOPE].astype(BF16)
        k_ref[:, h * HEAD_PAD + QK_NOPE:(h + 1) * HEAD_PAD] = kdup


def _cache_call(cache_ckv, cache_kpe, w_uk, w_uv, dup):
    n_tok = DEC_BATCH * PAST_LEN
    return pl.pallas_call(
        _cache_kernel,
        out_shape=(jax.ShapeDtypeStruct((DEPTH, n_tok, QK_W), BF16),
                   jax.ShapeDtypeStruct((DEPTH, n_tok, V_W), BF16)),
        grid=(DEPTH, DEC_BATCH),
        in_specs=[
            pl.BlockSpec((None, None, PAST_LEN, KV_LORA), lambda l, b: (b, l, 0, 0)),
            pl.BlockSpec((None, None, QK_ROPE, PAST_LEN), lambda l, b: (b, l, 0, 0)),
            pl.BlockSpec((None, KV_LORA, N_HEADS * QK_NOPE), lambda l, b: (l, 0, 0)),
            pl.BlockSpec((None, KV_LORA, V_W), lambda l, b: (l, 0, 0)),
            pl.BlockSpec((QK_ROPE, LANES), lambda l, b: (0, 0)),
        ],
        out_specs=(pl.BlockSpec((None, PAST_LEN, QK_W), lambda l, b: (l, b, 0)),
                   pl.BlockSpec((None, PAST_LEN, V_W), lambda l, b: (l, b, 0))),
        compiler_params=_cparams("arbitrary", "arbitrary"),
        name="cache_kv",
    )(cache_ckv, jnp.swapaxes(cache_kpe, -1, -2), w_uk, w_uv, dup)


def _in_kernel(n_prev, n_cast, xc_ref, xl_ref, sh_ref, sc_ref, g_ref, win_ref, qg_ref, kvg_ref, wuq_ref, wuk_ref,
               wuv_ref, tq_ref, tc_ref, ts_ref, wpool_ref, ps_ref, *rest):
    if n_prev:
        prev_ckv_ref, prev_kpe_ref, *rest = rest
    cast_in, rest = rest[:n_cast], rest[n_cast:]
    q_out, k_out, v_out, ckv_out, kpe_out, pool_out, *rest = rest
    cast_out, (u_scr,) = rest[:n_cast], rest[n_cast:]
    for w_ref, wb_ref in zip(cast_in, cast_out):
        wb_ref[...] = w_ref[...].astype(BF16)
    is_ctx = pl.program_id(0) >= (T_ALL - T_CTX) // TILE_IN
    if n_prev:
        ckv_out[:, :n_prev] = prev_ckv_ref[...]
        kpe_out[:, :n_prev] = prev_kpe_ref[...]
    shift = sh_ref[...]
    scale1 = 1.0 + sc_ref[...]
    qk_scale = (QK_NOPE + QK_ROPE) ** -0.5
    n_chunks = TILE_IN // CHUNK
    zeros = jnp.zeros((HALO, POOL_W), F32)
    seq_len = jnp.where(is_ctx, SEQ, DEC_SEQ)
    row = lax.broadcasted_iota(jnp.int32, (CHUNK, POOL_CH), 0)

    def pool_chunk(c):
        base = c * REGION + HALO
        rows = slice(c * CHUNK, (c + 1) * CHUNK)
        t = row + jnp.where(is_ctx, 0, c * CHUNK)
        for g, w in enumerate(POOL_WINDOWS):
            cols = slice(g * POOL_CH, (g + 1) * POOL_CH)
            acc = u_scr[base - w // 2:base - w // 2 + CHUNK, cols]
            for j in range(-w // 2 + 1, w // 2):
                acc = acc + u_scr[base + j:base + j + CHUNK, cols]
            cnt = jnp.minimum(t + w // 2, seq_len) - jnp.maximum(t - w // 2, 0)
            pooled = acc / cnt.astype(F32) - u_scr[base:base + CHUNK, cols]
            lin = _dot(pooled.astype(BF16), wpool_ref[g]) * ps_ref[:, cols]
            pool_out[rows, cols] = lin.astype(BF16)

    for c in range(n_chunks):
        r0 = c * CHUNK
        rows = slice(r0, r0 + CHUNK)
        x = jnp.where(is_ctx, xc_ref[rows, :], xl_ref[rows, :])
        h = (_rms(x) * g_ref[...] * scale1 + shift).astype(BF16)
        proj = _dot(h, win_ref[...])

        qn = (_rms(proj[:, :Q_LORA]) * qg_ref[...]).astype(BF16)
        q = _dot(qn, wuq_ref[...]) * qk_scale
        tq = tq_ref[rows, :]
        for hd in range(N_HEADS):
            lo = hd * HEAD_PAD
            q_out[rows, lo:lo + QK_NOPE] = q[:, lo:lo + QK_NOPE].astype(BF16)
            q_out[rows, lo + QK_NOPE:lo + HEAD_PAD] = (q[:, lo + QK_NOPE:lo + HEAD_PAD] * tq).astype(BF16)

        ckv = _rms(proj[:, OFF_CKV:OFF_U]) * kvg_ref[...]
        ckv_out[c, n_prev] = ckv
        ckv_b = ckv.astype(BF16)
        knope = _dot(ckv_b, wuk_ref[...])
        v_out[rows, :] = _dot(ckv_b, wuv_ref[...]).astype(BF16)

        k_a = proj[:, OFF_KA:OFF_KB]
        k_b = proj[:, OFF_KB:N_PROJ]
        kpe_out[c, n_prev] = k_a.T[:QK_ROPE, :]
        kr = (k_a * tc_ref[rows, :] + k_b * ts_ref[rows, :]).astype(BF16)
        for hd in range(N_HEADS):
            lo = hd * HEAD_PAD
            k_out[rows, lo:lo + QK_NOPE] = knope[:, hd * QK_NOPE:(hd + 1) * QK_NOPE].astype(BF16)
            k_out[rows, lo + QK_NOPE:lo + HEAD_PAD] = kr

        u = proj[:, OFF_U:OFF_KA]
        base = c * REGION + HALO
        u_scr[base:base + CHUNK, :] = u
        if c == 0:
            u_scr[0:HALO, :] = zeros
        else:
            u_scr[base - 2 * HALO:base - HALO, :] = jnp.where(is_ctx, zeros, u[:HALO, :])
        if c == n_chunks - 1:
            u_scr[base + CHUNK:base + CHUNK + HALO, :] = zeros
        else:
            u_scr[base + CHUNK + HALO:base + CHUNK + 2 * HALO, :] = jnp.where(is_ctx, zeros, u[CHUNK - HALO:, :])
        if c > 0:
            pool_chunk(c - 1)
    pool_chunk(n_chunks - 1)


def _in_call(layer, x_ctx, x_lat, ada, g1, w_in, qg, kvg, w_uq, w_uk, w_uv, tq, tc, ts, w_pool, pscale, prev_ckv, prev_kpe,
             cast_ws=()):
    assert CHUNK == SEQ
    n_tiles = T_ALL // TILE_IN
    n_ctx = T_CTX // TILE_IN
    n_prev = 0 if prev_ckv is None else prev_ckv.shape[1]
    seqs = TILE_IN // SEQ
    lat_tile0 = n_ctx if x_lat.shape[0] == T_ALL else 0

    def tile(i):
        return jnp.where(i < n_tiles - n_ctx, i + n_ctx, i - (n_tiles - n_ctx))

    def cond_row(i):
        return jnp.maximum(tile(i) - n_ctx + 1, 0)

    def tab(i):
        return (jnp.minimum(jnp.maximum(tile(i) - n_ctx + 1, 0), 1), 0, 0)

    rows = lambda i: (tile(i), 0)
    ctx_blk4 = lambda i: (jnp.where(tile(i) < n_ctx, tile(i), 0), 0, 0, 0)
    const2 = lambda i: (0, 0)
    of_layer = lambda i: (layer, 0, 0)
    prev_specs = [pl.BlockSpec((seqs, n_prev, SEQ, KV_LORA), ctx_blk4),
                  pl.BlockSpec((seqs, n_prev, QK_ROPE, SEQ), ctx_blk4)] if n_prev else []
    prev_args = [prev_ckv, prev_kpe] if n_prev else []
    casts = [_cast_slab(n_tiles, w) for w in cast_ws]
    step = lambda i: i
    return pl.pallas_call(
        functools.partial(_in_kernel, n_prev, len(casts)),
        out_shape=(jax.ShapeDtypeStruct((T_ALL, QK_W), BF16),
                   jax.ShapeDtypeStruct((T_ALL, QK_W), BF16),
                   jax.ShapeDtypeStruct((T_ALL, V_W), BF16),
                   jax.ShapeDtypeStruct((BATCH, n_prev + 1, SEQ, KV_LORA), F32),
                   jax.ShapeDtypeStruct((BATCH, n_prev + 1, QK_ROPE, SEQ), F32),
                   jax.ShapeDtypeStruct((T_ALL, POOL_W), BF16)) + tuple(c[2] for c in casts),
        grid=(n_tiles,),
        in_specs=[
            pl.BlockSpec((TILE_IN, D_MODEL), lambda i: (jnp.minimum(tile(i), n_ctx - 1), 0)),
            pl.BlockSpec((TILE_IN, D_MODEL), lambda i: (jnp.maximum(tile(i) - n_ctx, 0) + lat_tile0, 0)),
            pl.BlockSpec((None, 1, D_MODEL), lambda i: (cond_row(i), 0, 0)),
            pl.BlockSpec((None, 1, D_MODEL), lambda i: (cond_row(i), 0, 1)),
            pl.BlockSpec((1, D_MODEL), const2),
            pl.BlockSpec((None, D_MODEL, N_PROJ), of_layer),
            pl.BlockSpec((1, Q_LORA), const2),
            pl.BlockSpec((1, KV_LORA), const2),
            pl.BlockSpec((None, Q_LORA, QK_W), of_layer),
            pl.BlockSpec((None, KV_LORA, N_HEADS * QK_NOPE), of_layer),
            pl.BlockSpec((None, KV_LORA, V_W), of_layer),
            pl.BlockSpec((None, TILE_IN, LANES), tab),
            pl.BlockSpec((None, TILE_IN, LANES), tab),
            pl.BlockSpec((None, TILE_IN, LANES), tab),
            pl.BlockSpec((POOL_GROUPS, POOL_CH, POOL_CH), lambda i: (0, 0, 0)),
            pl.BlockSpec((1, POOL_W), const2),
        ] + prev_specs + [c[1](step) for c in casts],
        out_specs=(pl.BlockSpec((TILE_IN, QK_W), rows),
                   pl.BlockSpec((TILE_IN, QK_W), rows),
                   pl.BlockSpec((TILE_IN, V_W), rows),
                   pl.BlockSpec((seqs, n_prev + 1, SEQ, KV_LORA), ctx_blk4),
                   pl.BlockSpec((seqs, n_prev + 1, QK_ROPE, SEQ), ctx_blk4),
                   pl.BlockSpec((TILE_IN, POOL_W), rows)) + tuple(c[3](step) for c in casts),
        scratch_shapes=[pltpu.VMEM((TILE_IN // CHUNK * REGION, POOL_W), F32)],
        compiler_params=_cparams("arbitrary"),
        name="in_proj",
    )(x_ctx, x_lat, ada, ada, g1, w_in, qg, kvg, w_uq, w_uk, w_uv, tq, tc, ts, w_pool, pscale, *prev_args,
      *(c[0] for c in casts))


def _softmax_pv(scores, values):
    m = scores[0].max(axis=-1, keepdims=True)
    for s in scores[1:]:
        m = jnp.maximum(m, s.max(axis=-1, keepdims=True))
    den = None
    out = None
    for s, v in zip(scores, values):
        p = jnp.exp(s - m)
        d = p.sum(axis=-1, keepdims=True)
        o = _dot(p.astype(BF16), v)
        den = d if den is None else den + d
        out = o if out is None else out + o
    return out / den


def _attn_ctx_kernel(q_ref, k_ref, v_ref, o_ref):
    for b in range(CTX_SEQS_PER_STEP):
        rows = slice(b * SEQ, (b + 1) * SEQ)
        for hd in range(N_HEADS):
            qk = slice(hd * HEAD_PAD, (hd + 1) * HEAD_PAD)
            vv = slice(hd * V_DIM, (hd + 1) * V_DIM)
            s = _dot_nt(q_ref[rows, qk], k_ref[rows, qk])
            o_ref[rows, vv] = _softmax_pv([s], [v_ref[rows, vv]]).astype(BF16)


def _cast_slab(n_steps, w):
    n_rows, n_cols = w.shape
    slab = n_rows // n_steps
    assert slab * n_steps == n_rows
    in_spec = lambda step_of: pl.BlockSpec((slab, n_cols), lambda *g: (step_of(*g), 0))
    return w, in_spec, jax.ShapeDtypeStruct(w.shape, BF16), in_spec


def _attn_lat_kernel(q_ref, k_ref, v_ref, kc_ref, vc_ref, *rest):
    if len(rest) == 3:
        w_ref, o_ref, wb_ref = rest
        wb_ref[...] = w_ref[...].astype(BF16)
    else:
        (o_ref,) = rest
    for hd in range(N_HEADS):
        qk = slice(hd * HEAD_PAD, (hd + 1) * HEAD_PAD)
        vv = slice(hd * V_DIM, (hd + 1) * V_DIM)
        q = q_ref[:, qk]
        s1 = _dot_nt(q, k_ref[:, qk])
        s2 = _dot_nt(q, kc_ref[:, qk])
        o_ref[:, vv] = _softmax_pv([s1, s2], [v_ref[:, vv], vc_ref[:, vv]]).astype(BF16)


def _attn_call(layer, q, k, v, kc, vc, cast_w=None):
    attn_ctx = pl.pallas_call(
        _attn_ctx_kernel,
        out_shape=jax.ShapeDtypeStruct((T_CTX, V_W), BF16),
        grid=(BATCH // CTX_SEQS_PER_STEP,),
        in_specs=[pl.BlockSpec((CTX_SEQS_PER_STEP * SEQ, QK_W), lambda b: (b, 0)),
                  pl.BlockSpec((CTX_SEQS_PER_STEP * SEQ, QK_W), lambda b: (b, 0)),
                  pl.BlockSpec((CTX_SEQS_PER_STEP * SEQ, V_W), lambda b: (b, 0))],
        out_specs=pl.BlockSpec((CTX_SEQS_PER_STEP * SEQ, V_W), lambda b: (b, 0)),
        compiler_params=_cparams("arbitrary"),
        name="attn_ctx",
    )(q, k, v)

    tq = LAT_Q_TILE if cast_w is None else LAT_Q_TILE // 2
    n_q = DEC_SEQ // tq
    lat0 = T_CTX // DEC_SEQ
    in_specs = [pl.BlockSpec((tq, QK_W), lambda b, i: (T_CTX // tq + b * n_q + i, 0)),
                pl.BlockSpec((DEC_SEQ, QK_W), lambda b, i: (lat0 + b, 0)),
                pl.BlockSpec((DEC_SEQ, V_W), lambda b, i: (lat0 + b, 0)),
                pl.BlockSpec((None, PAST_LEN, QK_W), lambda b, i: (layer, b, 0)),
                pl.BlockSpec((None, PAST_LEN, V_W), lambda b, i: (layer, b, 0))]
    out_shape = [jax.ShapeDtypeStruct((T_LAT, V_W), BF16)]
    out_specs = [pl.BlockSpec((tq, V_W), lambda b, i: (b * n_q + i, 0))]
    args = [q, k, v, kc, vc]
    if cast_w is not None:
        w, w_in_spec, wb_shape, wb_spec = _cast_slab(DEC_BATCH * n_q, cast_w)
        step_of = lambda b, i: b * n_q + i
        in_specs.append(w_in_spec(step_of))
        out_shape.append(wb_shape)
        out_specs.append(wb_spec(step_of))
        args.append(w)
    outs = pl.pallas_call(
        _attn_lat_kernel,
        out_shape=tuple(out_shape),
        grid=(DEC_BATCH, n_q),
        in_specs=in_specs,
        out_specs=tuple(out_specs),
        compiler_params=_cparams("arbitrary", "arbitrary"),
        name="attn_lat",
    )(*args)
    return (attn_ctx, *outs)


def _swiglu_hidden(h, wg_ref, wu_ref, a_buf):
    width = a_buf.shape[1]
    for c0 in range(0, width, FF_CHUNK):
        cols = slice(c0, min(c0 + FF_CHUNK, width))
        a_buf[:, cols] = (_silu(_dot(h, wg_ref[:, cols])) * _dot(h, wu_ref[:, cols])).astype(BF16)


def _mix_kernel(with_router, attn_c_ref, attn_l_ref, pool_ref, xc_ref, xl_ref, g1_ref, sh2_ref, sc2_ref, n2_ref, wo_ref,
                *rest):
    if with_router:
        wr_ref, ltri_ref, utri_ref, w_ref, x_out, xs_out, gs_out, route_out, meta_out, wb_ref = rest
    else:
        g2_ref, wg_ref, wu_ref, wd_ref, w_ref, x_out, wb_ref, a_buf = rest
    wb_ref[...] = w_ref[...].astype(BF16)
    is_ctx = pl.program_id(0) < T_CTX // TILE_MIX
    halves = []
    for r0 in range(0, TILE_MIX, TILE_MIX // 2):
        rows = slice(r0, r0 + TILE_MIX // 2)
        attn = jnp.where(is_ctx, attn_c_ref[rows, :], attn_l_ref[rows, :])
        y = _dot(attn, wo_ref[:V_W, :]) + _dot(pool_ref[rows, :], wo_ref[V_W:, :])
        x_new = jnp.where(is_ctx, xc_ref[rows, :], xl_ref[rows, :]) + g1_ref[...] * y
        x_out[rows, :] = x_new
        halves.append((_rms(x_new) * n2_ref[...] * (1.0 + sc2_ref[...]) + sh2_ref[...]).astype(BF16))
    h = jnp.concatenate(halves, axis=0)
    if not with_router:
        _swiglu_hidden(h, wg_ref, wu_ref, a_buf)
        x_out[...] += g2_ref[...] * _dot(a_buf[...], wd_ref[...])
        return

    logits = _dot(h, wr_ref[...])
    lane = lax.broadcasted_iota(jnp.int32, logits.shape, 1)
    neg = float(jnp.finfo(F32).min)
    lg = jnp.where(lane < N_EXPERTS, logits, neg)
    m1 = lg.max(axis=-1, keepdims=True)
    i1 = jnp.where(lg == m1, lane, LANES).min(axis=-1, keepdims=True)
    lg2 = jnp.where(lane == i1, neg, lg)
    m2 = lg2.max(axis=-1, keepdims=True)
    i2 = jnp.where(lg2 == m2, lane, LANES).min(axis=-1, keepdims=True)
    e = jnp.exp(m2 - m1)
    w1 = 1.0 / (1.0 + e)
    w2 = e / (1.0 + e)

    sel1 = lane == i1
    sel2 = lane == i2
    member = jnp.where(jnp.logical_or(sel1, sel2), 1.0, 0.0)
    rank = _dot(ltri_ref[...], member.astype(BF16))
    n_tok = member.sum(axis=0, keepdims=True)
    units = jnp.floor((n_tok + (SLOT_UNIT - 1)) * (1.0 / SLOT_UNIT))
    unit_off = _dot(jnp.broadcast_to(units, (8, LANES)).astype(BF16), utri_ref[...])[0:1, :]
    slot_of = SLOT_UNIT * unit_off + rank
    slot1 = jnp.where(sel1, slot_of, 0.0).sum(axis=-1, keepdims=True)
    slot2 = jnp.where(sel2, slot_of, 0.0).sum(axis=-1, keepdims=True)
    route = jnp.where(lane == 0, slot1, jnp.where(lane == 1, slot2,
                      jnp.where(lane == 2, w1, jnp.where(lane == 3, w2, 0.0))))
    route_out[...] = route
    sub = lax.broadcasted_iota(jnp.int32, (8, LANES), 0)
    meta_out[...] = jnp.where(sub == 0, units, jnp.where(sub == 1, unit_off, 0.0)).astype(jnp.int32)

    rt = route.T
    s1 = rt[0:1, :].astype(jnp.int32)
    s2 = rt[1:2, :].astype(jnp.int32)
    for r0 in range(0, SLOTS_PER_TILE, SLOT_BLOCK):
        blk = slice(r0, r0 + SLOT_BLOCK)
        srow = r0 + lax.broadcasted_iota(jnp.int32, (SLOT_BLOCK, TILE_MIX), 0)
        hit1 = srow == s1
        hit2 = srow == s2
        perm = (jnp.where(hit1, 1.0, 0.0) + jnp.where(hit2, 1.0, 0.0)).astype(BF16)
        xs_out[blk, :] = _dot(perm, h).astype(BF16)
        gate = (jnp.where(hit1, rt[2:3, :], 0.0) + jnp.where(hit2, rt[3:4, :], 0.0)).sum(axis=-1, keepdims=True)
        gs_out[blk, :] = jnp.broadcast_to(gate, (SLOT_BLOCK, LANES))


def _mod_row(i, tile):
    n_ctx = T_CTX // tile
    per_seq = DEC_SEQ // tile
    return jnp.where(i < n_ctx, 0, 1 + (i - n_ctx) // per_seq)


def _mix_call(attn_ctx, attn_lat, pooled, x_ctx, x_lat, ada, n2, w_out, cast_w, w_router=None, ffn_w=None):
    with_router = w_router is not None
    assert with_router != (ffn_w is not None)
    tm = TILE_MIX
    n_ctx = T_CTX // tm
    lat_tile0 = n_ctx if x_lat.shape[0] == T_ALL else 0
    row = lambda i: (i, 0)
    const2 = lambda i: (0, 0)
    mod = lambda k: pl.BlockSpec((None, 1, D_MODEL), lambda i: (_mod_row(i, tm), 0, k))
    ctx_blk = lambda i: (jnp.minimum(i, n_ctx - 1), 0)
    in_specs = [pl.BlockSpec((tm, V_W), ctx_blk),
                pl.BlockSpec((tm, V_W), lambda i: (jnp.maximum(i - n_ctx, 0), 0)),
                pl.BlockSpec((tm, POOL_W), row),
                pl.BlockSpec((tm, D_MODEL), ctx_blk),
                pl.BlockSpec((tm, D_MODEL), lambda i: (jnp.maximum(i - n_ctx, 0) + lat_tile0, 0)),
                mod(2), mod(3), mod(4),
                pl.BlockSpec((1, D_MODEL), const2), pl.BlockSpec((D_MODEL, D_MODEL), const2)]
    args = [attn_ctx, attn_lat, pooled, x_ctx, x_lat, ada, ada, ada, n2, w_out]
    if with_router:
        t_i = jnp.arange(tm)
        ltri = (t_i[None, :] < t_i[:, None]).astype(BF16)
        l_i = jnp.arange(LANES)
        utri = (l_i[:, None] < l_i[None, :]).astype(BF16)
        in_specs += [pl.BlockSpec((D_MODEL, LANES), const2), pl.BlockSpec((tm, tm), const2),
                     pl.BlockSpec((LANES, LANES), const2)]
        args += [w_router, ltri, utri]
        out_shape = [jax.ShapeDtypeStruct((T_ALL, D_MODEL), F32),
                     jax.ShapeDtypeStruct((N_ROUTE_TILES * SLOTS_PER_TILE, D_MODEL), BF16),
                     jax.ShapeDtypeStruct((N_ROUTE_TILES * SLOTS_PER_TILE, LANES), F32),
                     jax.ShapeDtypeStruct((T_ALL, LANES), F32),
                     jax.ShapeDtypeStruct((N_ROUTE_TILES, 8, LANES), jnp.int32)]
        out_specs = [pl.BlockSpec((tm, D_MODEL), row), pl.BlockSpec((SLOTS_PER_TILE, D_MODEL), row),
                     pl.BlockSpec((SLOTS_PER_TILE, LANES), row), pl.BlockSpec((tm, LANES), row),
                     pl.BlockSpec((None, 8, LANES), lambda i: (i, 0, 0))]
        scratch = []
    else:
        resident = lambda shape: pl.BlockSpec(shape, const2, pipeline_mode=pl.Buffered(1))
        in_specs += [mod(5), resident((D_MODEL, D_FF)), resident((D_MODEL, D_FF)), resident((D_FF, D_MODEL))]
        args += [ada, *ffn_w]
        out_shape = [jax.ShapeDtypeStruct((T_ALL, D_MODEL), F32)]
        out_specs = [pl.BlockSpec((tm, D_MODEL), row)]
        scratch = [pltpu.VMEM((tm, D_FF), BF16)]
    w, w_spec, wb_shape, wb_spec = _cast_slab(T_ALL // tm, cast_w)
    in_specs.append(w_spec(lambda i: i))
    args.append(w)
    out_shape.append(wb_shape)
    out_specs.append(wb_spec(lambda i: i))
    return pl.pallas_call(
        functools.partial(_mix_kernel, with_router),
        out_shape=tuple(out_shape),
        grid=(T_ALL // tm,),
        in_specs=in_specs,
        out_specs=tuple(out_specs),
        scratch_shapes=scratch,
        compiler_params=_cparams("arbitrary"),
        name="mix_router" if with_router else "mix_ffn",
    )(*args)


def _route_tables(meta):
    units = meta[:, 0, :N_EXPERTS]
    offs = meta[:, 1, :N_EXPERTS]
    cum = jnp.cumsum(units, axis=0)
    total = cum[-1]
    tiles_e = (total + GEMM_UNITS - 1) // GEMM_UNITS
    tile_end = jnp.cumsum(tiles_e)
    n_act = tile_end[-1]
    m = jnp.arange(N_GEMM_TILES)
    m_eff = jnp.minimum(m, jnp.maximum(n_act - 1, 0))
    expert = jnp.minimum(jnp.sum(tile_end[None, :] <= m_eff[:, None], axis=1), N_EXPERTS - 1)
    is_e = (expert[:, None] == jnp.arange(N_EXPERTS)[None, :]).astype(jnp.int32)
    pick = lambda per_expert: jnp.sum(is_e * per_expert[None, :], axis=1)
    first_q = (m_eff - pick(tile_end - tiles_e)) * GEMM_UNITS
    count = jnp.where(m < n_act, jnp.clip(pick(total) - first_q, 0, GEMM_UNITS), 0)
    q = first_q[:, None] + jnp.arange(GEMM_UNITS)[None, :]
    rows_of = lambda table: jnp.sum(is_e[:, None, :] * table[None, :, :], axis=2)
    cum_e, units_e, offs_e = rows_of(cum), rows_of(units), rows_of(offs)
    src_tile = jnp.minimum(jnp.sum(cum_e[:, None, :] <= q[:, :, None], axis=2), N_ROUTE_TILES - 1)
    is_t = (src_tile[:, :, None] == jnp.arange(N_ROUTE_TILES)[None, None, :]).astype(jnp.int32)
    at_tile = lambda per_tile: jnp.sum(is_t * per_tile[:, None, :], axis=2)
    uid = src_tile * UNITS_PER_TILE + at_tile(offs_e) + (q - at_tile(cum_e - units_e))
    uid = jnp.where(jnp.arange(GEMM_UNITS)[None, :] < count[:, None], uid, ZERO_UNIT)
    i32 = jnp.int32
    return expert.astype(i32), count.astype(i32), uid.reshape(-1).astype(i32), n_act.reshape(1).astype(i32)


def _gemm_kernel(em_ref, cnt_ref, ul_ref, nact_ref, xs_hbm, gs_hbm, wg_ref, wu_ref, wd_ref, ys_hbm,
                 xbuf, gbuf, obuf, acc_ref, a_buf, sem_in, sem_out):
    del em_ref
    m = pl.program_id(0)
    j = pl.program_id(1)
    last_j = pl.num_programs(1) - 1
    n_act = nact_ref[0]
    active = m < n_act
    slot = m % 2

    def rows(r):
        return pl.ds(r * SLOT_UNIT if isinstance(r, int) else pl.multiple_of(r * SLOT_UNIT, SLOT_UNIT), SLOT_UNIT)

    def in_copies(mm, sl, r):
        uid = ul_ref[mm * GEMM_UNITS + r]
        return (pltpu.make_async_copy(xs_hbm.at[uid], xbuf.at[sl, rows(r)], sem_in.at[sl]),
                pltpu.make_async_copy(gs_hbm.at[uid], gbuf.at[sl, rows(r)], sem_in.at[sl]))

    def out_copy(mm, r):
        uid = ul_ref[mm * GEMM_UNITS + r]
        return pltpu.make_async_copy(obuf.at[rows(r)], ys_hbm.at[uid], sem_out.at[0])

    def for_slot(sl_dyn, fn):
        if isinstance(sl_dyn, int):
            fn(sl_dyn)
            return
        for sl in range(2):
            pl.when(sl_dyn == sl)(functools.partial(fn, sl))

    def start_in(mm, sl_dyn):
        def issue(sl):
            for r in range(GEMM_UNITS):
                for cp in in_copies(mm, sl, r):
                    cp.start()

        for_slot(sl_dyn, issue)

    def wait_in(mm, sl_dyn):
        def wait(sl):
            for r in range(GEMM_UNITS):
                for cp in in_copies(mm, sl, r):
                    cp.wait()

        for_slot(sl_dyn, wait)

    def start_out(mm):
        n = cnt_ref[mm]
        for r in range(GEMM_UNITS):
            pl.when(r < n)(lambda r=r: out_copy(mm, r).start())

    def wait_out(mm):
        def body(r, carry):
            out_copy(mm, r).wait()
            return carry

        lax.fori_loop(0, cnt_ref[mm], body, 0)

    @pl.when(jnp.logical_and(active, j == 0))
    def _():
        @pl.when(m == 0)
        def _():
            start_in(0, 0)

        wait_in(m, slot)

        @pl.when(m + 1 < n_act)
        def _():
            start_in(m + 1, 1 - slot)

    @pl.when(jnp.logical_and(jnp.logical_and(active, j == last_j), m > 0))
    def _():
        wait_out(m - 1)

    n_valid = cnt_ref[m]
    for quarter in range(1, GEMM_QUARTERS + 1):
        n_rows = quarter * GEMM_ROWS // GEMM_QUARTERS
        lo_units = (quarter - 1) * GEMM_UNITS // GEMM_QUARTERS
        hi_units = quarter * GEMM_UNITS // GEMM_QUARTERS

        @pl.when(jnp.logical_and(active, jnp.logical_and(n_valid > lo_units, n_valid <= hi_units)))
        def _(n_rows=n_rows):
            _swiglu_hidden(xbuf[slot, :n_rows, :], wg_ref, wu_ref, a_buf.at[:n_rows, :])
            part = _dot(a_buf[:n_rows, :], wd_ref[...])

            @pl.when(j == 0)
            def _():
                acc_ref[:n_rows, :] = part

            @pl.when(jnp.logical_and(j > 0, j < last_j))
            def _():
                acc_ref[:n_rows, :] += part

            @pl.when(j == last_j)
            def _():
                gate = gbuf[slot, :n_rows, :]
                for cb in range(D_MODEL // LANES):
                    cols = slice(cb * LANES, (cb + 1) * LANES)
                    obuf[:n_rows, cols] = ((acc_ref[:n_rows, cols] + part[:, cols]) * gate).astype(BF16)

    @pl.when(jnp.logical_and(active, j == last_j))
    def _():
        start_out(m)

        @pl.when(m == n_act - 1)
        def _():
            wait_out(m)


def _gemm_call(expert, count, uids, n_act, xs, gs, wg, wu, wd):
    tf = TILE_FE
    n_j = D_FF_EXPERT // tf
    assert n_j >= 2

    def w_col(m, j, em, cnt, ul, nact):
        return (em[m], 0, jnp.where(m < nact[0], j, n_j - 1))

    def w_row(m, j, em, cnt, ul, nact):
        return (em[m], jnp.where(m < nact[0], j, n_j - 1), 0)

    grid_spec = pltpu.PrefetchScalarGridSpec(
        num_scalar_prefetch=4,
        grid=(N_GEMM_TILES, n_j),
        in_specs=[pl.BlockSpec(memory_space=pl.ANY),
                  pl.BlockSpec(memory_space=pl.ANY),
                  pl.BlockSpec((None, D_MODEL, tf), w_col),
                  pl.BlockSpec((None, D_MODEL, tf), w_col),
                  pl.BlockSpec((None, tf, D_MODEL), w_row)],
        out_specs=pl.BlockSpec(memory_space=pl.ANY),
        scratch_shapes=[pltpu.VMEM((2, GEMM_ROWS, D_MODEL), BF16),
                        pltpu.VMEM((2, GEMM_ROWS, LANES), F32),
                        pltpu.VMEM((GEMM_ROWS, D_MODEL), BF16),
                        pltpu.VMEM((GEMM_ROWS, D_MODEL), F32),
                        pltpu.VMEM((GEMM_ROWS, tf), BF16),
                        pltpu.SemaphoreType.DMA((2,)),
                        pltpu.SemaphoreType.DMA((1,))])
    return pl.pallas_call(
        _gemm_kernel,
        out_shape=jax.ShapeDtypeStruct((N_UNITS, SLOT_UNIT, D_MODEL), BF16),
        grid_spec=grid_spec,
        input_output_aliases={4: 0},
        compiler_params=_cparams("arbitrary", "arbitrary"),
        name="moe_experts",
    )(expert, count, uids, n_act,
      xs.reshape(N_UNITS, SLOT_UNIT, D_MODEL), gs.reshape(N_UNITS, SLOT_UNIT, LANES), wg, wu, wd)


def _combine_kernel(ys_ref, route_ref, x_ref, g2_ref, fg_ref, o_ref):
    route = route_ref[...]
    s1 = route[:, 0:1].astype(jnp.int32)
    s2 = route[:, 1:2].astype(jnp.int32)
    scol = lax.broadcasted_iota(jnp.int32, (TILE_MIX, SLOTS_PER_TILE), 1)
    unperm = (jnp.where(scol == s1, 1.0, 0.0) + jnp.where(scol == s2, 1.0, 0.0)).astype(BF16)
    y = _dot(unperm, ys_ref[...])
    o_ref[...] = _rms(x_ref[...] + g2_ref[...] * y) * fg_ref[...]


def _combine_call(ys, route, x, ada, final_g, tile0, n_tok):
    tm = TILE_MIX
    return pl.pallas_call(
        _combine_kernel,
        out_shape=jax.ShapeDtypeStruct((n_tok, D_MODEL), F32),
        grid=(n_tok // tm,),
        in_specs=[pl.BlockSpec((SLOTS_PER_TILE, D_MODEL), lambda i: (i + tile0, 0)),
                  pl.BlockSpec((tm, LANES), lambda i: (i + tile0, 0)),
                  pl.BlockSpec((tm, D_MODEL), lambda i: (i + tile0, 0)),
                  pl.BlockSpec((None, 1, D_MODEL), lambda i: (_mod_row(i + tile0, tm), 0, 5)),
                  pl.BlockSpec((1, D_MODEL), lambda i: (0, 0))],
        out_specs=pl.BlockSpec((tm, D_MODEL), lambda i: (i, 0)),
        compiler_params=_cparams("arbitrary"),
        name="moe_combine",
    )(ys.reshape(N_UNITS * SLOT_UNIT, D_MODEL), route, x, ada, final_g)


def _rot_cols(w):
    q = QK_ROPE // 4
    src = jnp.arange(QK_ROPE)[:, None]
    dst = jnp.arange(QK_ROPE)[None, :]
    first_half = (dst // q) % 2 == 0
    rot = jnp.where(first_half, -(src == dst + q).astype(F32), (src == dst - q).astype(F32))
    return jnp.matmul(w, rot.astype(w.dtype))


def _rope_tables():
    t = jnp.arange(DEC_SEQ)
    rows = (t // GRID_W).astype(F32)
    cols = (t % GRID_W).astype(F32)
    half = QK_ROPE // 2
    freqs = ROPE_BASE ** (-jnp.arange(0, half, 2, dtype=F32) / half)
    ang_r = rows[:, None] * freqs
    ang_c = cols[:, None] * freqs
    ang = jnp.concatenate([ang_r, ang_r, ang_c, ang_c], axis=-1)
    cos, sin = jnp.cos(ang), jnp.sin(ang)
    one, zero = jnp.ones_like(cos), jnp.zeros_like(cos)
    tq = jnp.stack([jnp.concatenate([one, zero], -1), jnp.concatenate([cos, sin], -1)])
    tc = jnp.stack([jnp.concatenate([one, one], -1), jnp.concatenate([cos, cos], -1)])
    ts = jnp.stack([jnp.concatenate([zero, zero], -1), jnp.concatenate([sin, sin], -1)])
    return tq, tc, ts


def _layer_weights(w_in, w_uq, w_ukv):
    wi = w_in.astype(BF16)
    w_kpe = wi[..., OFF_U:OFF_U + QK_ROPE]
    w_rot = _rot_cols(w_kpe)
    w_in_r = jnp.concatenate([wi[..., :OFF_U], wi[..., OFF_U + QK_ROPE:], w_kpe, w_kpe, w_rot, w_rot], axis=-1)
    wq = w_uq.astype(BF16).reshape(DEPTH, Q_LORA, N_HEADS, QK_NOPE + QK_ROPE)
    w_uq_r = jnp.concatenate([wq, _rot_cols(wq[..., QK_NOPE:])], axis=-1).reshape(DEPTH, Q_LORA, QK_W)
    kv = w_ukv.astype(BF16).reshape(DEPTH, KV_LORA, N_HEADS, QK_NOPE + V_DIM)
    w_uk = kv[..., :QK_NOPE].reshape(DEPTH, KV_LORA, N_HEADS * QK_NOPE)
    w_uv = kv[..., QK_NOPE:].reshape(DEPTH, KV_LORA, V_W)
    return w_in_r, w_uq_r, w_uk, w_uv


def kernel(x_prompt, x_sample, c, cache_ckv, cache_kpe, c_ctx, norm1_g, norm2_g, w_ada, b_ada, w_in, q_norm_g,
           kv_norm_g, w_uq, w_ukv, w_pool, pool_scale, w_out, ffn_w_gate, ffn_w_up, ffn_w_down, moe_w_router,
           moe_w_gate, moe_w_up, moe_w_down, final_norm_g):
    x_ctx = x_prompt.reshape(T_CTX, D_MODEL)
    x_lat = x_sample.reshape(T_LAT, D_MODEL)
    cond = jnp.concatenate([c_ctx[None, :], c, jnp.zeros((COND_ROWS - 1 - DEC_BATCH, D_MODEL), F32)], axis=0)
    ada_all = _ada_call(cond, w_ada, b_ada)
    tq, tc, ts = _rope_tables()

    w_in_all, w_uq_all, w_uk_all, w_uv_all = _layer_weights(w_in, w_uq, w_ukv)
    dup = jnp.concatenate([jnp.eye(QK_ROPE, dtype=BF16)] * 2, axis=1)
    kc_all, vc_all = _cache_call(cache_ckv, cache_kpe, w_uk_all, w_uv_all, dup)

    assert DEPTH == 2
    moe_gate_2d = moe_w_gate[0].reshape(N_EXPERTS * D_MODEL, D_FF_EXPERT)
    moe_up_2d = moe_w_up[0].reshape(N_EXPERTS * D_MODEL, D_FF_EXPERT)
    moe_down_2d = moe_w_down[0].reshape(N_EXPERTS * D_FF_EXPERT, D_MODEL)
    new_ckv = new_kpe = None
    for l in range(DEPTH):
        ada = ada_all[l].reshape(COND_ROWS, 1, 6 * D_MODEL)
        cast_ws = (ffn_w_gate[0], ffn_w_up[0], ffn_w_down[0],
                   w_out.reshape(DEPTH * (V_W + POOL_W), D_MODEL)) if l == 0 else ()
        q, k, v, new_ckv, new_kpe, pooled, *cast = _in_call(
            l, x_ctx, x_lat, ada, norm1_g[l][None, :], w_in_all, q_norm_g[l][None, :], kv_norm_g[l][None, :],
            w_uq_all, w_uk_all, w_uv_all, tq, tc, ts, w_pool[l].astype(BF16), pool_scale[l][None, :], new_ckv,
            new_kpe, cast_ws)
        if l == 0:
            ffn_w, w_out_b = cast[:3], cast[3].reshape(DEPTH, V_W + POOL_W, D_MODEL)
        j = l // 2
        if l % 2 == 0:
            attn_c, attn_l, moe_wd = _attn_call(l, q, k, v, kc_all, vc_all, cast_w=moe_down_2d)
            x, moe_wg = _mix_call(attn_c, attn_l, pooled, x_ctx, x_lat, ada, norm2_g[l][None, :],
                                  w_out_b[l], moe_gate_2d, ffn_w=ffn_w)
            x_ctx = x_lat = x
        else:
            attn_c, attn_l = _attn_call(l, q, k, v, kc_all, vc_all)
            w_r = jnp.pad(moe_w_router[j], ((0, 0), (0, LANES - N_EXPERTS))).astype(BF16)
            x, xs, gs, route, meta, moe_wu = _mix_call(attn_c, attn_l, pooled, x_ctx, x_lat, ada,
                                                       norm2_g[l][None, :], w_out_b[l], moe_up_2d,
                                                       w_router=w_r)
            expert, count, uids, n_act = _route_tables(meta)
            ys = _gemm_call(expert, count, uids, n_act, xs, gs,
                            moe_wg.reshape(N_EXPERTS, D_MODEL, D_FF_EXPERT),
                            moe_wu.reshape(N_EXPERTS, D_MODEL, D_FF_EXPERT),
                            moe_wd.reshape(N_EXPERTS, D_FF_EXPERT, D_MODEL))
            fg = final_norm_g[None, :]
            y_prompt = _combine_call(ys, route, x, ada, fg, 0, T_CTX).reshape(BATCH, SEQ, D_MODEL)
            y_sample = _combine_call(ys, route, x, ada, fg, T_CTX // TILE_MIX, T_LAT).reshape(
                DEC_BATCH, DEC_SEQ, D_MODEL)

    return y_prompt, y_sample, new_ckv, jnp.swapaxes(new_kpe, -1, -2)
```

```python
import functools

import jax
import jax.numpy as jnp
from jax import lax
from jax.experimental import pallas as pl
from jax.experimental.pallas import tpu as pltpu

D_MODEL = 1024
BATCH = 32
SEQ = 256
DEPTH = 2
DEC_BATCH = 8
DEC_SEQ = 1024
PAST_LEN = 512
GRID_W = 64
N_HEADS = 4
QK_NOPE = 128
QK_ROPE = 64
V_DIM = 128
Q_LORA = 384
KV_LORA = 256
POOL_W = 512
POOL_GROUPS = 4
POOL_WINDOWS = (2, 4, 8, 16)
POOL_CH = POOL_W // POOL_GROUPS
D_FF = 2816
N_EXPERTS = 8
D_FF_EXPERT = 3584
ROPE_BASE = 10000.0
EPS = 1e-6

T_CTX = BATCH * SEQ
T_LAT = DEC_BATCH * DEC_SEQ
T_ALL = T_CTX + T_LAT

LANES = 128
HEAD_PAD = 256
QK_W = N_HEADS * HEAD_PAD
V_W = N_HEADS * V_DIM
OFF_CKV = Q_LORA
OFF_U = Q_LORA + KV_LORA
OFF_KA = OFF_U + POOL_W
OFF_KB = OFF_KA + LANES
N_PROJ = OFF_KB + LANES

TILE_IN = 1024
CHUNK = 256
HALO = 8
REGION = CHUNK + 2 * HALO
LAT_Q_TILE = 1024
TILE_MIX = 512
TOP_K = 2
SLOT_UNIT = 16
UNITS_PER_TILE = TILE_MIX * TOP_K // SLOT_UNIT + N_EXPERTS
SLOTS_PER_TILE = UNITS_PER_TILE * SLOT_UNIT
SLOT_BLOCK = SLOTS_PER_TILE // 4
N_ROUTE_TILES = T_ALL // TILE_MIX
N_UNITS = N_ROUTE_TILES * UNITS_PER_TILE
ZERO_UNIT = UNITS_PER_TILE - 1
GEMM_UNITS = 64
GEMM_ROWS = GEMM_UNITS * SLOT_UNIT
GEMM_QUARTERS = 4
N_GEMM_TILES = -(-N_UNITS // GEMM_UNITS) + N_EXPERTS
TILE_FE = D_FF_EXPERT // 2
FF_CHUNK = 256
COND_ROWS = 16
VMEM_LIMIT = 56 * 1024 * 1024

F32 = jnp.float32
BF16 = jnp.bfloat16


def _rms(x):
    return x * lax.rsqrt(jnp.mean(x * x, axis=-1, keepdims=True) + EPS)


def _dot(a, b):
    return jnp.dot(a, b, preferred_element_type=F32)


def _dot_nt(a, b):
    return lax.dot_general(a, b, (((1,), (1,)), ((), ())), preferred_element_type=F32)


def _silu(x):
    return x * (1.0 / (1.0 + jnp.exp(-x)))


def _cparams(*sem):
    return pltpu.CompilerParams(dimension_semantics=sem, vmem_limit_bytes=VMEM_LIMIT)


def _ada_kernel(cond_ref, w_ref, b_ref, o_ref):
    k = pl.program_id(1)
    part = _dot(_silu(cond_ref[...]).astype(BF16), w_ref[...].astype(BF16))

    @pl.when(k == 0)
    def _():
        o_ref[...] = part + b_ref[...]

    @pl.when(k > 0)
    def _():
        o_ref[...] += part


def _ada_call(cond, w_ada, b_ada):
    n_blk = 4
    kb = D_MODEL // n_blk
    return pl.pallas_call(
        _ada_kernel,
        out_shape=jax.ShapeDtypeStruct((DEPTH, COND_ROWS, 6 * D_MODEL), F32),
        grid=(DEPTH, n_blk),
        in_specs=[
            pl.BlockSpec((COND_ROWS, kb), lambda l, k: (0, k)),
            pl.BlockSpec((None, kb, 6 * D_MODEL), lambda l, k: (l, k, 0)),
            pl.BlockSpec((None, 1, 6 * D_MODEL), lambda l, k: (l, 0, 0)),
        ],
        out_specs=pl.BlockSpec((None, COND_ROWS, 6 * D_MODEL), lambda l, k: (l, 0, 0)),
        compiler_params=_cparams("arbitrary", "arbitrary"),
        name="ada_params",
    )(cond, w_ada, b_ada.reshape(DEPTH, 1, 6 * D_MODEL))


def _cache_kernel(ckv_ref, kpe_ref, wuk_ref, wuv_ref, dup_ref, k_ref, v_ref):
    ckv = ckv_ref[...].astype(BF16)
    knope = _dot(ckv, wuk_ref[...])
    v_ref[...] = _dot(ckv, wuv_ref[...]).astype(BF16)
    kpe = kpe_ref[...].T
    kdup = _dot(kpe.astype(BF16), dup_ref[...]).astype(BF16)
    for h in range(N_HEADS):
        k_ref[:, h * HEAD_PAD:h * HEAD_PAD + QK_NOPE] = knope[:, h * QK_NOPE:(h + 1) * QK_NOPE].astype(BF16)
        k_ref[:, h * HEAD_PAD + QK_NOPE:(h + 1) * HEAD_PAD] = kdup


def _cache_call(cache_ckv, cache_kpe, w_uk, w_uv, dup):
    n_tok = DEC_BATCH * PAST_LEN
    return pl.pallas_call(
        _cache_kernel,
        out_shape=(jax.ShapeDtypeStruct((DEPTH, n_tok, QK_W), BF16),
                   jax.ShapeDtypeStruct((DEPTH, n_tok, V_W), BF16)),
        grid=(DEPTH, DEC_BATCH),
        in_specs=[
            pl.BlockSpec((None, None, PAST_LEN, KV_LORA), lambda l, b: (b, l, 0, 0)),
            pl.BlockSpec((None, None, QK_ROPE, PAST_LEN), lambda l, b: (b, l, 0, 0)),
            pl.BlockSpec((None, KV_LORA, N_HEADS * QK_NOPE), lambda l, b: (l, 0, 0)),
            pl.BlockSpec((None, KV_LORA, V_W), lambda l, b: (l, 0, 0)),
            pl.BlockSpec((QK_ROPE, LANES), lambda l, b: (0, 0)),
        ],
        out_specs=(pl.BlockSpec((None, PAST_LEN, QK_W), lambda l, b: (l, b, 0)),
                   pl.BlockSpec((None, PAST_LEN, V_W), lambda l, b: (l, b, 0))),
        compiler_params=_cparams("arbitrary", "arbitrary"),
        name="cache_kv",
    )(cache_ckv, jnp.swapaxes(cache_kpe, -1, -2), w_uk, w_uv, dup)


def _in_kernel(n_prev, n_cast, xc_ref, xl_ref, sh_ref, sc_ref, g_ref, win_ref, qg_ref, kvg_ref, wuq_ref, wuk_ref,
               wuv_ref, tq_ref, tc_ref, ts_ref, wpool_ref, ps_ref, *rest):
    if n_prev:
        prev_ckv_ref, prev_kpe_ref, *rest = rest
    cast_in, rest = rest[:n_cast], rest[n_cast:]
    q_out, k_out, v_out, ckv_out, kpe_out, pool_out, attn_out, *rest = rest
    cast_out, (u_scr,) = rest[:n_cast], rest[n_cast:]
    for w_ref, wb_ref in zip(cast_in, cast_out):
        wb_ref[...] = w_ref[...].astype(BF16)
    is_ctx = pl.program_id(0) >= (T_ALL - T_CTX) // TILE_IN
    if n_prev:
        ckv_out[:, :n_prev] = prev_ckv_ref[...]
        kpe_out[:, :n_prev] = prev_kpe_ref[...]
    shift = sh_ref[...]
    scale1 = 1.0 + sc_ref[...]
    qk_scale = (QK_NOPE + QK_ROPE) ** -0.5
    n_chunks = TILE_IN // CHUNK
    zeros = jnp.zeros((HALO, POOL_W), F32)
    seq_len = jnp.where(is_ctx, SEQ, DEC_SEQ)
    row = lax.broadcasted_iota(jnp.int32, (CHUNK, POOL_CH), 0)

    def pool_chunk(c):
        base = c * REGION + HALO
        rows = slice(c * CHUNK, (c + 1) * CHUNK)
        t = row + jnp.where(is_ctx, 0, c * CHUNK)
        for g, w in enumerate(POOL_WINDOWS):
            cols = slice(g * POOL_CH, (g + 1) * POOL_CH)
            acc = u_scr[base - w // 2:base - w // 2 + CHUNK, cols]
            for j in range(-w // 2 + 1, w // 2):
                acc = acc + u_scr[base + j:base + j + CHUNK, cols]
            cnt = jnp.minimum(t + w // 2, seq_len) - jnp.maximum(t - w // 2, 0)
            pooled = acc / cnt.astype(F32) - u_scr[base:base + CHUNK, cols]
            lin = _dot(pooled.astype(BF16), wpool_ref[g]) * ps_ref[:, cols]
            pool_out[rows, cols] = lin.astype(BF16)

    for c in range(n_chunks):
        r0 = c * CHUNK
        rows = slice(r0, r0 + CHUNK)
        x = jnp.where(is_ctx, xc_ref[rows, :], xl_ref[rows, :])
        h = (_rms(x) * g_ref[...] * scale1 + shift).astype(BF16)
        proj = _dot(h, win_ref[...])

        qn = (_rms(proj[:, :Q_LORA]) * qg_ref[...]).astype(BF16)
        q = _dot(qn, wuq_ref[...]) * qk_scale
        tq = tq_ref[rows, :]
        for hd in range(N_HEADS):
            lo = hd * HEAD_PAD
            q_out[rows, lo:lo + QK_NOPE] = q[:, lo:lo + QK_NOPE].astype(BF16)
            q_out[rows, lo + QK_NOPE:lo + HEAD_PAD] = (q[:, lo + QK_NOPE:lo + HEAD_PAD] * tq).astype(BF16)

        ckv = _rms(proj[:, OFF_CKV:OFF_U]) * kvg_ref[...]
        ckv_out[c, n_prev] = ckv
        ckv_b = ckv.astype(BF16)
        knope = _dot(ckv_b, wuk_ref[...])
        v_out[rows, :] = _dot(ckv_b, wuv_ref[...]).astype(BF16)

        k_a = proj[:, OFF_KA:OFF_KB]
        k_b = proj[:, OFF_KB:N_PROJ]
        kpe_out[c, n_prev] = k_a.T[:QK_ROPE, :]
        kr = (k_a * tc_ref[rows, :] + k_b * ts_ref[rows, :]).astype(BF16)
        for hd in range(N_HEADS):
            lo = hd * HEAD_PAD
            k_out[rows, lo:lo + QK_NOPE] = knope[:, hd * QK_NOPE:(hd + 1) * QK_NOPE].astype(BF16)
            k_out[rows, lo + QK_NOPE:lo + HEAD_PAD] = kr

        u = proj[:, OFF_U:OFF_KA]
        base = c * REGION + HALO
        u_scr[base:base + CHUNK, :] = u
        if c == 0:
            u_scr[0:HALO, :] = zeros
        else:
            u_scr[base - 2 * HALO:base - HALO, :] = jnp.where(is_ctx, zeros, u[:HALO, :])
        if c == n_chunks - 1:
            u_scr[base + CHUNK:base + CHUNK + HALO, :] = zeros
        else:
            u_scr[base + CHUNK + HALO:base + CHUNK + 2 * HALO, :] = jnp.where(is_ctx, zeros, u[CHUNK - HALO:, :])
        if c > 0:
            pool_chunk(c - 1)
    pool_chunk(n_chunks - 1)

    @pl.when(is_ctx)
    def _():
        _attn_ctx_rows(q_out, k_out, v_out, attn_out)


def _in_call(layer, x_ctx, x_lat, ada, g1, w_in, qg, kvg, w_uq, w_uk, w_uv, tq, tc, ts, w_pool, pscale, prev_ckv, prev_kpe,
             cast_ws=()):
    assert CHUNK == SEQ
    n_tiles = T_ALL // TILE_IN
    n_ctx = T_CTX // TILE_IN
    n_prev = 0 if prev_ckv is None else prev_ckv.shape[1]
    seqs = TILE_IN // SEQ
    lat_tile0 = n_ctx if x_lat.shape[0] == T_ALL else 0

    def tile(i):
        return jnp.where(i < n_tiles - n_ctx, i + n_ctx, i - (n_tiles - n_ctx))

    def cond_row(i):
        return jnp.maximum(tile(i) - n_ctx + 1, 0)

    def tab(i):
        return (jnp.minimum(jnp.maximum(tile(i) - n_ctx + 1, 0), 1), 0, 0)

    rows = lambda i: (tile(i), 0)
    ctx_blk4 = lambda i: (jnp.where(tile(i) < n_ctx, tile(i), 0), 0, 0, 0)
    const2 = lambda i: (0, 0)
    of_layer = lambda i: (layer, 0, 0)
    prev_specs = [pl.BlockSpec((seqs, n_prev, SEQ, KV_LORA), ctx_blk4),
                  pl.BlockSpec((seqs, n_prev, QK_ROPE, SEQ), ctx_blk4)] if n_prev else []
    prev_args = [prev_ckv, prev_kpe] if n_prev else []
    casts = [_cast_slab(n_tiles, w) for w in cast_ws]
    step = lambda i: i
    return pl.pallas_call(
        functools.partial(_in_kernel, n_prev, len(casts)),
        out_shape=(jax.ShapeDtypeStruct((T_ALL, QK_W), BF16),
                   jax.ShapeDtypeStruct((T_ALL, QK_W), BF16),
                   jax.ShapeDtypeStruct((T_ALL, V_W), BF16),
                   jax.ShapeDtypeStruct((BATCH, n_prev + 1, SEQ, KV_LORA), F32),
                   jax.ShapeDtypeStruct((BATCH, n_prev + 1, QK_ROPE, SEQ), F32),
                   jax.ShapeDtypeStruct((T_ALL, POOL_W), BF16),
                   jax.ShapeDtypeStruct((T_CTX, V_W), BF16)) + tuple(c[2] for c in casts),
        grid=(n_tiles,),
        in_specs=[
            pl.BlockSpec((TILE_IN, D_MODEL), lambda i: (jnp.minimum(tile(i), n_ctx - 1), 0)),
            pl.BlockSpec((TILE_IN, D_MODEL), lambda i: (jnp.maximum(tile(i) - n_ctx, 0) + lat_tile0, 0)),
            pl.BlockSpec((None, 1, D_MODEL), lambda i: (cond_row(i), 0, 0)),
            pl.BlockSpec((None, 1, D_MODEL), lambda i: (cond_row(i), 0, 1)),
            pl.BlockSpec((1, D_MODEL), const2),
            pl.BlockSpec((None, D_MODEL, N_PROJ), of_layer),
            pl.BlockSpec((1, Q_LORA), const2),
            pl.BlockSpec((1, KV_LORA), const2),
            pl.BlockSpec((None, Q_LORA, QK_W), of_layer),
            pl.BlockSpec((None, KV_LORA, N_HEADS * QK_NOPE), of_layer),
            pl.BlockSpec((None, KV_LORA, V_W), of_layer),
            pl.BlockSpec((None, TILE_IN, LANES), tab),
            pl.BlockSpec((None, TILE_IN, LANES), tab),
            pl.BlockSpec((None, TILE_IN, LANES), tab),
            pl.BlockSpec((POOL_GROUPS, POOL_CH, POOL_CH), lambda i: (0, 0, 0)),
            pl.BlockSpec((1, POOL_W), const2),
        ] + prev_specs + [c[1](step) for c in casts],
        out_specs=(pl.BlockSpec((TILE_IN, QK_W), rows),
                   pl.BlockSpec((TILE_IN, QK_W), rows),
                   pl.BlockSpec((TILE_IN, V_W), rows),
                   pl.BlockSpec((seqs, n_prev + 1, SEQ, KV_LORA), ctx_blk4),
                   pl.BlockSpec((seqs, n_prev + 1, QK_ROPE, SEQ), ctx_blk4),
                   pl.BlockSpec((TILE_IN, POOL_W), rows),
                   pl.BlockSpec((TILE_IN, V_W), lambda i: ctx_blk4(i)[:2])) + tuple(c[3](step) for c in casts),
        scratch_shapes=[pltpu.VMEM((TILE_IN // CHUNK * REGION, POOL_W), F32)],
        compiler_params=_cparams("arbitrary"),
        name="in_proj",
    )(x_ctx, x_lat, ada, ada, g1, w_in, qg, kvg, w_uq, w_uk, w_uv, tq, tc, ts, w_pool, pscale, *prev_args,
      *(c[0] for c in casts))


def _softmax_pv(scores, values):
    m = scores[0].max(axis=-1, keepdims=True)
    for s in scores[1:]:
        m = jnp.maximum(m, s.max(axis=-1, keepdims=True))
    den = None
    out = None
    for s, v in zip(scores, values):
        p = jnp.exp(s - m)
        d = p.sum(axis=-1, keepdims=True)
        o = _dot(p.astype(BF16), v)
        den = d if den is None else den + d
        out = o if out is None else out + o
    return out / den


def _attn_ctx_rows(q_ref, k_ref, v_ref, o_ref):
    for b in range(q_ref.shape[0] // SEQ):
        rows = slice(b * SEQ, (b + 1) * SEQ)
        for hd in range(N_HEADS):
            qk = slice(hd * HEAD_PAD, (hd + 1) * HEAD_PAD)
            vv = slice(hd * V_DIM, (hd + 1) * V_DIM)
            s = _dot_nt(q_ref[rows, qk], k_ref[rows, qk])
            o_ref[rows, vv] = _softmax_pv([s], [v_ref[rows, vv]]).astype(BF16)


def _cast_slab(n_steps, w):
    n_rows, n_cols = w.shape
    slab = n_rows // n_steps
    assert slab * n_steps == n_rows
    in_spec = lambda step_of: pl.BlockSpec((slab, n_cols), lambda *g: (step_of(*g), 0))
    return w, in_spec, jax.ShapeDtypeStruct(w.shape, BF16), in_spec


def _attn_lat_kernel(q_ref, k_ref, v_ref, kc_ref, vc_ref, *rest):
    if len(rest) == 3:
        w_ref, o_ref, wb_ref = rest
        wb_ref[...] = w_ref[...].astype(BF16)
    else:
        (o_ref,) = rest
    for hd in range(N_HEADS):
        qk = slice(hd * HEAD_PAD, (hd + 1) * HEAD_PAD)
        vv = slice(hd * V_DIM, (hd + 1) * V_DIM)
        q = q_ref[:, qk]
        s1 = _dot_nt(q, k_ref[:, qk])
        s2 = _dot_nt(q, kc_ref[:, qk])
        o_ref[:, vv] = _softmax_pv([s1, s2], [v_ref[:, vv], vc_ref[:, vv]]).astype(BF16)


def _attn_lat_call(layer, q, k, v, kc, vc, cast_w=None):
    tq = LAT_Q_TILE if cast_w is None else LAT_Q_TILE // 2
    n_q = DEC_SEQ // tq
    lat0 = T_CTX // DEC_SEQ
    in_specs = [pl.BlockSpec((tq, QK_W), lambda b, i: (T_CTX // tq + b * n_q + i, 0)),
                pl.BlockSpec((DEC_SEQ, QK_W), lambda b, i: (lat0 + b, 0)),
                pl.BlockSpec((DEC_SEQ, V_W), lambda b, i: (lat0 + b, 0)),
                pl.BlockSpec((None, PAST_LEN, QK_W), lambda b, i: (layer, b, 0)),
                pl.BlockSpec((None, PAST_LEN, V_W), lambda b, i: (layer, b, 0))]
    out_shape = [jax.ShapeDtypeStruct((T_LAT, V_W), BF16)]
    out_specs = [pl.BlockSpec((tq, V_W), lambda b, i: (b * n_q + i, 0))]
    args = [q, k, v, kc, vc]
    if cast_w is not None:
        w, w_in_spec, wb_shape, wb_spec = _cast_slab(DEC_BATCH * n_q, cast_w)
        step_of = lambda b, i: b * n_q + i
        in_specs.append(w_in_spec(step_of))
        out_shape.append(wb_shape)
        out_specs.append(wb_spec(step_of))
        args.append(w)
    outs = pl.pallas_call(
        _attn_lat_kernel,
        out_shape=tuple(out_shape),
        grid=(DEC_BATCH, n_q),
        in_specs=in_specs,
        out_specs=tuple(out_specs),
        compiler_params=_cparams("arbitrary", "arbitrary"),
        name="attn_lat",
    )(*args)
    return outs


def _swiglu_hidden(h, wg_ref, wu_ref, a_buf):
    width = a_buf.shape[1]
    for c0 in range(0, width, FF_CHUNK):
        cols = slice(c0, min(c0 + FF_CHUNK, width))
        a_buf[:, cols] = (_silu(_dot(h, wg_ref[:, cols])) * _dot(h, wu_ref[:, cols])).astype(BF16)


def _mix_kernel(with_router, n_cast, attn_c_ref, attn_l_ref, pool_ref, xc_ref, xl_ref, g1_ref, sh2_ref, sc2_ref, n2_ref, wo_ref,
                *rest):
    if with_router:
        wr_ref, ltri_ref, utri_ref, *rest = rest
        cast_in, (x_out, xs_out, gs_out, route_out, meta_out, *cast_out) = rest[:n_cast], rest[n_cast:]
    else:
        g2_ref, wg_ref, wu_ref, wd_ref, *rest = rest
        cast_in, (x_out, *cast_out, a_buf) = rest[:n_cast], rest[n_cast:]
    for w_ref, wb_ref in zip(cast_in, cast_out):
        wb_ref[...] = w_ref[...].astype(BF16)
    is_ctx = pl.program_id(0) < T_CTX // TILE_MIX
    halves = []
    for r0 in range(0, TILE_MIX, TILE_MIX // 2):
        rows = slice(r0, r0 + TILE_MIX // 2)
        attn = jnp.where(is_ctx, attn_c_ref[rows, :], attn_l_ref[rows, :])
        y = _dot(attn, wo_ref[:V_W, :]) + _dot(pool_ref[rows, :], wo_ref[V_W:, :])
        x_new = jnp.where(is_ctx, xc_ref[rows, :], xl_ref[rows, :]) + g1_ref[...] * y
        x_out[rows, :] = x_new
        halves.append((_rms(x_new) * n2_ref[...] * (1.0 + sc2_ref[...]) + sh2_ref[...]).astype(BF16))
    h = jnp.concatenate(halves, axis=0)
    if not with_router:
        _swiglu_hidden(h, wg_ref, wu_ref, a_buf)
        x_out[...] += g2_ref[...] * _dot(a_buf[...], wd_ref[...])
        return

    logits = _dot(h, wr_ref[...])
    lane = lax.broadcasted_iota(jnp.int32, logits.shape, 1)
    neg = float(jnp.finfo(F32).min)
    lg = jnp.where(lane < N_EXPERTS, logits, neg)
    m1 = lg.max(axis=-1, keepdims=True)
    i1 = jnp.where(lg == m1, lane, LANES).min(axis=-1, keepdims=True)
    lg2 = jnp.where(lane == i1, neg, lg)
    m2 = lg2.max(axis=-1, keepdims=True)
    i2 = jnp.where(lg2 == m2, lane, LANES).min(axis=-1, keepdims=True)
    e = jnp.exp(m2 - m1)
    w1 = 1.0 / (1.0 + e)
    w2 = e / (1.0 + e)

    sel1 = lane == i1
    sel2 = lane == i2
    member = jnp.where(jnp.logical_or(sel1, sel2), 1.0, 0.0)
    rank = _dot(ltri_ref[...], member.astype(BF16))
    n_tok = member.sum(axis=0, keepdims=True)
    units = jnp.floor((n_tok + (SLOT_UNIT - 1)) * (1.0 / SLOT_UNIT))
    unit_off = _dot(jnp.broadcast_to(units, (8, LANES)).astype(BF16), utri_ref[...])[0:1, :]
    slot_of = SLOT_UNIT * unit_off + rank
    slot1 = jnp.where(sel1, slot_of, 0.0).sum(axis=-1, keepdims=True)
    slot2 = jnp.where(sel2, slot_of, 0.0).sum(axis=-1, keepdims=True)
    route = jnp.where(lane == 0, slot1, jnp.where(lane == 1, slot2,
                      jnp.where(lane == 2, w1, jnp.where(lane == 3, w2, 0.0))))
    route_out[...] = route
    sub = lax.broadcasted_iota(jnp.int32, (8, LANES), 0)
    meta_out[...] = jnp.where(sub == 0, units, jnp.where(sub == 1, unit_off, 0.0)).astype(jnp.int32)

    rt = route.T
    s1 = rt[0:1, :].astype(jnp.int32)
    s2 = rt[1:2, :].astype(jnp.int32)
    for r0 in range(0, SLOTS_PER_TILE, SLOT_BLOCK):
        blk = slice(r0, r0 + SLOT_BLOCK)
        srow = r0 + lax.broadcasted_iota(jnp.int32, (SLOT_BLOCK, TILE_MIX), 0)
        hit1 = srow == s1
        hit2 = srow == s2
        perm = (jnp.where(hit1, 1.0, 0.0) + jnp.where(hit2, 1.0, 0.0)).astype(BF16)
        xs_out[blk, :] = _dot(perm, h).astype(BF16)
        gate = (jnp.where(hit1, rt[2:3, :], 0.0) + jnp.where(hit2, rt[3:4, :], 0.0)).sum(axis=-1, keepdims=True)
        gs_out[blk, :] = jnp.broadcast_to(gate, (SLOT_BLOCK, LANES))


def _mod_row(i, tile):
    n_ctx = T_CTX // tile
    per_seq = DEC_SEQ // tile
    return jnp.where(i < n_ctx, 0, 1 + (i - n_ctx) // per_seq)


def _mix_call(attn_ctx, attn_lat, pooled, x_ctx, x_lat, ada, n2, w_out, cast_ws, w_router=None, ffn_w=None):
    with_router = w_router is not None
    assert with_router != (ffn_w is not None)
    tm = TILE_MIX
    n_ctx = T_CTX // tm
    lat_tile0 = n_ctx if x_lat.shape[0] == T_ALL else 0
    row = lambda i: (i, 0)
    const2 = lambda i: (0, 0)
    mod = lambda k: pl.BlockSpec((None, 1, D_MODEL), lambda i: (_mod_row(i, tm), 0, k))
    ctx_blk = lambda i: (jnp.minimum(i, n_ctx - 1), 0)
    in_specs = [pl.BlockSpec((tm, V_W), ctx_blk),
                pl.BlockSpec((tm, V_W), lambda i: (jnp.maximum(i - n_ctx, 0), 0)),
                pl.BlockSpec((tm, POOL_W), row),
                pl.BlockSpec((tm, D_MODEL), ctx_blk),
                pl.BlockSpec((tm, D_MODEL), lambda i: (jnp.maximum(i - n_ctx, 0) + lat_tile0, 0)),
                mod(2), mod(3), mod(4),
                pl.BlockSpec((1, D_MODEL), const2), pl.BlockSpec((D_MODEL, D_MODEL), const2)]
    args = [attn_ctx, attn_lat, pooled, x_ctx, x_lat, ada, ada, ada, n2, w_out]
    if with_router:
        t_i = jnp.arange(tm)
        ltri = (t_i[None, :] < t_i[:, None]).astype(BF16)
        l_i = jnp.arange(LANES)
        utri = (l_i[:, None] < l_i[None, :]).astype(BF16)
        in_specs += [pl.BlockSpec((D_MODEL, LANES), const2), pl.BlockSpec((tm, tm), const2),
                     pl.BlockSpec((LANES, LANES), const2)]
        args += [w_router, ltri, utri]
        out_shape = [jax.ShapeDtypeStruct((T_ALL, D_MODEL), F32),
                     jax.ShapeDtypeStruct((N_ROUTE_TILES * SLOTS_PER_TILE, D_MODEL), BF16),
                     jax.ShapeDtypeStruct((N_ROUTE_TILES * SLOTS_PER_TILE, LANES), F32),
                     jax.ShapeDtypeStruct((T_ALL, LANES), F32),
                     jax.ShapeDtypeStruct((N_ROUTE_TILES, 8, LANES), jnp.int32)]
        out_specs = [pl.BlockSpec((tm, D_MODEL), row), pl.BlockSpec((SLOTS_PER_TILE, D_MODEL), row),
                     pl.BlockSpec((SLOTS_PER_TILE, LANES), row), pl.BlockSpec((tm, LANES), row),
                     pl.BlockSpec((None, 8, LANES), lambda i: (i, 0, 0))]
        scratch = []
    else:
        resident = lambda shape: pl.BlockSpec(shape, const2, pipeline_mode=pl.Buffered(1))
        in_specs += [mod(5), resident((D_MODEL, D_FF)), resident((D_MODEL, D_FF)), resident((D_FF, D_MODEL))]
        args += [ada, *ffn_w]
        out_shape = [jax.ShapeDtypeStruct((T_ALL, D_MODEL), F32)]
        out_specs = [pl.BlockSpec((tm, D_MODEL), row)]
        scratch = [pltpu.VMEM((tm, D_FF), BF16)]
    for w, w_spec, wb_shape, wb_spec in (_cast_slab(T_ALL // tm, w) for w in cast_ws):
        in_specs.append(w_spec(lambda i: i))
        args.append(w)
        out_shape.append(wb_shape)
        out_specs.append(wb_spec(lambda i: i))
    return pl.pallas_call(
        functools.partial(_mix_kernel, with_router, len(cast_ws)),
        out_shape=tuple(out_shape),
        grid=(T_ALL // tm,),
        in_specs=in_specs,
        out_specs=tuple(out_specs),
        scratch_shapes=scratch,
        compiler_params=_cparams("arbitrary"),
        name="mix_router" if with_router else "mix_ffn",
    )(*args)


def _route_tables(meta):
    units = meta[:, 0, :N_EXPERTS]
    offs = meta[:, 1, :N_EXPERTS]
    cum = jnp.cumsum(units, axis=0)
    total = cum[-1]
    tiles_e = (total + GEMM_UNITS - 1) // GEMM_UNITS
    tile_end = jnp.cumsum(tiles_e)
    n_act = tile_end[-1]
    m = jnp.arange(N_GEMM_TILES)
    m_eff = jnp.minimum(m, jnp.maximum(n_act - 1, 0))
    expert = jnp.minimum(jnp.sum(tile_end[None, :] <= m_eff[:, None], axis=1), N_EXPERTS - 1)
    is_e = (expert[:, None] == jnp.arange(N_EXPERTS)[None, :]).astype(jnp.int32)
    pick = lambda per_expert: jnp.sum(is_e * per_expert[None, :], axis=1)
    first_q = (m_eff - pick(tile_end - tiles_e)) * GEMM_UNITS
    count = jnp.where(m < n_act, jnp.clip(pick(total) - first_q, 0, GEMM_UNITS), 0)
    q = first_q[:, None] + jnp.arange(GEMM_UNITS)[None, :]
    rows_of = lambda table: jnp.sum(is_e[:, None, :] * table[None, :, :], axis=2)
    cum_e, units_e, offs_e = rows_of(cum), rows_of(units), rows_of(offs)
    src_tile = jnp.minimum(jnp.sum(cum_e[:, None, :] <= q[:, :, None], axis=2), N_ROUTE_TILES - 1)
    is_t = (src_tile[:, :, None] == jnp.arange(N_ROUTE_TILES)[None, None, :]).astype(jnp.int32)
    at_tile = lambda per_tile: jnp.sum(is_t * per_tile[:, None, :], axis=2)
    uid = src_tile * UNITS_PER_TILE + at_tile(offs_e) + (q - at_tile(cum_e - units_e))
    uid = jnp.where(jnp.arange(GEMM_UNITS)[None, :] < count[:, None], uid, ZERO_UNIT)
    i32 = jnp.int32
    return expert.astype(i32), count.astype(i32), uid.reshape(-1).astype(i32), n_act.reshape(1).astype(i32)


def _gemm_kernel(em_ref, cnt_ref, ul_ref, nact_ref, xs_hbm, gs_hbm, wg_ref, wu_ref, wd_ref, ys_hbm,
                 xbuf, gbuf, obuf, acc_ref, a_buf, sem_in, sem_out):
    del em_ref
    m = pl.program_id(0)
    j = pl.program_id(1)
    last_j = pl.num_programs(1) - 1
    n_act = nact_ref[0]
    active = m < n_act
    slot = m % 2

    def rows(r):
        return pl.ds(r * SLOT_UNIT if isinstance(r, int) else pl.multiple_of(r * SLOT_UNIT, SLOT_UNIT), SLOT_UNIT)

    def in_copies(mm, sl, r):
        uid = ul_ref[mm * GEMM_UNITS + r]
        return (pltpu.make_async_copy(xs_hbm.at[uid], xbuf.at[sl, rows(r)], sem_in.at[sl]),
                pltpu.make_async_copy(gs_hbm.at[uid], gbuf.at[sl, rows(r)], sem_in.at[sl]))

    def out_copy(mm, r):
        uid = ul_ref[mm * GEMM_UNITS + r]
        return pltpu.make_async_copy(obuf.at[rows(r)], ys_hbm.at[uid], sem_out.at[0])

    def for_slot(sl_dyn, fn):
        if isinstance(sl_dyn, int):
            fn(sl_dyn)
            return
        for sl in range(2):
            pl.when(sl_dyn == sl)(functools.partial(fn, sl))

    def start_in(mm, sl_dyn):
        def issue(sl):
            for r in range(GEMM_UNITS):
                for cp in in_copies(mm, sl, r):
                    cp.start()

        for_slot(sl_dyn, issue)

    def wait_in(mm, sl_dyn):
        def wait(sl):
            for r in range(GEMM_UNITS):
                for cp in in_copies(mm, sl, r):
                    cp.wait()

        for_slot(sl_dyn, wait)

    def start_out(mm):
        n = cnt_ref[mm]
        for r in range(GEMM_UNITS):
            pl.when(r < n)(lambda r=r: out_copy(mm, r).start())

    def wait_out(mm):
        def body(r, carry):
            out_copy(mm, r).wait()
            return carry

        lax.fori_loop(0, cnt_ref[mm], body, 0)

    @pl.when(jnp.logical_and(active, j == 0))
    def _():
        @pl.when(m == 0)
        def _():
            start_in(0, 0)

        wait_in(m, slot)

        @pl.when(m + 1 < n_act)
        def _():
            start_in(m + 1, 1 - slot)

    @pl.when(jnp.logical_and(jnp.logical_and(active, j == last_j), m > 0))
    def _():
        wait_out(m - 1)

    n_valid = cnt_ref[m]
    for quarter in range(1, GEMM_QUARTERS + 1):
        n_rows = quarter * GEMM_ROWS // GEMM_QUARTERS
        lo_units = (quarter - 1) * GEMM_UNITS // GEMM_QUARTERS
        hi_units = quarter * GEMM_UNITS // GEMM_QUARTERS

        @pl.when(jnp.logical_and(active, jnp.logical_and(n_valid > lo_units, n_valid <= hi_units)))
        def _(n_rows=n_rows):
            _swiglu_hidden(xbuf[slot, :n_rows, :], wg_ref, wu_ref, a_buf.at[:n_rows, :])
            part = _dot(a_buf[:n_rows, :], wd_ref[...])

            @pl.when(j == 0)
            def _():
                acc_ref[:n_rows, :] = part

            @pl.when(jnp.logical_and(j > 0, j < last_j))
            def _():
                acc_ref[:n_rows, :] += part

            @pl.when(j == last_j)
            def _():
                gate = gbuf[slot, :n_rows, :]
                for cb in range(D_MODEL // LANES):
                    cols = slice(cb * LANES, (cb + 1) * LANES)
                    obuf[:n_rows, cols] = ((acc_ref[:n_rows, cols] + part[:, cols]) * gate).astype(BF16)

    @pl.when(jnp.logical_and(active, j == last_j))
    def _():
        start_out(m)

        @pl.when(m == n_act - 1)
        def _():
            wait_out(m)


def _gemm_call(expert, count, uids, n_act, xs, gs, wg, wu, wd):
    tf = TILE_FE
    n_j = D_FF_EXPERT // tf
    assert n_j >= 2

    def w_col(m, j, em, cnt, ul, nact):
        return (em[m], 0, jnp.where(m < nact[0], j, n_j - 1))

    def w_row(m, j, em, cnt, ul, nact):
        return (em[m], jnp.where(m < nact[0], j, n_j - 1), 0)

    grid_spec = pltpu.PrefetchScalarGridSpec(
        num_scalar_prefetch=4,
        grid=(N_GEMM_TILES, n_j),
        in_specs=[pl.BlockSpec(memory_space=pl.ANY),
                  pl.BlockSpec(memory_space=pl.ANY),
                  pl.BlockSpec((None, D_MODEL, tf), w_col),
                  pl.BlockSpec((None, D_MODEL, tf), w_col),
                  pl.BlockSpec((None, tf, D_MODEL), w_row)],
        out_specs=pl.BlockSpec(memory_space=pl.ANY),
        scratch_shapes=[pltpu.VMEM((2, GEMM_ROWS, D_MODEL), BF16),
                        pltpu.VMEM((2, GEMM_ROWS, LANES), F32),
                        pltpu.VMEM((GEMM_ROWS, D_MODEL), BF16),
                        pltpu.VMEM((GEMM_ROWS, D_MODEL), F32),
                        pltpu.VMEM((GEMM_ROWS, tf), BF16),
                        pltpu.SemaphoreType.DMA((2,)),
                        pltpu.SemaphoreType.DMA((1,))])
    return pl.pallas_call(
        _gemm_kernel,
        out_shape=jax.ShapeDtypeStruct((N_UNITS, SLOT_UNIT, D_MODEL), BF16),
        grid_spec=grid_spec,
        input_output_aliases={4: 0},
        compiler_params=_cparams("arbitrary", "arbitrary"),
        name="moe_experts",
    )(expert, count, uids, n_act,
      xs.reshape(N_UNITS, SLOT_UNIT, D_MODEL), gs.reshape(N_UNITS, SLOT_UNIT, LANES), wg, wu, wd)


def _combine_kernel(ys_ref, route_ref, x_ref, g2_ref, fg_ref, o_ref):
    route = route_ref[...]
    s1 = route[:, 0:1].astype(jnp.int32)
    s2 = route[:, 1:2].astype(jnp.int32)
    scol = lax.broadcasted_iota(jnp.int32, (TILE_MIX, SLOTS_PER_TILE), 1)
    unperm = (jnp.where(scol == s1, 1.0, 0.0) + jnp.where(scol == s2, 1.0, 0.0)).astype(BF16)
    y = _dot(unperm, ys_ref[...])
    o_ref[...] = _rms(x_ref[...] + g2_ref[...] * y) * fg_ref[...]


def _combine_call(ys, route, x, ada, final_g, tile0, n_tok):
    tm = TILE_MIX
    return pl.pallas_call(
        _combine_kernel,
        out_shape=jax.ShapeDtypeStruct((n_tok, D_MODEL), F32),
        grid=(n_tok // tm,),
        in_specs=[pl.BlockSpec((SLOTS_PER_TILE, D_MODEL), lambda i: (i + tile0, 0)),
                  pl.BlockSpec((tm, LANES), lambda i: (i + tile0, 0)),
                  pl.BlockSpec((tm, D_MODEL), lambda i: (i + tile0, 0)),
                  pl.BlockSpec((None, 1, D_MODEL), lambda i: (_mod_row(i + tile0, tm), 0, 5)),
                  pl.BlockSpec((1, D_MODEL), lambda i: (0, 0))],
        out_specs=pl.BlockSpec((tm, D_MODEL), lambda i: (i, 0)),
        compiler_params=_cparams("arbitrary"),
        name="moe_combine",
    )(ys.reshape(N_UNITS * SLOT_UNIT, D_MODEL), route, x, ada, final_g)


def _rot_cols(w):
    q = QK_ROPE // 4
    src = jnp.arange(QK_ROPE)[:, None]
    dst = jnp.arange(QK_ROPE)[None, :]
    first_half = (dst // q) % 2 == 0
    rot = jnp.where(first_half, -(src == dst + q).astype(F32), (src == dst - q).astype(F32))
    return jnp.matmul(w, rot.astype(w.dtype))


def _rope_tables():
    t = jnp.arange(DEC_SEQ)
    rows = (t // GRID_W).astype(F32)
    cols = (t % GRID_W).astype(F32)
    half = QK_ROPE // 2
    freqs = ROPE_BASE ** (-jnp.arange(0, half, 2, dtype=F32) / half)
    ang_r = rows[:, None] * freqs
    ang_c = cols[:, None] * freqs
    ang = jnp.concatenate([ang_r, ang_r, ang_c, ang_c], axis=-1)
    cos, sin = jnp.cos(ang), jnp.sin(ang)
    one, zero = jnp.ones_like(cos), jnp.zeros_like(cos)
    tq = jnp.stack([jnp.concatenate([one, zero], -1), jnp.concatenate([cos, sin], -1)])
    tc = jnp.stack([jnp.concatenate([one, one], -1), jnp.concatenate([cos, cos], -1)])
    ts = jnp.stack([jnp.concatenate([zero, zero], -1), jnp.concatenate([sin, sin], -1)])
    return tq, tc, ts


def _layer_weights(w_in, w_uq, w_ukv):
    wi = w_in.astype(BF16)
    w_kpe = wi[..., OFF_U:OFF_U + QK_ROPE]
    w_rot = _rot_cols(w_kpe)
    w_in_r = jnp.concatenate([wi[..., :OFF_U], wi[..., OFF_U + QK_ROPE:], w_kpe, w_kpe, w_rot, w_rot], axis=-1)
    wq = w_uq.astype(BF16).reshape(DEPTH, Q_LORA, N_HEADS, QK_NOPE + QK_ROPE)
    w_uq_r = jnp.concatenate([wq, _rot_cols(wq[..., QK_NOPE:])], axis=-1).reshape(DEPTH, Q_LORA, QK_W)
    kv = w_ukv.astype(BF16).reshape(DEPTH, KV_LORA, N_HEADS, QK_NOPE + V_DIM)
    w_uk = kv[..., :QK_NOPE].reshape(DEPTH, KV_LORA, N_HEADS * QK_NOPE)
    w_uv = kv[..., QK_NOPE:].reshape(DEPTH, KV_LORA, V_W)
    return w_in_r, w_uq_r, w_uk, w_uv


def kernel(x_prompt, x_sample, c, cache_ckv, cache_kpe, c_ctx, norm1_g, norm2_g, w_ada, b_ada, w_in, q_norm_g,
           kv_norm_g, w_uq, w_ukv, w_pool, pool_scale, w_out, ffn_w_gate, ffn_w_up, ffn_w_down, moe_w_router,
           moe_w_gate, moe_w_up, moe_w_down, final_norm_g):
    x_ctx = x_prompt.reshape(T_CTX, D_MODEL)
    x_lat = x_sample.reshape(T_LAT, D_MODEL)
    cond = jnp.concatenate([c_ctx[None, :], c, jnp.zeros((COND_ROWS - 1 - DEC_BATCH, D_MODEL), F32)], axis=0)
    ada_all = _ada_call(cond, w_ada, b_ada)
    tq, tc, ts = _rope_tables()

    w_in_all, w_uq_all, w_uk_all, w_uv_all = _layer_weights(w_in, w_uq, w_ukv)
    dup = jnp.concatenate([jnp.eye(QK_ROPE, dtype=BF16)] * 2, axis=1)
    kc_all, vc_all = _cache_call(cache_ckv, cache_kpe, w_uk_all, w_uv_all, dup)

    assert DEPTH == 2
    moe_gate_2d = moe_w_gate[0].reshape(N_EXPERTS * D_MODEL, D_FF_EXPERT)
    moe_up_2d = moe_w_up[0].reshape(N_EXPERTS * D_MODEL, D_FF_EXPERT)
    moe_down_2d = moe_w_down[0].reshape(N_EXPERTS * D_FF_EXPERT, D_MODEL)
    new_ckv = new_kpe = None
    for l in range(DEPTH):
        ada = ada_all[l].reshape(COND_ROWS, 1, 6 * D_MODEL)
        cast_ws = (ffn_w_gate[0], ffn_w_up[0], ffn_w_down[0],
                   w_out.reshape(DEPTH * (V_W + POOL_W), D_MODEL)) if l == 0 else ()
        q, k, v, new_ckv, new_kpe, pooled, attn_c, *cast = _in_call(
            l, x_ctx, x_lat, ada, norm1_g[l][None, :], w_in_all, q_norm_g[l][None, :], kv_norm_g[l][None, :],
            w_uq_all, w_uk_all, w_uv_all, tq, tc, ts, w_pool[l].astype(BF16), pool_scale[l][None, :], new_ckv,
            new_kpe, cast_ws)
        if l == 0:
            ffn_w, w_out_b = cast[:3], cast[3].reshape(DEPTH, V_W + POOL_W, D_MODEL)
        j = l // 2
        if l % 2 == 0:
            attn_l, moe_wd = _attn_lat_call(l, q, k, v, kc_all, vc_all, cast_w=moe_down_2d)
            x, moe_wg = _mix_call(attn_c, attn_l, pooled, x_ctx, x_lat, ada, norm2_g[l][None, :],
                                  w_out_b[l], (moe_gate_2d,), ffn_w=ffn_w)
            x_ctx = x_lat = x
        else:
            (attn_l,) = _attn_lat_call(l, q, k, v, kc_all, vc_all)
            w_r = jnp.pad(moe_w_router[j], ((0, 0), (0, LANES - N_EXPERTS))).astype(BF16)
            x, xs, gs, route, meta, moe_wu = _mix_call(attn_c, attn_l, pooled, x_ctx, x_lat, ada,
                                                       norm2_g[l][None, :], w_out_b[l], (moe_up_2d,),
                                                       w_router=w_r)
            expert, count, uids, n_act = _route_tables(meta)
            ys = _gemm_call(expert, count, uids, n_act, xs, gs,
                            moe_wg.reshape(N_EXPERTS, D_MODEL, D_FF_EXPERT),
                            moe_wu.reshape(N_EXPERTS, D_MODEL, D_FF_EXPERT),
                            moe_wd.reshape(N_EXPERTS, D_FF_EXPERT, D_MODEL))
            fg = final_norm_g[None, :]
            y_prompt = _combine_call(ys, route, x, ada, fg, 0, T_CTX).reshape(BATCH, SEQ, D_MODEL)
            y_sample = _combine_call(ys, route, x, ada, fg, T_CTX // TILE_MIX, T_LAT).reshape(
                DEC_BATCH, DEC_SEQ, D_MODEL)

    return y_prompt, y_sample, new_ckv, jnp.swapaxes(new_kpe, -1, -2)
```

```python
import functools

import jax
import jax.numpy as jnp
from jax import lax
from jax.experimental import pallas as pl
from jax.experimental.pallas import tpu as pltpu

D_MODEL = 1024
BATCH = 32
SEQ = 256
DEPTH = 2
DEC_BATCH = 8
DEC_SEQ = 1024
PAST_LEN = 512
GRID_W = 64
N_HEADS = 4
QK_NOPE = 128
QK_ROPE = 64
V_DIM = 128
Q_LORA = 384
KV_LORA = 256
POOL_W = 512
POOL_GROUPS = 4
POOL_WINDOWS = (2, 4, 8, 16)
POOL_CH = POOL_W // POOL_GROUPS
D_FF = 2816
N_EXPERTS = 8
D_FF_EXPERT = 3584
ROPE_BASE = 10000.0
EPS = 1e-6

T_CTX = BATCH * SEQ
T_LAT = DEC_BATCH * DEC_SEQ
T_ALL = T_CTX + T_LAT

LANES = 128
HEAD_PAD = 256
QK_W = N_HEADS * HEAD_PAD
V_W = N_HEADS * V_DIM
OFF_CKV = Q_LORA
OFF_U = Q_LORA + KV_LORA
OFF_KA = OFF_U + POOL_W
OFF_KB = OFF_KA + LANES
N_PROJ = OFF_KB + LANES

TILE_IN = 1024
CHUNK = 256
HALO = 8
REGION = CHUNK + 2 * HALO
LAT_Q_TILE = 1024
TILE_MIX = 512
TOP_K = 2
SLOT_UNIT = 16
UNITS_PER_TILE = TILE_MIX * TOP_K // SLOT_UNIT + N_EXPERTS
SLOTS_PER_TILE = UNITS_PER_TILE * SLOT_UNIT
SLOT_BLOCK = SLOTS_PER_TILE // 4
N_ROUTE_TILES = T_ALL // TILE_MIX
N_UNITS = N_ROUTE_TILES * UNITS_PER_TILE
ZERO_UNIT = UNITS_PER_TILE - 1
GEMM_UNITS = 64
GEMM_ROWS = GEMM_UNITS * SLOT_UNIT
GEMM_QUARTERS = 4
N_GEMM_TILES = -(-N_UNITS // GEMM_UNITS) + N_EXPERTS
TILE_FE = D_FF_EXPERT // 2
FF_CHUNK = 256
COND_ROWS = 16
VMEM_LIMIT = 56 * 1024 * 1024

F32 = jnp.float32
BF16 = jnp.bfloat16


def _rms(x):
    return x * lax.rsqrt(jnp.mean(x * x, axis=-1, keepdims=True) + EPS)


def _dot(a, b):
    return jnp.dot(a, b, preferred_element_type=F32)


def _dot_nt(a, b):
    return lax.dot_general(a, b, (((1,), (1,)), ((), ())), preferred_element_type=F32)


def _silu(x):
    return x * (1.0 / (1.0 + jnp.exp(-x)))


def _cparams(*sem):
    return pltpu.CompilerParams(dimension_semantics=sem, vmem_limit_bytes=VMEM_LIMIT)


def _ada_kernel(cond_ref, w_ref, b_ref, o_ref):
    k = pl.program_id(1)
    part = _dot(_silu(cond_ref[...]).astype(BF16), w_ref[...].astype(BF16))

    @pl.when(k == 0)
    def _():
        o_ref[...] = part + b_ref[...]

    @pl.when(k > 0)
    def _():
        o_ref[...] += part


def _ada_call(cond, w_ada, b_ada):
    n_blk = 4
    kb = D_MODEL // n_blk
    return pl.pallas_call(
        _ada_kernel,
        out_shape=jax.ShapeDtypeStruct((DEPTH, COND_ROWS, 6 * D_MODEL), F32),
        grid=(DEPTH, n_blk),
        in_specs=[
            pl.BlockSpec((COND_ROWS, kb), lambda l, k: (0, k)),
            pl.BlockSpec((None, kb, 6 * D_MODEL), lambda l, k: (l, k, 0)),
            pl.BlockSpec((None, 1, 6 * D_MODEL), lambda l, k: (l, 0, 0)),
        ],
        out_specs=pl.BlockSpec((None, COND_ROWS, 6 * D_MODEL), lambda l, k: (l, 0, 0)),
        compiler_params=_cparams("arbitrary", "arbitrary"),
        name="ada_params",
    )(cond, w_ada, b_ada.reshape(DEPTH, 1, 6 * D_MODEL))


def _expand_cache(ckv_ref, kpe_ref, wuk_ref, wuv_ref, dup_ref, k_ref, v_ref):
    ckv = ckv_ref[...].astype(BF16)
    knope = _dot(ckv, wuk_ref[...])
    v_ref[...] = _dot(ckv, wuv_ref[...]).astype(BF16)
    kpe = kpe_ref[...].T
    kdup = _dot(kpe.astype(BF16), dup_ref[...]).astype(BF16)
    for h in range(N_HEADS):
        k_ref[:, h * HEAD_PAD:h * HEAD_PAD + QK_NOPE] = knope[:, h * QK_NOPE:(h + 1) * QK_NOPE].astype(BF16)
        k_ref[:, h * HEAD_PAD + QK_NOPE:(h + 1) * HEAD_PAD] = kdup


def _in_kernel(n_prev, n_cast, xc_ref, xl_ref, sh_ref, sc_ref, g_ref, win_ref, qg_ref, kvg_ref, wuq_ref, wuk_ref,
               wuv_ref, tq_ref, tc_ref, ts_ref, wpool_ref, ps_ref, *rest):
    if n_prev:
        prev_ckv_ref, prev_kpe_ref, *rest = rest
    cast_in, rest = rest[:n_cast], rest[n_cast:]
    q_out, k_out, v_out, ckv_out, kpe_out, pool_out, attn_out, *rest = rest
    cast_out, (u_scr,) = rest[:n_cast], rest[n_cast:]
    for w_ref, wb_ref in zip(cast_in, cast_out):
        wb_ref[...] = w_ref[...].astype(BF16)
    is_ctx = pl.program_id(0) >= (T_ALL - T_CTX) // TILE_IN
    if n_prev:
        ckv_out[:, :n_prev] = prev_ckv_ref[...]
        kpe_out[:, :n_prev] = prev_kpe_ref[...]
    shift = sh_ref[...]
    scale1 = 1.0 + sc_ref[...]
    qk_scale = (QK_NOPE + QK_ROPE) ** -0.5
    n_chunks = TILE_IN // CHUNK
    zeros = jnp.zeros((HALO, POOL_W), F32)
    seq_len = jnp.where(is_ctx, SEQ, DEC_SEQ)
    row = lax.broadcasted_iota(jnp.int32, (CHUNK, POOL_CH), 0)

    def pool_chunk(c):
        base = c * REGION + HALO
        rows = slice(c * CHUNK, (c + 1) * CHUNK)
        t = row + jnp.where(is_ctx, 0, c * CHUNK)
        for g, w in enumerate(POOL_WINDOWS):
            cols = slice(g * POOL_CH, (g + 1) * POOL_CH)
            acc = u_scr[base - w // 2:base - w // 2 + CHUNK, cols]
            for j in range(-w // 2 + 1, w // 2):
                acc = acc + u_scr[base + j:base + j + CHUNK, cols]
            cnt = jnp.minimum(t + w // 2, seq_len) - jnp.maximum(t - w // 2, 0)
            pooled = acc / cnt.astype(F32) - u_scr[base:base + CHUNK, cols]
            lin = _dot(pooled.astype(BF16), wpool_ref[g]) * ps_ref[:, cols]
            pool_out[rows, cols] = lin.astype(BF16)

    for c in range(n_chunks):
        r0 = c * CHUNK
        rows = slice(r0, r0 + CHUNK)
        x = jnp.where(is_ctx, xc_ref[rows, :], xl_ref[rows, :])
        h = (_rms(x) * g_ref[...] * scale1 + shift).astype(BF16)
        proj = _dot(h, win_ref[...])

        qn = (_rms(proj[:, :Q_LORA]) * qg_ref[...]).astype(BF16)
        q = _dot(qn, wuq_ref[...]) * qk_scale
        tq = tq_ref[rows, :]
        for hd in range(N_HEADS):
            lo = hd * HEAD_PAD
            q_out[rows, lo:lo + QK_NOPE] = q[:, lo:lo + QK_NOPE].astype(BF16)
            q_out[rows, lo + QK_NOPE:lo + HEAD_PAD] = (q[:, lo + QK_NOPE:lo + HEAD_PAD] * tq).astype(BF16)

        ckv = _rms(proj[:, OFF_CKV:OFF_U]) * kvg_ref[...]
        ckv_out[c, n_prev] = ckv
        ckv_b = ckv.astype(BF16)
        knope = _dot(ckv_b, wuk_ref[...])
        v_out[rows, :] = _dot(ckv_b, wuv_ref[...]).astype(BF16)

        k_a = proj[:, OFF_KA:OFF_KB]
        k_b = proj[:, OFF_KB:N_PROJ]
        kpe_out[c, n_prev] = k_a.T[:QK_ROPE, :]
        kr = (k_a * tc_ref[rows, :] + k_b * ts_ref[rows, :]).astype(BF16)
        for hd in range(N_HEADS):
            lo = hd * HEAD_PAD
            k_out[rows, lo:lo + QK_NOPE] = knope[:, hd * QK_NOPE:(hd + 1) * QK_NOPE].astype(BF16)
            k_out[rows, lo + QK_NOPE:lo + HEAD_PAD] = kr

        u = proj[:, OFF_U:OFF_KA]
        base = c * REGION + HALO
        u_scr[base:base + CHUNK, :] = u
        if c == 0:
            u_scr[0:HALO, :] = zeros
        else:
            u_scr[base - 2 * HALO:base - HALO, :] = jnp.where(is_ctx, zeros, u[:HALO, :])
        if c == n_chunks - 1:
            u_scr[base + CHUNK:base + CHUNK + HALO, :] = zeros
        else:
            u_scr[base + CHUNK + HALO:base + CHUNK + 2 * HALO, :] = jnp.where(is_ctx, zeros, u[CHUNK - HALO:, :])
        if c > 0:
            pool_chunk(c - 1)
    pool_chunk(n_chunks - 1)

    @pl.when(is_ctx)
    def _():
        _attn_ctx_rows(q_out, k_out, v_out, attn_out)


def _in_call(layer, x_ctx, x_lat, ada, g1, w_in, qg, kvg, w_uq, w_uk, w_uv, tq, tc, ts, w_pool, pscale, prev_ckv, prev_kpe,
             cast_ws=()):
    assert CHUNK == SEQ
    n_tiles = T_ALL // TILE_IN
    n_ctx = T_CTX // TILE_IN
    n_prev = 0 if prev_ckv is None else prev_ckv.shape[1]
    seqs = TILE_IN // SEQ
    lat_tile0 = n_ctx if x_lat.shape[0] == T_ALL else 0

    def tile(i):
        return jnp.where(i < n_tiles - n_ctx, i + n_ctx, i - (n_tiles - n_ctx))

    def cond_row(i):
        return jnp.maximum(tile(i) - n_ctx + 1, 0)

    def tab(i):
        return (jnp.minimum(jnp.maximum(tile(i) - n_ctx + 1, 0), 1), 0, 0)

    rows = lambda i: (tile(i), 0)
    ctx_blk4 = lambda i: (jnp.where(tile(i) < n_ctx, tile(i), 0), 0, 0, 0)
    const2 = lambda i: (0, 0)
    of_layer = lambda i: (layer, 0, 0)
    prev_specs = [pl.BlockSpec((seqs, n_prev, SEQ, KV_LORA), ctx_blk4),
                  pl.BlockSpec((seqs, n_prev, QK_ROPE, SEQ), ctx_blk4)] if n_prev else []
    prev_args = [prev_ckv, prev_kpe] if n_prev else []
    casts = [_cast_slab(n_tiles, w) for w in cast_ws]
    step = lambda i: i
    return pl.pallas_call(
        functools.partial(_in_kernel, n_prev, len(casts)),
        out_shape=(jax.ShapeDtypeStruct((T_ALL, QK_W), BF16),
                   jax.ShapeDtypeStruct((T_ALL, QK_W), BF16),
                   jax.ShapeDtypeStruct((T_ALL, V_W), BF16),
                   jax.ShapeDtypeStruct((BATCH, n_prev + 1, SEQ, KV_LORA), F32),
                   jax.ShapeDtypeStruct((BATCH, n_prev + 1, QK_ROPE, SEQ), F32),
                   jax.ShapeDtypeStruct((T_ALL, POOL_W), BF16),
                   jax.ShapeDtypeStruct((T_CTX, V_W), BF16)) + tuple(c[2] for c in casts),
        grid=(n_tiles,),
        in_specs=[
            pl.BlockSpec((TILE_IN, D_MODEL), lambda i: (jnp.minimum(tile(i), n_ctx - 1), 0)),
            pl.BlockSpec((TILE_IN, D_MODEL), lambda i: (jnp.maximum(tile(i) - n_ctx, 0) + lat_tile0, 0)),
            pl.BlockSpec((None, 1, D_MODEL), lambda i: (cond_row(i), 0, 0)),
            pl.BlockSpec((None, 1, D_MODEL), lambda i: (cond_row(i), 0, 1)),
            pl.BlockSpec((1, D_MODEL), const2),
            pl.BlockSpec((None, D_MODEL, N_PROJ), of_layer),
            pl.BlockSpec((1, Q_LORA), const2),
            pl.BlockSpec((1, KV_LORA), const2),
            pl.BlockSpec((None, Q_LORA, QK_W), of_layer),
            pl.BlockSpec((None, KV_LORA, N_HEADS * QK_NOPE), of_layer),
            pl.BlockSpec((None, KV_LORA, V_W), of_layer),
            pl.BlockSpec((None, TILE_IN, LANES), tab),
            pl.BlockSpec((None, TILE_IN, LANES), tab),
            pl.BlockSpec((None, TILE_IN, LANES), tab),
            pl.BlockSpec((POOL_GROUPS, POOL_CH, POOL_CH), lambda i: (0, 0, 0)),
            pl.BlockSpec((1, POOL_W), const2),
        ] + prev_specs + [c[1](step) for c in casts],
        out_specs=(pl.BlockSpec((TILE_IN, QK_W), rows),
                   pl.BlockSpec((TILE_IN, QK_W), rows),
                   pl.BlockSpec((TILE_IN, V_W), rows),
                   pl.BlockSpec((seqs, n_prev + 1, SEQ, KV_LORA), ctx_blk4),
                   pl.BlockSpec((seqs, n_prev + 1, QK_ROPE, SEQ), ctx_blk4),
                   pl.BlockSpec((TILE_IN, POOL_W), rows),
                   pl.BlockSpec((TILE_IN, V_W), lambda i: ctx_blk4(i)[:2])) + tuple(c[3](step) for c in casts),
        scratch_shapes=[pltpu.VMEM((TILE_IN // CHUNK * REGION, POOL_W), F32)],
        compiler_params=_cparams("arbitrary"),
        name="in_proj",
    )(x_ctx, x_lat, ada, ada, g1, w_in, qg, kvg, w_uq, w_uk, w_uv, tq, tc, ts, w_pool, pscale, *prev_args,
      *(c[0] for c in casts))


def _softmax_pv(scores, values):
    m = scores[0].max(axis=-1, keepdims=True)
    for s in scores[1:]:
        m = jnp.maximum(m, s.max(axis=-1, keepdims=True))
    den = None
    out = None
    for s, v in zip(scores, values):
        p = jnp.exp(s - m)
        d = p.sum(axis=-1, keepdims=True)
        o = _dot(p.astype(BF16), v)
        den = d if den is None else den + d
        out = o if out is None else out + o
    return out / den


def _attn_ctx_rows(q_ref, k_ref, v_ref, o_ref):
    for b in range(q_ref.shape[0] // SEQ):
        rows = slice(b * SEQ, (b + 1) * SEQ)
        for hd in range(N_HEADS):
            qk = slice(hd * HEAD_PAD, (hd + 1) * HEAD_PAD)
            vv = slice(hd * V_DIM, (hd + 1) * V_DIM)
            s = _dot_nt(q_ref[rows, qk], k_ref[rows, qk])
            o_ref[rows, vv] = _softmax_pv([s], [v_ref[rows, vv]]).astype(BF16)


def _cast_slab(n_steps, w):
    n_rows, n_cols = w.shape
    slab = n_rows // n_steps
    assert slab * n_steps == n_rows
    in_spec = lambda step_of: pl.BlockSpec((slab, n_cols), lambda *g: (step_of(*g), 0))
    return w, in_spec, jax.ShapeDtypeStruct(w.shape, BF16), in_spec


def _attn_lat_kernel(q_ref, k_ref, v_ref, cckv_ref, ckpe_ref, wuk_ref, wuv_ref, dup_ref, *rest):
    if len(rest) == 5:
        w_ref, o_ref, wb_ref, kc_ref, vc_ref = rest
        wb_ref[...] = w_ref[...].astype(BF16)
    else:
        o_ref, kc_ref, vc_ref = rest

    @pl.when(pl.program_id(1) == 0)
    def _():
        _expand_cache(cckv_ref, ckpe_ref, wuk_ref, wuv_ref, dup_ref, kc_ref, vc_ref)

    for hd in range(N_HEADS):
        qk = slice(hd * HEAD_PAD, (hd + 1) * HEAD_PAD)
        vv = slice(hd * V_DIM, (hd + 1) * V_DIM)
        q = q_ref[:, qk]
        s1 = _dot_nt(q, k_ref[:, qk])
        s2 = _dot_nt(q, kc_ref[:, qk])
        o_ref[:, vv] = _softmax_pv([s1, s2], [v_ref[:, vv], vc_ref[:, vv]]).astype(BF16)


def _attn_lat_call(layer, q, k, v, cache_ckv, cache_kpe_t, w_uk, w_uv, dup, cast_w=None):
    tq = LAT_Q_TILE if cast_w is None else LAT_Q_TILE // 2
    n_q = DEC_SEQ // tq
    lat0 = T_CTX // DEC_SEQ
    in_specs = [pl.BlockSpec((tq, QK_W), lambda b, i: (T_CTX // tq + b * n_q + i, 0)),
                pl.BlockSpec((DEC_SEQ, QK_W), lambda b, i: (lat0 + b, 0)),
                pl.BlockSpec((DEC_SEQ, V_W), lambda b, i: (lat0 + b, 0)),
                pl.BlockSpec((None, None, PAST_LEN, KV_LORA), lambda b, i: (b, layer, 0, 0)),
                pl.BlockSpec((None, None, QK_ROPE, PAST_LEN), lambda b, i: (b, layer, 0, 0)),
                pl.BlockSpec((None, KV_LORA, N_HEADS * QK_NOPE), lambda b, i: (layer, 0, 0)),
                pl.BlockSpec((None, KV_LORA, V_W), lambda b, i: (layer, 0, 0)),
                pl.BlockSpec((QK_ROPE, LANES), lambda b, i: (0, 0))]
    out_shape = [jax.ShapeDtypeStruct((T_LAT, V_W), BF16)]
    out_specs = [pl.BlockSpec((tq, V_W), lambda b, i: (b * n_q + i, 0))]
    args = [q, k, v, cache_ckv, cache_kpe_t, w_uk, w_uv, dup]
    if cast_w is not None:
        w, w_in_spec, wb_shape, wb_spec = _cast_slab(DEC_BATCH * n_q, cast_w)
        step_of = lambda b, i: b * n_q + i
        in_specs.append(w_in_spec(step_of))
        out_shape.append(wb_shape)
        out_specs.append(wb_spec(step_of))
        args.append(w)
    outs = pl.pallas_call(
        _attn_lat_kernel,
        out_shape=tuple(out_shape),
        grid=(DEC_BATCH, n_q),
        in_specs=in_specs,
        out_specs=tuple(out_specs),
        scratch_shapes=[pltpu.VMEM((PAST_LEN, QK_W), BF16), pltpu.VMEM((PAST_LEN, V_W), BF16)],
        compiler_params=_cparams("arbitrary", "arbitrary"),
        name="attn_lat",
    )(*args)
    return outs


def _swiglu_hidden(h, wg_ref, wu_ref, a_buf):
    width = a_buf.shape[1]
    for c0 in range(0, width, FF_CHUNK):
        cols = slice(c0, min(c0 + FF_CHUNK, width))
        a_buf[:, cols] = (_silu(_dot(h, wg_ref[:, cols])) * _dot(h, wu_ref[:, cols])).astype(BF16)


def _mix_kernel(with_router, n_cast, attn_c_ref, attn_l_ref, pool_ref, xc_ref, xl_ref, g1_ref, sh2_ref, sc2_ref, n2_ref, wo_ref,
                *rest):
    if with_router:
        wr_ref, ltri_ref, utri_ref, *rest = rest
        cast_in, (x_out, xs_out, gs_out, route_out, meta_out, *cast_out) = rest[:n_cast], rest[n_cast:]
    else:
        g2_ref, wg_ref, wu_ref, wd_ref, *rest = rest
        cast_in, (x_out, *cast_out, a_buf) = rest[:n_cast], rest[n_cast:]
    for w_ref, wb_ref in zip(cast_in, cast_out):
        wb_ref[...] = w_ref[...].astype(BF16)
    is_ctx = pl.program_id(0) < T_CTX // TILE_MIX
    halves = []
    for r0 in range(0, TILE_MIX, TILE_MIX // 2):
        rows = slice(r0, r0 + TILE_MIX // 2)
        attn = jnp.where(is_ctx, attn_c_ref[rows, :], attn_l_ref[rows, :])
        y = _dot(attn, wo_ref[:V_W, :]) + _dot(pool_ref[rows, :], wo_ref[V_W:, :])
        x_new = jnp.where(is_ctx, xc_ref[rows, :], xl_ref[rows, :]) + g1_ref[...] * y
        x_out[rows, :] = x_new
        halves.append((_rms(x_new) * n2_ref[...] * (1.0 + sc2_ref[...]) + sh2_ref[...]).astype(BF16))
    h = jnp.concatenate(halves, axis=0)
    if not with_router:
        _swiglu_hidden(h, wg_ref, wu_ref, a_buf)
        x_out[...] += g2_ref[...] * _dot(a_buf[...], wd_ref[...])
        return

    logits = _dot(h, wr_ref[...])
    lane = lax.broadcasted_iota(jnp.int32, logits.shape, 1)
    neg = float(jnp.finfo(F32).min)
    lg = jnp.where(lane < N_EXPERTS, logits, neg)
    m1 = lg.max(axis=-1, keepdims=True)
    i1 = jnp.where(lg == m1, lane, LANES).min(axis=-1, keepdims=True)
    lg2 = jnp.where(lane == i1, neg, lg)
    m2 = lg2.max(axis=-1, keepdims=True)
    i2 = jnp.where(lg2 == m2, lane, LANES).min(axis=-1, keepdims=True)
    e = jnp.exp(m2 - m1)
    w1 = 1.0 / (1.0 + e)
    w2 = e / (1.0 + e)

    sel1 = lane == i1
    sel2 = lane == i2
    member = jnp.where(jnp.logical_or(sel1, sel2), 1.0, 0.0)
    rank = _dot(ltri_ref[...], member.astype(BF16))
    n_tok = member.sum(axis=0, keepdims=True)
    units = jnp.floor((n_tok + (SLOT_UNIT - 1)) * (1.0 / SLOT_UNIT))
    unit_off = _dot(jnp.broadcast_to(units, (8, LANES)).astype(BF16), utri_ref[...])[0:1, :]
    slot_of = SLOT_UNIT * unit_off + rank
    slot1 = jnp.where(sel1, slot_of, 0.0).sum(axis=-1, keepdims=True)
    slot2 = jnp.where(sel2, slot_of, 0.0).sum(axis=-1, keepdims=True)
    route = jnp.where(lane == 0, slot1, jnp.where(lane == 1, slot2,
                      jnp.where(lane == 2, w1, jnp.where(lane == 3, w2, 0.0))))
    route_out[...] = route
    sub = lax.broadcasted_iota(jnp.int32, (8, LANES), 0)
    meta_out[...] = jnp.where(sub == 0, units, jnp.where(sub == 1, unit_off, 0.0)).astype(jnp.int32)

    rt = route.T
    s1 = rt[0:1, :].astype(jnp.int32)
    s2 = rt[1:2, :].astype(jnp.int32)
    for r0 in range(0, SLOTS_PER_TILE, SLOT_BLOCK):
        blk = slice(r0, r0 + SLOT_BLOCK)
        srow = r0 + lax.broadcasted_iota(jnp.int32, (SLOT_BLOCK, TILE_MIX), 0)
        hit1 = srow == s1
        hit2 = srow == s2
        perm = (jnp.where(hit1, 1.0, 0.0) + jnp.where(hit2, 1.0, 0.0)).astype(BF16)
        xs_out[blk, :] = _dot(perm, h).astype(BF16)
        gate = (jnp.where(hit1, rt[2:3, :], 0.0) + jnp.where(hit2, rt[3:4, :], 0.0)).sum(axis=-1, keepdims=True)
        gs_out[blk, :] = jnp.broadcast_to(gate, (SLOT_BLOCK, LANES))


def _mod_row(i, tile):
    n_ctx = T_CTX // tile
    per_seq = DEC_SEQ // tile
    return jnp.where(i < n_ctx, 0, 1 + (i - n_ctx) // per_seq)


def _mix_call(attn_ctx, attn_lat, pooled, x_ctx, x_lat, ada, n2, w_out, cast_ws, w_router=None, ffn_w=None):
    with_router = w_router is not None
    assert with_router != (ffn_w is not None)
    tm = TILE_MIX
    n_ctx = T_CTX // tm
    lat_tile0 = n_ctx if x_lat.shape[0] == T_ALL else 0
    row = lambda i: (i, 0)
    const2 = lambda i: (0, 0)
    mod = lambda k: pl.BlockSpec((None, 1, D_MODEL), lambda i: (_mod_row(i, tm), 0, k))
    ctx_blk = lambda i: (jnp.minimum(i, n_ctx - 1), 0)
    in_specs = [pl.BlockSpec((tm, V_W), ctx_blk),
                pl.BlockSpec((tm, V_W), lambda i: (jnp.maximum(i - n_ctx, 0), 0)),
                pl.BlockSpec((tm, POOL_W), row),
                pl.BlockSpec((tm, D_MODEL), ctx_blk),
                pl.BlockSpec((tm, D_MODEL), lambda i: (jnp.maximum(i - n_ctx, 0) + lat_tile0, 0)),
                mod(2), mod(3), mod(4),
                pl.BlockSpec((1, D_MODEL), const2), pl.BlockSpec((D_MODEL, D_MODEL), const2)]
    args = [attn_ctx, attn_lat, pooled, x_ctx, x_lat, ada, ada, ada, n2, w_out]
    if with_router:
        t_i = jnp.arange(tm)
        ltri = (t_i[None, :] < t_i[:, None]).astype(BF16)
        l_i = jnp.arange(LANES)
        utri = (l_i[:, None] < l_i[None, :]).astype(BF16)
        in_specs += [pl.BlockSpec((D_MODEL, LANES), const2), pl.BlockSpec((tm, tm), const2),
                     pl.BlockSpec((LANES, LANES), const2)]
        args += [w_router, ltri, utri]
        out_shape = [jax.ShapeDtypeStruct((T_ALL, D_MODEL), F32),
                     jax.ShapeDtypeStruct((N_ROUTE_TILES * SLOTS_PER_TILE, D_MODEL), BF16),
                     jax.ShapeDtypeStruct((N_ROUTE_TILES * SLOTS_PER_TILE, LANES), F32),
                     jax.ShapeDtypeStruct((T_ALL, LANES), F32),
                     jax.ShapeDtypeStruct((N_ROUTE_TILES, 8, LANES), jnp.int32)]
        out_specs = [pl.BlockSpec((tm, D_MODEL), row), pl.BlockSpec((SLOTS_PER_TILE, D_MODEL), row),
                     pl.BlockSpec((SLOTS_PER_TILE, LANES), row), pl.BlockSpec((tm, LANES), row),
                     pl.BlockSpec((None, 8, LANES), lambda i: (i, 0, 0))]
        scratch = []
    else:
        resident = lambda shape: pl.BlockSpec(shape, const2, pipeline_mode=pl.Buffered(1))
        in_specs += [mod(5), resident((D_MODEL, D_FF)), resident((D_MODEL, D_FF)), resident((D_FF, D_MODEL))]
        args += [ada, *ffn_w]
        out_shape = [jax.ShapeDtypeStruct((T_ALL, D_MODEL), F32)]
        out_specs = [pl.BlockSpec((tm, D_MODEL), row)]
        scratch = [pltpu.VMEM((tm, D_FF), BF16)]
    for w, w_spec, wb_shape, wb_spec in (_cast_slab(T_ALL // tm, w) for w in cast_ws):
        in_specs.append(w_spec(lambda i: i))
        args.append(w)
        out_shape.append(wb_shape)
        out_specs.append(wb_spec(lambda i: i))
    return pl.pallas_call(
        functools.partial(_mix_kernel, with_router, len(cast_ws)),
        out_shape=tuple(out_shape),
        grid=(T_ALL // tm,),
        in_specs=in_specs,
        out_specs=tuple(out_specs),
        scratch_shapes=scratch,
        compiler_params=_cparams("arbitrary"),
        name="mix_router" if with_router else "mix_ffn",
    )(*args)


def _route_tables(meta):
    units = meta[:, 0, :N_EXPERTS]
    offs = meta[:, 1, :N_EXPERTS]
    cum = jnp.cumsum(units, axis=0)
    total = cum[-1]
    tiles_e = (total + GEMM_UNITS - 1) // GEMM_UNITS
    tile_end = jnp.cumsum(tiles_e)
    n_act = tile_end[-1]
    m = jnp.arange(N_GEMM_TILES)
    m_eff = jnp.minimum(m, jnp.maximum(n_act - 1, 0))
    expert = jnp.minimum(jnp.sum(tile_end[None, :] <= m_eff[:, None], axis=1), N_EXPERTS - 1)
    is_e = (expert[:, None] == jnp.arange(N_EXPERTS)[None, :]).astype(jnp.int32)
    pick = lambda per_expert: jnp.sum(is_e * per_expert[None, :], axis=1)
    first_q = (m_eff - pick(tile_end - tiles_e)) * GEMM_UNITS
    count = jnp.where(m < n_act, jnp.clip(pick(total) - first_q, 0, GEMM_UNITS), 0)
    q = first_q[:, None] + jnp.arange(GEMM_UNITS)[None, :]
    rows_of = lambda table: jnp.sum(is_e[:, None, :] * table[None, :, :], axis=2)
    cum_e, units_e, offs_e = rows_of(cum), rows_of(units), rows_of(offs)
    src_tile = jnp.minimum(jnp.sum(cum_e[:, None, :] <= q[:, :, None], axis=2), N_ROUTE_TILES - 1)
    is_t = (src_tile[:, :, None] == jnp.arange(N_ROUTE_TILES)[None, None, :]).astype(jnp.int32)
    at_tile = lambda per_tile: jnp.sum(is_t * per_tile[:, None, :], axis=2)
    uid = src_tile * UNITS_PER_TILE + at_tile(offs_e) + (q - at_tile(cum_e - units_e))
    uid = jnp.where(jnp.arange(GEMM_UNITS)[None, :] < count[:, None], uid, ZERO_UNIT)
    i32 = jnp.int32
    return expert.astype(i32), count.astype(i32), uid.reshape(-1).astype(i32), n_act.reshape(1).astype(i32)


def _gemm_kernel(em_ref, cnt_ref, ul_ref, nact_ref, xs_hbm, gs_hbm, wg_ref, wu_ref, wd_ref, ys_hbm,
                 xbuf, gbuf, obuf, acc_ref, a_buf, sem_in, sem_out):
    del em_ref
    m = pl.program_id(0)
    j = pl.program_id(1)
    last_j = pl.num_programs(1) - 1
    n_act = nact_ref[0]
    active = m < n_act
    slot = m % 2

    def rows(r):
        return pl.ds(r * SLOT_UNIT if isinstance(r, int) else pl.multiple_of(r * SLOT_UNIT, SLOT_UNIT), SLOT_UNIT)

    def in_copies(mm, sl, r):
        uid = ul_ref[mm * GEMM_UNITS + r]
        return (pltpu.make_async_copy(xs_hbm.at[uid], xbuf.at[sl, rows(r)], sem_in.at[sl]),
                pltpu.make_async_copy(gs_hbm.at[uid], gbuf.at[sl, rows(r)], sem_in.at[sl]))

    def out_copy(mm, r):
        uid = ul_ref[mm * GEMM_UNITS + r]
        return pltpu.make_async_copy(obuf.at[rows(r)], ys_hbm.at[uid], sem_out.at[0])

    def for_slot(sl_dyn, fn):
        if isinstance(sl_dyn, int):
            fn(sl_dyn)
            return
        for sl in range(2):
            pl.when(sl_dyn == sl)(functools.partial(fn, sl))

    def start_in(mm, sl_dyn):
        def issue(sl):
            for r in range(GEMM_UNITS):
                for cp in in_copies(mm, sl, r):
                    cp.start()

        for_slot(sl_dyn, issue)

    def wait_in(mm, sl_dyn):
        def wait(sl):
            for r in range(GEMM_UNITS):
                for cp in in_copies(mm, sl, r):
                    cp.wait()

        for_slot(sl_dyn, wait)

    def start_out(mm):
        n = cnt_ref[mm]
        for r in range(GEMM_UNITS):
            pl.when(r < n)(lambda r=r: out_copy(mm, r).start())

    def wait_out(mm):
        def body(r, carry):
            out_copy(mm, r).wait()
            return carry

        lax.fori_loop(0, cnt_ref[mm], body, 0)

    @pl.when(jnp.logical_and(active, j == 0))
    def _():
        @pl.when(m == 0)
        def _():
            start_in(0, 0)

        wait_in(m, slot)

        @pl.when(m + 1 < n_act)
        def _():
            start_in(m + 1, 1 - slot)

    @pl.when(jnp.logical_and(jnp.logical_and(active, j == last_j), m > 0))
    def _():
        wait_out(m - 1)

    n_valid = cnt_ref[m]
    for quarter in range(1, GEMM_QUARTERS + 1):
        n_rows = quarter * GEMM_ROWS // GEMM_QUARTERS
        lo_units = (quarter - 1) * GEMM_UNITS // GEMM_QUARTERS
        hi_units = quarter * GEMM_UNITS // GEMM_QUARTERS

        @pl.when(jnp.logical_and(active, jnp.logical_and(n_valid > lo_units, n_valid <= hi_units)))
        def _(n_rows=n_rows):
            _swiglu_hidden(xbuf[slot, :n_rows, :], wg_ref, wu_ref, a_buf.at[:n_rows, :])
            part = _dot(a_buf[:n_rows, :], wd_ref[...])

            @pl.when(j == 0)
            def _():
                acc_ref[:n_rows, :] = part

            @pl.when(jnp.logical_and(j > 0, j < last_j))
            def _():
                acc_ref[:n_rows, :] += part

            @pl.when(j == last_j)
            def _():
                gate = gbuf[slot, :n_rows, :]
                for cb in range(D_MODEL // LANES):
                    cols = slice(cb * LANES, (cb + 1) * LANES)
                    obuf[:n_rows, cols] = ((acc_ref[:n_rows, cols] + part[:, cols]) * gate).astype(BF16)

    @pl.when(jnp.logical_and(active, j == last_j))
    def _():
        start_out(m)

        @pl.when(m == n_act - 1)
        def _():
            wait_out(m)


def _gemm_call(expert, count, uids, n_act, xs, gs, wg, wu, wd):
    tf = TILE_FE
    n_j = D_FF_EXPERT // tf
    assert n_j >= 2

    def w_col(m, j, em, cnt, ul, nact):
        return (em[m], 0, jnp.where(m < nact[0], j, n_j - 1))

    def w_row(m, j, em, cnt, ul, nact):
        return (em[m], jnp.where(m < nact[0], j, n_j - 1), 0)

    grid_spec = pltpu.PrefetchScalarGridSpec(
        num_scalar_prefetch=4,
        grid=(N_GEMM_TILES, n_j),
        in_specs=[pl.BlockSpec(memory_space=pl.ANY),
                  pl.BlockSpec(memory_space=pl.ANY),
                  pl.BlockSpec((None, D_MODEL, tf), w_col),
                  pl.BlockSpec((None, D_MODEL, tf), w_col),
                  pl.BlockSpec((None, tf, D_MODEL), w_row)],
        out_specs=pl.BlockSpec(memory_space=pl.ANY),
        scratch_shapes=[pltpu.VMEM((2, GEMM_ROWS, D_MODEL), BF16),
                        pltpu.VMEM((2, GEMM_ROWS, LANES), F32),
                        pltpu.VMEM((GEMM_ROWS, D_MODEL), BF16),
                        pltpu.VMEM((GEMM_ROWS, D_MODEL), F32),
                        pltpu.VMEM((GEMM_ROWS, tf), BF16),
                        pltpu.SemaphoreType.DMA((2,)),
                        pltpu.SemaphoreType.DMA((1,))])
    return pl.pallas_call(
        _gemm_kernel,
        out_shape=jax.ShapeDtypeStruct((N_UNITS, SLOT_UNIT, D_MODEL), BF16),
        grid_spec=grid_spec,
        input_output_aliases={4: 0},
        compiler_params=_cparams("arbitrary", "arbitrary"),
        name="moe_experts",
    )(expert, count, uids, n_act,
      xs.reshape(N_UNITS, SLOT_UNIT, D_MODEL), gs.reshape(N_UNITS, SLOT_UNIT, LANES), wg, wu, wd)


def _combine_kernel(ys_ref, route_ref, x_ref, g2_ref, fg_ref, o_ref):
    route = route_ref[...]
    s1 = route[:, 0:1].astype(jnp.int32)
    s2 = route[:, 1:2].astype(jnp.int32)
    scol = lax.broadcasted_iota(jnp.int32, (TILE_MIX, SLOTS_PER_TILE), 1)
    unperm = (jnp.where(scol == s1, 1.0, 0.0) + jnp.where(scol == s2, 1.0, 0.0)).astype(BF16)
    y = _dot(unperm, ys_ref[...])
    o_ref[...] = _rms(x_ref[...] + g2_ref[...] * y) * fg_ref[...]


def _combine_call(ys, route, x, ada, final_g, tile0, n_tok):
    tm = TILE_MIX
    return pl.pallas_call(
        _combine_kernel,
        out_shape=jax.ShapeDtypeStruct((n_tok, D_MODEL), F32),
        grid=(n_tok // tm,),
        in_specs=[pl.BlockSpec((SLOTS_PER_TILE, D_MODEL), lambda i: (i + tile0, 0)),
                  pl.BlockSpec((tm, LANES), lambda i: (i + tile0, 0)),
                  pl.BlockSpec((tm, D_MODEL), lambda i: (i + tile0, 0)),
                  pl.BlockSpec((None, 1, D_MODEL), lambda i: (_mod_row(i + tile0, tm), 0, 5)),
                  pl.BlockSpec((1, D_MODEL), lambda i: (0, 0))],
        out_specs=pl.BlockSpec((tm, D_MODEL), lambda i: (i, 0)),
        compiler_params=_cparams("arbitrary"),
        name="moe_combine",
    )(ys.reshape(N_UNITS * SLOT_UNIT, D_MODEL), route, x, ada, final_g)


def _rot_cols(w):
    q = QK_ROPE // 4
    src = jnp.arange(QK_ROPE)[:, None]
    dst = jnp.arange(QK_ROPE)[None, :]
    first_half = (dst // q) % 2 == 0
    rot = jnp.where(first_half, -(src == dst + q).astype(F32), (src == dst - q).astype(F32))
    return jnp.matmul(w, rot.astype(w.dtype))


def _rope_tables():
    t = jnp.arange(DEC_SEQ)
    rows = (t // GRID_W).astype(F32)
    cols = (t % GRID_W).astype(F32)
    half = QK_ROPE // 2
    freqs = ROPE_BASE ** (-jnp.arange(0, half, 2, dtype=F32) / half)
    ang_r = rows[:, None] * freqs
    ang_c = cols[:, None] * freqs
    ang = jnp.concatenate([ang_r, ang_r, ang_c, ang_c], axis=-1)
    cos, sin = jnp.cos(ang), jnp.sin(ang)
    one, zero = jnp.ones_like(cos), jnp.zeros_like(cos)
    tq = jnp.stack([jnp.concatenate([one, zero], -1), jnp.concatenate([cos, sin], -1)])
    tc = jnp.stack([jnp.concatenate([one, one], -1), jnp.concatenate([cos, cos], -1)])
    ts = jnp.stack([jnp.concatenate([zero, zero], -1), jnp.concatenate([sin, sin], -1)])
    return tq, tc, ts


def _layer_weights(w_in, w_uq, w_ukv):
    wi = w_in.astype(BF16)
    w_kpe = wi[..., OFF_U:OFF_U + QK_ROPE]
    w_rot = _rot_cols(w_kpe)
    w_in_r = jnp.concatenate([wi[..., :OFF_U], wi[..., OFF_U + QK_ROPE:], w_kpe, w_kpe, w_rot, w_rot], axis=-1)
    wq = w_uq.astype(BF16).reshape(DEPTH, Q_LORA, N_HEADS, QK_NOPE + QK_ROPE)
    w_uq_r = jnp.concatenate([wq, _rot_cols(wq[..., QK_NOPE:])], axis=-1).reshape(DEPTH, Q_LORA, QK_W)
    kv = w_ukv.astype(BF16).reshape(DEPTH, KV_LORA, N_HEADS, QK_NOPE + V_DIM)
    w_uk = kv[..., :QK_NOPE].reshape(DEPTH, KV_LORA, N_HEADS * QK_NOPE)
    w_uv = kv[..., QK_NOPE:].reshape(DEPTH, KV_LORA, V_W)
    return w_in_r, w_uq_r, w_uk, w_uv


def kernel(x_prompt, x_sample, c, cache_ckv, cache_kpe, c_ctx, norm1_g, norm2_g, w_ada, b_ada, w_in, q_norm_g,
           kv_norm_g, w_uq, w_ukv, w_pool, pool_scale, w_out, ffn_w_gate, ffn_w_up, ffn_w_down, moe_w_router,
           moe_w_gate, moe_w_up, moe_w_down, final_norm_g):
    x_ctx = x_prompt.reshape(T_CTX, D_MODEL)
    x_lat = x_sample.reshape(T_LAT, D_MODEL)
    cond = jnp.concatenate([c_ctx[None, :], c, jnp.zeros((COND_ROWS - 1 - DEC_BATCH, D_MODEL), F32)], axis=0)
    ada_all = _ada_call(cond, w_ada, b_ada)
    tq, tc, ts = _rope_tables()

    w_in_all, w_uq_all, w_uk_all, w_uv_all = _layer_weights(w_in, w_uq, w_ukv)
    dup = jnp.concatenate([jnp.eye(QK_ROPE, dtype=BF16)] * 2, axis=1)
    cache = (cache_ckv, jnp.swapaxes(cache_kpe, -1, -2), w_uk_all, w_uv_all, dup)

    assert DEPTH == 2
    moe_gate_2d = moe_w_gate[0].reshape(N_EXPERTS * D_MODEL, D_FF_EXPERT)
    moe_up_2d = moe_w_up[0].reshape(N_EXPERTS * D_MODEL, D_FF_EXPERT)
    moe_down_2d = moe_w_down[0].reshape(N_EXPERTS * D_FF_EXPERT, D_MODEL)
    new_ckv = new_kpe = None
    for l in range(DEPTH):
        ada = ada_all[l].reshape(COND_ROWS, 1, 6 * D_MODEL)
        cast_ws = (ffn_w_gate[0], ffn_w_up[0], ffn_w_down[0],
                   w_out.reshape(DEPTH * (V_W + POOL_W), D_MODEL)) if l == 0 else ()
        q, k, v, new_ckv, new_kpe, pooled, attn_c, *cast = _in_call(
            l, x_ctx, x_lat, ada, norm1_g[l][None, :], w_in_all, q_norm_g[l][None, :], kv_norm_g[l][None, :],
            w_uq_all, w_uk_all, w_uv_all, tq, tc, ts, w_pool[l].astype(BF16), pool_scale[l][None, :], new_ckv,
            new_kpe, cast_ws)
        if l == 0:
            ffn_w, w_out_b = cast[:3], cast[3].reshape(DEPTH, V_W + POOL_W, D_MODEL)
        j = l // 2
        if l % 2 == 0:
            attn_l, moe_wd = _attn_lat_call(l, q, k, v, *cache, cast_w=moe_down_2d)
            x, moe_wg = _mix_call(attn_c, attn_l, pooled, x_ctx, x_lat, ada, norm2_g[l][None, :],
                                  w_out_b[l], (moe_gate_2d,), ffn_w=ffn_w)
            x_ctx = x_lat = x
        else:
            (attn_l,) = _attn_lat_call(l, q, k, v, *cache)
            w_r = jnp.pad(moe_w_router[j], ((0, 0), (0, LANES - N_EXPERTS))).astype(BF16)
            x, xs, gs, route, meta, moe_wu = _mix_call(attn_c, attn_l, pooled, x_ctx, x_lat, ada,
                                                       norm2_g[l][None, :], w_out_b[l], (moe_up_2d,),
                                                       w_router=w_r)
            expert, count, uids, n_act = _route_tables(meta)
            ys = _gemm_call(expert, count, uids, n_act, xs, gs,
                            moe_wg.reshape(N_EXPERTS, D_MODEL, D_FF_EXPERT),
                            moe_wu.reshape(N_EXPERTS, D_MODEL, D_FF_EXPERT),
                            moe_wd.reshape(N_EXPERTS, D_FF_EXPERT, D_MODEL))
            fg = final_norm_g[None, :]
            y_prompt = _combine_call(ys, route, x, ada, fg, 0, T_CTX).reshape(BATCH, SEQ, D_MODEL)
            y_sample = _combine_call(ys, route, x, ada, fg, T_CTX // TILE_MIX, T_LAT).reshape(
                DEC_BATCH, DEC_SEQ, D_MODEL)

    return y_prompt, y_sample, new_ckv, jnp.swapaxes(new_kpe, -1, -2)
```

```python
import functools

import jax
import jax.numpy as jnp
import numpy as np
from jax import lax
from jax.experimental import pallas as pl
from jax.experimental.pallas import tpu as pltpu

D_MODEL = 1024
BATCH = 32
SEQ = 256
DEPTH = 2
DEC_BATCH = 8
DEC_SEQ = 1024
PAST_LEN = 512
GRID_W = 64
N_HEADS = 4
QK_NOPE = 128
QK_ROPE = 64
V_DIM = 128
Q_LORA = 384
KV_LORA = 256
POOL_W = 512
POOL_GROUPS = 4
POOL_WINDOWS = (2, 4, 8, 16)
POOL_CH = POOL_W // POOL_GROUPS
D_FF = 2816
N_EXPERTS = 8
D_FF_EXPERT = 3584
ROPE_BASE = 10000.0
EPS = 1e-6

T_CTX = BATCH * SEQ
T_LAT = DEC_BATCH * DEC_SEQ
T_ALL = T_CTX + T_LAT

LANES = 128
HEAD_PAD = 256
QK_W = N_HEADS * HEAD_PAD
V_W = N_HEADS * V_DIM
OFF_CKV = Q_LORA
OFF_U = Q_LORA + KV_LORA
OFF_KA = OFF_U + POOL_W
OFF_KB = OFF_KA + LANES
N_PROJ = OFF_KB + LANES

TILE_IN = 1024
CHUNK = 256
HALO = 8
REGION = CHUNK + 2 * HALO
LAT_Q_TILE = 1024
TILE_MIX = 512
TOP_K = 2
SLOT_UNIT = 16
UNITS_PER_TILE = TILE_MIX * TOP_K // SLOT_UNIT + N_EXPERTS
SLOTS_PER_TILE = UNITS_PER_TILE * SLOT_UNIT
SLOT_BLOCK = SLOTS_PER_TILE // 4
N_ROUTE_TILES = T_ALL // TILE_MIX
N_UNITS = N_ROUTE_TILES * UNITS_PER_TILE
ZERO_UNIT = UNITS_PER_TILE - 1
GEMM_UNITS = 64
GEMM_ROWS = GEMM_UNITS * SLOT_UNIT
GEMM_QUARTERS = 4
N_GEMM_TILES = -(-N_UNITS // GEMM_UNITS) + N_EXPERTS
TILE_FE = D_FF_EXPERT // 2
FF_CHUNK = 256
COND_ROWS = 16
VMEM_LIMIT = 56 * 1024 * 1024

F32 = jnp.float32
BF16 = jnp.bfloat16


def _rms(x):
    return x * lax.rsqrt(jnp.mean(x * x, axis=-1, keepdims=True) + EPS)


def _dot(a, b):
    return jnp.dot(a, b, preferred_element_type=F32)


def _dot_nt(a, b):
    return lax.dot_general(a, b, (((1,), (1,)), ((), ())), preferred_element_type=F32)


def _silu(x):
    return x * (1.0 / (1.0 + jnp.exp(-x)))


def _cparams(*sem):
    return pltpu.CompilerParams(dimension_semantics=sem, vmem_limit_bytes=VMEM_LIMIT)


def _ada_kernel(cond_ref, w_ref, b_ref, o_ref):
    k = pl.program_id(1)
    part = _dot(_silu(cond_ref[...]).astype(BF16), w_ref[...].astype(BF16))

    @pl.when(k == 0)
    def _():
        o_ref[...] = part + b_ref[...]

    @pl.when(k > 0)
    def _():
        o_ref[...] += part


def _ada_call(cond, w_ada, b_ada):
    n_blk = 4
    kb = D_MODEL // n_blk
    return pl.pallas_call(
        _ada_kernel,
        out_shape=jax.ShapeDtypeStruct((DEPTH, COND_ROWS, 6 * D_MODEL), F32),
        grid=(DEPTH, n_blk),
        in_specs=[
            pl.BlockSpec((COND_ROWS, kb), lambda l, k: (0, k)),
            pl.BlockSpec((None, kb, 6 * D_MODEL), lambda l, k: (l, k, 0)),
            pl.BlockSpec((None, 1, 6 * D_MODEL), lambda l, k: (l, 0, 0)),
        ],
        out_specs=pl.BlockSpec((None, COND_ROWS, 6 * D_MODEL), lambda l, k: (l, 0, 0)),
        compiler_params=_cparams("arbitrary", "arbitrary"),
        name="ada_params",
    )(cond, w_ada, b_ada.reshape(DEPTH, 1, 6 * D_MODEL))


def _expand_cache(ckv_ref, kpe_ref, wuk_ref, wuv_ref, dup_ref, k_ref, v_ref):
    ckv = ckv_ref[...].astype(BF16)
    knope = _dot(ckv, wuk_ref[...])
    v_ref[...] = _dot(ckv, wuv_ref[...]).astype(BF16)
    kpe = kpe_ref[...].T
    kdup = _dot(kpe.astype(BF16), dup_ref[...]).astype(BF16)
    for h in range(N_HEADS):
        k_ref[:, h * HEAD_PAD:h * HEAD_PAD + QK_NOPE] = knope[:, h * QK_NOPE:(h + 1) * QK_NOPE].astype(BF16)
        k_ref[:, h * HEAD_PAD + QK_NOPE:(h + 1) * HEAD_PAD] = kdup


def _in_kernel(n_prev, n_cast, xc_ref, xl_ref, sh_ref, sc_ref, g_ref, win_ref, qg_ref, kvg_ref, wuq_ref, wuk_ref,
               wuv_ref, tq_ref, tc_ref, ts_ref, wpool_ref, ps_ref, *rest):
    if n_prev:
        prev_ckv_ref, prev_kpe_ref, *rest = rest
    cast_in, rest = rest[:n_cast], rest[n_cast:]
    q_out, k_out, v_out, ckv_out, kpe_out, pool_out, attn_out, *rest = rest
    cast_out, (u_scr,) = rest[:n_cast], rest[n_cast:]
    for w_ref, wb_ref in zip(cast_in, cast_out):
        wb_ref[...] = w_ref[...].astype(BF16)
    is_ctx = pl.program_id(0) >= (T_ALL - T_CTX) // TILE_IN
    if n_prev:
        ckv_out[:, :n_prev] = prev_ckv_ref[...]
        kpe_out[:, :n_prev] = prev_kpe_ref[...]
    shift = sh_ref[...]
    scale1 = 1.0 + sc_ref[...]
    qk_scale = (QK_NOPE + QK_ROPE) ** -0.5
    n_chunks = TILE_IN // CHUNK
    zeros = jnp.zeros((HALO, POOL_W), F32)
    seq_len = jnp.where(is_ctx, SEQ, DEC_SEQ)
    row = lax.broadcasted_iota(jnp.int32, (CHUNK, POOL_CH), 0)

    def pool_chunk(c):
        base = c * REGION + HALO
        rows = slice(c * CHUNK, (c + 1) * CHUNK)
        t = row + jnp.where(is_ctx, 0, c * CHUNK)
        for g, w in enumerate(POOL_WINDOWS):
            cols = slice(g * POOL_CH, (g + 1) * POOL_CH)
            acc = u_scr[base - w // 2:base - w // 2 + CHUNK, cols]
            for j in range(-w // 2 + 1, w // 2):
                acc = acc + u_scr[base + j:base + j + CHUNK, cols]
            cnt = jnp.minimum(t + w // 2, seq_len) - jnp.maximum(t - w // 2, 0)
            pooled = acc / cnt.astype(F32) - u_scr[base:base + CHUNK, cols]
            lin = _dot(pooled.astype(BF16), wpool_ref[g]) * ps_ref[:, cols]
            pool_out[rows, cols] = lin.astype(BF16)

    for c in range(n_chunks):
        r0 = c * CHUNK
        rows = slice(r0, r0 + CHUNK)
        x = jnp.where(is_ctx, xc_ref[rows, :], xl_ref[rows, :])
        h = (_rms(x) * g_ref[...] * scale1 + shift).astype(BF16)
        proj = _dot(h, win_ref[...])

        qn = (_rms(proj[:, :Q_LORA]) * qg_ref[...]).astype(BF16)
        q = _dot(qn, wuq_ref[...]) * qk_scale
        tq = tq_ref[rows, :]
        for hd in range(N_HEADS):
            lo = hd * HEAD_PAD
            q_out[rows, lo:lo + QK_NOPE] = q[:, lo:lo + QK_NOPE].astype(BF16)
            q_out[rows, lo + QK_NOPE:lo + HEAD_PAD] = (q[:, lo + QK_NOPE:lo + HEAD_PAD] * tq).astype(BF16)

        ckv = _rms(proj[:, OFF_CKV:OFF_U]) * kvg_ref[...]
        ckv_out[c, n_prev] = ckv
        ckv_b = ckv.astype(BF16)
        knope = _dot(ckv_b, wuk_ref[...])
        v_out[rows, :] = _dot(ckv_b, wuv_ref[...]).astype(BF16)

        k_a = proj[:, OFF_KA:OFF_KB]
        k_b = proj[:, OFF_KB:N_PROJ]
        kpe_out[c, n_prev] = k_a.T[:QK_ROPE, :]
        kr = (k_a * tc_ref[rows, :] + k_b * ts_ref[rows, :]).astype(BF16)
        for hd in range(N_HEADS):
            lo = hd * HEAD_PAD
            k_out[rows, lo:lo + QK_NOPE] = knope[:, hd * QK_NOPE:(hd + 1) * QK_NOPE].astype(BF16)
            k_out[rows, lo + QK_NOPE:lo + HEAD_PAD] = kr

        u = proj[:, OFF_U:OFF_KA]
        base = c * REGION + HALO
        u_scr[base:base + CHUNK, :] = u
        if c == 0:
            u_scr[0:HALO, :] = zeros
        else:
            u_scr[base - 2 * HALO:base - HALO, :] = jnp.where(is_ctx, zeros, u[:HALO, :])
        if c == n_chunks - 1:
            u_scr[base + CHUNK:base + CHUNK + HALO, :] = zeros
        else:
            u_scr[base + CHUNK + HALO:base + CHUNK + 2 * HALO, :] = jnp.where(is_ctx, zeros, u[CHUNK - HALO:, :])
        if c > 0:
            pool_chunk(c - 1)
    pool_chunk(n_chunks - 1)

    @pl.when(is_ctx)
    def _():
        _attn_ctx_rows(q_out, k_out, v_out, attn_out)


def _in_call(layer, x_ctx, x_lat, ada, g1, w_in, qg, kvg, w_uq, w_uk, w_uv, tq, tc, ts, w_pool, pscale, prev_ckv, prev_kpe,
             cast_ws=()):
    assert CHUNK == SEQ
    n_tiles = T_ALL // TILE_IN
    n_ctx = T_CTX // TILE_IN
    n_prev = 0 if prev_ckv is None else prev_ckv.shape[1]
    seqs = TILE_IN // SEQ
    lat_tile0 = n_ctx if x_lat.shape[0] == T_ALL else 0

    def tile(i):
        return jnp.where(i < n_tiles - n_ctx, i + n_ctx, i - (n_tiles - n_ctx))

    def cond_row(i):
        return jnp.maximum(tile(i) - n_ctx + 1, 0)

    def tab(i):
        return (jnp.minimum(jnp.maximum(tile(i) - n_ctx + 1, 0), 1), 0, 0)

    rows = lambda i: (tile(i), 0)
    ctx_blk4 = lambda i: (jnp.where(tile(i) < n_ctx, tile(i), 0), 0, 0, 0)
    const2 = lambda i: (0, 0)
    of_layer = lambda i: (layer, 0, 0)
    prev_specs = [pl.BlockSpec((seqs, n_prev, SEQ, KV_LORA), ctx_blk4),
                  pl.BlockSpec((seqs, n_prev, QK_ROPE, SEQ), ctx_blk4)] if n_prev else []
    prev_args = [prev_ckv, prev_kpe] if n_prev else []
    casts = [_cast_slab(n_tiles, w) for w in cast_ws]
    step = lambda i: i
    return pl.pallas_call(
        functools.partial(_in_kernel, n_prev, len(casts)),
        out_shape=(jax.ShapeDtypeStruct((T_ALL, QK_W), BF16),
                   jax.ShapeDtypeStruct((T_ALL, QK_W), BF16),
                   jax.ShapeDtypeStruct((T_ALL, V_W), BF16),
                   jax.ShapeDtypeStruct((BATCH, n_prev + 1, SEQ, KV_LORA), F32),
                   jax.ShapeDtypeStruct((BATCH, n_prev + 1, QK_ROPE, SEQ), F32),
                   jax.ShapeDtypeStruct((T_ALL, POOL_W), BF16),
                   jax.ShapeDtypeStruct((T_CTX, V_W), BF16)) + tuple(c[2] for c in casts),
        grid=(n_tiles,),
        in_specs=[
            pl.BlockSpec((TILE_IN, D_MODEL), lambda i: (jnp.minimum(tile(i), n_ctx - 1), 0)),
            pl.BlockSpec((TILE_IN, D_MODEL), lambda i: (jnp.maximum(tile(i) - n_ctx, 0) + lat_tile0, 0)),
            pl.BlockSpec((None, 1, D_MODEL), lambda i: (cond_row(i), 0, 0)),
            pl.BlockSpec((None, 1, D_MODEL), lambda i: (cond_row(i), 0, 1)),
            pl.BlockSpec((None, 1, D_MODEL), of_layer),
            pl.BlockSpec((None, D_MODEL, N_PROJ), of_layer),
            pl.BlockSpec((None, 1, Q_LORA), of_layer),
            pl.BlockSpec((None, 1, KV_LORA), of_layer),
            pl.BlockSpec((None, Q_LORA, QK_W), of_layer),
            pl.BlockSpec((None, KV_LORA, N_HEADS * QK_NOPE), of_layer),
            pl.BlockSpec((None, KV_LORA, V_W), of_layer),
            pl.BlockSpec((None, TILE_IN, LANES), tab),
            pl.BlockSpec((None, TILE_IN, LANES), tab),
            pl.BlockSpec((None, TILE_IN, LANES), tab),
            pl.BlockSpec((None, POOL_GROUPS, POOL_CH, POOL_CH), lambda i: (layer, 0, 0, 0)),
            pl.BlockSpec((None, 1, POOL_W), of_layer),
        ] + prev_specs + [c[1](step) for c in casts],
        out_specs=(pl.BlockSpec((TILE_IN, QK_W), rows),
                   pl.BlockSpec((TILE_IN, QK_W), rows),
                   pl.BlockSpec((TILE_IN, V_W), rows),
                   pl.BlockSpec((seqs, n_prev + 1, SEQ, KV_LORA), ctx_blk4),
                   pl.BlockSpec((seqs, n_prev + 1, QK_ROPE, SEQ), ctx_blk4),
                   pl.BlockSpec((TILE_IN, POOL_W), rows),
                   pl.BlockSpec((TILE_IN, V_W), lambda i: ctx_blk4(i)[:2])) + tuple(c[3](step) for c in casts),
        scratch_shapes=[pltpu.VMEM((TILE_IN // CHUNK * REGION, POOL_W), F32)],
        compiler_params=_cparams("arbitrary"),
        name="in_proj",
    )(x_ctx, x_lat, ada, ada, g1, w_in, qg, kvg, w_uq, w_uk, w_uv, tq, tc, ts, w_pool, pscale, *prev_args,
      *(c[0] for c in casts))


def _softmax_pv(scores, values):
    m = scores[0].max(axis=-1, keepdims=True)
    for s in scores[1:]:
        m = jnp.maximum(m, s.max(axis=-1, keepdims=True))
    den = None
    out = None
    for s, v in zip(scores, values):
        p = jnp.exp(s - m)
        d = p.sum(axis=-1, keepdims=True)
        o = _dot(p.astype(BF16), v)
        den = d if den is None else den + d
        out = o if out is None else out + o
    return out / den


def _attn_ctx_rows(q_ref, k_ref, v_ref, o_ref):
    for b in range(q_ref.shape[0] // SEQ):
        rows = slice(b * SEQ, (b + 1) * SEQ)
        for hd in range(N_HEADS):
            qk = slice(hd * HEAD_PAD, (hd + 1) * HEAD_PAD)
            vv = slice(hd * V_DIM, (hd + 1) * V_DIM)
            s = _dot_nt(q_ref[rows, qk], k_ref[rows, qk])
            o_ref[rows, vv] = _softmax_pv([s], [v_ref[rows, vv]]).astype(BF16)


def _cast_slab(n_steps, w):
    n_rows, n_cols = w.shape
    slab = n_rows // n_steps
    assert slab * n_steps == n_rows
    in_spec = lambda step_of: pl.BlockSpec((slab, n_cols), lambda *g: (step_of(*g), 0))
    return w, in_spec, jax.ShapeDtypeStruct(w.shape, BF16), in_spec


def _attn_lat_kernel(q_ref, k_ref, v_ref, cckv_ref, ckpe_ref, wuk_ref, wuv_ref, dup_ref, *rest):
    if len(rest) == 5:
        w_ref, o_ref, wb_ref, kc_ref, vc_ref = rest
        wb_ref[...] = w_ref[...].astype(BF16)
    else:
        o_ref, kc_ref, vc_ref = rest

    @pl.when(pl.program_id(1) == 0)
    def _():
        _expand_cache(cckv_ref, ckpe_ref, wuk_ref, wuv_ref, dup_ref, kc_ref, vc_ref)

    for hd in range(N_HEADS):
        qk = slice(hd * HEAD_PAD, (hd + 1) * HEAD_PAD)
        vv = slice(hd * V_DIM, (hd + 1) * V_DIM)
        q = q_ref[:, qk]
        s1 = _dot_nt(q, k_ref[:, qk])
        s2 = _dot_nt(q, kc_ref[:, qk])
        o_ref[:, vv] = _softmax_pv([s1, s2], [v_ref[:, vv], vc_ref[:, vv]]).astype(BF16)


def _attn_lat_call(layer, q, k, v, cache_ckv, cache_kpe_t, w_uk, w_uv, dup, cast_w=None):
    tq = LAT_Q_TILE if cast_w is None else LAT_Q_TILE // 2
    n_q = DEC_SEQ // tq
    lat0 = T_CTX // DEC_SEQ
    in_specs = [pl.BlockSpec((tq, QK_W), lambda b, i: (T_CTX // tq + b * n_q + i, 0)),
                pl.BlockSpec((DEC_SEQ, QK_W), lambda b, i: (lat0 + b, 0)),
                pl.BlockSpec((DEC_SEQ, V_W), lambda b, i: (lat0 + b, 0)),
                pl.BlockSpec((None, None, PAST_LEN, KV_LORA), lambda b, i: (b, layer, 0, 0)),
                pl.BlockSpec((None, None, QK_ROPE, PAST_LEN), lambda b, i: (b, layer, 0, 0)),
                pl.BlockSpec((None, KV_LORA, N_HEADS * QK_NOPE), lambda b, i: (layer, 0, 0)),
                pl.BlockSpec((None, KV_LORA, V_W), lambda b, i: (layer, 0, 0)),
                pl.BlockSpec((QK_ROPE, LANES), lambda b, i: (0, 0))]
    out_shape = [jax.ShapeDtypeStruct((T_LAT, V_W), BF16)]
    out_specs = [pl.BlockSpec((tq, V_W), lambda b, i: (b * n_q + i, 0))]
    args = [q, k, v, cache_ckv, cache_kpe_t, w_uk, w_uv, dup]
    if cast_w is not None:
        w, w_in_spec, wb_shape, wb_spec = _cast_slab(DEC_BATCH * n_q, cast_w)
        step_of = lambda b, i: b * n_q + i
        in_specs.append(w_in_spec(step_of))
        out_shape.append(wb_shape)
        out_specs.append(wb_spec(step_of))
        args.append(w)
    outs = pl.pallas_call(
        _attn_lat_kernel,
        out_shape=tuple(out_shape),
        grid=(DEC_BATCH, n_q),
        in_specs=in_specs,
        out_specs=tuple(out_specs),
        scratch_shapes=[pltpu.VMEM((PAST_LEN, QK_W), BF16), pltpu.VMEM((PAST_LEN, V_W), BF16)],
        compiler_params=_cparams("arbitrary", "arbitrary"),
        name="attn_lat",
    )(*args)
    return outs


def _swiglu_hidden(h, wg_ref, wu_ref, a_buf):
    width = a_buf.shape[1]
    for c0 in range(0, width, FF_CHUNK):
        cols = slice(c0, min(c0 + FF_CHUNK, width))
        a_buf[:, cols] = (_silu(_dot(h, wg_ref[:, cols])) * _dot(h, wu_ref[:, cols])).astype(BF16)


def _mix_kernel(with_router, n_cast, attn_c_ref, attn_l_ref, pool_ref, xc_ref, xl_ref, g1_ref, sh2_ref, sc2_ref, n2_ref, wo_ref,
                *rest):
    if with_router:
        wr_ref, ltri_ref, utri_ref, *rest = rest
        cast_in, (x_out, xs_out, gs_out, route_out, meta_out, *cast_out) = rest[:n_cast], rest[n_cast:]
    else:
        g2_ref, wg_ref, wu_ref, wd_ref, *rest = rest
        cast_in, (x_out, *cast_out, a_buf) = rest[:n_cast], rest[n_cast:]
    for w_ref, wb_ref in zip(cast_in, cast_out):
        wb_ref[...] = w_ref[...].astype(BF16)
    is_ctx = pl.program_id(0) < T_CTX // TILE_MIX
    halves = []
    for r0 in range(0, TILE_MIX, TILE_MIX // 2):
        rows = slice(r0, r0 + TILE_MIX // 2)
        attn = jnp.where(is_ctx, attn_c_ref[rows, :], attn_l_ref[rows, :])
        y = _dot(attn, wo_ref[:V_W, :]) + _dot(pool_ref[rows, :], wo_ref[V_W:, :])
        x_new = jnp.where(is_ctx, xc_ref[rows, :], xl_ref[rows, :]) + g1_ref[...] * y
        x_out[rows, :] = x_new
        halves.append((_rms(x_new) * n2_ref[...] * (1.0 + sc2_ref[...]) + sh2_ref[...]).astype(BF16))
    h = jnp.concatenate(halves, axis=0)
    if not with_router:
        _swiglu_hidden(h, wg_ref, wu_ref, a_buf)
        x_out[...] += g2_ref[...] * _dot(a_buf[...], wd_ref[...])
        return

    logits = _dot(h, wr_ref[...])
    lane = lax.broadcasted_iota(jnp.int32, logits.shape, 1)
    neg = float(jnp.finfo(F32).min)
    lg = jnp.where(lane < N_EXPERTS, logits, neg)
    m1 = lg.max(axis=-1, keepdims=True)
    i1 = jnp.where(lg == m1, lane, LANES).min(axis=-1, keepdims=True)
    lg2 = jnp.where(lane == i1, neg, lg)
    m2 = lg2.max(axis=-1, keepdims=True)
    i2 = jnp.where(lg2 == m2, lane, LANES).min(axis=-1, keepdims=True)
    e = jnp.exp(m2 - m1)
    w1 = 1.0 / (1.0 + e)
    w2 = e / (1.0 + e)

    sel1 = lane == i1
    sel2 = lane == i2
    member = jnp.where(jnp.logical_or(sel1, sel2), 1.0, 0.0)
    rank = _dot(ltri_ref[...], member.astype(BF16))
    n_tok = member.sum(axis=0, keepdims=True)
    units = jnp.floor((n_tok + (SLOT_UNIT - 1)) * (1.0 / SLOT_UNIT))
    unit_off = _dot(jnp.broadcast_to(units, (8, LANES)).astype(BF16), utri_ref[...])[0:1, :]
    slot_of = SLOT_UNIT * unit_off + rank
    slot1 = jnp.where(sel1, slot_of, 0.0).sum(axis=-1, keepdims=True)
    slot2 = jnp.where(sel2, slot_of, 0.0).sum(axis=-1, keepdims=True)
    route = jnp.where(lane == 0, slot1, jnp.where(lane == 1, slot2,
                      jnp.where(lane == 2, w1, jnp.where(lane == 3, w2, 0.0))))
    route_out[...] = route
    sub = lax.broadcasted_iota(jnp.int32, (8, LANES), 0)
    meta_out[...] = jnp.where(sub == 0, units, jnp.where(sub == 1, unit_off, 0.0)).astype(jnp.int32)

    rt = route.T
    s1 = rt[0:1, :].astype(jnp.int32)
    s2 = rt[1:2, :].astype(jnp.int32)
    for r0 in range(0, SLOTS_PER_TILE, SLOT_BLOCK):
        blk = slice(r0, r0 + SLOT_BLOCK)
        srow = r0 + lax.broadcasted_iota(jnp.int32, (SLOT_BLOCK, TILE_MIX), 0)
        hit1 = srow == s1
        hit2 = srow == s2
        perm = (jnp.where(hit1, 1.0, 0.0) + jnp.where(hit2, 1.0, 0.0)).astype(BF16)
        xs_out[blk, :] = _dot(perm, h).astype(BF16)
        gate = (jnp.where(hit1, rt[2:3, :], 0.0) + jnp.where(hit2, rt[3:4, :], 0.0)).sum(axis=-1, keepdims=True)
        gs_out[blk, :] = jnp.broadcast_to(gate, (SLOT_BLOCK, LANES))


def _mod_row(i, tile):
    n_ctx = T_CTX // tile
    per_seq = DEC_SEQ // tile
    return jnp.where(i < n_ctx, 0, 1 + (i - n_ctx) // per_seq)


def _mix_call(layer, attn_ctx, attn_lat, pooled, x_ctx, x_lat, ada, n2, w_out, cast_ws, w_router=None, ffn_w=None):
    with_router = w_router is not None
    assert with_router != (ffn_w is not None)
    tm = TILE_MIX
    n_ctx = T_CTX // tm
    lat_tile0 = n_ctx if x_lat.shape[0] == T_ALL else 0
    row = lambda i: (i, 0)
    const2 = lambda i: (0, 0)
    mod = lambda k: pl.BlockSpec((None, 1, D_MODEL), lambda i: (_mod_row(i, tm), 0, k))
    ctx_blk = lambda i: (jnp.minimum(i, n_ctx - 1), 0)
    in_specs = [pl.BlockSpec((tm, V_W), ctx_blk),
                pl.BlockSpec((tm, V_W), lambda i: (jnp.maximum(i - n_ctx, 0), 0)),
                pl.BlockSpec((tm, POOL_W), row),
                pl.BlockSpec((tm, D_MODEL), ctx_blk),
                pl.BlockSpec((tm, D_MODEL), lambda i: (jnp.maximum(i - n_ctx, 0) + lat_tile0, 0)),
                mod(2), mod(3), mod(4),
                pl.BlockSpec((None, 1, D_MODEL), lambda i: (layer, 0, 0)),
                pl.BlockSpec((None, D_MODEL, D_MODEL), lambda i: (layer, 0, 0))]
    args = [attn_ctx, attn_lat, pooled, x_ctx, x_lat, ada, ada, ada, n2, w_out]
    if with_router:
        t_i = jnp.arange(tm)
        ltri = (t_i[None, :] < t_i[:, None]).astype(BF16)
        l_i = jnp.arange(LANES)
        utri = (l_i[:, None] < l_i[None, :]).astype(BF16)
        in_specs += [pl.BlockSpec((D_MODEL, LANES), const2), pl.BlockSpec((tm, tm), const2),
                     pl.BlockSpec((LANES, LANES), const2)]
        args += [w_router, ltri, utri]
        out_shape = [jax.ShapeDtypeStruct((T_ALL, D_MODEL), F32),
                     jax.ShapeDtypeStruct((N_ROUTE_TILES * SLOTS_PER_TILE, D_MODEL), BF16),
                     jax.ShapeDtypeStruct((N_ROUTE_TILES * SLOTS_PER_TILE, LANES), F32),
                     jax.ShapeDtypeStruct((T_ALL, LANES), F32),
                     jax.ShapeDtypeStruct((N_ROUTE_TILES, 8, LANES), jnp.int32)]
        out_specs = [pl.BlockSpec((tm, D_MODEL), row), pl.BlockSpec((SLOTS_PER_TILE, D_MODEL), row),
                     pl.BlockSpec((SLOTS_PER_TILE, LANES), row), pl.BlockSpec((tm, LANES), row),
                     pl.BlockSpec((None, 8, LANES), lambda i: (i, 0, 0))]
        scratch = []
    else:
        resident = lambda shape: pl.BlockSpec(shape, const2, pipeline_mode=pl.Buffered(1))
        in_specs += [mod(5), resident((D_MODEL, D_FF)), resident((D_MODEL, D_FF)), resident((D_FF, D_MODEL))]
        args += [ada, *ffn_w]
        out_shape = [jax.ShapeDtypeStruct((T_ALL, D_MODEL), F32)]
        out_specs = [pl.BlockSpec((tm, D_MODEL), row)]
        scratch = [pltpu.VMEM((tm, D_FF), BF16)]
    for w, w_spec, wb_shape, wb_spec in (_cast_slab(T_ALL // tm, w) for w in cast_ws):
        in_specs.append(w_spec(lambda i: i))
        args.append(w)
        out_shape.append(wb_shape)
        out_specs.append(wb_spec(lambda i: i))
    return pl.pallas_call(
        functools.partial(_mix_kernel, with_router, len(cast_ws)),
        out_shape=tuple(out_shape),
        grid=(T_ALL // tm,),
        in_specs=in_specs,
        out_specs=tuple(out_specs),
        scratch_shapes=scratch,
        compiler_params=_cparams("arbitrary"),
        name="mix_router" if with_router else "mix_ffn",
    )(*args)


def _route_tables(meta):
    units = meta[:, 0, :N_EXPERTS]
    offs = meta[:, 1, :N_EXPERTS]
    cum = jnp.cumsum(units, axis=0)
    total = cum[-1]
    tiles_e = (total + GEMM_UNITS - 1) // GEMM_UNITS
    tile_end = jnp.cumsum(tiles_e)
    n_act = tile_end[-1]
    m = jnp.arange(N_GEMM_TILES)
    m_eff = jnp.minimum(m, jnp.maximum(n_act - 1, 0))
    expert = jnp.minimum(jnp.sum(tile_end[None, :] <= m_eff[:, None], axis=1), N_EXPERTS - 1)
    is_e = (expert[:, None] == jnp.arange(N_EXPERTS)[None, :]).astype(jnp.int32)
    pick = lambda per_expert: jnp.sum(is_e * per_expert[None, :], axis=1)
    first_q = (m_eff - pick(tile_end - tiles_e)) * GEMM_UNITS
    count = jnp.where(m < n_act, jnp.clip(pick(total) - first_q, 0, GEMM_UNITS), 0)
    q = first_q[:, None] + jnp.arange(GEMM_UNITS)[None, :]
    rows_of = lambda table: jnp.sum(is_e[:, None, :] * table[None, :, :], axis=2)
    cum_e, units_e, offs_e = rows_of(cum), rows_of(units), rows_of(offs)
    src_tile = jnp.minimum(jnp.sum(cum_e[:, None, :] <= q[:, :, None], axis=2), N_ROUTE_TILES - 1)
    is_t = (src_tile[:, :, None] == jnp.arange(N_ROUTE_TILES)[None, None, :]).astype(jnp.int32)
    at_tile = lambda per_tile: jnp.sum(is_t * per_tile[:, None, :], axis=2)
    uid = src_tile * UNITS_PER_TILE + at_tile(offs_e) + (q - at_tile(cum_e - units_e))
    uid = jnp.where(jnp.arange(GEMM_UNITS)[None, :] < count[:, None], uid, ZERO_UNIT)
    i32 = jnp.int32
    return expert.astype(i32), count.astype(i32), uid.reshape(-1).astype(i32), n_act.reshape(1).astype(i32)


def _gemm_kernel(em_ref, cnt_ref, ul_ref, nact_ref, xs_hbm, gs_hbm, wg_ref, wu_ref, wd_ref, ys_hbm,
                 xbuf, gbuf, obuf, acc_ref, a_buf, sem_in, sem_out):
    del em_ref
    m = pl.program_id(0)
    j = pl.program_id(1)
    last_j = pl.num_programs(1) - 1
    n_act = nact_ref[0]
    active = m < n_act
    slot = m % 2

    def rows(r):
        return pl.ds(r * SLOT_UNIT if isinstance(r, int) else pl.multiple_of(r * SLOT_UNIT, SLOT_UNIT), SLOT_UNIT)

    def in_copies(mm, sl, r):
        uid = ul_ref[mm * GEMM_UNITS + r]
        return (pltpu.make_async_copy(xs_hbm.at[uid], xbuf.at[sl, rows(r)], sem_in.at[sl]),
                pltpu.make_async_copy(gs_hbm.at[uid], gbuf.at[sl, rows(r)], sem_in.at[sl]))

    def out_copy(mm, r):
        uid = ul_ref[mm * GEMM_UNITS + r]
        return pltpu.make_async_copy(obuf.at[rows(r)], ys_hbm.at[uid], sem_out.at[0])

    def for_slot(sl_dyn, fn):
        if isinstance(sl_dyn, int):
            fn(sl_dyn)
            return
        for sl in range(2):
            pl.when(sl_dyn == sl)(functools.partial(fn, sl))

    def start_in(mm, sl_dyn):
        def issue(sl):
            for r in range(GEMM_UNITS):
                for cp in in_copies(mm, sl, r):
                    cp.start()

        for_slot(sl_dyn, issue)

    def wait_in(mm, sl_dyn):
        def wait(sl):
            for r in range(GEMM_UNITS):
                for cp in in_copies(mm, sl, r):
                    cp.wait()

        for_slot(sl_dyn, wait)

    def start_out(mm):
        n = cnt_ref[mm]
        for r in range(GEMM_UNITS):
            pl.when(r < n)(lambda r=r: out_copy(mm, r).start())

    def wait_out(mm):
        def body(r, carry):
            out_copy(mm, r).wait()
            return carry

        lax.fori_loop(0, cnt_ref[mm], body, 0)

    @pl.when(jnp.logical_and(active, j == 0))
    def _():
        @pl.when(m == 0)
        def _():
            start_in(0, 0)

        wait_in(m, slot)

        @pl.when(m + 1 < n_act)
        def _():
            start_in(m + 1, 1 - slot)

    @pl.when(jnp.logical_and(jnp.logical_and(active, j == last_j), m > 0))
    def _():
        wait_out(m - 1)

    n_valid = cnt_ref[m]
    for quarter in range(1, GEMM_QUARTERS + 1):
        n_rows = quarter * GEMM_ROWS // GEMM_QUARTERS
        lo_units = (quarter - 1) * GEMM_UNITS // GEMM_QUARTERS
        hi_units = quarter * GEMM_UNITS // GEMM_QUARTERS

        @pl.when(jnp.logical_and(active, jnp.logical_and(n_valid > lo_units, n_valid <= hi_units)))
        def _(n_rows=n_rows):
            _swiglu_hidden(xbuf[slot, :n_rows, :], wg_ref, wu_ref, a_buf.at[:n_rows, :])
            part = _dot(a_buf[:n_rows, :], wd_ref[...])

            @pl.when(j == 0)
            def _():
                acc_ref[:n_rows, :] = part

            @pl.when(jnp.logical_and(j > 0, j < last_j))
            def _():
                acc_ref[:n_rows, :] += part

            @pl.when(j == last_j)
            def _():
                gate = gbuf[slot, :n_rows, :]
                for cb in range(D_MODEL // LANES):
                    cols = slice(cb * LANES, (cb + 1) * LANES)
                    obuf[:n_rows, cols] = ((acc_ref[:n_rows, cols] + part[:, cols]) * gate).astype(BF16)

    @pl.when(jnp.logical_and(active, j == last_j))
    def _():
        start_out(m)

        @pl.when(m == n_act - 1)
        def _():
            wait_out(m)


def _gemm_call(expert, count, uids, n_act, xs, gs, wg, wu, wd):
    tf = TILE_FE
    n_j = D_FF_EXPERT // tf
    assert n_j >= 2

    def w_col(m, j, em, cnt, ul, nact):
        return (em[m], 0, jnp.where(m < nact[0], j, n_j - 1))

    def w_row(m, j, em, cnt, ul, nact):
        return (em[m], jnp.where(m < nact[0], j, n_j - 1), 0)

    grid_spec = pltpu.PrefetchScalarGridSpec(
        num_scalar_prefetch=4,
        grid=(N_GEMM_TILES, n_j),
        in_specs=[pl.BlockSpec(memory_space=pl.ANY),
                  pl.BlockSpec(memory_space=pl.ANY),
                  pl.BlockSpec((None, D_MODEL, tf), w_col),
                  pl.BlockSpec((None, D_MODEL, tf), w_col),
                  pl.BlockSpec((None, tf, D_MODEL), w_row)],
        out_specs=pl.BlockSpec(memory_space=pl.ANY),
        scratch_shapes=[pltpu.VMEM((2, GEMM_ROWS, D_MODEL), BF16),
                        pltpu.VMEM((2, GEMM_ROWS, LANES), F32),
                        pltpu.VMEM((GEMM_ROWS, D_MODEL), BF16),
                        pltpu.VMEM((GEMM_ROWS, D_MODEL), F32),
                        pltpu.VMEM((GEMM_ROWS, tf), BF16),
                        pltpu.SemaphoreType.DMA((2,)),
                        pltpu.SemaphoreType.DMA((1,))])
    return pl.pallas_call(
        _gemm_kernel,
        out_shape=jax.ShapeDtypeStruct((N_UNITS, SLOT_UNIT, D_MODEL), BF16),
        grid_spec=grid_spec,
        input_output_aliases={4: 0},
        compiler_params=_cparams("arbitrary", "arbitrary"),
        name="moe_experts",
    )(expert, count, uids, n_act,
      xs.reshape(N_UNITS, SLOT_UNIT, D_MODEL), gs.reshape(N_UNITS, SLOT_UNIT, LANES), wg, wu, wd)


def _combine_kernel(ys_ref, route_ref, x_ref, g2_ref, fg_ref, o_ref):
    route = route_ref[...]
    s1 = route[:, 0:1].astype(jnp.int32)
    s2 = route[:, 1:2].astype(jnp.int32)
    scol = lax.broadcasted_iota(jnp.int32, (TILE_MIX, SLOTS_PER_TILE), 1)
    unperm = (jnp.where(scol == s1, 1.0, 0.0) + jnp.where(scol == s2, 1.0, 0.0)).astype(BF16)
    y = _dot(unperm, ys_ref[...])
    o_ref[...] = _rms(x_ref[...] + g2_ref[...] * y) * fg_ref[...]


def _combine_call(ys, route, x, ada, final_g, tile0, n_tok):
    tm = TILE_MIX
    return pl.pallas_call(
        _combine_kernel,
        out_shape=jax.ShapeDtypeStruct((n_tok, D_MODEL), F32),
        grid=(n_tok // tm,),
        in_specs=[pl.BlockSpec((SLOTS_PER_TILE, D_MODEL), lambda i: (i + tile0, 0)),
                  pl.BlockSpec((tm, LANES), lambda i: (i + tile0, 0)),
                  pl.BlockSpec((tm, D_MODEL), lambda i: (i + tile0, 0)),
                  pl.BlockSpec((None, 1, D_MODEL), lambda i: (_mod_row(i + tile0, tm), 0, 5)),
                  pl.BlockSpec((1, D_MODEL), lambda i: (0, 0))],
        out_specs=pl.BlockSpec((tm, D_MODEL), lambda i: (i, 0)),
        compiler_params=_cparams("arbitrary"),
        name="moe_combine",
    )(ys.reshape(N_UNITS * SLOT_UNIT, D_MODEL), route, x, ada, final_g)


def _rot_cols(w):
    q = QK_ROPE // 4
    src = jnp.arange(QK_ROPE)[:, None]
    dst = jnp.arange(QK_ROPE)[None, :]
    first_half = (dst // q) % 2 == 0
    rot = jnp.where(first_half, -(src == dst + q).astype(F32), (src == dst - q).astype(F32))
    return jnp.matmul(w, rot.astype(w.dtype))


def _rope_tables():
    t = np.arange(DEC_SEQ)
    rows = (t // GRID_W).astype(np.float32)
    cols = (t % GRID_W).astype(np.float32)
    half = QK_ROPE // 2
    freqs = np.float32(ROPE_BASE) ** (-np.arange(0, half, 2, dtype=np.float32) / np.float32(half))
    ang_r = rows[:, None] * freqs
    ang_c = cols[:, None] * freqs
    ang = np.concatenate([ang_r, ang_r, ang_c, ang_c], axis=-1).astype(np.float32)
    cos, sin = np.cos(ang), np.sin(ang)
    one, zero = np.ones_like(cos), np.zeros_like(cos)
    tq = np.stack([np.concatenate([one, zero], -1), np.concatenate([cos, sin], -1)])
    tc = np.stack([np.concatenate([one, one], -1), np.concatenate([cos, cos], -1)])
    ts = np.stack([np.concatenate([zero, zero], -1), np.concatenate([sin, sin], -1)])
    return jnp.asarray(tq), jnp.asarray(tc), jnp.asarray(ts)


def _layer_weights(w_in, w_uq, w_ukv):
    wi = w_in.astype(BF16)
    w_kpe = wi[..., OFF_U:OFF_U + QK_ROPE]
    w_rot = _rot_cols(w_kpe)
    w_in_r = jnp.concatenate([wi[..., :OFF_U], wi[..., OFF_U + QK_ROPE:], w_kpe, w_kpe, w_rot, w_rot], axis=-1)
    wq = w_uq.astype(BF16).reshape(DEPTH, Q_LORA, N_HEADS, QK_NOPE + QK_ROPE)
    w_uq_r = jnp.concatenate([wq, _rot_cols(wq[..., QK_NOPE:])], axis=-1).reshape(DEPTH, Q_LORA, QK_W)
    kv = w_ukv.astype(BF16).reshape(DEPTH, KV_LORA, N_HEADS, QK_NOPE + V_DIM)
    w_uk = kv[..., :QK_NOPE].reshape(DEPTH, KV_LORA, N_HEADS * QK_NOPE)
    w_uv = kv[..., QK_NOPE:].reshape(DEPTH, KV_LORA, V_W)
    return w_in_r, w_uq_r, w_uk, w_uv


def kernel(x_prompt, x_sample, c, cache_ckv, cache_kpe, c_ctx, norm1_g, norm2_g, w_ada, b_ada, w_in, q_norm_g,
           kv_norm_g, w_uq, w_ukv, w_pool, pool_scale, w_out, ffn_w_gate, ffn_w_up, ffn_w_down, moe_w_router,
           moe_w_gate, moe_w_up, moe_w_down, final_norm_g):
    x_ctx = x_prompt.reshape(T_CTX, D_MODEL)
    x_lat = x_sample.reshape(T_LAT, D_MODEL)
    cond = jnp.concatenate([c_ctx[None, :], c, jnp.zeros((COND_ROWS - 1 - DEC_BATCH, D_MODEL), F32)], axis=0)
    ada_all = _ada_call(cond, w_ada, b_ada)
    tq, tc, ts = _rope_tables()

    w_in_all, w_uq_all, w_uk_all, w_uv_all = _layer_weights(w_in, w_uq, w_ukv)
    dup = jnp.concatenate([jnp.eye(QK_ROPE, dtype=BF16)] * 2, axis=1)
    cache = (cache_ckv, jnp.swapaxes(cache_kpe, -1, -2), w_uk_all, w_uv_all, dup)

    assert DEPTH == 2
    moe_gate_2d = moe_w_gate[0].reshape(N_EXPERTS * D_MODEL, D_FF_EXPERT)
    moe_up_2d = moe_w_up[0].reshape(N_EXPERTS * D_MODEL, D_FF_EXPERT)
    moe_down_2d = moe_w_down[0].reshape(N_EXPERTS * D_FF_EXPERT, D_MODEL)
    per_layer_row = lambda g: g.reshape(DEPTH, 1, g.shape[-1])
    gains = {"norm1": per_layer_row(norm1_g), "norm2": per_layer_row(norm2_g), "q": per_layer_row(q_norm_g),
             "kv": per_layer_row(kv_norm_g), "pool": per_layer_row(pool_scale)}
    w_pool_b = w_pool.astype(BF16)
    new_ckv = new_kpe = None
    for l in range(DEPTH):
        ada = ada_all[l].reshape(COND_ROWS, 1, 6 * D_MODEL)
        cast_ws = (ffn_w_gate[0], ffn_w_up[0], ffn_w_down[0],
                   w_out.reshape(DEPTH * (V_W + POOL_W), D_MODEL)) if l == 0 else ()
        q, k, v, new_ckv, new_kpe, pooled, attn_c, *cast = _in_call(
            l, x_ctx, x_lat, ada, gains["norm1"], w_in_all, gains["q"], gains["kv"],
            w_uq_all, w_uk_all, w_uv_all, tq, tc, ts, w_pool_b, gains["pool"], new_ckv,
            new_kpe, cast_ws)
        if l == 0:
            ffn_w, w_out_b = cast[:3], cast[3].reshape(DEPTH, V_W + POOL_W, D_MODEL)
        j = l // 2
        if l % 2 == 0:
            attn_l, moe_wd = _attn_lat_call(l, q, k, v, *cache, cast_w=moe_down_2d)
            x, moe_wg = _mix_call(l, attn_c, attn_l, pooled, x_ctx, x_lat, ada, gains["norm2"],
                                  w_out_b, (moe_gate_2d,), ffn_w=ffn_w)
            x_ctx = x_lat = x
        else:
            (attn_l,) = _attn_lat_call(l, q, k, v, *cache)
            w_r = jnp.pad(moe_w_router[j], ((0, 0), (0, LANES - N_EXPERTS))).astype(BF16)
            x, xs, gs, route, meta, moe_wu = _mix_call(l, attn_c, attn_l, pooled, x_ctx, x_lat, ada,
                                                       gains["norm2"], w_out_b, (moe_up_2d,),
                                                       w_router=w_r)
            expert, count, uids, n_act = _route_tables(meta)
            ys = _gemm_call(expert, count, uids, n_act, xs, gs,
                            moe_wg.reshape(N_EXPERTS, D_MODEL, D_FF_EXPERT),
                            moe_wu.reshape(N_EXPERTS, D_MODEL, D_FF_EXPERT),
                            moe_wd.reshape(N_EXPERTS, D_FF_EXPERT, D_MODEL))
            fg = final_norm_g[None, :]
            y_prompt = _combine_call(ys, route, x, ada, fg, 0, T_CTX).reshape(BATCH, SEQ, D_MODEL)
            y_sample = _combine_call(ys, route, x, ada, fg, T_CTX // TILE_MIX, T_LAT).reshape(
                DEC_BATCH, DEC_SEQ, D_MODEL)

    return y_prompt, y_sample, new_ckv, jnp.swapaxes(new_kpe, -1, -2)
```

```python
import functools

import jax
import jax.numpy as jnp
import numpy as np
from jax import lax
from jax.experimental import pallas as pl
from jax.experimental.pallas import tpu as pltpu

D_MODEL = 1024
BATCH = 32
SEQ = 256
DEPTH = 2
DEC_BATCH = 8
DEC_SEQ = 1024
PAST_LEN = 512
GRID_W = 64
N_HEADS = 4
QK_NOPE = 128
QK_ROPE = 64
V_DIM = 128
Q_LORA = 384
KV_LORA = 256
POOL_W = 512
POOL_GROUPS = 4
POOL_WINDOWS = (2, 4, 8, 16)
POOL_CH = POOL_W // POOL_GROUPS
D_FF = 2816
N_EXPERTS = 8
D_FF_EXPERT = 3584
ROPE_BASE = 10000.0
EPS = 1e-6

T_CTX = BATCH * SEQ
T_LAT = DEC_BATCH * DEC_SEQ
T_ALL = T_CTX + T_LAT

LANES = 128
HEAD_PAD = 256
QK_W = N_HEADS * HEAD_PAD
V_W = N_HEADS * V_DIM
OFF_CKV = Q_LORA
OFF_U = Q_LORA + KV_LORA
OFF_KA = OFF_U + POOL_W
OFF_KB = OFF_KA + LANES
N_PROJ = OFF_KB + LANES

TILE_IN = 1024
CHUNK = 256
HALO = 8
REGION = CHUNK + 2 * HALO
LAT_Q_TILE = 1024
TILE_MIX = 512
COMBINE_TILES = 2
TOP_K = 2
SLOT_UNIT = 16
UNITS_PER_TILE = TILE_MIX * TOP_K // SLOT_UNIT + N_EXPERTS
SLOTS_PER_TILE = UNITS_PER_TILE * SLOT_UNIT
SLOT_BLOCK = SLOTS_PER_TILE // 4
N_ROUTE_TILES = T_ALL // TILE_MIX
N_UNITS = N_ROUTE_TILES * UNITS_PER_TILE
ZERO_UNIT = UNITS_PER_TILE - 1
GEMM_UNITS = 64
GEMM_ROWS = GEMM_UNITS * SLOT_UNIT
GEMM_QUARTERS = 4
N_GEMM_TILES = -(-N_UNITS // GEMM_UNITS) + N_EXPERTS
TILE_FE = D_FF_EXPERT // 2
FF_CHUNK = 256
COND_ROWS = 16
VMEM_LIMIT = 56 * 1024 * 1024

F32 = jnp.float32
BF16 = jnp.bfloat16


def _rms(x):
    return x * lax.rsqrt(jnp.mean(x * x, axis=-1, keepdims=True) + EPS)


def _dot(a, b):
    return jnp.dot(a, b, preferred_element_type=F32)


def _dot_nt(a, b):
    return lax.dot_general(a, b, (((1,), (1,)), ((), ())), preferred_element_type=F32)


def _silu(x):
    return x * (1.0 / (1.0 + jnp.exp(-x)))


def _cparams(*sem):
    return pltpu.CompilerParams(dimension_semantics=sem, vmem_limit_bytes=VMEM_LIMIT)


def _ada_kernel(cond_ref, w_ref, b_ref, o_ref):
    k = pl.program_id(1)
    part = _dot(_silu(cond_ref[...]).astype(BF16), w_ref[...].astype(BF16))

    @pl.when(k == 0)
    def _():
        o_ref[...] = part + b_ref[...]

    @pl.when(k > 0)
    def _():
        o_ref[...] += part


def _ada_call(cond, w_ada, b_ada):
    n_blk = 4
    kb = D_MODEL // n_blk
    return pl.pallas_call(
        _ada_kernel,
        out_shape=jax.ShapeDtypeStruct((DEPTH, COND_ROWS, 6 * D_MODEL), F32),
        grid=(DEPTH, n_blk),
        in_specs=[
            pl.BlockSpec((COND_ROWS, kb), lambda l, k: (0, k)),
            pl.BlockSpec((None, kb, 6 * D_MODEL), lambda l, k: (l, k, 0)),
            pl.BlockSpec((None, 1, 6 * D_MODEL), lambda l, k: (l, 0, 0)),
        ],
        out_specs=pl.BlockSpec((None, COND_ROWS, 6 * D_MODEL), lambda l, k: (l, 0, 0)),
        compiler_params=_cparams("arbitrary", "arbitrary"),
        name="ada_params",
    )(cond, w_ada, b_ada.reshape(DEPTH, 1, 6 * D_MODEL))


def _expand_cache(ckv_ref, kpe_ref, wuk_ref, wuv_ref, dup_ref, k_ref, v_ref):
    ckv = ckv_ref[...].astype(BF16)
    knope = _dot(ckv, wuk_ref[...])
    v_ref[...] = _dot(ckv, wuv_ref[...]).astype(BF16)
    kpe = kpe_ref[...].T
    kdup = _dot(kpe.astype(BF16), dup_ref[...]).astype(BF16)
    for h in range(N_HEADS):
        k_ref[:, h * HEAD_PAD:h * HEAD_PAD + QK_NOPE] = knope[:, h * QK_NOPE:(h + 1) * QK_NOPE].astype(BF16)
        k_ref[:, h * HEAD_PAD + QK_NOPE:(h + 1) * HEAD_PAD] = kdup


def _in_kernel(n_prev, n_cast, xc_ref, xl_ref, sh_ref, sc_ref, g_ref, win_ref, qg_ref, kvg_ref, wuq_ref, wuk_ref,
               wuv_ref, tq_ref, tc_ref, ts_ref, wpool_ref, ps_ref, *rest):
    if n_prev:
        prev_ckv_ref, prev_kpe_ref, *rest = rest
    cast_in, rest = rest[:n_cast], rest[n_cast:]
    q_out, k_out, v_out, ckv_out, kpe_out, pool_out, attn_out, *rest = rest
    cast_out, (u_scr,) = rest[:n_cast], rest[n_cast:]
    for w_ref, wb_ref in zip(cast_in, cast_out):
        wb_ref[...] = w_ref[...].astype(BF16)
    is_ctx = pl.program_id(0) >= (T_ALL - T_CTX) // TILE_IN
    if n_prev:
        ckv_out[:, :n_prev] = prev_ckv_ref[...]
        kpe_out[:, :n_prev] = prev_kpe_ref[...]
    shift = sh_ref[...]
    scale1 = 1.0 + sc_ref[...]
    qk_scale = (QK_NOPE + QK_ROPE) ** -0.5
    n_chunks = TILE_IN // CHUNK
    zeros = jnp.zeros((HALO, POOL_W), F32)
    seq_len = jnp.where(is_ctx, SEQ, DEC_SEQ)
    row = lax.broadcasted_iota(jnp.int32, (CHUNK, POOL_CH), 0)

    def pool_chunk(c):
        base = c * REGION + HALO
        rows = slice(c * CHUNK, (c + 1) * CHUNK)
        t = row + jnp.where(is_ctx, 0, c * CHUNK)
        for g, w in enumerate(POOL_WINDOWS):
            cols = slice(g * POOL_CH, (g + 1) * POOL_CH)
            acc = u_scr[base - w // 2:base - w // 2 + CHUNK, cols]
            for j in range(-w // 2 + 1, w // 2):
                acc = acc + u_scr[base + j:base + j + CHUNK, cols]
            cnt = jnp.minimum(t + w // 2, seq_len) - jnp.maximum(t - w // 2, 0)
            pooled = acc / cnt.astype(F32) - u_scr[base:base + CHUNK, cols]
            lin = _dot(pooled.astype(BF16), wpool_ref[g]) * ps_ref[:, cols]
            pool_out[rows, cols] = lin.astype(BF16)

    for c in range(n_chunks):
        r0 = c * CHUNK
        rows = slice(r0, r0 + CHUNK)
        x = jnp.where(is_ctx, xc_ref[rows, :], xl_ref[rows, :])
        h = (_rms(x) * g_ref[...] * scale1 + shift).astype(BF16)
        proj = _dot(h, win_ref[...])

        qn = (_rms(proj[:, :Q_LORA]) * qg_ref[...]).astype(BF16)
        q = _dot(qn, wuq_ref[...]) * qk_scale
        tq = tq_ref[rows, :]
        for hd in range(N_HEADS):
            lo = hd * HEAD_PAD
            q_out[rows, lo:lo + QK_NOPE] = q[:, lo:lo + QK_NOPE].astype(BF16)
            q_out[rows, lo + QK_NOPE:lo + HEAD_PAD] = (q[:, lo + QK_NOPE:lo + HEAD_PAD] * tq).astype(BF16)

        ckv = _rms(proj[:, OFF_CKV:OFF_U]) * kvg_ref[...]
        ckv_out[c, n_prev] = ckv
        ckv_b = ckv.astype(BF16)
        knope = _dot(ckv_b, wuk_ref[...])
        v_out[rows, :] = _dot(ckv_b, wuv_ref[...]).astype(BF16)

        k_a = proj[:, OFF_KA:OFF_KB]
        k_b = proj[:, OFF_KB:N_PROJ]
        kpe_out[c, n_prev] = k_a.T[:QK_ROPE, :]
        kr = (k_a * tc_ref[rows, :] + k_b * ts_ref[rows, :]).astype(BF16)
        for hd in range(N_HEADS):
            lo = hd * HEAD_PAD
            k_out[rows, lo:lo + QK_NOPE] = knope[:, hd * QK_NOPE:(hd + 1) * QK_NOPE].astype(BF16)
            k_out[rows, lo + QK_NOPE:lo + HEAD_PAD] = kr

        u = proj[:, OFF_U:OFF_KA]
        base = c * REGION + HALO
        u_scr[base:base + CHUNK, :] = u
        if c == 0:
            u_scr[0:HALO, :] = zeros
        else:
            u_scr[base - 2 * HALO:base - HALO, :] = jnp.where(is_ctx, zeros, u[:HALO, :])
        if c == n_chunks - 1:
            u_scr[base + CHUNK:base + CHUNK + HALO, :] = zeros
        else:
            u_scr[base + CHUNK + HALO:base + CHUNK + 2 * HALO, :] = jnp.where(is_ctx, zeros, u[CHUNK - HALO:, :])
        if c > 0:
            pool_chunk(c - 1)
    pool_chunk(n_chunks - 1)

    @pl.when(is_ctx)
    def _():
        _attn_ctx_rows(q_out, k_out, v_out, attn_out)


def _in_call(layer, x_ctx, x_lat, ada, g1, w_in, qg, kvg, w_uq, w_uk, w_uv, tq, tc, ts, w_pool, pscale, prev_ckv, prev_kpe,
             cast_ws=()):
    assert CHUNK == SEQ
    n_tiles = T_ALL // TILE_IN
    n_ctx = T_CTX // TILE_IN
    n_prev = 0 if prev_ckv is None else prev_ckv.shape[1]
    seqs = TILE_IN // SEQ
    lat_tile0 = n_ctx if x_lat.shape[0] == T_ALL else 0

    def tile(i):
        return jnp.where(i < n_tiles - n_ctx, i + n_ctx, i - (n_tiles - n_ctx))

    def cond_row(i):
        return jnp.maximum(tile(i) - n_ctx + 1, 0)

    def tab(i):
        return (jnp.minimum(jnp.maximum(tile(i) - n_ctx + 1, 0), 1), 0, 0)

    rows = lambda i: (tile(i), 0)
    ctx_blk4 = lambda i: (jnp.where(tile(i) < n_ctx, tile(i), 0), 0, 0, 0)
    const2 = lambda i: (0, 0)
    of_layer = lambda i: (layer, 0, 0)
    prev_specs = [pl.BlockSpec((seqs, n_prev, SEQ, KV_LORA), ctx_blk4),
                  pl.BlockSpec((seqs, n_prev, QK_ROPE, SEQ), ctx_blk4)] if n_prev else []
    prev_args = [prev_ckv, prev_kpe] if n_prev else []
    casts = [_cast_slab(n_tiles, w) for w in cast_ws]
    step = lambda i: i
    return pl.pallas_call(
        functools.partial(_in_kernel, n_prev, len(casts)),
        out_shape=(jax.ShapeDtypeStruct((T_ALL, QK_W), BF16),
                   jax.ShapeDtypeStruct((T_ALL, QK_W), BF16),
                   jax.ShapeDtypeStruct((T_ALL, V_W), BF16),
                   jax.ShapeDtypeStruct((BATCH, n_prev + 1, SEQ, KV_LORA), F32),
                   jax.ShapeDtypeStruct((BATCH, n_prev + 1, QK_ROPE, SEQ), F32),
                   jax.ShapeDtypeStruct((T_ALL, POOL_W), BF16),
                   jax.ShapeDtypeStruct((T_CTX, V_W), BF16)) + tuple(c[2] for c in casts),
        grid=(n_tiles,),
        in_specs=[
            pl.BlockSpec((TILE_IN, D_MODEL), lambda i: (jnp.minimum(tile(i), n_ctx - 1), 0)),
            pl.BlockSpec((TILE_IN, D_MODEL), lambda i: (jnp.maximum(tile(i) - n_ctx, 0) + lat_tile0, 0)),
            pl.BlockSpec((None, 1, D_MODEL), lambda i: (cond_row(i), 0, 0)),
            pl.BlockSpec((None, 1, D_MODEL), lambda i: (cond_row(i), 0, 1)),
            pl.BlockSpec((None, 1, D_MODEL), of_layer),
            pl.BlockSpec((None, D_MODEL, N_PROJ), of_layer),
            pl.BlockSpec((None, 1, Q_LORA), of_layer),
            pl.BlockSpec((None, 1, KV_LORA), of_layer),
            pl.BlockSpec((None, Q_LORA, QK_W), of_layer),
            pl.BlockSpec((None, KV_LORA, N_HEADS * QK_NOPE), of_layer),
            pl.BlockSpec((None, KV_LORA, V_W), of_layer),
            pl.BlockSpec((None, TILE_IN, LANES), tab),
            pl.BlockSpec((None, TILE_IN, LANES), tab),
            pl.BlockSpec((None, TILE_IN, LANES), tab),
            pl.BlockSpec((None, POOL_GROUPS, POOL_CH, POOL_CH), lambda i: (layer, 0, 0, 0)),
            pl.BlockSpec((None, 1, POOL_W), of_layer),
        ] + prev_specs + [c[1](step) for c in casts],
        out_specs=(pl.BlockSpec((TILE_IN, QK_W), rows),
                   pl.BlockSpec((TILE_IN, QK_W), rows),
                   pl.BlockSpec((TILE_IN, V_W), rows),
                   pl.BlockSpec((seqs, n_prev + 1, SEQ, KV_LORA), ctx_blk4),
                   pl.BlockSpec((seqs, n_prev + 1, QK_ROPE, SEQ), ctx_blk4),
                   pl.BlockSpec((TILE_IN, POOL_W), rows),
                   pl.BlockSpec((TILE_IN, V_W), lambda i: ctx_blk4(i)[:2])) + tuple(c[3](step) for c in casts),
        scratch_shapes=[pltpu.VMEM((TILE_IN // CHUNK * REGION, POOL_W), F32)],
        compiler_params=_cparams("arbitrary"),
        name="in_proj",
    )(x_ctx, x_lat, ada, ada, g1, w_in, qg, kvg, w_uq, w_uk, w_uv, tq, tc, ts, w_pool, pscale, *prev_args,
      *(c[0] for c in casts))


def _softmax_pv(scores, values):
    m = scores[0].max(axis=-1, keepdims=True)
    for s in scores[1:]:
        m = jnp.maximum(m, s.max(axis=-1, keepdims=True))
    den = None
    out = None
    for s, v in zip(scores, values):
        p = jnp.exp(s - m)
        d = p.sum(axis=-1, keepdims=True)
        o = _dot(p.astype(BF16), v)
        den = d if den is None else den + d
        out = o if out is None else out + o
    return out / den


def _attn_ctx_rows(q_ref, k_ref, v_ref, o_ref):
    for b in range(q_ref.shape[0] // SEQ):
        rows = slice(b * SEQ, (b + 1) * SEQ)
        for hd in range(N_HEADS):
            qk = slice(hd * HEAD_PAD, (hd + 1) * HEAD_PAD)
            vv = slice(hd * V_DIM, (hd + 1) * V_DIM)
            s = _dot_nt(q_ref[rows, qk], k_ref[rows, qk])
            o_ref[rows, vv] = _softmax_pv([s], [v_ref[rows, vv]]).astype(BF16)


def _cast_slab(n_steps, w):
    n_rows, n_cols = w.shape
    slab = n_rows // n_steps
    assert slab * n_steps == n_rows
    in_spec = lambda step_of: pl.BlockSpec((slab, n_cols), lambda *g: (step_of(*g), 0))
    return w, in_spec, jax.ShapeDtypeStruct(w.shape, BF16), in_spec


def _attn_lat_kernel(q_ref, k_ref, v_ref, cckv_ref, ckpe_ref, wuk_ref, wuv_ref, dup_ref, *rest):
    if len(rest) == 5:
        w_ref, o_ref, wb_ref, kc_ref, vc_ref = rest
        wb_ref[...] = w_ref[...].astype(BF16)
    else:
        o_ref, kc_ref, vc_ref = rest

    @pl.when(pl.program_id(1) == 0)
    def _():
        _expand_cache(cckv_ref, ckpe_ref, wuk_ref, wuv_ref, dup_ref, kc_ref, vc_ref)

    for hd in range(N_HEADS):
        qk = slice(hd * HEAD_PAD, (hd + 1) * HEAD_PAD)
        vv = slice(hd * V_DIM, (hd + 1) * V_DIM)
        q = q_ref[:, qk]
        s1 = _dot_nt(q, k_ref[:, qk])
        s2 = _dot_nt(q, kc_ref[:, qk])
        o_ref[:, vv] = _softmax_pv([s1, s2], [v_ref[:, vv], vc_ref[:, vv]]).astype(BF16)


def _attn_lat_call(layer, q, k, v, cache_ckv, cache_kpe_t, w_uk, w_uv, dup, cast_w=None):
    tq = LAT_Q_TILE if cast_w is None else LAT_Q_TILE // 2
    n_q = DEC_SEQ // tq
    lat0 = T_CTX // DEC_SEQ
    in_specs = [pl.BlockSpec((tq, QK_W), lambda b, i: (T_CTX // tq + b * n_q + i, 0)),
                pl.BlockSpec((DEC_SEQ, QK_W), lambda b, i: (lat0 + b, 0)),
                pl.BlockSpec((DEC_SEQ, V_W), lambda b, i: (lat0 + b, 0)),
                pl.BlockSpec((None, None, PAST_LEN, KV_LORA), lambda b, i: (b, layer, 0, 0)),
                pl.BlockSpec((None, None, QK_ROPE, PAST_LEN), lambda b, i: (b, layer, 0, 0)),
                pl.BlockSpec((None, KV_LORA, N_HEADS * QK_NOPE), lambda b, i: (layer, 0, 0)),
                pl.BlockSpec((None, KV_LORA, V_W), lambda b, i: (layer, 0, 0)),
                pl.BlockSpec((QK_ROPE, LANES), lambda b, i: (0, 0))]
    out_shape = [jax.ShapeDtypeStruct((T_LAT, V_W), BF16)]
    out_specs = [pl.BlockSpec((tq, V_W), lambda b, i: (b * n_q + i, 0))]
    args = [q, k, v, cache_ckv, cache_kpe_t, w_uk, w_uv, dup]
    if cast_w is not None:
        w, w_in_spec, wb_shape, wb_spec = _cast_slab(DEC_BATCH * n_q, cast_w)
        step_of = lambda b, i: b * n_q + i
        in_specs.append(w_in_spec(step_of))
        out_shape.append(wb_shape)
        out_specs.append(wb_spec(step_of))
        args.append(w)
    outs = pl.pallas_call(
        _attn_lat_kernel,
        out_shape=tuple(out_shape),
        grid=(DEC_BATCH, n_q),
        in_specs=in_specs,
        out_specs=tuple(out_specs),
        scratch_shapes=[pltpu.VMEM((PAST_LEN, QK_W), BF16), pltpu.VMEM((PAST_LEN, V_W), BF16)],
        compiler_params=_cparams("arbitrary", "arbitrary"),
        name="attn_lat",
    )(*args)
    return outs


def _swiglu_hidden(h, wg_ref, wu_ref, a_buf):
    width = a_buf.shape[1]
    for c0 in range(0, width, FF_CHUNK):
        cols = slice(c0, min(c0 + FF_CHUNK, width))
        a_buf[:, cols] = (_silu(_dot(h, wg_ref[:, cols])) * _dot(h, wu_ref[:, cols])).astype(BF16)


def _mix_kernel(with_router, n_cast, attn_c_ref, attn_l_ref, pool_ref, xc_ref, xl_ref, g1_ref, sh2_ref, sc2_ref, n2_ref, wo_ref,
                *rest):
    if with_router:
        wr_ref, ltri_ref, utri_ref, *rest = rest
        cast_in, (x_out, xs_out, gs_out, route_out, meta_out, *cast_out) = rest[:n_cast], rest[n_cast:]
    else:
        g2_ref, wg_ref, wu_ref, wd_ref, *rest = rest
        cast_in, (x_out, *cast_out, a_buf) = rest[:n_cast], rest[n_cast:]
    for w_ref, wb_ref in zip(cast_in, cast_out):
        wb_ref[...] = w_ref[...].astype(BF16)
    is_ctx = pl.program_id(0) < T_CTX // TILE_MIX
    halves = []
    for r0 in range(0, TILE_MIX, TILE_MIX // 2):
        rows = slice(r0, r0 + TILE_MIX // 2)
        attn = jnp.where(is_ctx, attn_c_ref[rows, :], attn_l_ref[rows, :])
        y = _dot(attn, wo_ref[:V_W, :]) + _dot(pool_ref[rows, :], wo_ref[V_W:, :])
        x_new = jnp.where(is_ctx, xc_ref[rows, :], xl_ref[rows, :]) + g1_ref[...] * y
        x_out[rows, :] = x_new
        halves.append((_rms(x_new) * n2_ref[...] * (1.0 + sc2_ref[...]) + sh2_ref[...]).astype(BF16))
    h = jnp.concatenate(halves, axis=0)
    if not with_router:
        _swiglu_hidden(h, wg_ref, wu_ref, a_buf)
        x_out[...] += g2_ref[...] * _dot(a_buf[...], wd_ref[...])
        return

    logits = _dot(h, wr_ref[...])
    lane = lax.broadcasted_iota(jnp.int32, logits.shape, 1)
    neg = float(jnp.finfo(F32).min)
    lg = jnp.where(lane < N_EXPERTS, logits, neg)
    m1 = lg.max(axis=-1, keepdims=True)
    i1 = jnp.where(lg == m1, lane, LANES).min(axis=-1, keepdims=True)
    lg2 = jnp.where(lane == i1, neg, lg)
    m2 = lg2.max(axis=-1, keepdims=True)
    i2 = jnp.where(lg2 == m2, lane, LANES).min(axis=-1, keepdims=True)
    e = jnp.exp(m2 - m1)
    w1 = 1.0 / (1.0 + e)
    w2 = e / (1.0 + e)

    sel1 = lane == i1
    sel2 = lane == i2
    member = jnp.where(jnp.logical_or(sel1, sel2), 1.0, 0.0)
    rank = _dot(ltri_ref[...], member.astype(BF16))
    n_tok = member.sum(axis=0, keepdims=True)
    units = jnp.floor((n_tok + (SLOT_UNIT - 1)) * (1.0 / SLOT_UNIT))
    unit_off = _dot(jnp.broadcast_to(units, (8, LANES)).astype(BF16), utri_ref[...])[0:1, :]
    slot_of = SLOT_UNIT * unit_off + rank
    slot1 = jnp.where(sel1, slot_of, 0.0).sum(axis=-1, keepdims=True)
    slot2 = jnp.where(sel2, slot_of, 0.0).sum(axis=-1, keepdims=True)
    route = jnp.where(lane == 0, slot1, jnp.where(lane == 1, slot2,
                      jnp.where(lane == 2, w1, jnp.where(lane == 3, w2, 0.0))))
    route_out[...] = route
    sub = lax.broadcasted_iota(jnp.int32, (8, LANES), 0)
    meta_out[...] = jnp.where(sub == 0, units, jnp.where(sub == 1, unit_off, 0.0)).astype(jnp.int32)

    rt = route.T
    s1 = rt[0:1, :].astype(jnp.int32)
    s2 = rt[1:2, :].astype(jnp.int32)
    for r0 in range(0, SLOTS_PER_TILE, SLOT_BLOCK):
        blk = slice(r0, r0 + SLOT_BLOCK)
        srow = r0 + lax.broadcasted_iota(jnp.int32, (SLOT_BLOCK, TILE_MIX), 0)
        hit1 = srow == s1
        hit2 = srow == s2
        perm = (jnp.where(hit1, 1.0, 0.0) + jnp.where(hit2, 1.0, 0.0)).astype(BF16)
        xs_out[blk, :] = _dot(perm, h).astype(BF16)
        gate = (jnp.where(hit1, rt[2:3, :], 0.0) + jnp.where(hit2, rt[3:4, :], 0.0)).sum(axis=-1, keepdims=True)
        gs_out[blk, :] = jnp.broadcast_to(gate, (SLOT_BLOCK, LANES))


def _mod_row(i, tile):
    n_ctx = T_CTX // tile
    per_seq = DEC_SEQ // tile
    return jnp.where(i < n_ctx, 0, 1 + (i - n_ctx) // per_seq)


def _mix_call(layer, attn_ctx, attn_lat, pooled, x_ctx, x_lat, ada, n2, w_out, cast_ws, w_router=None, ffn_w=None):
    with_router = w_router is not None
    assert with_router != (ffn_w is not None)
    tm = TILE_MIX
    n_ctx = T_CTX // tm
    lat_tile0 = n_ctx if x_lat.shape[0] == T_ALL else 0
    row = lambda i: (i, 0)
    const2 = lambda i: (0, 0)
    mod = lambda k: pl.BlockSpec((None, 1, D_MODEL), lambda i: (_mod_row(i, tm), 0, k))
    ctx_blk = lambda i: (jnp.minimum(i, n_ctx - 1), 0)
    in_specs = [pl.BlockSpec((tm, V_W), ctx_blk),
                pl.BlockSpec((tm, V_W), lambda i: (jnp.maximum(i - n_ctx, 0), 0)),
                pl.BlockSpec((tm, POOL_W), row),
                pl.BlockSpec((tm, D_MODEL), ctx_blk),
                pl.BlockSpec((tm, D_MODEL), lambda i: (jnp.maximum(i - n_ctx, 0) + lat_tile0, 0)),
                mod(2), mod(3), mod(4),
                pl.BlockSpec((None, 1, D_MODEL), lambda i: (layer, 0, 0)),
                pl.BlockSpec((None, D_MODEL, D_MODEL), lambda i: (layer, 0, 0))]
    args = [attn_ctx, attn_lat, pooled, x_ctx, x_lat, ada, ada, ada, n2, w_out]
    if with_router:
        t_i = jnp.arange(tm)
        ltri = (t_i[None, :] < t_i[:, None]).astype(BF16)
        l_i = jnp.arange(LANES)
        utri = (l_i[:, None] < l_i[None, :]).astype(BF16)
        in_specs += [pl.BlockSpec((D_MODEL, LANES), const2), pl.BlockSpec((tm, tm), const2),
                     pl.BlockSpec((LANES, LANES), const2)]
        args += [w_router, ltri, utri]
        out_shape = [jax.ShapeDtypeStruct((T_ALL, D_MODEL), F32),
                     jax.ShapeDtypeStruct((N_ROUTE_TILES * SLOTS_PER_TILE, D_MODEL), BF16),
                     jax.ShapeDtypeStruct((N_ROUTE_TILES * SLOTS_PER_TILE, LANES), F32),
                     jax.ShapeDtypeStruct((T_ALL, LANES), F32),
                     jax.ShapeDtypeStruct((N_ROUTE_TILES, 8, LANES), jnp.int32)]
        out_specs = [pl.BlockSpec((tm, D_MODEL), row), pl.BlockSpec((SLOTS_PER_TILE, D_MODEL), row),
                     pl.BlockSpec((SLOTS_PER_TILE, LANES), row), pl.BlockSpec((tm, LANES), row),
                     pl.BlockSpec((None, 8, LANES), lambda i: (i, 0, 0))]
        scratch = []
    else:
        resident = lambda shape: pl.BlockSpec(shape, const2, pipeline_mode=pl.Buffered(1))
        in_specs += [mod(5), resident((D_MODEL, D_FF)), resident((D_MODEL, D_FF)), resident((D_FF, D_MODEL))]
        args += [ada, *ffn_w]
        out_shape = [jax.ShapeDtypeStruct((T_ALL, D_MODEL), F32)]
        out_specs = [pl.BlockSpec((tm, D_MODEL), row)]
        scratch = [pltpu.VMEM((tm, D_FF), BF16)]
    for w, w_spec, wb_shape, wb_spec in (_cast_slab(T_ALL // tm, w) for w in cast_ws):
        in_specs.append(w_spec(lambda i: i))
        args.append(w)
        out_shape.append(wb_shape)
        out_specs.append(wb_spec(lambda i: i))
    return pl.pallas_call(
        functools.partial(_mix_kernel, with_router, len(cast_ws)),
        out_shape=tuple(out_shape),
        grid=(T_ALL // tm,),
        in_specs=in_specs,
        out_specs=tuple(out_specs),
        scratch_shapes=scratch,
        compiler_params=_cparams("arbitrary"),
        name="mix_router" if with_router else "mix_ffn",
    )(*args)


def _route_tables(meta):
    units = meta[:, 0, :N_EXPERTS]
    offs = meta[:, 1, :N_EXPERTS]
    cum = jnp.cumsum(units, axis=0)
    total = cum[-1]
    tiles_e = (total + GEMM_UNITS - 1) // GEMM_UNITS
    tile_end = jnp.cumsum(tiles_e)
    n_act = tile_end[-1]
    m = jnp.arange(N_GEMM_TILES)
    m_eff = jnp.minimum(m, jnp.maximum(n_act - 1, 0))
    expert = jnp.minimum(jnp.sum(tile_end[None, :] <= m_eff[:, None], axis=1), N_EXPERTS - 1)
    is_e = (expert[:, None] == jnp.arange(N_EXPERTS)[None, :]).astype(jnp.int32)
    pick = lambda per_expert: jnp.sum(is_e * per_expert[None, :], axis=1)
    first_q = (m_eff - pick(tile_end - tiles_e)) * GEMM_UNITS
    count = jnp.where(m < n_act, jnp.clip(pick(total) - first_q, 0, GEMM_UNITS), 0)
    q = first_q[:, None] + jnp.arange(GEMM_UNITS)[None, :]
    rows_of = lambda table: jnp.sum(is_e[:, None, :] * table[None, :, :], axis=2)
    cum_e, units_e, offs_e = rows_of(cum), rows_of(units), rows_of(offs)
    src_tile = jnp.minimum(jnp.sum(cum_e[:, None, :] <= q[:, :, None], axis=2), N_ROUTE_TILES - 1)
    is_t = (src_tile[:, :, None] == jnp.arange(N_ROUTE_TILES)[None, None, :]).astype(jnp.int32)
    at_tile = lambda per_tile: jnp.sum(is_t * per_tile[:, None, :], axis=2)
    uid = src_tile * UNITS_PER_TILE + at_tile(offs_e) + (q - at_tile(cum_e - units_e))
    uid = jnp.where(jnp.arange(GEMM_UNITS)[None, :] < count[:, None], uid, ZERO_UNIT)
    i32 = jnp.int32
    return expert.astype(i32), count.astype(i32), uid.reshape(-1).astype(i32), n_act.reshape(1).astype(i32)


def _gemm_kernel(em_ref, cnt_ref, ul_ref, nact_ref, xs_hbm, gs_hbm, wg_ref, wu_ref, wd_ref, ys_hbm,
                 xbuf, gbuf, obuf, acc_ref, a_buf, sem_in, sem_out):
    del em_ref
    m = pl.program_id(0)
    j = pl.program_id(1)
    last_j = pl.num_programs(1) - 1
    n_act = nact_ref[0]
    active = m < n_act
    slot = m % 2

    def rows(r):
        return pl.ds(r * SLOT_UNIT if isinstance(r, int) else pl.multiple_of(r * SLOT_UNIT, SLOT_UNIT), SLOT_UNIT)

    def in_copies(mm, sl, r):
        uid = ul_ref[mm * GEMM_UNITS + r]
        return (pltpu.make_async_copy(xs_hbm.at[uid], xbuf.at[sl, rows(r)], sem_in.at[sl]),
                pltpu.make_async_copy(gs_hbm.at[uid], gbuf.at[sl, rows(r)], sem_in.at[sl]))

    def out_copy(mm, r):
        uid = ul_ref[mm * GEMM_UNITS + r]
        return pltpu.make_async_copy(obuf.at[rows(r)], ys_hbm.at[uid], sem_out.at[0])

    def for_slot(sl_dyn, fn):
        if isinstance(sl_dyn, int):
            fn(sl_dyn)
            return
        for sl in range(2):
            pl.when(sl_dyn == sl)(functools.partial(fn, sl))

    def start_in(mm, sl_dyn):
        def issue(sl):
            for r in range(GEMM_UNITS):
                for cp in in_copies(mm, sl, r):
                    cp.start()

        for_slot(sl_dyn, issue)

    def wait_in(mm, sl_dyn):
        def wait(sl):
            for r in range(GEMM_UNITS):
                for cp in in_copies(mm, sl, r):
                    cp.wait()

        for_slot(sl_dyn, wait)

    def start_out(mm):
        n = cnt_ref[mm]
        for r in range(GEMM_UNITS):
            pl.when(r < n)(lambda r=r: out_copy(mm, r).start())

    def wait_out(mm):
        def body(r, carry):
            out_copy(mm, r).wait()
            return carry

        lax.fori_loop(0, cnt_ref[mm], body, 0)

    @pl.when(jnp.logical_and(active, j == 0))
    def _():
        @pl.when(m == 0)
        def _():
            start_in(0, 0)

        wait_in(m, slot)

        @pl.when(m + 1 < n_act)
        def _():
            start_in(m + 1, 1 - slot)

    @pl.when(jnp.logical_and(jnp.logical_and(active, j == last_j), m > 0))
    def _():
        wait_out(m - 1)

    n_valid = cnt_ref[m]
    for quarter in range(1, GEMM_QUARTERS + 1):
        n_rows = quarter * GEMM_ROWS // GEMM_QUARTERS
        lo_units = (quarter - 1) * GEMM_UNITS // GEMM_QUARTERS
        hi_units = quarter * GEMM_UNITS // GEMM_QUARTERS

        @pl.when(jnp.logical_and(active, jnp.logical_and(n_valid > lo_units, n_valid <= hi_units)))
        def _(n_rows=n_rows):
            _swiglu_hidden(xbuf[slot, :n_rows, :], wg_ref, wu_ref, a_buf.at[:n_rows, :])
            part = _dot(a_buf[:n_rows, :], wd_ref[...])

            @pl.when(j == 0)
            def _():
                acc_ref[:n_rows, :] = part

            @pl.when(jnp.logical_and(j > 0, j < last_j))
            def _():
                acc_ref[:n_rows, :] += part

            @pl.when(j == last_j)
            def _():
                gate = gbuf[slot, :n_rows, :]
                for cb in range(D_MODEL // LANES):
                    cols = slice(cb * LANES, (cb + 1) * LANES)
                    obuf[:n_rows, cols] = ((acc_ref[:n_rows, cols] + part[:, cols]) * gate).astype(BF16)

    @pl.when(jnp.logical_and(active, j == last_j))
    def _():
        start_out(m)

        @pl.when(m == n_act - 1)
        def _():
            wait_out(m)


def _gemm_call(expert, count, uids, n_act, xs, gs, wg, wu, wd):
    tf = TILE_FE
    n_j = D_FF_EXPERT // tf
    assert n_j >= 2

    def w_col(m, j, em, cnt, ul, nact):
        return (em[m], 0, jnp.where(m < nact[0], j, n_j - 1))

    def w_row(m, j, em, cnt, ul, nact):
        return (em[m], jnp.where(m < nact[0], j, n_j - 1), 0)

    grid_spec = pltpu.PrefetchScalarGridSpec(
        num_scalar_prefetch=4,
        grid=(N_GEMM_TILES, n_j),
        in_specs=[pl.BlockSpec(memory_space=pl.ANY),
                  pl.BlockSpec(memory_space=pl.ANY),
                  pl.BlockSpec((None, D_MODEL, tf), w_col),
                  pl.BlockSpec((None, D_MODEL, tf), w_col),
                  pl.BlockSpec((None, tf, D_MODEL), w_row)],
        out_specs=pl.BlockSpec(memory_space=pl.ANY),
        scratch_shapes=[pltpu.VMEM((2, GEMM_ROWS, D_MODEL), BF16),
                        pltpu.VMEM((2, GEMM_ROWS, LANES), F32),
                        pltpu.VMEM((GEMM_ROWS, D_MODEL), BF16),
                        pltpu.VMEM((GEMM_ROWS, D_MODEL), F32),
                        pltpu.VMEM((GEMM_ROWS, tf), BF16),
                        pltpu.SemaphoreType.DMA((2,)),
                        pltpu.SemaphoreType.DMA((1,))])
    return pl.pallas_call(
        _gemm_kernel,
        out_shape=jax.ShapeDtypeStruct((N_UNITS, SLOT_UNIT, D_MODEL), BF16),
        grid_spec=grid_spec,
        input_output_aliases={4: 0},
        compiler_params=_cparams("arbitrary", "arbitrary"),
        name="moe_experts",
    )(expert, count, uids, n_act,
      xs.reshape(N_UNITS, SLOT_UNIT, D_MODEL), gs.reshape(N_UNITS, SLOT_UNIT, LANES), wg, wu, wd)


def _combine_kernel(ys_ref, route_ref, x_ref, g2_ref, fg_ref, o_ref):
    scol = lax.broadcasted_iota(jnp.int32, (TILE_MIX, SLOTS_PER_TILE), 1)
    for t in range(COMBINE_TILES):
        rows = slice(t * TILE_MIX, (t + 1) * TILE_MIX)
        slots = slice(t * SLOTS_PER_TILE, (t + 1) * SLOTS_PER_TILE)
        route = route_ref[rows, :]
        s1 = route[:, 0:1].astype(jnp.int32)
        s2 = route[:, 1:2].astype(jnp.int32)
        unperm = (jnp.where(scol == s1, 1.0, 0.0) + jnp.where(scol == s2, 1.0, 0.0)).astype(BF16)
        y = _dot(unperm, ys_ref[slots, :])
        o_ref[rows, :] = _rms(x_ref[rows, :] + g2_ref[...] * y) * fg_ref[...]


def _combine_call(ys, route, x, ada, final_g, tile0, n_tok):
    tm = COMBINE_TILES * TILE_MIX
    assert tile0 % COMBINE_TILES == 0 and DEC_SEQ % tm == 0
    blk0 = tile0 // COMBINE_TILES
    return pl.pallas_call(
        _combine_kernel,
        out_shape=jax.ShapeDtypeStruct((n_tok, D_MODEL), F32),
        grid=(n_tok // tm,),
        in_specs=[pl.BlockSpec((COMBINE_TILES * SLOTS_PER_TILE, D_MODEL), lambda i: (i + blk0, 0)),
                  pl.BlockSpec((tm, LANES), lambda i: (i + blk0, 0)),
                  pl.BlockSpec((tm, D_MODEL), lambda i: (i + blk0, 0)),
                  pl.BlockSpec((None, 1, D_MODEL), lambda i: (_mod_row(i + blk0, tm), 0, 5)),
                  pl.BlockSpec((1, D_MODEL), lambda i: (0, 0))],
        out_specs=pl.BlockSpec((tm, D_MODEL), lambda i: (i, 0)),
        compiler_params=_cparams("arbitrary"),
        name="moe_combine",
    )(ys.reshape(N_UNITS * SLOT_UNIT, D_MODEL), route, x, ada, final_g)


def _rot_cols(w):
    q = QK_ROPE // 4
    src = jnp.arange(QK_ROPE)[:, None]
    dst = jnp.arange(QK_ROPE)[None, :]
    first_half = (dst // q) % 2 == 0
    rot = jnp.where(first_half, -(src == dst + q).astype(F32), (src == dst - q).astype(F32))
    return jnp.matmul(w, rot.astype(w.dtype))


def _rope_tables():
    t = np.arange(DEC_SEQ)
    rows = (t // GRID_W).astype(np.float32)
    cols = (t % GRID_W).astype(np.float32)
    half = QK_ROPE // 2
    freqs = np.float32(ROPE_BASE) ** (-np.arange(0, half, 2, dtype=np.float32) / np.float32(half))
    ang_r = rows[:, None] * freqs
    ang_c = cols[:, None] * freqs
    ang = np.concatenate([ang_r, ang_r, ang_c, ang_c], axis=-1).astype(np.float32)
    cos, sin = np.cos(ang), np.sin(ang)
    one, zero = np.ones_like(cos), np.zeros_like(cos)
    tq = np.stack([np.concatenate([one, zero], -1), np.concatenate([cos, sin], -1)])
    tc = np.stack([np.concatenate([one, one], -1), np.concatenate([cos, cos], -1)])
    ts = np.stack([np.concatenate([zero, zero], -1), np.concatenate([sin, sin], -1)])
    return jnp.asarray(tq), jnp.asarray(tc), jnp.asarray(ts)


def _layer_weights(w_in, w_uq, w_ukv):
    wi = w_in.astype(BF16)
    w_kpe = wi[..., OFF_U:OFF_U + QK_ROPE]
    w_rot = _rot_cols(w_kpe)
    w_in_r = jnp.concatenate([wi[..., :OFF_U], wi[..., OFF_U + QK_ROPE:], w_kpe, w_kpe, w_rot, w_rot], axis=-1)
    wq = w_uq.astype(BF16).reshape(DEPTH, Q_LORA, N_HEADS, QK_NOPE + QK_ROPE)
    w_uq_r = jnp.concatenate([wq, _rot_cols(wq[..., QK_NOPE:])], axis=-1).reshape(DEPTH, Q_LORA, QK_W)
    kv = w_ukv.astype(BF16).reshape(DEPTH, KV_LORA, N_HEADS, QK_NOPE + V_DIM)
    w_uk = kv[..., :QK_NOPE].reshape(DEPTH, KV_LORA, N_HEADS * QK_NOPE)
    w_uv = kv[..., QK_NOPE:].reshape(DEPTH, KV_LORA, V_W)
    return w_in_r, w_uq_r, w_uk, w_uv


def kernel(x_prompt, x_sample, c, cache_ckv, cache_kpe, c_ctx, norm1_g, norm2_g, w_ada, b_ada, w_in, q_norm_g,
           kv_norm_g, w_uq, w_ukv, w_pool, pool_scale, w_out, ffn_w_gate, ffn_w_up, ffn_w_down, moe_w_router,
           moe_w_gate, moe_w_up, moe_w_down, final_norm_g):
    x_ctx = x_prompt.reshape(T_CTX, D_MODEL)
    x_lat = x_sample.reshape(T_LAT, D_MODEL)
    cond = jnp.concatenate([c_ctx[None, :], c, jnp.zeros((COND_ROWS - 1 - DEC_BATCH, D_MODEL), F32)], axis=0)
    ada_all = _ada_call(cond, w_ada, b_ada)
    tq, tc, ts = _rope_tables()

    w_in_all, w_uq_all, w_uk_all, w_uv_all = _layer_weights(w_in, w_uq, w_ukv)
    dup = jnp.concatenate([jnp.eye(QK_ROPE, dtype=BF16)] * 2, axis=1)
    cache = (cache_ckv, jnp.swapaxes(cache_kpe, -1, -2), w_uk_all, w_uv_all, dup)

    assert DEPTH == 2
    moe_gate_2d = moe_w_gate[0].reshape(N_EXPERTS * D_MODEL, D_FF_EXPERT)
    moe_up_2d = moe_w_up[0].reshape(N_EXPERTS * D_MODEL, D_FF_EXPERT)
    moe_down_2d = moe_w_down[0].reshape(N_EXPERTS * D_FF_EXPERT, D_MODEL)
    per_layer_row = lambda g: g.reshape(DEPTH, 1, g.shape[-1])
    gains = {"norm1": per_layer_row(norm1_g), "norm2": per_layer_row(norm2_g), "q": per_layer_row(q_norm_g),
             "kv": per_layer_row(kv_norm_g), "pool": per_layer_row(pool_scale)}
    w_pool_b = w_pool.astype(BF16)
    new_ckv = new_kpe = None
    for l in range(DEPTH):
        ada = ada_all[l].reshape(COND_ROWS, 1, 6 * D_MODEL)
        cast_ws = (ffn_w_gate[0], ffn_w_up[0], ffn_w_down[0],
                   w_out.reshape(DEPTH * (V_W + POOL_W), D_MODEL)) if l == 0 else ()
        q, k, v, new_ckv, new_kpe, pooled, attn_c, *cast = _in_call(
            l, x_ctx, x_lat, ada, gains["norm1"], w_in_all, gains["q"], gains["kv"],
            w_uq_all, w_uk_all, w_uv_all, tq, tc, ts, w_pool_b, gains["pool"], new_ckv,
            new_kpe, cast_ws)
        if l == 0:
            ffn_w, w_out_b = cast[:3], cast[3].reshape(DEPTH, V_W + POOL_W, D_MODEL)
        j = l // 2
        if l % 2 == 0:
            attn_l, moe_wd = _attn_lat_call(l, q, k, v, *cache, cast_w=moe_down_2d)
            x, moe_wg = _mix_call(l, attn_c, attn_l, pooled, x_ctx, x_lat, ada, gains["norm2"],
                                  w_out_b, (moe_gate_2d,), ffn_w=ffn_w)
            x_ctx = x_lat = x
        else:
            (attn_l,) = _attn_lat_call(l, q, k, v, *cache)
            w_r = jnp.pad(moe_w_router[j], ((0, 0), (0, LANES - N_EXPERTS))).astype(BF16)
            x, xs, gs, route, meta, moe_wu = _mix_call(l, attn_c, attn_l, pooled, x_ctx, x_lat, ada,
                                                       gains["norm2"], w_out_b, (moe_up_2d,),
                                                       w_router=w_r)
            expert, count, uids, n_act = _route_tables(meta)
            ys = _gemm_call(expert, count, uids, n_act, xs, gs,
                            moe_wg.reshape(N_EXPERTS, D_MODEL, D_FF_EXPERT),
                            moe_wu.reshape(N_EXPERTS, D_MODEL, D_FF_EXPERT),
                            moe_wd.reshape(N_EXPERTS, D_FF_EXPERT, D_MODEL))
            fg = final_norm_g[None, :]
            y_prompt = _combine_call(ys, route, x, ada, fg, 0, T_CTX).reshape(BATCH, SEQ, D_MODEL)
            y_sample = _combine_call(ys, route, x, ada, fg, T_CTX // TILE_MIX, T_LAT).reshape(
                DEC_BATCH, DEC_SEQ, D_MODEL)

    return y_prompt, y_sample, new_ckv, jnp.swapaxes(new_kpe, -1, -2)
```

```python
import functools

import jax
import jax.numpy as jnp
import numpy as np
from jax import lax
from jax.experimental import pallas as pl
from jax.experimental.pallas import tpu as pltpu

D_MODEL = 1024
BATCH = 32
SEQ = 256
DEPTH = 2
DEC_BATCH = 8
DEC_SEQ = 1024
PAST_LEN = 512
GRID_W = 64
N_HEADS = 4
QK_NOPE = 128
QK_ROPE = 64
V_DIM = 128
Q_LORA = 384
KV_LORA = 256
POOL_W = 512
POOL_GROUPS = 4
POOL_WINDOWS = (2, 4, 8, 16)
POOL_CH = POOL_W // POOL_GROUPS
D_FF = 2816
N_EXPERTS = 8
D_FF_EXPERT = 3584
ROPE_BASE = 10000.0
EPS = 1e-6

T_CTX = BATCH * SEQ
T_LAT = DEC_BATCH * DEC_SEQ
T_ALL = T_CTX + T_LAT

LANES = 128
HEAD_PAD = 256
QK_W = N_HEADS * HEAD_PAD
V_W = N_HEADS * V_DIM
OFF_CKV = Q_LORA
OFF_U = Q_LORA + KV_LORA
OFF_KA = OFF_U + POOL_W
OFF_KB = OFF_KA + LANES
N_PROJ = OFF_KB + LANES

TILE_IN = 1024
CHUNK = 256
HALO = 8
REGION = CHUNK + 2 * HALO
LAT_Q_TILE = 1024
TILE_MIX = 512
COMBINE_TILES = 2
TOP_K = 2
SLOT_UNIT = 16
UNITS_PER_TILE = TILE_MIX * TOP_K // SLOT_UNIT + N_EXPERTS
SLOTS_PER_TILE = UNITS_PER_TILE * SLOT_UNIT
SLOT_BLOCK = SLOTS_PER_TILE // 4
N_ROUTE_TILES = T_ALL // TILE_MIX
N_UNITS = N_ROUTE_TILES * UNITS_PER_TILE
ZERO_UNIT = UNITS_PER_TILE - 1
GEMM_UNITS = 64
GEMM_ROWS = GEMM_UNITS * SLOT_UNIT
GEMM_QUARTERS = 4
N_GEMM_TILES = -(-N_UNITS // GEMM_UNITS) + N_EXPERTS
TILE_FE = D_FF_EXPERT // 2
FF_CHUNK = 256
COND_ROWS = 16
VMEM_LIMIT = 56 * 1024 * 1024

F32 = jnp.float32
BF16 = jnp.bfloat16


def _rms(x):
    return x * lax.rsqrt(jnp.mean(x * x, axis=-1, keepdims=True) + EPS)


def _dot(a, b):
    return jnp.dot(a, b, preferred_element_type=F32)


def _dot_nt(a, b):
    return lax.dot_general(a, b, (((1,), (1,)), ((), ())), preferred_element_type=F32)


def _silu(x):
    return x * (1.0 / (1.0 + jnp.exp(-x)))


def _cparams(*sem):
    return pltpu.CompilerParams(dimension_semantics=sem, vmem_limit_bytes=VMEM_LIMIT)


def _ada_kernel(cond_ref, w_ref, b_ref, o_ref):
    k = pl.program_id(1)
    part = _dot(_silu(cond_ref[...]).astype(BF16), w_ref[...].astype(BF16))

    @pl.when(k == 0)
    def _():
        o_ref[...] = part + b_ref[...]

    @pl.when(k > 0)
    def _():
        o_ref[...] += part


def _ada_call(cond, w_ada, b_ada):
    n_blk = 4
    kb = D_MODEL // n_blk
    return pl.pallas_call(
        _ada_kernel,
        out_shape=jax.ShapeDtypeStruct((DEPTH, COND_ROWS, 6 * D_MODEL), F32),
        grid=(DEPTH, n_blk),
        in_specs=[
            pl.BlockSpec((COND_ROWS, kb), lambda l, k: (0, k)),
            pl.BlockSpec((None, kb, 6 * D_MODEL), lambda l, k: (l, k, 0)),
            pl.BlockSpec((None, 1, 6 * D_MODEL), lambda l, k: (l, 0, 0)),
        ],
        out_specs=pl.BlockSpec((None, COND_ROWS, 6 * D_MODEL), lambda l, k: (l, 0, 0)),
        compiler_params=_cparams("arbitrary", "arbitrary"),
        name="ada_params",
    )(cond, w_ada, b_ada.reshape(DEPTH, 1, 6 * D_MODEL))


def _expand_cache(ckv_ref, kpe_ref, wuk_ref, wuv_ref, dup_ref, k_ref, v_ref):
    ckv = ckv_ref[...].astype(BF16)
    knope = _dot(ckv, wuk_ref[...])
    v_ref[...] = _dot(ckv, wuv_ref[...]).astype(BF16)
    kpe = kpe_ref[...].T
    kdup = _dot(kpe.astype(BF16), dup_ref[...]).astype(BF16)
    for h in range(N_HEADS):
        k_ref[:, h * HEAD_PAD:h * HEAD_PAD + QK_NOPE] = knope[:, h * QK_NOPE:(h + 1) * QK_NOPE].astype(BF16)
        k_ref[:, h * HEAD_PAD + QK_NOPE:(h + 1) * HEAD_PAD] = kdup


def _x_rows(x_refs, is_ctx, rows):
    if len(x_refs) == 1:
        return x_refs[0][rows, :]
    return jnp.where(is_ctx, x_refs[0][rows, :], x_refs[1][rows, :])


def _in_kernel(n_prev, n_cast, n_x, *refs):
    x_refs, refs = refs[:n_x], refs[n_x:]
    (sh_ref, sc_ref, g_ref, win_ref, qg_ref, kvg_ref, wuq_ref, wuk_ref, wuv_ref, tq_ref, tc_ref, ts_ref, wpool_ref,
     ps_ref, *rest) = refs
    if n_prev:
        prev_ckv_ref, prev_kpe_ref, *rest = rest
    cast_in, rest = rest[:n_cast], rest[n_cast:]
    q_out, k_out, v_out, ckv_out, kpe_out, pool_out, attn_out, *rest = rest
    cast_out, (u_scr,) = rest[:n_cast], rest[n_cast:]
    for w_ref, wb_ref in zip(cast_in, cast_out):
        wb_ref[...] = w_ref[...].astype(BF16)
    is_ctx = pl.program_id(0) >= (T_ALL - T_CTX) // TILE_IN
    if n_prev:
        ckv_out[:, :n_prev] = prev_ckv_ref[...]
        kpe_out[:, :n_prev] = prev_kpe_ref[...]
    shift = sh_ref[...]
    scale1 = 1.0 + sc_ref[...]
    qk_scale = (QK_NOPE + QK_ROPE) ** -0.5
    n_chunks = TILE_IN // CHUNK
    zeros = jnp.zeros((HALO, POOL_W), F32)
    seq_len = jnp.where(is_ctx, SEQ, DEC_SEQ)
    row = lax.broadcasted_iota(jnp.int32, (CHUNK, POOL_CH), 0)

    def pool_chunk(c):
        base = c * REGION + HALO
        rows = slice(c * CHUNK, (c + 1) * CHUNK)
        t = row + jnp.where(is_ctx, 0, c * CHUNK)
        for g, w in enumerate(POOL_WINDOWS):
            cols = slice(g * POOL_CH, (g + 1) * POOL_CH)
            acc = u_scr[base - w // 2:base - w // 2 + CHUNK, cols]
            for j in range(-w // 2 + 1, w // 2):
                acc = acc + u_scr[base + j:base + j + CHUNK, cols]
            cnt = jnp.minimum(t + w // 2, seq_len) - jnp.maximum(t - w // 2, 0)
            pooled = acc / cnt.astype(F32) - u_scr[base:base + CHUNK, cols]
            lin = _dot(pooled.astype(BF16), wpool_ref[g]) * ps_ref[:, cols]
            pool_out[rows, cols] = lin.astype(BF16)

    for c in range(n_chunks):
        r0 = c * CHUNK
        rows = slice(r0, r0 + CHUNK)
        x = _x_rows(x_refs, is_ctx, rows)
        h = (_rms(x) * g_ref[...] * scale1 + shift).astype(BF16)
        proj = _dot(h, win_ref[...])

        qn = (_rms(proj[:, :Q_LORA]) * qg_ref[...]).astype(BF16)
        q = _dot(qn, wuq_ref[...]) * qk_scale
        tq = tq_ref[rows, :]
        for hd in range(N_HEADS):
            lo = hd * HEAD_PAD
            q_out[rows, lo:lo + QK_NOPE] = q[:, lo:lo + QK_NOPE].astype(BF16)
            q_out[rows, lo + QK_NOPE:lo + HEAD_PAD] = (q[:, lo + QK_NOPE:lo + HEAD_PAD] * tq).astype(BF16)

        ckv = _rms(proj[:, OFF_CKV:OFF_U]) * kvg_ref[...]
        ckv_out[c, n_prev] = ckv
        ckv_b = ckv.astype(BF16)
        knope = _dot(ckv_b, wuk_ref[...])
        v_out[rows, :] = _dot(ckv_b, wuv_ref[...]).astype(BF16)

        k_a = proj[:, OFF_KA:OFF_KB]
        k_b = proj[:, OFF_KB:N_PROJ]
        kpe_out[c, n_prev] = k_a.T[:QK_ROPE, :]
        kr = (k_a * tc_ref[rows, :] + k_b * ts_ref[rows, :]).astype(BF16)
        for hd in range(N_HEADS):
            lo = hd * HEAD_PAD
            k_out[rows, lo:lo + QK_NOPE] = knope[:, hd * QK_NOPE:(hd + 1) * QK_NOPE].astype(BF16)
            k_out[rows, lo + QK_NOPE:lo + HEAD_PAD] = kr

        u = proj[:, OFF_U:OFF_KA]
        base = c * REGION + HALO
        u_scr[base:base + CHUNK, :] = u
        if c == 0:
            u_scr[0:HALO, :] = zeros
        else:
            u_scr[base - 2 * HALO:base - HALO, :] = jnp.where(is_ctx, zeros, u[:HALO, :])
        if c == n_chunks - 1:
            u_scr[base + CHUNK:base + CHUNK + HALO, :] = zeros
        else:
            u_scr[base + CHUNK + HALO:base + CHUNK + 2 * HALO, :] = jnp.where(is_ctx, zeros, u[CHUNK - HALO:, :])
        if c > 0:
            pool_chunk(c - 1)
    pool_chunk(n_chunks - 1)

    @pl.when(is_ctx)
    def _():
        _attn_ctx_rows(q_out, k_out, v_out, attn_out)


def _in_call(layer, x_ctx, x_lat, ada, g1, w_in, qg, kvg, w_uq, w_uk, w_uv, tq, tc, ts, w_pool, pscale, prev_ckv, prev_kpe,
             cast_ws=()):
    assert CHUNK == SEQ
    n_tiles = T_ALL // TILE_IN
    n_ctx = T_CTX // TILE_IN
    n_prev = 0 if prev_ckv is None else prev_ckv.shape[1]
    seqs = TILE_IN // SEQ

    def tile(i):
        return jnp.where(i < n_tiles - n_ctx, i + n_ctx, i - (n_tiles - n_ctx))

    def cond_row(i):
        return jnp.maximum(tile(i) - n_ctx + 1, 0)

    def tab(i):
        return (jnp.minimum(jnp.maximum(tile(i) - n_ctx + 1, 0), 1), 0, 0)

    rows = lambda i: (tile(i), 0)
    if x_ctx is x_lat:
        x_args, x_specs = [x_ctx], [pl.BlockSpec((TILE_IN, D_MODEL), rows)]
    else:
        x_args = [x_ctx, x_lat]
        x_specs = [pl.BlockSpec((TILE_IN, D_MODEL), lambda i: (jnp.minimum(tile(i), n_ctx - 1), 0)),
                   pl.BlockSpec((TILE_IN, D_MODEL), lambda i: (jnp.maximum(tile(i) - n_ctx, 0), 0))]
    ctx_blk4 = lambda i: (jnp.where(tile(i) < n_ctx, tile(i), 0), 0, 0, 0)
    const2 = lambda i: (0, 0)
    of_layer = lambda i: (layer, 0, 0)
    prev_specs = [pl.BlockSpec((seqs, n_prev, SEQ, KV_LORA), ctx_blk4),
                  pl.BlockSpec((seqs, n_prev, QK_ROPE, SEQ), ctx_blk4)] if n_prev else []
    prev_args = [prev_ckv, prev_kpe] if n_prev else []
    casts = [_cast_slab(n_tiles, w) for w in cast_ws]
    step = lambda i: i
    return pl.pallas_call(
        functools.partial(_in_kernel, n_prev, len(casts), len(x_args)),
        out_shape=(jax.ShapeDtypeStruct((T_ALL, QK_W), BF16),
                   jax.ShapeDtypeStruct((T_ALL, QK_W), BF16),
                   jax.ShapeDtypeStruct((T_ALL, V_W), BF16),
                   jax.ShapeDtypeStruct((BATCH, n_prev + 1, SEQ, KV_LORA), F32),
                   jax.ShapeDtypeStruct((BATCH, n_prev + 1, QK_ROPE, SEQ), F32),
                   jax.ShapeDtypeStruct((T_ALL, POOL_W), BF16),
                   jax.ShapeDtypeStruct((T_CTX, V_W), BF16)) + tuple(c[2] for c in casts),
        grid=(n_tiles,),
        in_specs=x_specs + [
            pl.BlockSpec((None, 1, D_MODEL), lambda i: (cond_row(i), 0, 0)),
            pl.BlockSpec((None, 1, D_MODEL), lambda i: (cond_row(i), 0, 1)),
            pl.BlockSpec((None, 1, D_MODEL), of_layer),
            pl.BlockSpec((None, D_MODEL, N_PROJ), of_layer),
            pl.BlockSpec((None, 1, Q_LORA), of_layer),
            pl.BlockSpec((None, 1, KV_LORA), of_layer),
            pl.BlockSpec((None, Q_LORA, QK_W), of_layer),
            pl.BlockSpec((None, KV_LORA, N_HEADS * QK_NOPE), of_layer),
            pl.BlockSpec((None, KV_LORA, V_W), of_layer),
            pl.BlockSpec((None, TILE_IN, LANES), tab),
            pl.BlockSpec((None, TILE_IN, LANES), tab),
            pl.BlockSpec((None, TILE_IN, LANES), tab),
            pl.BlockSpec((None, POOL_GROUPS, POOL_CH, POOL_CH), lambda i: (layer, 0, 0, 0)),
            pl.BlockSpec((None, 1, POOL_W), of_layer),
        ] + prev_specs + [c[1](step) for c in casts],
        out_specs=(pl.BlockSpec((TILE_IN, QK_W), rows),
                   pl.BlockSpec((TILE_IN, QK_W), rows),
                   pl.BlockSpec((TILE_IN, V_W), rows),
                   pl.BlockSpec((seqs, n_prev + 1, SEQ, KV_LORA), ctx_blk4),
                   pl.BlockSpec((seqs, n_prev + 1, QK_ROPE, SEQ), ctx_blk4),
                   pl.BlockSpec((TILE_IN, POOL_W), rows),
                   pl.BlockSpec((TILE_IN, V_W), lambda i: ctx_blk4(i)[:2])) + tuple(c[3](step) for c in casts),
        scratch_shapes=[pltpu.VMEM((TILE_IN // CHUNK * REGION, POOL_W), F32)],
        compiler_params=_cparams("arbitrary"),
        name="in_proj",
    )(*x_args, ada, ada, g1, w_in, qg, kvg, w_uq, w_uk, w_uv, tq, tc, ts, w_pool, pscale, *prev_args,
      *(c[0] for c in casts))


def _softmax_pv(scores, values):
    m = scores[0].max(axis=-1, keepdims=True)
    for s in scores[1:]:
        m = jnp.maximum(m, s.max(axis=-1, keepdims=True))
    den = None
    out = None
    for s, v in zip(scores, values):
        p = jnp.exp(s - m)
        d = p.sum(axis=-1, keepdims=True)
        o = _dot(p.astype(BF16), v)
        den = d if den is None else den + d
        out = o if out is None else out + o
    return out / den


def _attn_ctx_rows(q_ref, k_ref, v_ref, o_ref):
    for b in range(q_ref.shape[0] // SEQ):
        rows = slice(b * SEQ, (b + 1) * SEQ)
        for hd in range(N_HEADS):
            qk = slice(hd * HEAD_PAD, (hd + 1) * HEAD_PAD)
            vv = slice(hd * V_DIM, (hd + 1) * V_DIM)
            s = _dot_nt(q_ref[rows, qk], k_ref[rows, qk])
            o_ref[rows, vv] = _softmax_pv([s], [v_ref[rows, vv]]).astype(BF16)


def _cast_slab(n_steps, w):
    n_rows, n_cols = w.shape
    slab = n_rows // n_steps
    assert slab * n_steps == n_rows
    in_spec = lambda step_of: pl.BlockSpec((slab, n_cols), lambda *g: (step_of(*g), 0))
    return w, in_spec, jax.ShapeDtypeStruct(w.shape, BF16), in_spec


def _attn_lat_kernel(q_ref, k_ref, v_ref, cckv_ref, ckpe_ref, wuk_ref, wuv_ref, dup_ref, *rest):
    if len(rest) == 5:
        w_ref, o_ref, wb_ref, kc_ref, vc_ref = rest
        wb_ref[...] = w_ref[...].astype(BF16)
    else:
        o_ref, kc_ref, vc_ref = rest

    @pl.when(pl.program_id(1) == 0)
    def _():
        _expand_cache(cckv_ref, ckpe_ref, wuk_ref, wuv_ref, dup_ref, kc_ref, vc_ref)

    for hd in range(N_HEADS):
        qk = slice(hd * HEAD_PAD, (hd + 1) * HEAD_PAD)
        vv = slice(hd * V_DIM, (hd + 1) * V_DIM)
        q = q_ref[:, qk]
        s1 = _dot_nt(q, k_ref[:, qk])
        s2 = _dot_nt(q, kc_ref[:, qk])
        o_ref[:, vv] = _softmax_pv([s1, s2], [v_ref[:, vv], vc_ref[:, vv]]).astype(BF16)


def _attn_lat_call(layer, q, k, v, cache_ckv, cache_kpe_t, w_uk, w_uv, dup, cast_w=None):
    tq = LAT_Q_TILE if cast_w is None else LAT_Q_TILE // 2
    n_q = DEC_SEQ // tq
    lat0 = T_CTX // DEC_SEQ
    in_specs = [pl.BlockSpec((tq, QK_W), lambda b, i: (T_CTX // tq + b * n_q + i, 0)),
                pl.BlockSpec((DEC_SEQ, QK_W), lambda b, i: (lat0 + b, 0)),
                pl.BlockSpec((DEC_SEQ, V_W), lambda b, i: (lat0 + b, 0)),
                pl.BlockSpec((None, None, PAST_LEN, KV_LORA), lambda b, i: (b, layer, 0, 0)),
                pl.BlockSpec((None, None, QK_ROPE, PAST_LEN), lambda b, i: (b, layer, 0, 0)),
                pl.BlockSpec((None, KV_LORA, N_HEADS * QK_NOPE), lambda b, i: (layer, 0, 0)),
                pl.BlockSpec((None, KV_LORA, V_W), lambda b, i: (layer, 0, 0)),
                pl.BlockSpec((QK_ROPE, LANES), lambda b, i: (0, 0))]
    out_shape = [jax.ShapeDtypeStruct((T_LAT, V_W), BF16)]
    out_specs = [pl.BlockSpec((tq, V_W), lambda b, i: (b * n_q + i, 0))]
    args = [q, k, v, cache_ckv, cache_kpe_t, w_uk, w_uv, dup]
    if cast_w is not None:
        w, w_in_spec, wb_shape, wb_spec = _cast_slab(DEC_BATCH * n_q, cast_w)
        step_of = lambda b, i: b * n_q + i
        in_specs.append(w_in_spec(step_of))
        out_shape.append(wb_shape)
        out_specs.append(wb_spec(step_of))
        args.append(w)
    outs = pl.pallas_call(
        _attn_lat_kernel,
        out_shape=tuple(out_shape),
        grid=(DEC_BATCH, n_q),
        in_specs=in_specs,
        out_specs=tuple(out_specs),
        scratch_shapes=[pltpu.VMEM((PAST_LEN, QK_W), BF16), pltpu.VMEM((PAST_LEN, V_W), BF16)],
        compiler_params=_cparams("arbitrary", "arbitrary"),
        name="attn_lat",
    )(*args)
    return outs


def _swiglu_hidden(h, wg_ref, wu_ref, a_buf):
    width = a_buf.shape[1]
    for c0 in range(0, width, FF_CHUNK):
        cols = slice(c0, min(c0 + FF_CHUNK, width))
        a_buf[:, cols] = (_silu(_dot(h, wg_ref[:, cols])) * _dot(h, wu_ref[:, cols])).astype(BF16)


def _mix_kernel(with_router, n_cast, n_x, attn_c_ref, attn_l_ref, pool_ref, *refs):
    x_refs, (g1_ref, sh2_ref, sc2_ref, n2_ref, wo_ref, *rest) = refs[:n_x], refs[n_x:]
    if with_router:
        wr_ref, ltri_ref, utri_ref, *rest = rest
        cast_in, (x_out, xs_out, gs_out, route_out, meta_out, *cast_out) = rest[:n_cast], rest[n_cast:]
    else:
        g2_ref, wg_ref, wu_ref, wd_ref, *rest = rest
        cast_in, (x_out, *cast_out, a_buf) = rest[:n_cast], rest[n_cast:]
    for w_ref, wb_ref in zip(cast_in, cast_out):
        wb_ref[...] = w_ref[...].astype(BF16)
    is_ctx = pl.program_id(0) < T_CTX // TILE_MIX
    halves = []
    for r0 in range(0, TILE_MIX, TILE_MIX // 2):
        rows = slice(r0, r0 + TILE_MIX // 2)
        attn = jnp.where(is_ctx, attn_c_ref[rows, :], attn_l_ref[rows, :])
        y = _dot(attn, wo_ref[:V_W, :]) + _dot(pool_ref[rows, :], wo_ref[V_W:, :])
        x_new = _x_rows(x_refs, is_ctx, rows) + g1_ref[...] * y
        x_out[rows, :] = x_new
        halves.append((_rms(x_new) * n2_ref[...] * (1.0 + sc2_ref[...]) + sh2_ref[...]).astype(BF16))
    h = jnp.concatenate(halves, axis=0)
    if not with_router:
        _swiglu_hidden(h, wg_ref, wu_ref, a_buf)
        x_out[...] += g2_ref[...] * _dot(a_buf[...], wd_ref[...])
        return

    logits = _dot(h, wr_ref[...])
    lane = lax.broadcasted_iota(jnp.int32, logits.shape, 1)
    neg = float(jnp.finfo(F32).min)
    lg = jnp.where(lane < N_EXPERTS, logits, neg)
    m1 = lg.max(axis=-1, keepdims=True)
    i1 = jnp.where(lg == m1, lane, LANES).min(axis=-1, keepdims=True)
    lg2 = jnp.where(lane == i1, neg, lg)
    m2 = lg2.max(axis=-1, keepdims=True)
    i2 = jnp.where(lg2 == m2, lane, LANES).min(axis=-1, keepdims=True)
    e = jnp.exp(m2 - m1)
    w1 = 1.0 / (1.0 + e)
    w2 = e / (1.0 + e)

    sel1 = lane == i1
    sel2 = lane == i2
    member = jnp.where(jnp.logical_or(sel1, sel2), 1.0, 0.0)
    rank = _dot(ltri_ref[...], member.astype(BF16))
    n_tok = member.sum(axis=0, keepdims=True)
    units = jnp.floor((n_tok + (SLOT_UNIT - 1)) * (1.0 / SLOT_UNIT))
    unit_off = _dot(jnp.broadcast_to(units, (8, LANES)).astype(BF16), utri_ref[...])[0:1, :]
    slot_of = SLOT_UNIT * unit_off + rank
    slot1 = jnp.where(sel1, slot_of, 0.0).sum(axis=-1, keepdims=True)
    slot2 = jnp.where(sel2, slot_of, 0.0).sum(axis=-1, keepdims=True)
    route = jnp.where(lane == 0, slot1, jnp.where(lane == 1, slot2,
                      jnp.where(lane == 2, w1, jnp.where(lane == 3, w2, 0.0))))
    route_out[...] = route
    sub = lax.broadcasted_iota(jnp.int32, (8, LANES), 0)
    meta_out[...] = jnp.where(sub == 0, units, jnp.where(sub == 1, unit_off, 0.0)).astype(jnp.int32)

    rt = route.T
    s1 = rt[0:1, :].astype(jnp.int32)
    s2 = rt[1:2, :].astype(jnp.int32)
    for r0 in range(0, SLOTS_PER_TILE, SLOT_BLOCK):
        blk = slice(r0, r0 + SLOT_BLOCK)
        srow = r0 + lax.broadcasted_iota(jnp.int32, (SLOT_BLOCK, TILE_MIX), 0)
        hit1 = srow == s1
        hit2 = srow == s2
        perm = (jnp.where(hit1, 1.0, 0.0) + jnp.where(hit2, 1.0, 0.0)).astype(BF16)
        xs_out[blk, :] = _dot(perm, h).astype(BF16)
        gate = (jnp.where(hit1, rt[2:3, :], 0.0) + jnp.where(hit2, rt[3:4, :], 0.0)).sum(axis=-1, keepdims=True)
        gs_out[blk, :] = jnp.broadcast_to(gate, (SLOT_BLOCK, LANES))


def _mod_row(i, tile):
    n_ctx = T_CTX // tile
    per_seq = DEC_SEQ // tile
    return jnp.where(i < n_ctx, 0, 1 + (i - n_ctx) // per_seq)


def _mix_call(layer, attn_ctx, attn_lat, pooled, x_ctx, x_lat, ada, n2, w_out, cast_ws, w_router=None, ffn_w=None):
    with_router = w_router is not None
    assert with_router != (ffn_w is not None)
    tm = TILE_MIX
    n_ctx = T_CTX // tm
    row = lambda i: (i, 0)
    const2 = lambda i: (0, 0)
    mod = lambda k: pl.BlockSpec((None, 1, D_MODEL), lambda i: (_mod_row(i, tm), 0, k))
    ctx_blk = lambda i: (jnp.minimum(i, n_ctx - 1), 0)
    if x_ctx is x_lat:
        x_args, x_specs = [x_ctx], [pl.BlockSpec((tm, D_MODEL), row)]
    else:
        x_args = [x_ctx, x_lat]
        x_specs = [pl.BlockSpec((tm, D_MODEL), ctx_blk),
                   pl.BlockSpec((tm, D_MODEL), lambda i: (jnp.maximum(i - n_ctx, 0), 0))]
    in_specs = [pl.BlockSpec((tm, V_W), ctx_blk),
                pl.BlockSpec((tm, V_W), lambda i: (jnp.maximum(i - n_ctx, 0), 0)),
                pl.BlockSpec((tm, POOL_W), row)] + x_specs + [
                mod(2), mod(3), mod(4),
                pl.BlockSpec((None, 1, D_MODEL), lambda i: (layer, 0, 0)),
                pl.BlockSpec((None, D_MODEL, D_MODEL), lambda i: (layer, 0, 0))]
    args = [attn_ctx, attn_lat, pooled, *x_args, ada, ada, ada, n2, w_out]
    if with_router:
        t_i = jnp.arange(tm)
        ltri = (t_i[None, :] < t_i[:, None]).astype(BF16)
        l_i = jnp.arange(LANES)
        utri = (l_i[:, None] < l_i[None, :]).astype(BF16)
        in_specs += [pl.BlockSpec((D_MODEL, LANES), const2), pl.BlockSpec((tm, tm), const2),
                     pl.BlockSpec((LANES, LANES), const2)]
        args += [w_router, ltri, utri]
        out_shape = [jax.ShapeDtypeStruct((T_ALL, D_MODEL), F32),
                     jax.ShapeDtypeStruct((N_ROUTE_TILES * SLOTS_PER_TILE, D_MODEL), BF16),
                     jax.ShapeDtypeStruct((N_ROUTE_TILES * SLOTS_PER_TILE, LANES), F32),
                     jax.ShapeDtypeStruct((T_ALL, LANES), F32),
                     jax.ShapeDtypeStruct((N_ROUTE_TILES, 8, LANES), jnp.int32)]
        out_specs = [pl.BlockSpec((tm, D_MODEL), row), pl.BlockSpec((SLOTS_PER_TILE, D_MODEL), row),
                     pl.BlockSpec((SLOTS_PER_TILE, LANES), row), pl.BlockSpec((tm, LANES), row),
                     pl.BlockSpec((None, 8, LANES), lambda i: (i, 0, 0))]
        scratch = []
    else:
        resident = lambda shape: pl.BlockSpec(shape, const2, pipeline_mode=pl.Buffered(1))
        in_specs += [mod(5), resident((D_MODEL, D_FF)), resident((D_MODEL, D_FF)), resident((D_FF, D_MODEL))]
        args += [ada, *ffn_w]
        out_shape = [jax.ShapeDtypeStruct((T_ALL, D_MODEL), F32)]
        out_specs = [pl.BlockSpec((tm, D_MODEL), row)]
        scratch = [pltpu.VMEM((tm, D_FF), BF16)]
    for w, w_spec, wb_shape, wb_spec in (_cast_slab(T_ALL // tm, w) for w in cast_ws):
        in_specs.append(w_spec(lambda i: i))
        args.append(w)
        out_shape.append(wb_shape)
        out_specs.append(wb_spec(lambda i: i))
    return pl.pallas_call(
        functools.partial(_mix_kernel, with_router, len(cast_ws), len(x_args)),
        out_shape=tuple(out_shape),
        grid=(T_ALL // tm,),
        in_specs=in_specs,
        out_specs=tuple(out_specs),
        scratch_shapes=scratch,
        compiler_params=_cparams("arbitrary"),
        name="mix_router" if with_router else "mix_ffn",
    )(*args)


def _route_tables(meta):
    units = meta[:, 0, :N_EXPERTS]
    offs = meta[:, 1, :N_EXPERTS]
    cum = jnp.cumsum(units, axis=0)
    total = cum[-1]
    tiles_e = (total + GEMM_UNITS - 1) // GEMM_UNITS
    tile_end = jnp.cumsum(tiles_e)
    n_act = tile_end[-1]
    m = jnp.arange(N_GEMM_TILES)
    m_eff = jnp.minimum(m, jnp.maximum(n_act - 1, 0))
    expert = jnp.minimum(jnp.sum(tile_end[None, :] <= m_eff[:, None], axis=1), N_EXPERTS - 1)
    is_e = (expert[:, None] == jnp.arange(N_EXPERTS)[None, :]).astype(jnp.int32)
    pick = lambda per_expert: jnp.sum(is_e * per_expert[None, :], axis=1)
    first_q = (m_eff - pick(tile_end - tiles_e)) * GEMM_UNITS
    count = jnp.where(m < n_act, jnp.clip(pick(total) - first_q, 0, GEMM_UNITS), 0)
    q = first_q[:, None] + jnp.arange(GEMM_UNITS)[None, :]
    rows_of = lambda table: jnp.sum(is_e[:, None, :] * table[None, :, :], axis=2)
    cum_e, units_e, offs_e = rows_of(cum), rows_of(units), rows_of(offs)
    src_tile = jnp.minimum(jnp.sum(cum_e[:, None, :] <= q[:, :, None], axis=2), N_ROUTE_TILES - 1)
    is_t = (src_tile[:, :, None] == jnp.arange(N_ROUTE_TILES)[None, None, :]).astype(jnp.int32)
    at_tile = lambda per_tile: jnp.sum(is_t * per_tile[:, None, :], axis=2)
    uid = src_tile * UNITS_PER_TILE + at_tile(offs_e) + (q - at_tile(cum_e - units_e))
    uid = jnp.where(jnp.arange(GEMM_UNITS)[None, :] < count[:, None], uid, ZERO_UNIT)
    i32 = jnp.int32
    return expert.astype(i32), count.astype(i32), uid.reshape(-1).astype(i32), n_act.reshape(1).astype(i32)


def _gemm_kernel(em_ref, cnt_ref, ul_ref, nact_ref, xs_hbm, gs_hbm, wg_ref, wu_ref, wd_ref, ys_hbm,
                 xbuf, gbuf, obuf, acc_ref, a_buf, sem_in, sem_out):
    del em_ref
    m = pl.program_id(0)
    j = pl.program_id(1)
    last_j = pl.num_programs(1) - 1
    n_act = nact_ref[0]
    active = m < n_act
    slot = m % 2

    def rows(r):
        return pl.ds(r * SLOT_UNIT if isinstance(r, int) else pl.multiple_of(r * SLOT_UNIT, SLOT_UNIT), SLOT_UNIT)

    def in_copies(mm, sl, r):
        uid = ul_ref[mm * GEMM_UNITS + r]
        return (pltpu.make_async_copy(xs_hbm.at[uid], xbuf.at[sl, rows(r)], sem_in.at[sl]),
                pltpu.make_async_copy(gs_hbm.at[uid], gbuf.at[sl, rows(r)], sem_in.at[sl]))

    def out_copy(mm, r):
        uid = ul_ref[mm * GEMM_UNITS + r]
        return pltpu.make_async_copy(obuf.at[rows(r)], ys_hbm.at[uid], sem_out.at[0])

    def for_slot(sl_dyn, fn):
        if isinstance(sl_dyn, int):
            fn(sl_dyn)
            return
        for sl in range(2):
            pl.when(sl_dyn == sl)(functools.partial(fn, sl))

    def start_in(mm, sl_dyn):
        def issue(sl):
            for r in range(GEMM_UNITS):
                for cp in in_copies(mm, sl, r):
                    cp.start()

        for_slot(sl_dyn, issue)

    def wait_in(mm, sl_dyn):
        def wait(sl):
            for r in range(GEMM_UNITS):
                for cp in in_copies(mm, sl, r):
                    cp.wait()

        for_slot(sl_dyn, wait)

    def start_out(mm):
        n = cnt_ref[mm]
        for r in range(GEMM_UNITS):
            pl.when(r < n)(lambda r=r: out_copy(mm, r).start())

    def wait_out(mm):
        def body(r, carry):
            out_copy(mm, r).wait()
            return carry

        lax.fori_loop(0, cnt_ref[mm], body, 0)

    @pl.when(jnp.logical_and(active, j == 0))
    def _():
        @pl.when(m == 0)
        def _():
            start_in(0, 0)

        wait_in(m, slot)

        @pl.when(m + 1 < n_act)
        def _():
            start_in(m + 1, 1 - slot)

    @pl.when(jnp.logical_and(jnp.logical_and(active, j == last_j), m > 0))
    def _():
        wait_out(m - 1)

    n_valid = cnt_ref[m]
    for quarter in range(1, GEMM_QUARTERS + 1):
        n_rows = quarter * GEMM_ROWS // GEMM_QUARTERS
        lo_units = (quarter - 1) * GEMM_UNITS // GEMM_QUARTERS
        hi_units = quarter * GEMM_UNITS // GEMM_QUARTERS

        @pl.when(jnp.logical_and(active, jnp.logical_and(n_valid > lo_units, n_valid <= hi_units)))
        def _(n_rows=n_rows):
            _swiglu_hidden(xbuf[slot, :n_rows, :], wg_ref, wu_ref, a_buf.at[:n_rows, :])
            part = _dot(a_buf[:n_rows, :], wd_ref[...])

            @pl.when(j == 0)
            def _():
                acc_ref[:n_rows, :] = part

            @pl.when(jnp.logical_and(j > 0, j < last_j))
            def _():
                acc_ref[:n_rows, :] += part

            @pl.when(j == last_j)
            def _():
                gate = gbuf[slot, :n_rows, :]
                for cb in range(D_MODEL // LANES):
                    cols = slice(cb * LANES, (cb + 1) * LANES)
                    obuf[:n_rows, cols] = ((acc_ref[:n_rows, cols] + part[:, cols]) * gate).astype(BF16)

    @pl.when(jnp.logical_and(active, j == last_j))
    def _():
        start_out(m)

        @pl.when(m == n_act - 1)
        def _():
            wait_out(m)


def _gemm_call(expert, count, uids, n_act, xs, gs, wg, wu, wd):
    tf = TILE_FE
    n_j = D_FF_EXPERT // tf
    assert n_j >= 2

    def w_col(m, j, em, cnt, ul, nact):
        return (em[m], 0, jnp.where(m < nact[0], j, n_j - 1))

    def w_row(m, j, em, cnt, ul, nact):
        return (em[m], jnp.where(m < nact[0], j, n_j - 1), 0)

    grid_spec = pltpu.PrefetchScalarGridSpec(
        num_scalar_prefetch=4,
        grid=(N_GEMM_TILES, n_j),
        in_specs=[pl.BlockSpec(memory_space=pl.ANY),
                  pl.BlockSpec(memory_space=pl.ANY),
                  pl.BlockSpec((None, D_MODEL, tf), w_col),
                  pl.BlockSpec((None, D_MODEL, tf), w_col),
                  pl.BlockSpec((None, tf, D_MODEL), w_row)],
        out_specs=pl.BlockSpec(memory_space=pl.ANY),
        scratch_shapes=[pltpu.VMEM((2, GEMM_ROWS, D_MODEL), BF16),
                        pltpu.VMEM((2, GEMM_ROWS, LANES), F32),
                        pltpu.VMEM((GEMM_ROWS, D_MODEL), BF16),
                        pltpu.VMEM((GEMM_ROWS, D_MODEL), F32),
                        pltpu.VMEM((GEMM_ROWS, tf), BF16),
                        pltpu.SemaphoreType.DMA((2,)),
                        pltpu.SemaphoreType.DMA((1,))])
    return pl.pallas_call(
        _gemm_kernel,
        out_shape=jax.ShapeDtypeStruct((N_UNITS, SLOT_UNIT, D_MODEL), BF16),
        grid_spec=grid_spec,
        input_output_aliases={4: 0},
        compiler_params=_cparams("arbitrary", "arbitrary"),
        name="moe_experts",
    )(expert, count, uids, n_act,
      xs.reshape(N_UNITS, SLOT_UNIT, D_MODEL), gs.reshape(N_UNITS, SLOT_UNIT, LANES), wg, wu, wd)


def _combine_kernel(ys_ref, route_ref, x_ref, g2_ref, fg_ref, o_ref):
    scol = lax.broadcasted_iota(jnp.int32, (TILE_MIX, SLOTS_PER_TILE), 1)
    for t in range(COMBINE_TILES):
        rows = slice(t * TILE_MIX, (t + 1) * TILE_MIX)
        slots = slice(t * SLOTS_PER_TILE, (t + 1) * SLOTS_PER_TILE)
        route = route_ref[rows, :]
        s1 = route[:, 0:1].astype(jnp.int32)
        s2 = route[:, 1:2].astype(jnp.int32)
        unperm = (jnp.where(scol == s1, 1.0, 0.0) + jnp.where(scol == s2, 1.0, 0.0)).astype(BF16)
        y = _dot(unperm, ys_ref[slots, :])
        o_ref[rows, :] = _rms(x_ref[rows, :] + g2_ref[...] * y) * fg_ref[...]


def _combine_call(ys, route, x, ada, final_g, tile0, n_tok):
    tm = COMBINE_TILES * TILE_MIX
    assert tile0 % COMBINE_TILES == 0 and DEC_SEQ % tm == 0
    blk0 = tile0 // COMBINE_TILES
    return pl.pallas_call(
        _combine_kernel,
        out_shape=jax.ShapeDtypeStruct((n_tok, D_MODEL), F32),
        grid=(n_tok // tm,),
        in_specs=[pl.BlockSpec((COMBINE_TILES * SLOTS_PER_TILE, D_MODEL), lambda i: (i + blk0, 0)),
                  pl.BlockSpec((tm, LANES), lambda i: (i + blk0, 0)),
                  pl.BlockSpec((tm, D_MODEL), lambda i: (i + blk0, 0)),
                  pl.BlockSpec((None, 1, D_MODEL), lambda i: (_mod_row(i + blk0, tm), 0, 5)),
                  pl.BlockSpec((1, D_MODEL), lambda i: (0, 0))],
        out_specs=pl.BlockSpec((tm, D_MODEL), lambda i: (i, 0)),
        compiler_params=_cparams("arbitrary"),
        name="moe_combine",
    )(ys.reshape(N_UNITS * SLOT_UNIT, D_MODEL), route, x, ada, final_g)


def _rot_cols(w):
    q = QK_ROPE // 4
    src = jnp.arange(QK_ROPE)[:, None]
    dst = jnp.arange(QK_ROPE)[None, :]
    first_half = (dst // q) % 2 == 0
    rot = jnp.where(first_half, -(src == dst + q).astype(F32), (src == dst - q).astype(F32))
    return jnp.matmul(w, rot.astype(w.dtype))


def _rope_tables():
    t = np.arange(DEC_SEQ)
    rows = (t // GRID_W).astype(np.float32)
    cols = (t % GRID_W).astype(np.float32)
    half = QK_ROPE // 2
    freqs = np.float32(ROPE_BASE) ** (-np.arange(0, half, 2, dtype=np.float32) / np.float32(half))
    ang_r = rows[:, None] * freqs
    ang_c = cols[:, None] * freqs
    ang = np.concatenate([ang_r, ang_r, ang_c, ang_c], axis=-1).astype(np.float32)
    cos, sin = np.cos(ang), np.sin(ang)
    one, zero = np.ones_like(cos), np.zeros_like(cos)
    tq = np.stack([np.concatenate([one, zero], -1), np.concatenate([cos, sin], -1)])
    tc = np.stack([np.concatenate([one, one], -1), np.concatenate([cos, cos], -1)])
    ts = np.stack([np.concatenate([zero, zero], -1), np.concatenate([sin, sin], -1)])
    return jnp.asarray(tq), jnp.asarray(tc), jnp.asarray(ts)


def _layer_weights(w_in, w_uq, w_ukv):
    wi = w_in.astype(BF16)
    w_kpe = wi[..., OFF_U:OFF_U + QK_ROPE]
    w_rot = _rot_cols(w_kpe)
    w_in_r = jnp.concatenate([wi[..., :OFF_U], wi[..., OFF_U + QK_ROPE:], w_kpe, w_kpe, w_rot, w_rot], axis=-1)
    wq = w_uq.astype(BF16).reshape(DEPTH, Q_LORA, N_HEADS, QK_NOPE + QK_ROPE)
    w_uq_r = jnp.concatenate([wq, _rot_cols(wq[..., QK_NOPE:])], axis=-1).reshape(DEPTH, Q_LORA, QK_W)
    kv = w_ukv.astype(BF16).reshape(DEPTH, KV_LORA, N_HEADS, QK_NOPE + V_DIM)
    w_uk = kv[..., :QK_NOPE].reshape(DEPTH, KV_LORA, N_HEADS * QK_NOPE)
    w_uv = kv[..., QK_NOPE:].reshape(DEPTH, KV_LORA, V_W)
    return w_in_r, w_uq_r, w_uk, w_uv


def kernel(x_prompt, x_sample, c, cache_ckv, cache_kpe, c_ctx, norm1_g, norm2_g, w_ada, b_ada, w_in, q_norm_g,
           kv_norm_g, w_uq, w_ukv, w_pool, pool_scale, w_out, ffn_w_gate, ffn_w_up, ffn_w_down, moe_w_router,
           moe_w_gate, moe_w_up, moe_w_down, final_norm_g):
    x_ctx = x_prompt.reshape(T_CTX, D_MODEL)
    x_lat = x_sample.reshape(T_LAT, D_MODEL)
    cond = jnp.concatenate([c_ctx[None, :], c, jnp.zeros((COND_ROWS - 1 - DEC_BATCH, D_MODEL), F32)], axis=0)
    ada_all = _ada_call(cond, w_ada, b_ada)
    tq, tc, ts = _rope_tables()

    w_in_all, w_uq_all, w_uk_all, w_uv_all = _layer_weights(w_in, w_uq, w_ukv)
    dup = jnp.concatenate([jnp.eye(QK_ROPE, dtype=BF16)] * 2, axis=1)
    cache = (cache_ckv, jnp.swapaxes(cache_kpe, -1, -2), w_uk_all, w_uv_all, dup)

    assert DEPTH == 2
    moe_gate_2d = moe_w_gate[0].reshape(N_EXPERTS * D_MODEL, D_FF_EXPERT)
    moe_up_2d = moe_w_up[0].reshape(N_EXPERTS * D_MODEL, D_FF_EXPERT)
    moe_down_2d = moe_w_down[0].reshape(N_EXPERTS * D_FF_EXPERT, D_MODEL)
    per_layer_row = lambda g: g.reshape(DEPTH, 1, g.shape[-1])
    gains = {"norm1": per_layer_row(norm1_g), "norm2": per_layer_row(norm2_g), "q": per_layer_row(q_norm_g),
             "kv": per_layer_row(kv_norm_g), "pool": per_layer_row(pool_scale)}
    w_pool_b = w_pool.astype(BF16)
    new_ckv = new_kpe = None
    for l in range(DEPTH):
        ada = ada_all[l].reshape(COND_ROWS, 1, 6 * D_MODEL)
        cast_ws = (ffn_w_gate[0], ffn_w_up[0], ffn_w_down[0],
                   w_out.reshape(DEPTH * (V_W + POOL_W), D_MODEL)) if l == 0 else ()
        q, k, v, new_ckv, new_kpe, pooled, attn_c, *cast = _in_call(
            l, x_ctx, x_lat, ada, gains["norm1"], w_in_all, gains["q"], gains["kv"],
            w_uq_all, w_uk_all, w_uv_all, tq, tc, ts, w_pool_b, gains["pool"], new_ckv,
            new_kpe, cast_ws)
        if l == 0:
            ffn_w, w_out_b = cast[:3], cast[3].reshape(DEPTH, V_W + POOL_W, D_MODEL)
        j = l // 2
        if l % 2 == 0:
            attn_l, moe_wd = _attn_lat_call(l, q, k, v, *cache, cast_w=moe_down_2d)
            x, moe_wg = _mix_call(l, attn_c, attn_l, pooled, x_ctx, x_lat, ada, gains["norm2"],
                                  w_out_b, (moe_gate_2d,), ffn_w=ffn_w)
            x_ctx = x_lat = x
        else:
            (attn_l,) = _attn_lat_call(l, q, k, v, *cache)
            w_r = jnp.pad(moe_w_router[j], ((0, 0), (0, LANES - N_EXPERTS))).astype(BF16)
            x, xs, gs, route, meta, moe_wu = _mix_call(l, attn_c, attn_l, pooled, x_ctx, x_lat, ada,
                                                       gains["norm2"], w_out_b, (moe_up_2d,),
                                                       w_router=w_r)
            expert, count, uids, n_act = _route_tables(meta)
            ys = _gemm_call(expert, count, uids, n_act, xs, gs,
                            moe_wg.reshape(N_EXPERTS, D_MODEL, D_FF_EXPERT),
                            moe_wu.reshape(N_EXPERTS, D_MODEL, D_FF_EXPERT),
                            moe_wd.reshape(N_EXPERTS, D_FF_EXPERT, D_MODEL))
            fg = final_norm_g[None, :]
            y_prompt = _combine_call(ys, route, x, ada, fg, 0, T_CTX).reshape(BATCH, SEQ, D_MODEL)
            y_sample = _combine_call(ys, route, x, ada, fg, T_CTX // TILE_MIX, T_LAT).reshape(
                DEC_BATCH, DEC_SEQ, D_MODEL)

    return y_prompt, y_sample, new_ckv, jnp.swapaxes(new_kpe, -1, -2)
```
